```python
import jax
import jax.numpy as jnp
from jax import lax
import numpy as np

D_MODEL = 1024
BATCH = 8
SEQ = 2048
DEPTH = 1

MEM_LEN = 256
HEAD_DIM = 64
NORM_EPS = 1e-5

RWKV_HEADS = 8
RWKV_DIM = RWKV_HEADS * HEAD_DIM
DECAY_LORA = 64
AAA_LORA = 64
GATE_LORA = 128
GN_EPS = HEAD_DIM * 1e-5
RWKV_SPLITS = (RWKV_DIM, RWKV_DIM, RWKV_DIM, DECAY_LORA, AAA_LORA, GATE_LORA)
RWKV_IN = 3 * RWKV_DIM + DECAY_LORA + AAA_LORA + GATE_LORA

DIL_PATTERNS = ((128, 1), (512, 4), (2048, 16))
DIL_HEADS_PER_GROUP = 4
DIL_HEADS = DIL_HEADS_PER_GROUP * 3
DIL_DIM = DIL_HEADS * HEAD_DIM
DIL_OUT_DIM = DIL_HEADS_PER_GROUP * HEAD_DIM
ATTN_BLOCK = 128

MEM_HEADS = 4
MEM_HEAD_DIM = 128
MEM_DIM = MEM_HEADS * MEM_HEAD_DIM

N_BRANCHES = 3
IN_COLS = RWKV_IN + 3 * DIL_DIM + MEM_DIM + N_BRANCHES * D_MODEL

N_EXPERTS = 32
TOP_K = 4
D_FF = D_MODEL
SWIGLU_ALPHA = 1.702
SWIGLU_LIMIT = 7.0
MOE_BLOCK = 128

kernel_name = 'rwkv7_dilated_memory_moe_hybrid'


def rms_norm(x, gain):
    xf = x.astype(jnp.float32)
    y = xf * lax.rsqrt(jnp.mean(xf * xf, axis=-1, keepdims=True) + NORM_EPS)
    return (y * gain.astype(jnp.float32)).astype(x.dtype)


def token_shift(z):
    return jnp.pad(z, ((0, 0), (1, 0), (0, 0)))[:, :-1]


def rwkv7_scan(r, w, k, v, a, b):
    bsz, _, h, n = r.shape

    def step(state, inp):
        r_t, w_t, k_t, v_t, a_t, b_t = inp
        sa = jnp.einsum('bhvk,bhk->bhv', state, a_t)
        state = (state * w_t[:, :, None, :] + sa[..., None] * b_t[:, :, None, :]
                 + v_t[..., None] * k_t[:, :, None, :])
        return state, jnp.einsum('bhvk,bhk->bhv', state, r_t)

    xs = tuple(jnp.swapaxes(t, 0, 1) for t in (r, w, k, v, a, b))
    _, ys = lax.scan(step, jnp.zeros((bsz, h, n, n), jnp.float32), xs)
    return jnp.swapaxes(ys, 0, 1)


def rwkv7_time_mix(zr, w0, w2, a0, a2, g2, k_k, k_a, r_k, ln_w, ln_b):
    bsz, s, _ = zr.shape
    r, k, v, zw, za, zg = jnp.split(zr, list(np.cumsum(RWKV_SPLITS)[:-1]), axis=-1)
    w = -jax.nn.softplus(-(w0 + jnp.tanh(zw) @ w2)) - 0.5
    decay = jnp.exp(-jnp.exp(w.astype(jnp.float32)))
    a = jax.nn.sigmoid(a0 + za @ a2)
    g = jax.nn.sigmoid(zg) @ g2

    def hd(t):
        return t.reshape(bsz, s, RWKV_HEADS, HEAD_DIM).astype(jnp.float32)

    kk = hd(k * k_k)
    kk = kk / jnp.maximum(jnp.sqrt(jnp.sum(kk * kk, axis=-1, keepdims=True)), 1e-12)
    k = k * (1.0 + (a - 1.0) * k_a)
    r_h, k_h, v_h, a_h = hd(r), hd(k), hd(v), hd(a)
    y = rwkv7_scan(r_h, hd(decay), k_h, v_h, -kk, kk * a_h)
    mu = jnp.mean(y, axis=-1, keepdims=True)
    var = jnp.mean(jnp.square(y - mu), axis=-1, keepdims=True)
    yn = ((y - mu) * lax.rsqrt(var + GN_EPS)).reshape(bsz, s, RWKV_DIM)
    yn = yn * ln_w.astype(jnp.float32) + ln_b.astype(jnp.float32)
    bonus = (jnp.sum(r_h * k_h * r_k.astype(jnp.float32), axis=-1, keepdims=True) * v_h).reshape(bsz, s, RWKV_DIM)
    return ((yn + bonus) * g.astype(jnp.float32)).astype(zr.dtype)


def dilated_group_attn(q, k, v, window, dilation):
    bsz, h, s, dh = q.shape
    span = window // dilation
    n = s // dilation
    nb = -(-n // ATTN_BLOCK)
    n_pad = nb * ATTN_BLOCK

    def strided(t, front):
        t = t.reshape(bsz, h, n, dilation, dh).transpose(0, 1, 3, 2, 4)
        return jnp.pad(t, ((0, 0), (0, 0), (0, 0), (front, n_pad - n), (0, 0)))

    qb = strided(q, 0).reshape(bsz, h, dilation, nb, ATTN_BLOCK, dh)

    def key_blocks(t):
        tb = strided(t, ATTN_BLOCK).reshape(bsz, h, dilation, nb + 1, ATTN_BLOCK, dh)
        return jnp.concatenate([tb[:, :, :, :-1], tb[:, :, :, 1:]], axis=4)

    kb = key_blocks(k)
    vb = key_blocks(v)
    sc = jnp.einsum('bhrnqd,bhrnkd->bhrnqk', qb, kb).astype(jnp.float32) * dh ** -0.5
    qi = jnp.arange(ATTN_BLOCK)[:, None]
    ki = jnp.arange(2 * ATTN_BLOCK)[None, :]
    offset = qi + ATTN_BLOCK - ki
    key_idx = jnp.arange(nb)[:, None, None] * ATTN_BLOCK - ATTN_BLOCK + ki
    mask = (offset >= 0) & (offset <= span) & (key_idx >= 0)
    sc = jnp.where(mask, sc, -jnp.inf)
    m = jnp.max(sc, axis=-1, keepdims=True)
    p = jnp.exp(sc - m)
    den = jnp.sum(p, axis=-1, keepdims=True)
    o = jnp.einsum('bhrnqk,bhrnkd->bhrnqd', p, vb.astype(jnp.float32)) / den
    lse = (m + jnp.log(den))[..., 0]
    o = o.reshape(bsz, h, dilation, n_pad, dh)[:, :, :, :n].transpose(0, 1, 3, 2, 4).reshape(bsz, h, s, dh)
    lse = lse.reshape(bsz, h, dilation, n_pad)[..., :n].transpose(0, 1, 3, 2).reshape(bsz, h, s)
    return o, lse


def dilated_attention(q, k, v, q_norm, k_norm):
    bsz, s, _ = q.shape

    def heads(t):
        return t.reshape(bsz, s, DIL_HEADS, HEAD_DIM)

    qh = rms_norm(heads(q), q_norm).transpose(0, 2, 1, 3)
    kh = rms_norm(heads(k), k_norm).transpose(0, 2, 1, 3)
    vh = heads(v).transpose(0, 2, 1, 3)
    outs, lses = [], []
    for gi, (window, dilation) in enumerate(DIL_PATTERNS):
        sl = slice(gi * DIL_HEADS_PER_GROUP, (gi + 1) * DIL_HEADS_PER_GROUP)
        o, l = dilated_group_attn(qh[:, sl], kh[:, sl], vh[:, sl], window, dilation)
        outs.append(o)
        lses.append(l)
    outs = jnp.stack(outs)
    wts = jax.nn.softmax(jnp.stack(lses), axis=0)
    o = jnp.sum(wts[..., None] * outs, axis=0)
    return o.transpose(0, 2, 1, 3).reshape(bsz, s, DIL_OUT_DIM).astype(q.dtype)


def memory_cross_attn(q, mem, ln_mem, w_mem_kv, q_norm, k_norm):
    bsz, s, _ = q.shape
    m_len = mem.shape[1]
    kv = rms_norm(mem, ln_mem) @ w_mem_kv
    k, v = jnp.split(kv, 2, axis=-1)
    qh = rms_norm(q.reshape(bsz, s, MEM_HEADS, MEM_HEAD_DIM), q_norm)
    kh = rms_norm(k.reshape(bsz, m_len, MEM_HEADS, MEM_HEAD_DIM), k_norm)
    vh = v.reshape(bsz, m_len, MEM_HEADS, MEM_HEAD_DIM)
    sc = jnp.einsum('bshd,bmhd->bhsm', qh, kh).astype(jnp.float32) * MEM_HEAD_DIM ** -0.5
    p = jax.nn.softmax(sc, axis=-1)
    o = jnp.einsum('bhsm,bmhd->bshd', p, vh.astype(jnp.float32))
    return o.reshape(bsz, s, MEM_DIM).astype(q.dtype)


def mixer_sublayer(x, mem, ln1, w_in, token_mu, rwkv_w0, rwkv_w2, rwkv_a0, rwkv_a2, rwkv_g2,
                   rwkv_k_k, rwkv_k_a, rwkv_r_k, rwkv_ln_w, rwkv_ln_b, dil_q_norm, dil_k_norm,
                   ln_mem, w_mem_kv, mem_q_norm, mem_k_norm, p_rwkv, p_dil, p_mem, w_out):
    bsz, s, _ = x.shape
    h = rms_norm(x, ln1)
    z = h @ w_in
    cut = [RWKV_IN, RWKV_IN + 3 * DIL_DIM, RWKV_IN + 3 * DIL_DIM + MEM_DIM]
    zr, z_dil, z_memq, z_gate = jnp.split(z, cut, axis=-1)
    zr = zr + (token_shift(zr) - zr) * token_mu
    y_rwkv = rwkv7_time_mix(zr, rwkv_w0, rwkv_w2, rwkv_a0, rwkv_a2, rwkv_g2,
                            rwkv_k_k, rwkv_k_a, rwkv_r_k, rwkv_ln_w, rwkv_ln_b)
    dq, dk, dv = jnp.split(z_dil, 3, axis=-1)
    y_dil = dilated_attention(dq, dk, dv, dil_q_norm, dil_k_norm)
    y_mem = memory_cross_attn(z_memq, mem, ln_mem, w_mem_kv, mem_q_norm, mem_k_norm)
    g = jax.nn.sigmoid(z_gate).reshape(bsz, s, N_BRANCHES, D_MODEL)
    mixed = (g[:, :, 0] * (y_rwkv @ p_rwkv) + g[:, :, 1] * (y_dil @ p_dil)
             + g[:, :, 2] * (y_mem @ p_mem))
    return x + mixed @ w_out


def moe_ffn(h, w_router, b_router, w1, b1, w2, b2):
    bsz, s, d = h.shape
    n_tok = bsz * s
    n_asg = n_tok * TOP_K
    xt = h.reshape(n_tok, d)
    logits = (xt @ w_router).astype(jnp.float32) + b_router.astype(jnp.float32)
    top_vals, top_idx = lax.top_k(logits, TOP_K)
    gates = jax.nn.softmax(top_vals, axis=-1)
    flat_e = top_idx.reshape(-1).astype(jnp.int32)
    flat_w = gates.reshape(-1)
    flat_tok = jnp.arange(n_asg, dtype=jnp.int32) // TOP_K
    order = jnp.argsort(flat_e)
    se, stok, sw = flat_e[order], flat_tok[order], flat_w[order]
    counts = jnp.bincount(flat_e, length=N_EXPERTS).astype(jnp.int32)
    padded = ((counts + MOE_BLOCK - 1) // MOE_BLOCK) * MOE_BLOCK
    pend = jnp.cumsum(padded)
    pstart = pend - padded
    sstart = jnp.cumsum(counts) - counts
    dest = pstart[se] + (jnp.arange(n_asg, dtype=jnp.int32) - sstart[se])
    n_blocks = -(-n_asg // MOE_BLOCK) + N_EXPERTS
    n_slots = n_blocks * MOE_BLOCK
    slot_tok = jnp.zeros((n_slots,), jnp.int32).at[dest].set(stok)
    slot_w = jnp.zeros((n_slots,), jnp.float32).at[dest].set(sw)
    block_start = jnp.arange(n_blocks, dtype=jnp.int32) * MOE_BLOCK
    block_e = jnp.minimum(jnp.searchsorted(pend, block_start, side='right'), N_EXPERTS - 1)
    xs = xt[slot_tok].reshape(n_blocks, MOE_BLOCK, d)

    def expert_block(args):
        xb, e = args
        hb = xb @ w1[e] + b1[e]
        x_glu = jnp.minimum(hb[:, ::2], SWIGLU_LIMIT)
        x_lin = jnp.clip(hb[:, 1::2], -SWIGLU_LIMIT, SWIGLU_LIMIT)
        act = x_glu * jax.nn.sigmoid(SWIGLU_ALPHA * x_glu) * (x_lin + 1.0)
        return act @ w2[e] + b2[e]

    ys = lax.map(expert_block, (xs, block_e)).reshape(n_slots, d)
    out = jnp.zeros((n_tok, d), jnp.float32).at[slot_tok].add(ys.astype(jnp.float32) * slot_w[:, None])
    return out.reshape(bsz, s, d).astype(h.dtype)


def setup_inputs(seed: int = 0) -> dict:
    key = jax.random.key(seed)
    ks = iter(jax.random.split(key, 40))
    f32 = jnp.float32
    L = (DEPTH,)

    def nrm(shape, scale):
        return jax.random.normal(next(ks), shape, f32) * scale

    def unif(shape, lo, hi):
        return jax.random.uniform(next(ks), shape, f32, lo, hi)

    return {
        'x': nrm((BATCH, SEQ, D_MODEL), 1.0),
        'mem': nrm((BATCH, MEM_LEN, D_MODEL), 1.0),
        'ln1': 1.0 + nrm(L + (D_MODEL,), 0.05),
        'w_in': nrm(L + (D_MODEL, IN_COLS), D_MODEL ** -0.5),
        'token_mu': unif(L + (RWKV_IN,), 0.0, 1.0),
        'rwkv_w0': unif(L + (RWKV_DIM,), -3.0, 1.0),
        'rwkv_w2': nrm(L + (DECAY_LORA, RWKV_DIM), 0.5 * DECAY_LORA ** -0.5),
        'rwkv_a0': nrm(L + (RWKV_DIM,), 0.1),
        'rwkv_a2': nrm(L + (AAA_LORA, RWKV_DIM), 0.5 * AAA_LORA ** -0.5),
        'rwkv_g2': nrm(L + (GATE_LORA, RWKV_DIM), GATE_LORA ** -0.5),
        'rwkv_k_k': 0.85 + nrm(L + (RWKV_DIM,), 0.05),
        'rwkv_k_a': 1.0 + nrm(L + (RWKV_DIM,), 0.05),
        'rwkv_r_k': nrm(L + (RWKV_HEADS, HEAD_DIM), 0.1),
        'rwkv_ln_w': 1.0 + nrm(L + (RWKV_DIM,), 0.05),
        'rwkv_ln_b': nrm(L + (RWKV_DIM,), 0.01),
        'dil_q_norm': 1.0 + nrm(L + (HEAD_DIM,), 0.05),
        'dil_k_norm': 1.0 + nrm(L + (HEAD_DIM,), 0.05),
        'ln_mem': 1.0 + nrm(L + (D_MODEL,), 0.05),
        'w_mem_kv': nrm(L + (D_MODEL, 2 * MEM_DIM), D_MODEL ** -0.5),
        'mem_q_norm': 1.0 + nrm(L + (MEM_HEAD_DIM,), 0.05),
        'mem_k_norm': 1.0 + nrm(L + (MEM_HEAD_DIM,), 0.05),
        'p_rwkv': nrm(L + (RWKV_DIM, D_MODEL), RWKV_DIM ** -0.5),
        'p_dil': nrm(L + (DIL_OUT_DIM, D_MODEL), DIL_OUT_DIM ** -0.5),
        'p_mem': nrm(L + (MEM_DIM, D_MODEL), MEM_DIM ** -0.5),
        'w_out': nrm(L + (D_MODEL, D_MODEL), D_MODEL ** -0.5),
        'ln2': 1.0 + nrm(L + (D_MODEL,), 0.05),
        'w_router': nrm(L + (D_MODEL, N_EXPERTS), D_MODEL ** -0.5),
        'b_router': nrm(L + (N_EXPERTS,), 0.01),
        'w1': nrm(L + (N_EXPERTS, D_MODEL, 2 * D_FF), D_MODEL ** -0.5),
        'b1': nrm(L + (N_EXPERTS, 2 * D_FF), 0.01),
        'w2': nrm(L + (N_EXPERTS, D_FF, D_MODEL), D_FF ** -0.5),
        'b2': nrm(L + (N_EXPERTS, D_MODEL), 0.01),
    }


def reference(x, mem, ln1, w_in, token_mu, rwkv_w0, rwkv_w2, rwkv_a0, rwkv_a2, rwkv_g2,
              rwkv_k_k, rwkv_k_a, rwkv_r_k, rwkv_ln_w, rwkv_ln_b, dil_q_norm, dil_k_norm,
              ln_mem, w_mem_kv, mem_q_norm, mem_k_norm, p_rwkv, p_dil, p_mem, w_out,
              ln2, w_router, b_router, w1, b1, w2, b2):
    for l in range(DEPTH):
        x = mixer_sublayer(x, mem, ln1[l], w_in[l], token_mu[l], rwkv_w0[l], rwkv_w2[l],
                           rwkv_a0[l], rwkv_a2[l], rwkv_g2[l], rwkv_k_k[l], rwkv_k_a[l],
                           rwkv_r_k[l], rwkv_ln_w[l], rwkv_ln_b[l], dil_q_norm[l], dil_k_norm[l],
                           ln_mem[l], w_mem_kv[l], mem_q_norm[l], mem_k_norm[l],
                           p_rwkv[l], p_dil[l], p_mem[l], w_out[l])
        x = x + moe_ffn(rms_norm(x, ln2[l]), w_router[l], b_router[l], w1[l], b1[l], w2[l], b2[l])
    return x
```

```python
import functools

import jax
import jax.numpy as jnp
from jax import lax
from jax.experimental import pallas as pl
from jax.experimental.pallas import tpu as pltpu

F32 = jnp.float32
BF16 = jnp.bfloat16

NORM_EPS = 1e-5
HEAD_DIM = 64
RWKV_DIM = 512
RWKV_IN = 1792
GN_EPS = HEAD_DIM * 1e-5
DIL_DIM = 768
DIL_GROUP_DIM = 256
DIL_PATTERNS = ((128, 1), (512, 4), (2048, 16))
ATTN_BLOCK = 128
MEM_DIM = 512
MEM_HEAD_DIM = 128
N_EXPERTS = 32
TOP_K = 4
MOE_BLOCK = 128
SWIGLU_ALPHA = 1.702
SWIGLU_LIMIT = 7.0
LANES = 128
CHUNK = 64
NEG_BIG = -1e30
VMEM_LIMIT = 48 * 1024 * 1024


def _dot(a, b):
    return jnp.dot(a, b, preferred_element_type=F32)


def _dot_nt(a, b):
    return lax.dot_general(a, b, (((1,), (1,)), ((), ())), preferred_element_type=F32)


def _split2(x):
    hi = x.astype(BF16)
    lo = (x - hi.astype(F32)).astype(BF16)
    return hi, lo


def _split3(x):
    hi = x.astype(BF16)
    r = x - hi.astype(F32)
    lo = r.astype(BF16)
    lo2 = (r - lo.astype(F32)).astype(BF16)
    return hi, lo, lo2


def _dot_lhs3(x, w_bf16):
    hi, lo, lo2 = _split3(x)
    return _dot(hi, w_bf16) + _dot(lo, w_bf16) + _dot(lo2, w_bf16)


def _dot_rhs3(w_bf16, x):
    hi, lo, lo2 = _split3(x)
    return _dot(w_bf16, hi) + _dot(w_bf16, lo) + _dot(w_bf16, lo2)


def _dot_x3(a, b):
    ah, al = _split2(a)
    bh, bl = _split2(b)
    return _dot(ah, bh) + _dot(al, bh) + _dot(ah, bl)


def _sigmoid(x):
    return 1.0 / (1.0 + jnp.exp(-x))


def _block_ones(n, seg, dtype=BF16):
    i = jnp.arange(n)
    return (i[:, None] // seg == i[None, :] // seg).astype(dtype)


def _cparams(sem, vmem=VMEM_LIMIT):
    return pltpu.CompilerParams(dimension_semantics=sem, vmem_limit_bytes=vmem)


IN_CHUNK = 256
N_ZR = RWKV_IN // IN_CHUNK
N_DIL = 3 * DIL_DIM // IN_CHUNK
N_MQ = MEM_DIM // IN_CHUNK
N_GATE = 3 * 1024 // IN_CHUNK


def _in_proj_kernel(x_ref, ln_ref, w_ref, bd64_ref, bd128_ref, qg_ref, kg_ref, mg_ref,
                    zr_ref, dq_ref, dk_ref, dv_ref, mq_ref, gt_ref):
    x = x_ref[...]
    h = x * lax.rsqrt(jnp.mean(x * x, axis=-1, keepdims=True) + NORM_EPS) * ln_ref[...]
    hb = h.astype(BF16)

    def seg_rms(z, bd_ref, seg, gain):
        hi, lo = _split2(z * z)
        ss = _dot(hi, bd_ref[...]) + _dot(lo, bd_ref[...])
        return z * lax.rsqrt(ss * (1.0 / seg) + NORM_EPS) * gain

    for c in range(N_ZR + N_DIL + N_MQ + N_GATE):
        z = _dot(hb, w_ref[:, c * IN_CHUNK:(c + 1) * IN_CHUNK])
        if c < N_ZR:
            zr_ref[:, c * IN_CHUNK:(c + 1) * IN_CHUNK] = z
            continue
        d = c - N_ZR
        if d < N_DIL:
            which, j = divmod(d, 3)
            sl = slice(j * IN_CHUNK, (j + 1) * IN_CHUNK)
            if which == 0:
                dq_ref[:, sl] = seg_rms(z, bd64_ref, HEAD_DIM, qg_ref[...]).astype(BF16)
            elif which == 1:
                dk_ref[:, sl] = seg_rms(z, bd64_ref, HEAD_DIM, kg_ref[...]).astype(BF16)
            else:
                dv_ref[:, sl] = z.astype(BF16)
            continue
        d -= N_DIL
        if d < N_MQ:
            mq_ref[:, d * IN_CHUNK:(d + 1) * IN_CHUNK] = seg_rms(
                z, bd128_ref, MEM_HEAD_DIM, mg_ref[...]).astype(BF16)
            continue
        d -= N_MQ
        gt_ref[:, d * IN_CHUNK:(d + 1) * IN_CHUNK] = _sigmoid(z).astype(BF16)


def _in_proj(x2, ln1, w_in, dil_q_norm, dil_k_norm, mem_q_norm, bm=256):
    t, d = x2.shape
    n_cols = w_in.shape[1]
    wb = w_in.astype(BF16)
    bd64 = _block_ones(IN_CHUNK, HEAD_DIM)
    bd128 = _block_ones(IN_CHUNK, MEM_HEAD_DIM)
    qg = (jnp.tile(dil_q_norm, IN_CHUNK // HEAD_DIM) * (HEAD_DIM ** -0.5)).reshape(1, IN_CHUNK)
    kg = jnp.tile(dil_k_norm, IN_CHUNK // HEAD_DIM).reshape(1, IN_CHUNK)
    mg = jnp.tile(mem_q_norm, IN_CHUNK // MEM_HEAD_DIM).reshape(1, IN_CHUNK)
    row = lambda w: pl.BlockSpec((bm, w), lambda i: (i, 0))
    const = lambda a: pl.BlockSpec(a.shape, lambda i: (0,) * a.ndim)
    return pl.pallas_call(
        _in_proj_kernel,
        grid=(t // bm,),
        in_specs=[row(d), pl.BlockSpec((1, d), lambda i: (0, 0)),
                  pl.BlockSpec((d, n_cols), lambda i: (0, 0), pipeline_mode=pl.Buffered(1)),
                  const(bd64), const(bd128), const(qg), const(kg), const(mg)],
        out_specs=[row(RWKV_IN), row(DIL_DIM), row(DIL_DIM), row(DIL_DIM), row(MEM_DIM), row(3 * d)],
        out_shape=[jax.ShapeDtypeStruct((t, RWKV_IN), F32),
                   jax.ShapeDtypeStruct((t, DIL_DIM), BF16),
                   jax.ShapeDtypeStruct((t, DIL_DIM), BF16),
                   jax.ShapeDtypeStruct((t, DIL_DIM), BF16),
                   jax.ShapeDtypeStruct((t, MEM_DIM), BF16),
                   jax.ShapeDtypeStruct((t, 3 * d), BF16)],
        compiler_params=_cparams(("parallel",)),
        name="in_proj",
    )(x2, ln1.reshape(1, d), wb, bd64, bd128, qg, kg, mg)


def _inv_unit_lower(a, r, c):
    n = a.shape[0]
    eye = (r == c).astype(F32)
    same16 = (r // 16) == (c // 16)
    same32 = (r // 32) == (c // 32)
    d1 = jnp.where(same16, a, 0.0)
    d1b = d1.astype(BF16)
    x = eye + d1
    d2 = _dot(d1b, d1b)
    d2b = d2.astype(BF16)
    x = x + _dot(d2b, x.astype(BF16))
    d4 = _dot(d2b, d2b)
    d4b = d4.astype(BF16)
    x = x + _dot(d4b, x.astype(BF16))
    d8 = _dot(d4b, d4b)
    x = x + _dot(d8.astype(BF16), x.astype(BF16))
    off = jnp.where(same32 & jnp.logical_not(same16), a, 0.0).astype(BF16)
    xb = x.astype(BF16)
    x = x + _dot(xb, _dot(off, xb).astype(BF16))
    off = jnp.where(jnp.logical_not(same32), a, 0.0).astype(BF16)
    xb = x.astype(BF16)
    x = x + _dot(xb, _dot(off, xb).astype(BF16))
    del n
    return x


def _rwkv_kernel(z_ref, mu_ref, w0_ref, a0_ref, wl_ref, kk_ref, ka_ref, rk_ref,
                 lnw_ref, lnb_ref, tri_ref, blk_ref, bd_ref, y_ref,
                 carry_ref, state_ref, ybuf_ref, *, tt):
    s_idx = pl.program_id(1)

    @pl.when(s_idx == 0)
    def _():
        carry_ref[...] = jnp.zeros_like(carry_ref)
        state_ref[...] = jnp.zeros_like(state_ref)

    z = z_ref[0]
    rows = lax.broadcasted_iota(jnp.int32, z.shape, 0)
    prev = jnp.where(rows == 0, carry_ref[...], pltpu.roll(z, 1, axis=0))
    carry_ref[...] = z[tt - 1:tt, :]
    zl = z + (prev - z) * mu_ref[...]
    r = zl[:, 0:RWKV_DIM]
    k = zl[:, RWKV_DIM:2 * RWKV_DIM]
    v = zl[:, 2 * RWKV_DIM:3 * RWKV_DIM]
    zlo = zl[:, 3 * RWKV_DIM:RWKV_IN]
    lcol = lax.broadcasted_iota(jnp.int32, zlo.shape, 1)
    feat = jnp.where(lcol < 64, jnp.tanh(zlo), jnp.where(lcol < 128, zlo, _sigmoid(zlo)))
    lora = _dot_x3(feat, wl_ref[...])
    nu = -(w0_ref[...] + lora[:, 0:RWKV_DIM])
    softplus = jnp.maximum(nu, 0.0) + jnp.log(1.0 + jnp.exp(-jnp.abs(nu)))
    lw = -jnp.exp(-softplus - 0.5)
    a = _sigmoid(a0_ref[...] + lora[:, RWKV_DIM:2 * RWKV_DIM])
    g = lora[:, 2 * RWKV_DIM:3 * RWKV_DIM]

    bd = bd_ref[...]

    def seg_sum(x):
        return jnp.concatenate(
            [_dot_lhs3(x[:, p * LANES:(p + 1) * LANES], bd) for p in range(RWKV_DIM // LANES)], axis=1)

    kk = k * kk_ref[...]
    kk = kk / jnp.maximum(jnp.sqrt(seg_sum(kk * kk)), 1e-12)
    k2 = k * (1.0 + (a - 1.0) * ka_ref[...])
    a_s = -kk
    b_s = kk * a

    cum = _dot_rhs3(tri_ref[...], lw)
    tot = _dot_rhs3(blk_ref[...], lw)
    e_in = jnp.exp(cum)
    e_ex = jnp.exp(cum - lw)
    e_neg = jnp.exp(-cum)
    e_end = jnp.exp(tot - cum)
    e_tot = jnp.exp(tot)
    at = a_s * e_ex
    rt = r * e_in
    bt = b_s * e_neg
    kt = k2 * e_neg
    be = b_s * e_end
    ke = k2 * e_end

    r128 = lax.broadcasted_iota(jnp.int32, (LANES, LANES), 0)
    c128 = lax.broadcasted_iota(jnp.int32, (LANES, LANES), 1)
    lane_lo = lax.broadcasted_iota(jnp.int32, (CHUNK, LANES), 1) < HEAD_DIM
    stril = c128 < r128
    tril = c128 <= r128
    diag = c128 == r128

    def bdiag(xp):
        return jnp.concatenate([jnp.where(lane_lo, xp, 0.0), jnp.where(lane_lo, 0.0, xp)], axis=0)

    for ci in range(tt // CHUNK):
        rs = slice(ci * CHUNK, (ci + 1) * CHUNK)
        for p in range(RWKV_DIM // LANES):
            cs = slice(p * LANES, (p + 1) * LANES)
            at_b, rt_b = bdiag(at[rs, cs]), bdiag(rt[rs, cs])
            bt_b, kt_b = bdiag(bt[rs, cs]), bdiag(kt[rs, cs])
            v_b = bdiag(v[rs, cs])
            be_b, ke_b = bdiag(be[rs, cs]), bdiag(ke[rs, cs])
            lhs = jnp.concatenate([at_b, rt_b], axis=0).astype(BF16)
            rhs = jnp.concatenate([bt_b, kt_b], axis=0).astype(BF16)
            mq = _dot_nt(lhs, rhs)
            m_ab = jnp.where(stril, mq[:LANES, :LANES], 0.0)
            m_ak = jnp.where(stril, mq[:LANES, LANES:], 0.0)
            m_rb = jnp.where(tril, mq[LANES:, :LANES], 0.0)
            m_rk = jnp.where(tril, mq[LANES:, LANES:], 0.0)
            t_inv = _inv_unit_lower(m_ab, r128, c128)
            v_bb = v_b.astype(BF16)
            w_ = _dot(m_ak.astype(BF16), v_bb)
            pq = _dot(t_inv.astype(BF16), jnp.concatenate([at_b, w_], axis=1).astype(BF16))
            p_, q_ = pq[:, :LANES], pq[:, LANES:]
            low = jnp.concatenate([jnp.zeros_like(v_b), v_b], axis=1)
            ry = _dot(jnp.concatenate([m_rb, m_rk], axis=1).astype(BF16),
                      jnp.concatenate([pq, low], axis=0).astype(BF16))
            r2 = rt_b + ry[:, :LANES]
            y0 = ry[:, LANES:]
            e_row = e_tot[ci * CHUNK:ci * CHUNK + 1, cs]
            m_c = jnp.where(diag, e_row, 0.0) + _dot(p_.T.astype(BF16), be_b.astype(BF16))
            n_c = _dot(jnp.concatenate([q_.T, v_b.T], axis=1).astype(BF16),
                       jnp.concatenate([be_b, ke_b], axis=0).astype(BF16))
            s0 = state_ref[p]
            y = _dot_nt(r2.astype(BF16), s0.astype(BF16)) + y0
            state_ref[p] = _dot_x3(s0, m_c) + n_c
            ybuf_ref[rs, cs] = jnp.where(lane_lo, y[:CHUNK], y[CHUNK:])

    y = ybuf_ref[...]
    mean = seg_sum(y) * (1.0 / HEAD_DIM)
    yc = y - mean
    var = seg_sum(yc * yc) * (1.0 / HEAD_DIM)
    yn = yc * lax.rsqrt(var + GN_EPS) * lnw_ref[...] + lnb_ref[...]
    bonus = seg_sum(r * k2 * rk_ref[...]) * v
    y_ref[0] = ((yn + bonus) * g).astype(y_ref.dtype)


def _rwkv(zr, token_mu, w0, w2, a0, a2, g2, k_k, k_a, r_k, ln_w, ln_b, tt=256):
    b, s, _ = zr.shape
    i = jnp.arange(tt)
    same = (i[:, None] // CHUNK) == (i[None, :] // CHUNK)
    tri = (same & (i[None, :] <= i[:, None])).astype(BF16)
    blk = same.astype(BF16)
    bd = _block_ones(LANES, HEAD_DIM)
    vec = lambda a: a.reshape(1, -1)
    wl = jnp.zeros((RWKV_IN - 3 * RWKV_DIM, 3 * RWKV_DIM), F32)
    wl = wl.at[0:64, 0:RWKV_DIM].set(w2).at[64:128, RWKV_DIM:2 * RWKV_DIM].set(a2)
    wl = wl.at[128:256, 2 * RWKV_DIM:3 * RWKV_DIM].set(g2)
    params = [vec(token_mu), vec(w0), vec(a0), wl, vec(k_k), vec(k_a), vec(r_k),
              vec(ln_w), vec(ln_b), tri, blk, bd]
    const = lambda a: pl.BlockSpec(a.shape, lambda bi, si: (0,) * a.ndim)
    return pl.pallas_call(
        functools.partial(_rwkv_kernel, tt=tt),
        grid=(b, s // tt),
        in_specs=[pl.BlockSpec((1, tt, RWKV_IN), lambda bi, si: (bi, si, 0))] + [const(a) for a in params],
        out_specs=pl.BlockSpec((1, tt, RWKV_DIM), lambda bi, si: (bi, si, 0)),
        out_shape=jax.ShapeDtypeStruct((b, s, RWKV_DIM), BF16),
        scratch_shapes=[pltpu.VMEM((1, RWKV_IN), F32),
                        pltpu.VMEM((RWKV_DIM // LANES, LANES, LANES), F32),
                        pltpu.VMEM((tt, RWKV_DIM), F32)],
        compiler_params=_cparams(("parallel", "arbitrary")),
        name="rwkv",
    )(zr, *params)


def _dil_kernel(q_ref, k_ref, v_ref, o_ref, l_ref):
    i = pl.program_id(2)
    bq = ATTN_BLOCK
    q = q_ref[0]
    pi = jnp.maximum(i - 1, 0)
    p0 = pl.multiple_of(pi * bq, bq)
    c0 = pl.multiple_of(i * bq, bq)
    kcat = jnp.concatenate([k_ref[0, pl.ds(p0, bq), :], k_ref[0, pl.ds(c0, bq), :]], axis=0)
    vcat = jnp.concatenate([v_ref[0, pl.ds(p0, bq), :], v_ref[0, pl.ds(c0, bq), :]], axis=0)
    qi = lax.broadcasted_iota(jnp.int32, (bq, 2 * bq), 0)
    kj = lax.broadcasted_iota(jnp.int32, (bq, 2 * bq), 1)
    first = (1 - jnp.minimum(i, 1)) * (2 * bq)
    mask = ((kj < bq) & (kj >= qi + first)) | ((kj >= bq) & ((kj - bq) <= qi))
    lane = lax.broadcasted_iota(jnp.int32, (bq, DIL_GROUP_DIM), 1)
    acc = jnp.zeros((bq, DIL_GROUP_DIM), F32)
    lacc = jnp.zeros((bq, DIL_GROUP_DIM), F32)
    for h in range(DIL_GROUP_DIM // HEAD_DIM):
        hm = (lane >= h * HEAD_DIM) & (lane < (h + 1) * HEAD_DIM)
        qh = jnp.where(hm, q, jnp.zeros_like(q))
        sc = jnp.where(mask, _dot_nt(qh, kcat), NEG_BIG)
        m = jnp.max(sc, axis=-1, keepdims=True)
        p = jnp.exp(sc - m)
        den = jnp.sum(p, axis=-1, keepdims=True)
        o = _dot(p.astype(BF16), vcat) / den
        acc = jnp.where(hm, o, acc)
        lacc = jnp.where(hm, m + jnp.log(den), lacc)
    o_ref[0] = acc
    l_ref[0] = lacc


def _dil_group(dq, dk, dv, g, dilation):
    b, s, _ = dq.shape
    n = s // dilation
    nb = n // ATTN_BLOCK
    ncol = DIL_DIM // DIL_GROUP_DIM
    view = lambda a: a.reshape(b, n, dilation * a.shape[-1])
    qmap = lambda bi, ri, ii: (bi, ii, ri * ncol + g)
    kmap = lambda bi, ri, ii: (bi, 0, ri * ncol + g)
    omap = lambda bi, ri, ii: (bi, ii, ri)
    o, l = pl.pallas_call(
        _dil_kernel,
        grid=(b, dilation, nb),
        in_specs=[pl.BlockSpec((1, ATTN_BLOCK, DIL_GROUP_DIM), qmap),
                  pl.BlockSpec((1, n, DIL_GROUP_DIM), kmap),
                  pl.BlockSpec((1, n, DIL_GROUP_DIM), kmap)],
        out_specs=[pl.BlockSpec((1, ATTN_BLOCK, DIL_GROUP_DIM), omap),
                   pl.BlockSpec((1, ATTN_BLOCK, DIL_GROUP_DIM), omap)],
        out_shape=[jax.ShapeDtypeStruct((b, n, dilation * DIL_GROUP_DIM), F32),
                   jax.ShapeDtypeStruct((b, n, dilation * DIL_GROUP_DIM), F32)],
        compiler_params=_cparams(("parallel", "parallel", "arbitrary")),
        name=f"dil_attn_d{dilation}",
    )(view(dq), view(dk), view(dv))
    return o.reshape(b, s, DIL_GROUP_DIM), l.reshape(b, s, DIL_GROUP_DIM)


def _mem_kv_kernel(m_ref, ln_ref, w_ref, kn_ref, k_ref, v_ref):
    x = m_ref[0]
    h = x * lax.rsqrt(jnp.mean(x * x, axis=-1, keepdims=True) + NORM_EPS) * ln_ref[...]
    kv = _dot(h.astype(BF16), w_ref[...])
    for hd in range(MEM_DIM // MEM_HEAD_DIM):
        sl = slice(hd * MEM_HEAD_DIM, (hd + 1) * MEM_HEAD_DIM)
        kh = kv[:, sl]
        kh = kh * lax.rsqrt(jnp.mean(kh * kh, axis=-1, keepdims=True) + NORM_EPS) * kn_ref[...]
        k_ref[0, :, sl] = kh.astype(BF16)
    v_ref[0] = kv[:, MEM_DIM:].astype(BF16)


def _mem_kv(mem, ln_mem, w_mem_kv, mem_k_norm):
    b, m, d = mem.shape
    return pl.pallas_call(
        _mem_kv_kernel,
        grid=(b,),
        in_specs=[pl.BlockSpec((1, m, d), lambda i: (i, 0, 0)),
                  pl.BlockSpec((1, d), lambda i: (0, 0)),
                  pl.BlockSpec((d, 2 * MEM_DIM), lambda i: (0, 0)),
                  pl.BlockSpec((1, MEM_HEAD_DIM), lambda i: (0, 0))],
        out_specs=[pl.BlockSpec((1, m, MEM_DIM), lambda i: (i, 0, 0)),
                   pl.BlockSpec((1, m, MEM_DIM), lambda i: (i, 0, 0))],
        out_shape=[jax.ShapeDtypeStruct((b, m, MEM_DIM), BF16),
                   jax.ShapeDtypeStruct((b, m, MEM_DIM), BF16)],
        compiler_params=_cparams(("parallel",)),
        name="mem_kv",
    )(mem, ln_mem.reshape(1, d), w_mem_kv.astype(BF16), mem_k_norm.reshape(1, MEM_HEAD_DIM))


def _mem_attn_kernel(q_ref, k_ref, v_ref, o_ref):
    for hd in range(MEM_DIM // MEM_HEAD_DIM):
        sl = slice(hd * MEM_HEAD_DIM, (hd + 1) * MEM_HEAD_DIM)
        sc = _dot_nt(q_ref[0, :, sl], k_ref[0, :, sl]) * (MEM_HEAD_DIM ** -0.5)
        m = jnp.max(sc, axis=-1, keepdims=True)
        p = jnp.exp(sc - m)
        den = jnp.sum(p, axis=-1, keepdims=True)
        o_ref[0, :, sl] = (_dot((p / den).astype(BF16), v_ref[0, :, sl])).astype(o_ref.dtype)


def _mem_attn(mq, mk, mv, bm=512):
    b, s, _ = mq.shape
    m = mk.shape[1]
    return pl.pallas_call(
        _mem_attn_kernel,
        grid=(b, s // bm),
        in_specs=[pl.BlockSpec((1, bm, MEM_DIM), lambda bi, si: (bi, si, 0)),
                  pl.BlockSpec((1, m, MEM_DIM), lambda bi, si: (bi, 0, 0)),
                  pl.BlockSpec((1, m, MEM_DIM), lambda bi, si: (bi, 0, 0))],
        out_specs=pl.BlockSpec((1, bm, MEM_DIM), lambda bi, si: (bi, si, 0)),
        out_shape=jax.ShapeDtypeStruct((b, s, MEM_DIM), BF16),
        compiler_params=_cparams(("parallel", "parallel")),
        name="mem_attn",
    )(mq, mk, mv)


def _mix_kernel(x_ref, yr_ref, o0_ref, o1_ref, o2_ref, l0_ref, l1_ref, l2_ref, ym_ref, gt_ref,
                pr_ref, pd_ref, pm_ref, wo_ref, ln2_ref, wr_ref, br_ref,
                x1_ref, h2_ref, route_ref, cnt_ref):
    d = x_ref.shape[1]
    l0, l1, l2 = l0_ref[...], l1_ref[...], l2_ref[...]
    m = jnp.maximum(jnp.maximum(l0, l1), l2)
    e0, e1, e2 = jnp.exp(l0 - m), jnp.exp(l1 - m), jnp.exp(l2 - m)
    y_dil = (e0 * o0_ref[...] + e1 * o1_ref[...] + e2 * o2_ref[...]) / (e0 + e1 + e2)
    mixed = (gt_ref[:, 0:d].astype(F32) * _dot(yr_ref[...], pr_ref[...])
             + gt_ref[:, d:2 * d].astype(F32) * _dot(y_dil.astype(BF16), pd_ref[...])
             + gt_ref[:, 2 * d:3 * d].astype(F32) * _dot(ym_ref[...], pm_ref[...]))
    x1 = x_ref[...] + _dot(mixed.astype(BF16), wo_ref[...])
    x1_ref[...] = x1
    h2 = x1 * lax.rsqrt(jnp.mean(x1 * x1, axis=-1, keepdims=True) + NORM_EPS) * ln2_ref[...]
    h2_ref[...] = h2
    logits = _dot_x3(h2, wr_ref[...]) + br_ref[...]
    lane = lax.broadcasted_iota(jnp.int32, logits.shape, 1)
    lane_f = lane.astype(F32)
    route = jnp.zeros(logits.shape, F32)
    onehot = jnp.zeros(logits.shape, F32)
    vals = []
    for kq in range(TOP_K):
        mx = jnp.max(logits, axis=-1, keepdims=True)
        idx = jnp.min(jnp.where(logits == mx, lane_f, float(LANES)), axis=-1, keepdims=True)
        hit = lane_f == idx
        vals.append(mx)
        route = jnp.where(lane == kq, idx, route)
        onehot = jnp.where(hit, 1.0, onehot)
        logits = jnp.where(hit, -jnp.inf, logits)
    ex = [jnp.exp(vq - vals[0]) for vq in vals]
    den = ex[0] + ex[1] + ex[2] + ex[3]
    for kq in range(TOP_K):
        route = jnp.where(lane == TOP_K + kq, ex[kq] / den, route)
    route_ref[...] = route

    @pl.when(pl.program_id(0) == 0)
    def _():
        cnt_ref[...] = jnp.zeros_like(cnt_ref)

    cnt_ref[...] += jnp.sum(onehot, axis=0, keepdims=True)


def _mix(x2, y_rwkv, outs, lses, y_mem, gates, p_rwkv, p_dil, p_mem, w_out, ln2, w_router, b_router, bm=256):
    t, d = x2.shape
    wr = jnp.zeros((d, LANES), F32).at[:, :N_EXPERTS].set(w_router)
    br = jnp.full((1, LANES), -jnp.inf, F32).at[0, :N_EXPERTS].set(b_router)
    row = lambda w: pl.BlockSpec((bm, w), lambda i: (i, 0))
    const = lambda a: pl.BlockSpec(a.shape, lambda i: (0,) * a.ndim)
    weights = [p_rwkv.astype(BF16), p_dil.astype(BF16), p_mem.astype(BF16), w_out.astype(BF16),
               ln2.reshape(1, d), wr, br]
    return pl.pallas_call(
        _mix_kernel,
        grid=(t // bm,),
        in_specs=[row(d), row(RWKV_DIM)] + [row(DIL_GROUP_DIM)] * 6 + [row(MEM_DIM), row(3 * d)]
                 + [const(a) for a in weights],
        out_specs=[row(d), row(d), row(LANES), pl.BlockSpec((1, LANES), lambda i: (0, 0))],
        out_shape=[jax.ShapeDtypeStruct((t, d), F32), jax.ShapeDtypeStruct((t, d), F32),
                   jax.ShapeDtypeStruct((t, LANES), F32), jax.ShapeDtypeStruct((1, LANES), F32)],
        compiler_params=_cparams(("arbitrary",)),
        name="mix",
    )(x2, y_rwkv, *outs, *lses, y_mem, gates, *weights)


def _route_kernel(route_ref, pstart_ref, tri_ref, dest_ref, carry_ref):
    @pl.when(pl.program_id(0) == 0)
    def _():
        carry_ref[...] = jnp.zeros_like(carry_ref)

    route = route_ref[...]
    lane = lax.broadcasted_iota(jnp.int32, route.shape, 1)
    lane_f = lane.astype(F32)
    hits = [lane_f == route[:, kq:kq + 1] for kq in range(TOP_K)]
    onehot = jnp.zeros(route.shape, F32)
    for hq in hits:
        onehot = jnp.where(hq, 1.0, onehot)
    rank = _dot(tri_ref[...], onehot.astype(BF16)) + carry_ref[...]
    slot = pstart_ref[...] + rank
    dest = jnp.zeros(route.shape, jnp.int32)
    for kq in range(TOP_K):
        dk = jnp.sum(jnp.where(hits[kq], slot, 0.0), axis=-1, keepdims=True)
        dest = jnp.where(lane == kq, dk.astype(jnp.int32), dest)
    dest_ref[...] = dest
    carry_ref[...] += jnp.sum(onehot, axis=0, keepdims=True)


def _route(route, pstart, bm=256):
    t = route.shape[0]
    i = jnp.arange(bm)
    tri = (i[None, :] < i[:, None]).astype(BF16)
    return pl.pallas_call(
        _route_kernel,
        grid=(t // bm,),
        in_specs=[pl.BlockSpec((bm, LANES), lambda i: (i, 0)),
                  pl.BlockSpec((1, LANES), lambda i: (0, 0)),
                  pl.BlockSpec((bm, bm), lambda i: (0, 0))],
        out_specs=pl.BlockSpec((bm, LANES), lambda i: (i, 0)),
        out_shape=jax.ShapeDtypeStruct((t, LANES), jnp.int32),
        scratch_shapes=[pltpu.VMEM((1, LANES), F32)],
        compiler_params=_cparams(("arbitrary",)),
        name="route",
    )(route, pstart, tri)


def _row_copy(src_ref, src_row, dst_ref, dst_row, sem):
    return pltpu.make_async_copy(src_ref.at[pl.ds(src_row, 1)], dst_ref.at[pl.ds(dst_row, 1)], sem)


def _dispatch_kernel(dest_ref, h_ref, xs_in_ref, xs_ref, sem, *, bm):
    del xs_in_ref
    base = pl.program_id(0) * bm

    def issue(j, carry):
        for kq in range(TOP_K):
            _row_copy(h_ref, base + j, xs_ref, dest_ref[j * TOP_K + kq], sem).start()
        return carry

    lax.fori_loop(0, bm, issue, 0)

    def drain(j, carry):
        for kq in range(TOP_K):
            _row_copy(h_ref, base + j, xs_ref, dest_ref[j * TOP_K + kq], sem).wait()
        return carry

    lax.fori_loop(0, bm, drain, 0)


def _dispatch(dest_flat, h2, n_slots, bm=256):
    t, d = h2.shape
    xs0 = jnp.zeros((n_slots, d), h2.dtype)
    return pl.pallas_call(
        functools.partial(_dispatch_kernel, bm=bm),
        grid=(t // bm,),
        in_specs=[pl.BlockSpec((bm * TOP_K,), lambda i: (i,), memory_space=pltpu.SMEM),
                  pl.BlockSpec(memory_space=pl.ANY),
                  pl.BlockSpec(memory_space=pl.ANY)],
        out_specs=pl.BlockSpec(memory_space=pl.ANY),
        out_shape=jax.ShapeDtypeStruct((n_slots, d), h2.dtype),
        scratch_shapes=[pltpu.SemaphoreType.DMA],
        input_output_aliases={2: 0},
        compiler_params=_cparams(("arbitrary",)),
        name="dispatch",
    )(dest_flat, h2, xs0)


FF_CHUNK = 256


def _expert_kernel(be_ref, nu_ref, xs_ref, w1_ref, b1_ref, w2_ref, b2_ref, sel_ref, ys_ref,
                   w1p_ref, w2b_ref):
    i = pl.program_id(0)
    e = be_ref[i]
    fresh = jnp.logical_or(i == 0, be_ref[jnp.maximum(i - 1, 0)] != e)
    used = i < nu_ref[0]
    d_ff2 = w1_ref.shape[2]
    half = FF_CHUNK // 2

    @pl.when(jnp.logical_and(fresh, used))
    def _():
        for c in range(d_ff2 // FF_CHUNK):
            sl = slice(c * FF_CHUNK, (c + 1) * FF_CHUNK)
            w1p_ref[:, sl] = _dot(w1_ref[0, :, sl].astype(BF16), sel_ref[...]).astype(BF16)
        w2b_ref[...] = w2_ref[0].astype(BF16)

    @pl.when(used)
    def _():
        x = xs_ref[...].astype(BF16)
        y = jnp.zeros(ys_ref.shape, F32) + b2_ref[0]
        for c in range(d_ff2 // (2 * FF_CHUNK)):
            sl = slice(2 * c * FF_CHUNK, 2 * (c + 1) * FF_CHUNK)
            hb = _dot(x, w1p_ref[:, sl]) + b1_ref[0, :, sl]
            acts = []
            for j in range(2):
                x_glu = jnp.minimum(hb[:, j * FF_CHUNK:j * FF_CHUNK + half], SWIGLU_LIMIT)
                x_lin = jnp.clip(hb[:, j * FF_CHUNK + half:(j + 1) * FF_CHUNK], -SWIGLU_LIMIT, SWIGLU_LIMIT)
                acts.append(x_glu * _sigmoid(SWIGLU_ALPHA * x_glu) * (x_lin + 1.0))
            act = jnp.concatenate(acts, axis=1).astype(BF16)
            y = y + _dot(act, w2b_ref[c * FF_CHUNK:(c + 1) * FF_CHUNK, :])
        ys_ref[...] = y

    @pl.when(jnp.logical_not(used))
    def _():
        ys_ref[...] = jnp.zeros_like(ys_ref)


def _chunk_deinterleave(a):
    lead = a.shape[:-1]
    a = a.reshape(lead + (a.shape[-1] // FF_CHUNK, FF_CHUNK // 2, 2))
    return jnp.swapaxes(a, -1, -2).reshape(lead + (-1,))


def _experts(block_e, n_used, xs, w1, b1, w2, b2):
    n_slots, d = xs.shape
    n_e, _, d_ff2 = w1.shape
    n_blocks = n_slots // MOE_BLOCK
    i = jnp.arange(FF_CHUNK)
    src = jnp.where(i < FF_CHUNK // 2, 2 * i, 2 * (i - FF_CHUNK // 2) + 1)
    sel = (jnp.arange(FF_CHUNK)[:, None] == src[None, :]).astype(BF16)
    b1p = _chunk_deinterleave(b1).reshape(n_e, 1, d_ff2)
    grid_spec = pltpu.PrefetchScalarGridSpec(
        num_scalar_prefetch=2,
        grid=(n_blocks,),
        in_specs=[pl.BlockSpec((MOE_BLOCK, d), lambda i, be, nu: (i, 0)),
                  pl.BlockSpec((1, d, d_ff2), lambda i, be, nu: (be[i], 0, 0)),
                  pl.BlockSpec((1, 1, d_ff2), lambda i, be, nu: (be[i], 0, 0)),
                  pl.BlockSpec((1, d_ff2 // 2, d), lambda i, be, nu: (be[i], 0, 0)),
                  pl.BlockSpec((1, 1, d), lambda i, be, nu: (be[i], 0, 0)),
                  pl.BlockSpec((FF_CHUNK, FF_CHUNK), lambda i, be, nu: (0, 0))],
        out_specs=pl.BlockSpec((MOE_BLOCK, d), lambda i, be, nu: (i, 0)),
        scratch_shapes=[pltpu.VMEM((d, d_ff2), BF16), pltpu.VMEM((d_ff2 // 2, d), BF16)],
    )
    return pl.pallas_call(
        _expert_kernel,
        grid_spec=grid_spec,
        out_shape=jax.ShapeDtypeStruct((n_slots, d), F32),
        compiler_params=_cparams(("arbitrary",)),
        name="experts",
    )(block_e, n_used, xs, w1, b1p, w2, b2.reshape(n_e, 1, d), sel)


def _combine_kernel(dest_ref, x1_ref, route_ref, ys_ref, o_ref, buf_ref, sem, *, bm):
    def issue(j, carry):
        for kq in range(TOP_K):
            _row_copy(ys_ref, dest_ref[j * TOP_K + kq], buf_ref.at[kq], j, sem).start()
        return carry

    lax.fori_loop(0, bm, issue, 0)

    def drain(j, carry):
        for kq in range(TOP_K):
            _row_copy(ys_ref, dest_ref[j * TOP_K + kq], buf_ref.at[kq], j, sem).wait()
        return carry

    lax.fori_loop(0, bm, drain, 0)
    acc = x1_ref[...]
    for kq in range(TOP_K):
        acc = acc + route_ref[:, TOP_K + kq:TOP_K + kq + 1] * buf_ref[kq]
    o_ref[...] = acc


def _combine(dest_flat, x1, route, ys, bm=256):
    t, d = x1.shape
    return pl.pallas_call(
        functools.partial(_combine_kernel, bm=bm),
        grid=(t // bm,),
        in_specs=[pl.BlockSpec((bm * TOP_K,), lambda i: (i,), memory_space=pltpu.SMEM),
                  pl.BlockSpec((bm, d), lambda i: (i, 0)),
                  pl.BlockSpec((bm, LANES), lambda i: (i, 0)),
                  pl.BlockSpec(memory_space=pl.ANY)],
        out_specs=pl.BlockSpec((bm, d), lambda i: (i, 0)),
        out_shape=jax.ShapeDtypeStruct((t, d), F32),
        scratch_shapes=[pltpu.VMEM((TOP_K, bm, d), F32), pltpu.SemaphoreType.DMA],
        compiler_params=_cparams(("arbitrary",)),
        name="combine",
    )(dest_flat, x1, route, ys)


def _layer(x, mem, ln1, w_in, token_mu, rwkv_w0, rwkv_w2, rwkv_a0, rwkv_a2, rwkv_g2,
           rwkv_k_k, rwkv_k_a, rwkv_r_k, rwkv_ln_w, rwkv_ln_b, dil_q_norm, dil_k_norm,
           ln_mem, w_mem_kv, mem_q_norm, mem_k_norm, p_rwkv, p_dil, p_mem, w_out,
           ln2, w_router, b_router, w1, b1, w2, b2):
    b, s, d = x.shape
    t = b * s
    x2 = x.reshape(t, d)
    zr, dq, dk, dv, mq, gates = _in_proj(x2, ln1, w_in, dil_q_norm, dil_k_norm, mem_q_norm)
    y_rwkv = _rwkv(zr.reshape(b, s, RWKV_IN), token_mu, rwkv_w0, rwkv_w2, rwkv_a0, rwkv_a2, rwkv_g2,
                   rwkv_k_k, rwkv_k_a, rwkv_r_k.reshape(-1), rwkv_ln_w, rwkv_ln_b)
    outs, lses = [], []
    for g, (_, dilation) in enumerate(DIL_PATTERNS):
        o, l = _dil_group(dq.reshape(b, s, DIL_DIM), dk.reshape(b, s, DIL_DIM), dv.reshape(b, s, DIL_DIM),
                          g, dilation)
        outs.append(o.reshape(t, DIL_GROUP_DIM))
        lses.append(l.reshape(t, DIL_GROUP_DIM))
    mk, mv = _mem_kv(mem, ln_mem, w_mem_kv, mem_k_norm)
    y_mem = _mem_attn(mq.reshape(b, s, MEM_DIM), mk, mv)
    x1, h2, route, counts = _mix(x2, y_rwkv.reshape(t, RWKV_DIM), outs, lses, y_mem.reshape(t, MEM_DIM), gates,
                                 p_rwkv, p_dil, p_mem, w_out, ln2, w_router, b_router)

    counts = counts[0, :N_EXPERTS].astype(jnp.int32)
    nblk = (counts + MOE_BLOCK - 1) // MOE_BLOCK
    bend = jnp.cumsum(nblk)
    pstart = ((bend - nblk) * MOE_BLOCK).astype(F32)
    pstart = jnp.zeros((1, LANES), F32).at[0, :N_EXPERTS].set(pstart)
    n_blocks = (t * TOP_K) // MOE_BLOCK + N_EXPERTS
    block_e = jnp.minimum(jnp.searchsorted(bend, jnp.arange(n_blocks, dtype=jnp.int32), side='right'),
                          N_EXPERTS - 1).astype(jnp.int32)
    n_used = bend[-1:].astype(jnp.int32)

    dest = _route(route, pstart)[:, :TOP_K].reshape(-1)
    xs = _dispatch(dest, h2, n_blocks * MOE_BLOCK)
    ys = _experts(block_e, n_used, xs, w1, b1, w2, b2)
    out = _combine(dest, x1, route, ys)
    return out.reshape(b, s, d)


def kernel(x, mem, ln1, w_in, token_mu, rwkv_w0, rwkv_w2, rwkv_a0, rwkv_a2, rwkv_g2, rwkv_k_k, rwkv_k_a, rwkv_r_k, rwkv_ln_w, rwkv_ln_b, dil_q_norm, dil_k_norm, ln_mem, w_mem_kv, mem_q_norm, mem_k_norm, p_rwkv, p_dil, p_mem, w_out, ln2, w_router, b_router, w1, b1, w2, b2):
    params = (ln1, w_in, token_mu, rwkv_w0, rwkv_w2, rwkv_a0, rwkv_a2, rwkv_g2, rwkv_k_k, rwkv_k_a,
              rwkv_r_k, rwkv_ln_w, rwkv_ln_b, dil_q_norm, dil_k_norm, ln_mem, w_mem_kv, mem_q_norm,
              mem_k_norm, p_rwkv, p_dil, p_mem, w_out, ln2, w_router, b_router, w1, b1, w2, b2)
    for l in range(ln1.shape[0]):
        x = _layer(x, mem, *[p[l] for p in params])
    return x
```

```python
import functools

import jax
import jax.numpy as jnp
from jax import lax
from jax.experimental import pallas as pl
from jax.experimental.pallas import tpu as pltpu

F32 = jnp.float32
BF16 = jnp.bfloat16

NORM_EPS = 1e-5
HEAD_DIM = 64
RWKV_DIM = 512
RWKV_IN = 1792
GN_EPS = HEAD_DIM * 1e-5
DIL_DIM = 768
DIL_GROUP_DIM = 256
DIL_PATTERNS = ((128, 1), (512, 4), (2048, 16))
ATTN_BLOCK = 128
MEM_DIM = 512
MEM_HEAD_DIM = 128
N_EXPERTS = 32
TOP_K = 4
MOE_BLOCK = 128
SWIGLU_ALPHA = 1.702
SWIGLU_LIMIT = 7.0
LANES = 128
CHUNK = 64
NEG_BIG = -1e30
VMEM_LIMIT = 48 * 1024 * 1024


def _dot(a, b):
    return jnp.dot(a, b, preferred_element_type=F32)


def _dot_nt(a, b):
    return lax.dot_general(a, b, (((1,), (1,)), ((), ())), preferred_element_type=F32)


def _split2(x):
    hi = x.astype(BF16)
    lo = (x - hi.astype(F32)).astype(BF16)
    return hi, lo


def _split3(x):
    hi = x.astype(BF16)
    r = x - hi.astype(F32)
    lo = r.astype(BF16)
    lo2 = (r - lo.astype(F32)).astype(BF16)
    return hi, lo, lo2


def _dot_lhs3(x, w_bf16):
    hi, lo, lo2 = _split3(x)
    return _dot(hi, w_bf16) + _dot(lo, w_bf16) + _dot(lo2, w_bf16)


def _dot_rhs3(w_bf16, x):
    hi, lo, lo2 = _split3(x)
    return _dot(w_bf16, hi) + _dot(w_bf16, lo) + _dot(w_bf16, lo2)


def _dot_x3(a, b):
    ah, al = _split2(a)
    bh, bl = _split2(b)
    return _dot(ah, bh) + _dot(al, bh) + _dot(ah, bl)


def _sigmoid(x):
    return 1.0 / (1.0 + jnp.exp(-x))


def _block_ones(n, seg, dtype=BF16):
    i = jnp.arange(n)
    return (i[:, None] // seg == i[None, :] // seg).astype(dtype)


def _cparams(sem, vmem=VMEM_LIMIT):
    return pltpu.CompilerParams(dimension_semantics=sem, vmem_limit_bytes=vmem)


IN_CHUNK = 256
N_ZR = RWKV_IN // IN_CHUNK
N_DIL = 3 * DIL_DIM // IN_CHUNK
N_MQ = MEM_DIM // IN_CHUNK
N_GATE = 3 * 1024 // IN_CHUNK


def _in_proj_kernel(x_ref, ln_ref, w_ref, bd64_ref, bd128_ref, qg_ref, kg_ref, mg_ref, p4_ref, p16_ref,
                    zr_ref, *rest, bm):
    dil_refs, (mq_ref, gt_ref) = rest[:N_DIL], rest[N_DIL:]
    perm_refs = (None, p4_ref, p16_ref)
    x = x_ref[...]
    h = x * lax.rsqrt(jnp.mean(x * x, axis=-1, keepdims=True) + NORM_EPS) * ln_ref[...]
    hb = h.astype(BF16)

    def seg_rms(z, bd_ref, seg, gain):
        hi, lo = _split2(z * z)
        ss = _dot(hi, bd_ref[...]) + _dot(lo, bd_ref[...])
        return z * lax.rsqrt(ss * (1.0 / seg) + NORM_EPS) * gain

    for c in range(N_ZR + N_DIL + N_MQ + N_GATE):
        z = _dot(hb, w_ref[:, c * IN_CHUNK:(c + 1) * IN_CHUNK])
        if c < N_ZR:
            zr_ref[:, c * IN_CHUNK:(c + 1) * IN_CHUNK] = z
            continue
        d = c - N_ZR
        if d < N_DIL:
            which, g = divmod(d, 3)
            if which == 0:
                z = seg_rms(z, bd64_ref, HEAD_DIM, qg_ref[...])
            elif which == 1:
                z = seg_rms(z, bd64_ref, HEAD_DIM, kg_ref[...])
            zb = z.astype(BF16)
            o_ref = dil_refs[g * 3 + which]
            dilation = DIL_PATTERNS[g][1]
            if dilation == 1:
                o_ref[...] = zb
            else:
                zp = _dot(perm_refs[g][...], zb).astype(BF16)
                rows = bm // dilation
                for r in range(dilation):
                    o_ref[:, r * IN_CHUNK:(r + 1) * IN_CHUNK] = zp[r * rows:(r + 1) * rows, :]
            continue
        d -= N_DIL
        if d < N_MQ:
            mq_ref[:, d * IN_CHUNK:(d + 1) * IN_CHUNK] = seg_rms(
                z, bd128_ref, MEM_HEAD_DIM, mg_ref[...]).astype(BF16)
            continue
        d -= N_MQ
        gt_ref[:, d * IN_CHUNK:(d + 1) * IN_CHUNK] = _sigmoid(z).astype(BF16)


def _class_perm(bm, dilation):
    i = jnp.arange(bm)
    src = (i % (bm // dilation)) * dilation + i // (bm // dilation)
    return (src[:, None] == i[None, :]).astype(BF16)


def _in_proj(x2, ln1, w_in, dil_q_norm, dil_k_norm, mem_q_norm, bm=256):
    t, d = x2.shape
    n_cols = w_in.shape[1]
    wb = w_in.astype(BF16)
    bd64 = _block_ones(IN_CHUNK, HEAD_DIM)
    bd128 = _block_ones(IN_CHUNK, MEM_HEAD_DIM)
    qg = (jnp.tile(dil_q_norm, IN_CHUNK // HEAD_DIM) * (HEAD_DIM ** -0.5)).reshape(1, IN_CHUNK)
    kg = jnp.tile(dil_k_norm, IN_CHUNK // HEAD_DIM).reshape(1, IN_CHUNK)
    mg = jnp.tile(mem_q_norm, IN_CHUNK // MEM_HEAD_DIM).reshape(1, IN_CHUNK)
    p4 = _class_perm(bm, DIL_PATTERNS[1][1])
    p16 = _class_perm(bm, DIL_PATTERNS[2][1])
    row = lambda w: pl.BlockSpec((bm, w), lambda i: (i, 0))
    const = lambda a: pl.BlockSpec(a.shape, lambda i: (0,) * a.ndim)
    dil_specs, dil_shapes = [], []
    for _, dilation in DIL_PATTERNS:
        for _ in range(3):
            dil_specs.append(pl.BlockSpec((bm // dilation, dilation * DIL_GROUP_DIM), lambda i: (i, 0)))
            dil_shapes.append(jax.ShapeDtypeStruct((t // dilation, dilation * DIL_GROUP_DIM), BF16))
    outs = pl.pallas_call(
        functools.partial(_in_proj_kernel, bm=bm),
        grid=(t // bm,),
        in_specs=[row(d), pl.BlockSpec((1, d), lambda i: (0, 0)),
                  pl.BlockSpec((d, n_cols), lambda i: (0, 0), pipeline_mode=pl.Buffered(1)),
                  const(bd64), const(bd128), const(qg), const(kg), const(mg), const(p4), const(p16)],
        out_specs=[row(RWKV_IN)] + dil_specs + [row(MEM_DIM), row(3 * d)],
        out_shape=[jax.ShapeDtypeStruct((t, RWKV_IN), F32)] + dil_shapes
                  + [jax.ShapeDtypeStruct((t, MEM_DIM), BF16), jax.ShapeDtypeStruct((t, 3 * d), BF16)],
        compiler_params=_cparams(("parallel",)),
        name="in_proj",
    )(x2, ln1.reshape(1, d), wb, bd64, bd128, qg, kg, mg, p4, p16)
    return outs[0], outs[1:1 + N_DIL], outs[1 + N_DIL], outs[2 + N_DIL]


def _inv_unit_lower(a, r, c):
    n = a.shape[0]
    eye = (r == c).astype(F32)
    same16 = (r // 16) == (c // 16)
    same32 = (r // 32) == (c // 32)
    d1 = jnp.where(same16, a, 0.0)
    d1b = d1.astype(BF16)
    x = eye + d1
    d2 = _dot(d1b, d1b)
    d2b = d2.astype(BF16)
    x = x + _dot(d2b, x.astype(BF16))
    d4 = _dot(d2b, d2b)
    d4b = d4.astype(BF16)
    x = x + _dot(d4b, x.astype(BF16))
    d8 = _dot(d4b, d4b)
    x = x + _dot(d8.astype(BF16), x.astype(BF16))
    off = jnp.where(same32 & jnp.logical_not(same16), a, 0.0).astype(BF16)
    xb = x.astype(BF16)
    x = x + _dot(xb, _dot(off, xb).astype(BF16))
    off = jnp.where(jnp.logical_not(same32), a, 0.0).astype(BF16)
    xb = x.astype(BF16)
    x = x + _dot(xb, _dot(off, xb).astype(BF16))
    del n
    return x


def _rwkv_kernel(z_ref, mu_ref, w0_ref, a0_ref, wl_ref, kk_ref, ka_ref, rk_ref,
                 lnw_ref, lnb_ref, tri_ref, blk_ref, bd_ref, y_ref,
                 carry_ref, state_ref, ybuf_ref, *, tt):
    s_idx = pl.program_id(1)

    @pl.when(s_idx == 0)
    def _():
        carry_ref[...] = jnp.zeros_like(carry_ref)
        state_ref[...] = jnp.zeros_like(state_ref)

    z = z_ref[0]
    rows = lax.broadcasted_iota(jnp.int32, z.shape, 0)
    prev = jnp.where(rows == 0, carry_ref[...], pltpu.roll(z, 1, axis=0))
    carry_ref[...] = z[tt - 1:tt, :]
    zl = z + (prev - z) * mu_ref[...]
    r = zl[:, 0:RWKV_DIM]
    k = zl[:, RWKV_DIM:2 * RWKV_DIM]
    v = zl[:, 2 * RWKV_DIM:3 * RWKV_DIM]
    zlo = zl[:, 3 * RWKV_DIM:RWKV_IN]
    lcol = lax.broadcasted_iota(jnp.int32, zlo.shape, 1)
    feat = jnp.where(lcol < 64, jnp.tanh(zlo), jnp.where(lcol < 128, zlo, _sigmoid(zlo)))
    lora = _dot_x3(feat, wl_ref[...])
    nu = -(w0_ref[...] + lora[:, 0:RWKV_DIM])
    softplus = jnp.maximum(nu, 0.0) + jnp.log(1.0 + jnp.exp(-jnp.abs(nu)))
    lw = -jnp.exp(-softplus - 0.5)
    a = _sigmoid(a0_ref[...] + lora[:, RWKV_DIM:2 * RWKV_DIM])
    g = lora[:, 2 * RWKV_DIM:3 * RWKV_DIM]

    bd = bd_ref[...]

    def seg_sum(x):
        return jnp.concatenate(
            [_dot_lhs3(x[:, p * LANES:(p + 1) * LANES], bd) for p in range(RWKV_DIM // LANES)], axis=1)

    kk = k * kk_ref[...]
    kk = kk / jnp.maximum(jnp.sqrt(seg_sum(kk * kk)), 1e-12)
    k2 = k * (1.0 + (a - 1.0) * ka_ref[...])
    a_s = -kk
    b_s = kk * a

    cum = _dot_rhs3(tri_ref[...], lw)
    tot = _dot_rhs3(blk_ref[...], lw)
    e_in = jnp.exp(cum)
    e_ex = jnp.exp(cum - lw)
    e_neg = jnp.exp(-cum)
    e_end = jnp.exp(tot - cum)
    e_tot = jnp.exp(tot)
    at = a_s * e_ex
    rt = r * e_in
    bt = b_s * e_neg
    kt = k2 * e_neg
    be = b_s * e_end
    ke = k2 * e_end

    r128 = lax.broadcasted_iota(jnp.int32, (LANES, LANES), 0)
    c128 = lax.broadcasted_iota(jnp.int32, (LANES, LANES), 1)
    lane_lo = lax.broadcasted_iota(jnp.int32, (CHUNK, LANES), 1) < HEAD_DIM
    stril = c128 < r128
    tril = c128 <= r128
    diag = c128 == r128

    def bdiag(xp):
        return jnp.concatenate([jnp.where(lane_lo, xp, 0.0), jnp.where(lane_lo, 0.0, xp)], axis=0)

    for ci in range(tt // CHUNK):
        rs = slice(ci * CHUNK, (ci + 1) * CHUNK)
        for p in range(RWKV_DIM // LANES):
            cs = slice(p * LANES, (p + 1) * LANES)
            at_b, rt_b = bdiag(at[rs, cs]), bdiag(rt[rs, cs])
            bt_b, kt_b = bdiag(bt[rs, cs]), bdiag(kt[rs, cs])
            v_b = bdiag(v[rs, cs])
            be_b, ke_b = bdiag(be[rs, cs]), bdiag(ke[rs, cs])
            lhs = jnp.concatenate([at_b, rt_b], axis=0).astype(BF16)
            rhs = jnp.concatenate([bt_b, kt_b], axis=0).astype(BF16)
            mq = _dot_nt(lhs, rhs)
            m_ab = jnp.where(stril, mq[:LANES, :LANES], 0.0)
            m_ak = jnp.where(stril, mq[:LANES, LANES:], 0.0)
            m_rb = jnp.where(tril, mq[LANES:, :LANES], 0.0)
            m_rk = jnp.where(tril, mq[LANES:, LANES:], 0.0)
            t_inv = _inv_unit_lower(m_ab, r128, c128)
            v_bb = v_b.astype(BF16)
            w_ = _dot(m_ak.astype(BF16), v_bb)
            pq = _dot(t_inv.astype(BF16), jnp.concatenate([at_b, w_], axis=1).astype(BF16))
            p_, q_ = pq[:, :LANES], pq[:, LANES:]
            low = jnp.concatenate([jnp.zeros_like(v_b), v_b], axis=1)
            ry = _dot(jnp.concatenate([m_rb, m_rk], axis=1).astype(BF16),
                      jnp.concatenate([pq, low], axis=0).astype(BF16))
            r2 = rt_b + ry[:, :LANES]
            y0 = ry[:, LANES:]
            e_row = e_tot[ci * CHUNK:ci * CHUNK + 1, cs]
            m_c = jnp.where(diag, e_row, 0.0) + _dot(p_.T.astype(BF16), be_b.astype(BF16))
            n_c = _dot(jnp.concatenate([q_.T, v_b.T], axis=1).astype(BF16),
                       jnp.concatenate([be_b, ke_b], axis=0).astype(BF16))
            s0 = state_ref[p]
            y = _dot_nt(r2.astype(BF16), s0.astype(BF16)) + y0
            state_ref[p] = _dot_x3(s0, m_c) + n_c
            ybuf_ref[rs, cs] = jnp.where(lane_lo, y[:CHUNK], y[CHUNK:])

    y = ybuf_ref[...]
    mean = seg_sum(y) * (1.0 / HEAD_DIM)
    yc = y - mean
    var = seg_sum(yc * yc) * (1.0 / HEAD_DIM)
    yn = yc * lax.rsqrt(var + GN_EPS) * lnw_ref[...] + lnb_ref[...]
    bonus = seg_sum(r * k2 * rk_ref[...]) * v
    y_ref[0] = ((yn + bonus) * g).astype(y_ref.dtype)


def _rwkv(zr, token_mu, w0, w2, a0, a2, g2, k_k, k_a, r_k, ln_w, ln_b, tt=256):
    b, s, _ = zr.shape
    i = jnp.arange(tt)
    same = (i[:, None] // CHUNK) == (i[None, :] // CHUNK)
    tri = (same & (i[None, :] <= i[:, None])).astype(BF16)
    blk = same.astype(BF16)
    bd = _block_ones(LANES, HEAD_DIM)
    vec = lambda a: a.reshape(1, -1)
    wl = jnp.zeros((RWKV_IN - 3 * RWKV_DIM, 3 * RWKV_DIM), F32)
    wl = wl.at[0:64, 0:RWKV_DIM].set(w2).at[64:128, RWKV_DIM:2 * RWKV_DIM].set(a2)
    wl = wl.at[128:256, 2 * RWKV_DIM:3 * RWKV_DIM].set(g2)
    params = [vec(token_mu), vec(w0), vec(a0), wl, vec(k_k), vec(k_a), vec(r_k),
              vec(ln_w), vec(ln_b), tri, blk, bd]
    const = lambda a: pl.BlockSpec(a.shape, lambda bi, si: (0,) * a.ndim)
    return pl.pallas_call(
        functools.partial(_rwkv_kernel, tt=tt),
        grid=(b, s // tt),
        in_specs=[pl.BlockSpec((1, tt, RWKV_IN), lambda bi, si: (bi, si, 0))] + [const(a) for a in params],
        out_specs=pl.BlockSpec((1, tt, RWKV_DIM), lambda bi, si: (bi, si, 0)),
        out_shape=jax.ShapeDtypeStruct((b, s, RWKV_DIM), BF16),
        scratch_shapes=[pltpu.VMEM((1, RWKV_IN), F32),
                        pltpu.VMEM((RWKV_DIM // LANES, LANES, LANES), F32),
                        pltpu.VMEM((tt, RWKV_DIM), F32)],
        compiler_params=_cparams(("parallel", "arbitrary")),
        name="rwkv",
    )(zr, *params)


def _dil_kernel(q_ref, k_ref, v_ref, o_ref, l_ref, *, dilation):
    r = pl.program_id(1)
    i = pl.program_id(2)
    bq = ATTN_BLOCK
    q = q_ref[0]
    pi = jnp.maximum(i - 1, 0)
    p0 = pl.multiple_of(pi * bq, bq)
    c0 = pl.multiple_of(i * bq, bq)
    kcat = jnp.concatenate([k_ref[0, pl.ds(p0, bq), :], k_ref[0, pl.ds(c0, bq), :]], axis=0)
    vcat = jnp.concatenate([v_ref[0, pl.ds(p0, bq), :], v_ref[0, pl.ds(c0, bq), :]], axis=0)
    qi = lax.broadcasted_iota(jnp.int32, (bq, 2 * bq), 0)
    kj = lax.broadcasted_iota(jnp.int32, (bq, 2 * bq), 1)
    first = (1 - jnp.minimum(i, 1)) * (2 * bq)
    mask = ((kj < bq) & (kj >= qi + first)) | ((kj >= bq) & ((kj - bq) <= qi))
    lane = lax.broadcasted_iota(jnp.int32, (bq, DIL_GROUP_DIM), 1)
    acc = jnp.zeros((bq, DIL_GROUP_DIM), F32)
    lacc = jnp.zeros((bq, DIL_GROUP_DIM), F32)
    for h in range(DIL_GROUP_DIM // HEAD_DIM):
        hm = (lane >= h * HEAD_DIM) & (lane < (h + 1) * HEAD_DIM)
        qh = jnp.where(hm, q, jnp.zeros_like(q))
        sc = jnp.where(mask, _dot_nt(qh, kcat), NEG_BIG)
        m = jnp.max(sc, axis=-1, keepdims=True)
        p = jnp.exp(sc - m)
        den = jnp.sum(p, axis=-1, keepdims=True)
        o = _dot(p.astype(BF16), vcat) / den
        acc = jnp.where(hm, o, acc)
        lacc = jnp.where(hm, m + jnp.log(den), lacc)
    rows = pl.ds(r + dilation * bq * i, bq, stride=dilation) if dilation > 1 else pl.ds(c0, bq)
    for half in range(DIL_GROUP_DIM // LANES):
        o_ref[0, half, rows, :] = acc[:, half * LANES:(half + 1) * LANES]
        l_ref[0, half, rows, :] = lacc[:, half * LANES:(half + 1) * LANES]


def _dil_group(q, k, v, b, dilation):
    n = q.shape[0] // b
    s = n * dilation
    nb = n // ATTN_BLOCK
    halves = DIL_GROUP_DIM // LANES
    view = lambda a: a.reshape(b, n, dilation * DIL_GROUP_DIM)
    qmap = lambda bi, ri, ii: (bi, ii, ri)
    kmap = lambda bi, ri, ii: (bi, 0, ri)
    omap = lambda bi, ri, ii: (bi, 0, 0, 0)
    return pl.pallas_call(
        functools.partial(_dil_kernel, dilation=dilation),
        grid=(b, dilation, nb),
        in_specs=[pl.BlockSpec((1, ATTN_BLOCK, DIL_GROUP_DIM), qmap),
                  pl.BlockSpec((1, n, DIL_GROUP_DIM), kmap),
                  pl.BlockSpec((1, n, DIL_GROUP_DIM), kmap)],
        out_specs=[pl.BlockSpec((1, halves, s, LANES), omap),
                   pl.BlockSpec((1, halves, s, LANES), omap)],
        out_shape=[jax.ShapeDtypeStruct((b, halves, s, LANES), F32),
                   jax.ShapeDtypeStruct((b, halves, s, LANES), F32)],
        compiler_params=_cparams(("parallel", "arbitrary", "arbitrary")),
        name=f"dil_attn_d{dilation}",
    )(view(q), view(k), view(v))


def _mem_kv_kernel(m_ref, ln_ref, w_ref, kn_ref, k_ref, v_ref):
    x = m_ref[0]
    h = x * lax.rsqrt(jnp.mean(x * x, axis=-1, keepdims=True) + NORM_EPS) * ln_ref[...]
    kv = _dot(h.astype(BF16), w_ref[...])
    for hd in range(MEM_DIM // MEM_HEAD_DIM):
        sl = slice(hd * MEM_HEAD_DIM, (hd + 1) * MEM_HEAD_DIM)
        kh = kv[:, sl]
        kh = kh * lax.rsqrt(jnp.mean(kh * kh, axis=-1, keepdims=True) + NORM_EPS) * kn_ref[...]
        k_ref[0, :, sl] = kh.astype(BF16)
    v_ref[0] = kv[:, MEM_DIM:].astype(BF16)


def _mem_kv(mem, ln_mem, w_mem_kv, mem_k_norm):
    b, m, d = mem.shape
    return pl.pallas_call(
        _mem_kv_kernel,
        grid=(b,),
        in_specs=[pl.BlockSpec((1, m, d), lambda i: (i, 0, 0)),
                  pl.BlockSpec((1, d), lambda i: (0, 0)),
                  pl.BlockSpec((d, 2 * MEM_DIM), lambda i: (0, 0)),
                  pl.BlockSpec((1, MEM_HEAD_DIM), lambda i: (0, 0))],
        out_specs=[pl.BlockSpec((1, m, MEM_DIM), lambda i: (i, 0, 0)),
                   pl.BlockSpec((1, m, MEM_DIM), lambda i: (i, 0, 0))],
        out_shape=[jax.ShapeDtypeStruct((b, m, MEM_DIM), BF16),
                   jax.ShapeDtypeStruct((b, m, MEM_DIM), BF16)],
        compiler_params=_cparams(("parallel",)),
        name="mem_kv",
    )(mem, ln_mem.reshape(1, d), w_mem_kv.astype(BF16), mem_k_norm.reshape(1, MEM_HEAD_DIM))


def _mem_attn_kernel(q_ref, k_ref, v_ref, o_ref):
    for hd in range(MEM_DIM // MEM_HEAD_DIM):
        sl = slice(hd * MEM_HEAD_DIM, (hd + 1) * MEM_HEAD_DIM)
        sc = _dot_nt(q_ref[0, :, sl], k_ref[0, :, sl]) * (MEM_HEAD_DIM ** -0.5)
        m = jnp.max(sc, axis=-1, keepdims=True)
        p = jnp.exp(sc - m)
        den = jnp.sum(p, axis=-1, keepdims=True)
        o_ref[0, :, sl] = (_dot((p / den).astype(BF16), v_ref[0, :, sl])).astype(o_ref.dtype)


def _mem_attn(mq, mk, mv, bm=512):
    b, s, _ = mq.shape
    m = mk.shape[1]
    return pl.pallas_call(
        _mem_attn_kernel,
        grid=(b, s // bm),
        in_specs=[pl.BlockSpec((1, bm, MEM_DIM), lambda bi, si: (bi, si, 0)),
                  pl.BlockSpec((1, m, MEM_DIM), lambda bi, si: (bi, 0, 0)),
                  pl.BlockSpec((1, m, MEM_DIM), lambda bi, si: (bi, 0, 0))],
        out_specs=pl.BlockSpec((1, bm, MEM_DIM), lambda bi, si: (bi, si, 0)),
        out_shape=jax.ShapeDtypeStruct((b, s, MEM_DIM), BF16),
        compiler_params=_cparams(("parallel", "parallel")),
        name="mem_attn",
    )(mq, mk, mv)


def _mix_kernel(x_ref, yr_ref, o0_ref, o1_ref, o2_ref, l0_ref, l1_ref, l2_ref, ym_ref, gt_ref,
                pr_ref, pd_ref, pm_ref, wo_ref, ln2_ref, wr_ref, br_ref,
                x1_ref, h2_ref, route_ref, cnt_ref):
    d = x_ref.shape[1]
    wide = lambda ref: jnp.concatenate([ref[0, hf] for hf in range(DIL_GROUP_DIM // LANES)], axis=1)
    l0, l1, l2 = wide(l0_ref), wide(l1_ref), wide(l2_ref)
    m = jnp.maximum(jnp.maximum(l0, l1), l2)
    e0, e1, e2 = jnp.exp(l0 - m), jnp.exp(l1 - m), jnp.exp(l2 - m)
    y_dil = (e0 * wide(o0_ref) + e1 * wide(o1_ref) + e2 * wide(o2_ref)) / (e0 + e1 + e2)
    mixed = (gt_ref[:, 0:d].astype(F32) * _dot(yr_ref[...], pr_ref[...])
             + gt_ref[:, d:2 * d].astype(F32) * _dot(y_dil.astype(BF16), pd_ref[...])
             + gt_ref[:, 2 * d:3 * d].astype(F32) * _dot(ym_ref[...], pm_ref[...]))
    x1 = x_ref[...] + _dot(mixed.astype(BF16), wo_ref[...])
    x1_ref[...] = x1
    h2 = x1 * lax.rsqrt(jnp.mean(x1 * x1, axis=-1, keepdims=True) + NORM_EPS) * ln2_ref[...]
    h2_ref[...] = h2
    logits = _dot_x3(h2, wr_ref[...]) + br_ref[...]
    lane = lax.broadcasted_iota(jnp.int32, logits.shape, 1)
    lane_f = lane.astype(F32)
    route = jnp.zeros(logits.shape, F32)
    onehot = jnp.zeros(logits.shape, F32)
    vals = []
    for kq in range(TOP_K):
        mx = jnp.max(logits, axis=-1, keepdims=True)
        idx = jnp.min(jnp.where(logits == mx, lane_f, float(LANES)), axis=-1, keepdims=True)
        hit = lane_f == idx
        vals.append(mx)
        route = jnp.where(lane == kq, idx, route)
        onehot = jnp.where(hit, 1.0, onehot)
        logits = jnp.where(hit, -jnp.inf, logits)
    ex = [jnp.exp(vq - vals[0]) for vq in vals]
    den = ex[0] + ex[1] + ex[2] + ex[3]
    for kq in range(TOP_K):
        route = jnp.where(lane == TOP_K + kq, ex[kq] / den, route)
    route_ref[...] = route

    @pl.when(pl.program_id(0) == 0)
    def _():
        cnt_ref[...] = jnp.zeros_like(cnt_ref)

    cnt_ref[...] += jnp.sum(onehot, axis=0, keepdims=True)


def _mix(x2, y_rwkv, outs, lses, y_mem, gates, p_rwkv, p_dil, p_mem, w_out, ln2, w_router, b_router, bm=256):
    t, d = x2.shape
    wr = jnp.zeros((d, LANES), F32).at[:, :N_EXPERTS].set(w_router)
    br = jnp.full((1, LANES), -jnp.inf, F32).at[0, :N_EXPERTS].set(b_router)
    row = lambda w: pl.BlockSpec((bm, w), lambda i: (i, 0))
    const = lambda a: pl.BlockSpec(a.shape, lambda i: (0,) * a.ndim)
    tiles_per_seq = outs[0].shape[2] // bm
    dil = pl.BlockSpec((1, DIL_GROUP_DIM // LANES, bm, LANES),
                       lambda i: (i // tiles_per_seq, 0, i % tiles_per_seq, 0))
    weights = [p_rwkv.astype(BF16), p_dil.astype(BF16), p_mem.astype(BF16), w_out.astype(BF16),
               ln2.reshape(1, d), wr, br]
    return pl.pallas_call(
        _mix_kernel,
        grid=(t // bm,),
        in_specs=[row(d), row(RWKV_DIM)] + [dil] * 6 + [row(MEM_DIM), row(3 * d)]
                 + [const(a) for a in weights],
        out_specs=[row(d), row(d), row(LANES), pl.BlockSpec((1, LANES), lambda i: (0, 0))],
        out_shape=[jax.ShapeDtypeStruct((t, d), F32), jax.ShapeDtypeStruct((t, d), F32),
                   jax.ShapeDtypeStruct((t, LANES), F32), jax.ShapeDtypeStruct((1, LANES), F32)],
        compiler_params=_cparams(("arbitrary",)),
        name="mix",
    )(x2, y_rwkv, *outs, *lses, y_mem, gates, *weights)


def _route_kernel(route_ref, pstart_ref, tri_ref, dest_ref, carry_ref):
    @pl.when(pl.program_id(0) == 0)
    def _():
        carry_ref[...] = jnp.zeros_like(carry_ref)

    route = route_ref[...]
    lane = lax.broadcasted_iota(jnp.int32, route.shape, 1)
    lane_f = lane.astype(F32)
    hits = [lane_f == route[:, kq:kq + 1] for kq in range(TOP_K)]
    onehot = jnp.zeros(route.shape, F32)
    for hq in hits:
        onehot = jnp.where(hq, 1.0, onehot)
    rank = _dot(tri_ref[...], onehot.astype(BF16)) + carry_ref[...]
    slot = pstart_ref[...] + rank
    dest = jnp.zeros(route.shape, jnp.int32)
    for kq in range(TOP_K):
        dk = jnp.sum(jnp.where(hits[kq], slot, 0.0), axis=-1, keepdims=True)
        dest = jnp.where(lane == kq, dk.astype(jnp.int32), dest)
    dest_ref[...] = dest
    carry_ref[...] += jnp.sum(onehot, axis=0, keepdims=True)


def _route(route, pstart, bm=256):
    t = route.shape[0]
    i = jnp.arange(bm)
    tri = (i[None, :] < i[:, None]).astype(BF16)
    return pl.pallas_call(
        _route_kernel,
        grid=(t // bm,),
        in_specs=[pl.BlockSpec((bm, LANES), lambda i: (i, 0)),
                  pl.BlockSpec((1, LANES), lambda i: (0, 0)),
                  pl.BlockSpec((bm, bm), lambda i: (0, 0))],
        out_specs=pl.BlockSpec((bm, LANES), lambda i: (i, 0)),
        out_shape=jax.ShapeDtypeStruct((t, LANES), jnp.int32),
        scratch_shapes=[pltpu.VMEM((1, LANES), F32)],
        compiler_params=_cparams(("arbitrary",)),
        name="route",
    )(route, pstart, tri)


def _row_copy(src_ref, src_row, dst_ref, dst_row, sem):
    return pltpu.make_async_copy(src_ref.at[pl.ds(src_row, 1)], dst_ref.at[pl.ds(dst_row, 1)], sem)


def _dispatch_kernel(dest_ref, h_ref, xs_in_ref, xs_ref, sem, *, bm):
    del xs_in_ref

    def issue(j, carry):
        for kq in range(TOP_K):
            _row_copy(h_ref, j, xs_ref, dest_ref[j * TOP_K + kq], sem).start()
        return carry

    lax.fori_loop(0, bm, issue, 0)

    def drain(j, carry):
        for kq in range(TOP_K):
            _row_copy(h_ref, j, xs_ref, dest_ref[j * TOP_K + kq], sem).wait()
        return carry

    lax.fori_loop(0, bm, drain, 0)


def _dispatch(dest_flat, h2, n_slots, bm=256):
    t, d = h2.shape
    xs0 = jnp.zeros((n_slots, d), h2.dtype)
    return pl.pallas_call(
        functools.partial(_dispatch_kernel, bm=bm),
        grid=(t // bm,),
        in_specs=[pl.BlockSpec((bm * TOP_K,), lambda i: (i,), memory_space=pltpu.SMEM),
                  pl.BlockSpec((bm, d), lambda i: (i, 0)),
                  pl.BlockSpec(memory_space=pl.ANY)],
        out_specs=pl.BlockSpec(memory_space=pl.ANY),
        out_shape=jax.ShapeDtypeStruct((n_slots, d), h2.dtype),
        scratch_shapes=[pltpu.SemaphoreType.DMA],
        input_output_aliases={2: 0},
        compiler_params=_cparams(("arbitrary",)),
        name="dispatch",
    )(dest_flat, h2, xs0)


FF_CHUNK = 256


def _expert_kernel(be_ref, nu_ref, xs_ref, w1_ref, b1_ref, w2_ref, b2_ref, sel_ref, ys_ref,
                   w1p_ref, w2b_ref):
    i = pl.program_id(0)
    e = be_ref[i]
    fresh = jnp.logical_or(i == 0, be_ref[jnp.maximum(i - 1, 0)] != e)
    used = i < nu_ref[0]
    d_ff2 = w1_ref.shape[2]
    half = FF_CHUNK // 2

    @pl.when(jnp.logical_and(fresh, used))
    def _():
        for c in range(d_ff2 // FF_CHUNK):
            sl = slice(c * FF_CHUNK, (c + 1) * FF_CHUNK)
            w1p_ref[:, sl] = _dot(w1_ref[0, :, sl].astype(BF16), sel_ref[...]).astype(BF16)
        w2b_ref[...] = w2_ref[0].astype(BF16)

    @pl.when(used)
    def _():
        x = xs_ref[...].astype(BF16)
        y = jnp.zeros(ys_ref.shape, F32) + b2_ref[0]
        for c in range(d_ff2 // (2 * FF_CHUNK)):
            sl = slice(2 * c * FF_CHUNK, 2 * (c + 1) * FF_CHUNK)
            hb = _dot(x, w1p_ref[:, sl]) + b1_ref[0, :, sl]
            acts = []
            for j in range(2):
                x_glu = jnp.minimum(hb[:, j * FF_CHUNK:j * FF_CHUNK + half], SWIGLU_LIMIT)
                x_lin = jnp.clip(hb[:, j * FF_CHUNK + half:(j + 1) * FF_CHUNK], -SWIGLU_LIMIT, SWIGLU_LIMIT)
                acts.append(x_glu * _sigmoid(SWIGLU_ALPHA * x_glu) * (x_lin + 1.0))
            act = jnp.concatenate(acts, axis=1).astype(BF16)
            y = y + _dot(act, w2b_ref[c * FF_CHUNK:(c + 1) * FF_CHUNK, :])
        ys_ref[...] = y

    @pl.when(jnp.logical_not(used))
    def _():
        ys_ref[...] = jnp.zeros_like(ys_ref)


def _chunk_deinterleave(a):
    lead = a.shape[:-1]
    a = a.reshape(lead + (a.shape[-1] // FF_CHUNK, FF_CHUNK // 2, 2))
    return jnp.swapaxes(a, -1, -2).reshape(lead + (-1,))


def _experts(block_e, n_used, xs, w1, b1, w2, b2):
    n_slots, d = xs.shape
    n_e, _, d_ff2 = w1.shape
    n_blocks = n_slots // MOE_BLOCK
    i = jnp.arange(FF_CHUNK)
    src = jnp.where(i < FF_CHUNK // 2, 2 * i, 2 * (i - FF_CHUNK // 2) + 1)
    sel = (jnp.arange(FF_CHUNK)[:, None] == src[None, :]).astype(BF16)
    b1p = _chunk_deinterleave(b1).reshape(n_e, 1, d_ff2)
    grid_spec = pltpu.PrefetchScalarGridSpec(
        num_scalar_prefetch=2,
        grid=(n_blocks,),
        in_specs=[pl.BlockSpec((MOE_BLOCK, d), lambda i, be, nu: (i, 0)),
                  pl.BlockSpec((1, d, d_ff2), lambda i, be, nu: (be[i], 0, 0)),
                  pl.BlockSpec((1, 1, d_ff2), lambda i, be, nu: (be[i], 0, 0)),
                  pl.BlockSpec((1, d_ff2 // 2, d), lambda i, be, nu: (be[i], 0, 0)),
                  pl.BlockSpec((1, 1, d), lambda i, be, nu: (be[i], 0, 0)),
                  pl.BlockSpec((FF_CHUNK, FF_CHUNK), lambda i, be, nu: (0, 0))],
        out_specs=pl.BlockSpec((MOE_BLOCK, d), lambda i, be, nu: (i, 0)),
        scratch_shapes=[pltpu.VMEM((d, d_ff2), BF16), pltpu.VMEM((d_ff2 // 2, d), BF16)],
    )
    return pl.pallas_call(
        _expert_kernel,
        grid_spec=grid_spec,
        out_shape=jax.ShapeDtypeStruct((n_slots, d), F32),
        compiler_params=_cparams(("arbitrary",)),
        name="experts",
    )(block_e, n_used, xs, w1, b1p, w2, b2.reshape(n_e, 1, d), sel)


def _combine_kernel(dest_ref, x1_ref, route_ref, ys_ref, o_ref, buf_ref, sem, *, bm):
    def issue(j, carry):
        for kq in range(TOP_K):
            _row_copy(ys_ref, dest_ref[j * TOP_K + kq], buf_ref.at[kq], j, sem).start()
        return carry

    lax.fori_loop(0, bm, issue, 0)

    def drain(j, carry):
        for kq in range(TOP_K):
            _row_copy(ys_ref, dest_ref[j * TOP_K + kq], buf_ref.at[kq], j, sem).wait()
        return carry

    lax.fori_loop(0, bm, drain, 0)
    acc = x1_ref[...]
    for kq in range(TOP_K):
        acc = acc + route_ref[:, TOP_K + kq:TOP_K + kq + 1] * buf_ref[kq]
    o_ref[...] = acc


def _combine(dest_flat, x1, route, ys, bm=256):
    t, d = x1.shape
    return pl.pallas_call(
        functools.partial(_combine_kernel, bm=bm),
        grid=(t // bm,),
        in_specs=[pl.BlockSpec((bm * TOP_K,), lambda i: (i,), memory_space=pltpu.SMEM),
                  pl.BlockSpec((bm, d), lambda i: (i, 0)),
                  pl.BlockSpec((bm, LANES), lambda i: (i, 0)),
                  pl.BlockSpec(memory_space=pl.ANY)],
        out_specs=pl.BlockSpec((bm, d), lambda i: (i, 0)),
        out_shape=jax.ShapeDtypeStruct((t, d), F32),
        scratch_shapes=[pltpu.VMEM((TOP_K, bm, d), F32), pltpu.SemaphoreType.DMA],
        compiler_params=_cparams(("arbitrary",)),
        name="combine",
    )(dest_flat, x1, route, ys)


def _layer(x, mem, ln1, w_in, token_mu, rwkv_w0, rwkv_w2, rwkv_a0, rwkv_a2, rwkv_g2,
           rwkv_k_k, rwkv_k_a, rwkv_r_k, rwkv_ln_w, rwkv_ln_b, dil_q_norm, dil_k_norm,
           ln_mem, w_mem_kv, mem_q_norm, mem_k_norm, p_rwkv, p_dil, p_mem, w_out,
           ln2, w_router, b_router, w1, b1, w2, b2):
    b, s, d = x.shape
    t = b * s
    x2 = x.reshape(t, d)
    zr, dil_qkv, mq, gates = _in_proj(x2, ln1, w_in, dil_q_norm, dil_k_norm, mem_q_norm)
    y_rwkv = _rwkv(zr.reshape(b, s, RWKV_IN), token_mu, rwkv_w0, rwkv_w2, rwkv_a0, rwkv_a2, rwkv_g2,
                   rwkv_k_k, rwkv_k_a, rwkv_r_k.reshape(-1), rwkv_ln_w, rwkv_ln_b)
    outs, lses = [], []
    for g, (_, dilation) in enumerate(DIL_PATTERNS):
        o, l = _dil_group(*dil_qkv[3 * g:3 * g + 3], b, dilation)
        outs.append(o)
        lses.append(l)
    mk, mv = _mem_kv(mem, ln_mem, w_mem_kv, mem_k_norm)
    y_mem = _mem_attn(mq.reshape(b, s, MEM_DIM), mk, mv)
    x1, h2, route, counts = _mix(x2, y_rwkv.reshape(t, RWKV_DIM), outs, lses, y_mem.reshape(t, MEM_DIM), gates,
                                 p_rwkv, p_dil, p_mem, w_out, ln2, w_router, b_router)

    counts = counts[0, :N_EXPERTS].astype(jnp.int32)
    nblk = (counts + MOE_BLOCK - 1) // MOE_BLOCK
    bend = jnp.cumsum(nblk)
    pstart = ((bend - nblk) * MOE_BLOCK).astype(F32)
    pstart = jnp.zeros((1, LANES), F32).at[0, :N_EXPERTS].set(pstart)
    n_blocks = (t * TOP_K) // MOE_BLOCK + N_EXPERTS
    block_e = jnp.sum(bend[None, :] <= jnp.arange(n_blocks, dtype=jnp.int32)[:, None], axis=1)
    block_e = jnp.minimum(block_e, N_EXPERTS - 1).astype(jnp.int32)
    n_used = bend[-1:].astype(jnp.int32)

    dest = _route(route, pstart)[:, :TOP_K].reshape(-1)
    xs = _dispatch(dest, h2, n_blocks * MOE_BLOCK)
    ys = _experts(block_e, n_used, xs, w1, b1, w2, b2)
    out = _combine(dest, x1, route, ys)
    return out.reshape(b, s, d)


def kernel(x, mem, ln1, w_in, token_mu, rwkv_w0, rwkv_w2, rwkv_a0, rwkv_a2, rwkv_g2, rwkv_k_k, rwkv_k_a, rwkv_r_k, rwkv_ln_w, rwkv_ln_b, dil_q_norm, dil_k_norm, ln_mem, w_mem_kv, mem_q_norm, mem_k_norm, p_rwkv, p_dil, p_mem, w_out, ln2, w_router, b_router, w1, b1, w2, b2):
    params = (ln1, w_in, token_mu, rwkv_w0, rwkv_w2, rwkv_a0, rwkv_a2, rwkv_g2, rwkv_k_k, rwkv_k_a,
              rwkv_r_k, rwkv_ln_w, rwkv_ln_b, dil_q_norm, dil_k_norm, ln_mem, w_mem_kv, mem_q_norm,
              mem_k_norm, p_rwkv, p_dil, p_mem, w_out, ln2, w_router, b_router, w1, b1, w2, b2)
    for l in range(ln1.shape[0]):
        x = _layer(x, mem, *[p[l] for p in params])
    return x
```

```python
import functools

import jax
import jax.numpy as jnp
from jax import lax
from jax.experimental import pallas as pl
from jax.experimental.pallas import tpu as pltpu

F32 = jnp.float32
BF16 = jnp.bfloat16

NORM_EPS = 1e-5
HEAD_DIM = 64
RWKV_DIM = 512
RWKV_IN = 1792
GN_EPS = HEAD_DIM * 1e-5
DIL_DIM = 768
DIL_GROUP_DIM = 256
DIL_PATTERNS = ((128, 1), (512, 4), (2048, 16))
ATTN_BLOCK = 128
MEM_DIM = 512
MEM_HEAD_DIM = 128
N_EXPERTS = 32
TOP_K = 4
SWIGLU_ALPHA = 1.702
SWIGLU_LIMIT = 7.0
LANES = 128
CHUNK = 64
NEG_BIG = -1e30
VMEM_LIMIT = 48 * 1024 * 1024
EXPERT_VMEM_LIMIT = 56 * 1024 * 1024


def _dot(a, b):
    return jnp.dot(a, b, preferred_element_type=F32)


def _dot_nt(a, b):
    return lax.dot_general(a, b, (((1,), (1,)), ((), ())), preferred_element_type=F32)


def _split2(x):
    hi = x.astype(BF16)
    lo = (x - hi.astype(F32)).astype(BF16)
    return hi, lo


def _split3(x):
    hi = x.astype(BF16)
    r = x - hi.astype(F32)
    lo = r.astype(BF16)
    lo2 = (r - lo.astype(F32)).astype(BF16)
    return hi, lo, lo2


def _dot_rhs3(w_bf16, x):
    hi, lo, lo2 = _split3(x)
    return _dot(w_bf16, hi) + _dot(w_bf16, lo) + _dot(w_bf16, lo2)


def _dot_x3(a, b):
    ah, al = _split2(a)
    bh, bl = _split2(b)
    return _dot(ah, bh) + _dot(al, bh) + _dot(ah, bl)


def _sigmoid(x):
    return 1.0 / (1.0 + jnp.exp(-x))


def _cparams(sem, vmem=VMEM_LIMIT):
    return pltpu.CompilerParams(dimension_semantics=sem, vmem_limit_bytes=vmem)


IN_CHUNK = 256
PERM_ROWS = 256
N_ZR = RWKV_IN // IN_CHUNK
N_DIL = 3 * DIL_DIM // IN_CHUNK
N_MQ = MEM_DIM // IN_CHUNK
N_GATE = 3 * 1024 // IN_CHUNK


def _in_proj_kernel(x_ref, ln_ref, w_ref, qg_ref, kg_ref, mg_ref, p4_ref, p16_ref,
                    zr_ref, *rest, bm):
    dil_refs, (mq_ref, gt_ref) = rest[:N_DIL], rest[N_DIL:]
    perm_refs = (None, p4_ref, p16_ref)
    x = x_ref[...]
    h = x * lax.rsqrt(jnp.mean(x * x, axis=-1, keepdims=True) + NORM_EPS) * ln_ref[...]
    hb = h.astype(BF16)

    lane = lax.broadcasted_iota(jnp.int32, (x.shape[0], LANES), 1)

    def seg_rms(z, seg, gain):
        z2 = z * z
        cols = []
        for cb in range(IN_CHUNK // LANES):
            blk = z2[:, cb * LANES:(cb + 1) * LANES]
            ss = jnp.zeros_like(blk)
            for sg in range(LANES // seg):
                m = (lane >= sg * seg) & (lane < (sg + 1) * seg)
                ss = jnp.where(m, jnp.sum(jnp.where(m, blk, 0.0), axis=-1, keepdims=True), ss)
            cols.append(ss)
        ss = jnp.concatenate(cols, axis=1)
        return z * lax.rsqrt(ss * (1.0 / seg) + NORM_EPS) * gain

    for c in range(N_ZR + N_DIL + N_MQ + N_GATE):
        z = _dot(hb, w_ref[:, c * IN_CHUNK:(c + 1) * IN_CHUNK])
        if c < N_ZR:
            zr_ref[:, c * IN_CHUNK:(c + 1) * IN_CHUNK] = z
            continue
        d = c - N_ZR
        if d < N_DIL:
            which, g = divmod(d, 3)
            if which == 0:
                z = seg_rms(z, HEAD_DIM, qg_ref[...])
            elif which == 1:
                z = seg_rms(z, HEAD_DIM, kg_ref[...])
            zb = z.astype(BF16)
            o_ref = dil_refs[g * 3 + which]
            dilation = DIL_PATTERNS[g][1]
            if dilation == 1:
                o_ref[...] = zb
            else:
                rows = PERM_ROWS // dilation
                for sub in range(bm // PERM_ROWS):
                    zp = _dot(perm_refs[g][...], zb[sub * PERM_ROWS:(sub + 1) * PERM_ROWS, :]).astype(BF16)
                    for r in range(dilation):
                        o_ref[sub * rows:(sub + 1) * rows, r * IN_CHUNK:(r + 1) * IN_CHUNK] = zp[r * rows:(r + 1) * rows, :]
            continue
        d -= N_DIL
        if d < N_MQ:
            mq_ref[:, d * IN_CHUNK:(d + 1) * IN_CHUNK] = seg_rms(z, MEM_HEAD_DIM, mg_ref[...]).astype(BF16)
            continue
        d -= N_MQ
        gt_ref[:, d * IN_CHUNK:(d + 1) * IN_CHUNK] = _sigmoid(z).astype(BF16)


def _class_perm(bm, dilation):
    i = jnp.arange(bm)
    src = (i % (bm // dilation)) * dilation + i // (bm // dilation)
    return (src[:, None] == i[None, :]).astype(BF16)


def _in_proj(x2, ln1, w_in, dil_q_norm, dil_k_norm, mem_q_norm, bm=512):
    t, d = x2.shape
    n_cols = w_in.shape[1]
    wb = w_in.astype(BF16)
    qg = (jnp.tile(dil_q_norm, IN_CHUNK // HEAD_DIM) * (HEAD_DIM ** -0.5)).reshape(1, IN_CHUNK)
    kg = jnp.tile(dil_k_norm, IN_CHUNK // HEAD_DIM).reshape(1, IN_CHUNK)
    mg = jnp.tile(mem_q_norm, IN_CHUNK // MEM_HEAD_DIM).reshape(1, IN_CHUNK)
    p4 = _class_perm(PERM_ROWS, DIL_PATTERNS[1][1])
    p16 = _class_perm(PERM_ROWS, DIL_PATTERNS[2][1])
    row = lambda w: pl.BlockSpec((bm, w), lambda i: (i, 0))
    const = lambda a: pl.BlockSpec(a.shape, lambda i: (0,) * a.ndim)
    dil_specs, dil_shapes = [], []
    for _, dilation in DIL_PATTERNS:
        for _ in range(3):
            dil_specs.append(pl.BlockSpec((bm // dilation, dilation * DIL_GROUP_DIM), lambda i: (i, 0)))
            dil_shapes.append(jax.ShapeDtypeStruct((t // dilation, dilation * DIL_GROUP_DIM), BF16))
    outs = pl.pallas_call(
        functools.partial(_in_proj_kernel, bm=bm),
        grid=(t // bm,),
        in_specs=[row(d), pl.BlockSpec((1, d), lambda i: (0, 0)),
                  pl.BlockSpec((d, n_cols), lambda i: (0, 0), pipeline_mode=pl.Buffered(1)),
                  const(qg), const(kg), const(mg), const(p4), const(p16)],
        out_specs=[row(RWKV_IN)] + dil_specs + [row(MEM_DIM), row(3 * d)],
        out_shape=[jax.ShapeDtypeStruct((t, RWKV_IN), F32)] + dil_shapes
                  + [jax.ShapeDtypeStruct((t, MEM_DIM), BF16), jax.ShapeDtypeStruct((t, 3 * d), BF16)],
        compiler_params=_cparams(("parallel",)),
        name="in_proj",
    )(x2, ln1.reshape(1, d), wb, qg, kg, mg, p4, p16)
    return outs[0], outs[1:1 + N_DIL], outs[1 + N_DIL], outs[2 + N_DIL]


def _rwkv_kernel(z_ref, mu_ref, w0_ref, a0_ref, wl_ref, kk_ref, ka_ref, rk_ref,
                 lnw_ref, lnb_ref, tri_ref, y_ref, carry_ref, state_ref, ybuf_ref, *, tt):
    s_idx = pl.program_id(1)

    @pl.when(s_idx == 0)
    def _():
        carry_ref[...] = jnp.zeros_like(carry_ref)
        state_ref[...] = jnp.zeros_like(state_ref)

    z = z_ref[0]
    rows = lax.broadcasted_iota(jnp.int32, z.shape, 0)
    prev = jnp.where(rows == 0, carry_ref[...], pltpu.roll(z, 1, axis=0))
    carry_ref[...] = z[tt - 1:tt, :]
    zl = z + (prev - z) * mu_ref[...]
    r = zl[:, 0:RWKV_DIM]
    k = zl[:, RWKV_DIM:2 * RWKV_DIM]
    v = zl[:, 2 * RWKV_DIM:3 * RWKV_DIM]
    zlo = zl[:, 3 * RWKV_DIM:RWKV_IN]
    lcol = lax.broadcasted_iota(jnp.int32, zlo.shape, 1)
    feat = jnp.where(lcol < 64, jnp.tanh(zlo), jnp.where(lcol < 128, zlo, _sigmoid(zlo)))
    f_hi, f_lo = _split2(feat)
    lora = _dot(f_hi, wl_ref[...]) + _dot(f_lo, wl_ref[...])
    nu = -(w0_ref[...] + lora[:, 0:RWKV_DIM])
    softplus = jnp.maximum(nu, 0.0) + jnp.log(1.0 + jnp.exp(-jnp.abs(nu)))
    lw = -jnp.exp(-softplus - 0.5)
    a = _sigmoid(a0_ref[...] + lora[:, RWKV_DIM:2 * RWKV_DIM])
    g = lora[:, 2 * RWKV_DIM:3 * RWKV_DIM]

    n_pairs = RWKV_DIM // LANES
    n_chunks = tt // CHUNK
    head_lo = lax.broadcasted_iota(jnp.int32, (tt, LANES), 1) < HEAD_DIM

    def seg_sum(x):
        parts = []
        for p in range(n_pairs):
            xp = x[:, p * LANES:(p + 1) * LANES]
            lo = jnp.sum(jnp.where(head_lo, xp, 0.0), axis=-1, keepdims=True)
            hi = jnp.sum(jnp.where(head_lo, 0.0, xp), axis=-1, keepdims=True)
            parts.append(jnp.where(head_lo, lo, hi))
        return jnp.concatenate(parts, axis=1)

    kk = k * kk_ref[...]
    kk = kk / jnp.maximum(jnp.sqrt(seg_sum(kk * kk)), 1e-12)
    k2 = k * (1.0 + (a - 1.0) * ka_ref[...])
    a_s = -kk
    b_s = kk * a

    cum = _dot_rhs3(tri_ref[...], lw)
    tot = jnp.concatenate(
        [jnp.broadcast_to(cum[(c + 1) * CHUNK - 1:(c + 1) * CHUNK, :], (CHUNK, RWKV_DIM)) for c in range(n_chunks)],
        axis=0)
    e_neg = jnp.exp(-cum)
    e_end = jnp.exp(tot - cum)
    at = (a_s * jnp.exp(cum - lw)).astype(BF16)
    rt = (r * jnp.exp(cum)).astype(BF16)
    bt = (b_s * e_neg).astype(BF16)
    kt = (k2 * e_neg).astype(BF16)
    be = (b_s * e_end).astype(BF16)
    ke = (k2 * e_end).astype(BF16)
    vb = v.astype(BF16)
    e_tot = jnp.exp(tot)

    r128 = lax.broadcasted_iota(jnp.int32, (LANES, LANES), 0)
    c128 = lax.broadcasted_iota(jnp.int32, (LANES, LANES), 1)
    lane_lo = lax.broadcasted_iota(jnp.int32, (CHUNK, LANES), 1) < HEAD_DIM
    stril = c128 < r128
    tril = c128 <= r128
    eye = (c128 == r128).astype(F32)
    same16 = (r128 // 16) == (c128 // 16)
    same32 = (r128 // 32) == (c128 // 32)
    off16 = same32 & jnp.logical_not(same16)
    off32 = jnp.logical_not(same32)

    def bdiag(xp):
        zero = jnp.zeros_like(xp)
        return jnp.concatenate([jnp.where(lane_lo, xp, zero), jnp.where(lane_lo, zero, xp)], axis=0)

    units = [(ci, p) for ci in range(n_chunks) for p in range(n_pairs)]
    pick = lambda arr, u: arr[u[0] * CHUNK:(u[0] + 1) * CHUNK, u[1] * LANES:(u[1] + 1) * LANES]
    cat0 = lambda xs: jnp.concatenate(xs, axis=0)
    cat1 = lambda xs: jnp.concatenate(xs, axis=1)
    zero_b = jnp.zeros((LANES, LANES), BF16)

    at_b = [bdiag(pick(at, u)) for u in units]
    v_b = [bdiag(pick(vb, u)) for u in units]
    be_b = [bdiag(pick(be, u)) for u in units]
    ke_b = [bdiag(pick(ke, u)) for u in units]
    rt_b = [bdiag(pick(rt, u)) for u in units]
    a_ab, a_ak, a_rbk = [], [], []
    for i, u in enumerate(units):
        mq = _dot_nt(cat0([at_b[i], rt_b[i]]), cat0([bdiag(pick(bt, u)), bdiag(pick(kt, u))]))
        a_ab.append(jnp.where(stril, mq[:LANES, :LANES], 0.0).astype(BF16))
        a_ak.append(jnp.where(stril, mq[:LANES, LANES:], 0.0).astype(BF16))
        a_rbk.append(cat1([jnp.where(tril, mq[LANES:, :LANES], 0.0), jnp.where(tril, mq[LANES:, LANES:], 0.0)]).astype(BF16))
    w_b = [_dot(a_ak[i], v_b[i]).astype(BF16) for i in range(len(units))]

    d1 = [jnp.where(same16, x, jnp.zeros_like(x)) for x in a_ab]
    xs = [eye + d.astype(F32) for d in d1]
    d2 = [_dot(d, d).astype(BF16) for d in d1]
    t_ = [_dot(d2[i], cat1([xs[i].astype(BF16), d2[i]])) for i in range(len(units))]
    xs = [xs[i] + t_[i][:, :LANES] for i in range(len(units))]
    d4 = [t[:, LANES:].astype(BF16) for t in t_]
    t_ = [_dot(d4[i], cat1([xs[i].astype(BF16), d4[i]])) for i in range(len(units))]
    xs = [xs[i] + t_[i][:, :LANES] for i in range(len(units))]
    d8 = [t[:, LANES:].astype(BF16) for t in t_]
    xs = [xs[i] + _dot(d8[i], xs[i].astype(BF16)) for i in range(len(units))]
    for off in (off16, off32):
        xb = [x.astype(BF16) for x in xs]
        g_ = [_dot(jnp.where(off, a_ab[i], jnp.zeros_like(a_ab[i])), xb[i]).astype(BF16) for i in range(len(units))]
        xs = [xs[i] + _dot(xb[i], g_[i]) for i in range(len(units))]

    pq = [_dot(xs[i].astype(BF16), cat1([at_b[i], w_b[i]])) for i in range(len(units))]
    ry = [_dot(a_rbk[i], cat0([pq[i].astype(BF16), cat1([zero_b, v_b[i]])])) for i in range(len(units))]
    r2 = [(rt_b[i].astype(F32) + ry[i][:, :LANES]).astype(BF16) for i in range(len(units))]
    m_c = [_dot(pq[i][:, :LANES].T.astype(BF16), be_b[i]) for i in range(len(units))]
    n_c = [_dot(cat1([pq[i][:, LANES:].T.astype(BF16), v_b[i].astype(F32).T.astype(BF16)]), cat0([be_b[i], ke_b[i]]))
           for i in range(len(units))]

    state = [state_ref[p] for p in range(n_pairs)]
    for i, (ci, p) in enumerate(units):
        s0 = state[p]
        s_hi, s_lo = _split2(s0)
        y = _dot_nt(r2[i], s_hi) + ry[i][:, LANES:]
        mcb = m_c[i].astype(BF16)
        e_row = e_tot[ci * CHUNK:ci * CHUNK + 1, p * LANES:(p + 1) * LANES]
        state[p] = s0 * e_row + (_dot(s_hi, mcb) + _dot(s_lo, mcb)) + n_c[i]
        ybuf_ref[ci * CHUNK:(ci + 1) * CHUNK, p * LANES:(p + 1) * LANES] = jnp.where(lane_lo, y[:CHUNK], y[CHUNK:])
    for p in range(n_pairs):
        state_ref[p] = state[p]

    y = ybuf_ref[...]
    mean = seg_sum(y) * (1.0 / HEAD_DIM)
    yc = y - mean
    var = seg_sum(yc * yc) * (1.0 / HEAD_DIM)
    yn = yc * lax.rsqrt(var + GN_EPS) * lnw_ref[...] + lnb_ref[...]
    bonus = seg_sum(r * k2 * rk_ref[...]) * v
    y_ref[0] = ((yn + bonus) * g).astype(y_ref.dtype)


def _rwkv(zr, token_mu, w0, w2, a0, a2, g2, k_k, k_a, r_k, ln_w, ln_b, tt=256):
    b, s, _ = zr.shape
    i = jnp.arange(tt)
    same = (i[:, None] // CHUNK) == (i[None, :] // CHUNK)
    tri = (same & (i[None, :] <= i[:, None])).astype(BF16)
    vec = lambda a: a.reshape(1, -1)
    wl = jnp.zeros((RWKV_IN - 3 * RWKV_DIM, 3 * RWKV_DIM), F32)
    wl = wl.at[0:64, 0:RWKV_DIM].set(w2).at[64:128, RWKV_DIM:2 * RWKV_DIM].set(a2)
    wl = wl.at[128:256, 2 * RWKV_DIM:3 * RWKV_DIM].set(g2).astype(BF16)
    params = [vec(token_mu), vec(w0), vec(a0), wl, vec(k_k), vec(k_a), vec(r_k),
              vec(ln_w), vec(ln_b), tri]
    const = lambda a: pl.BlockSpec(a.shape, lambda bi, si: (0,) * a.ndim)
    return pl.pallas_call(
        functools.partial(_rwkv_kernel, tt=tt),
        grid=(b, s // tt),
        in_specs=[pl.BlockSpec((1, tt, RWKV_IN), lambda bi, si: (bi, si, 0))] + [const(a) for a in params],
        out_specs=pl.BlockSpec((1, tt, RWKV_DIM), lambda bi, si: (bi, si, 0)),
        out_shape=jax.ShapeDtypeStruct((b, s, RWKV_DIM), BF16),
        scratch_shapes=[pltpu.VMEM((1, RWKV_IN), F32),
                        pltpu.VMEM((RWKV_DIM // LANES, LANES, LANES), F32),
                        pltpu.VMEM((tt, RWKV_DIM), F32)],
        compiler_params=_cparams(("parallel", "arbitrary")),
        name="rwkv",
    )(zr, *params)


DIL_UNITS = 4


def _dil_kernel(q_ref, k_ref, v_ref, o_ref, l_ref, *, dilation, cps, nbs):
    cg = pl.program_id(1)
    jb = pl.program_id(2)
    bq = ATTN_BLOCK
    n_heads = DIL_GROUP_DIM // HEAD_DIM
    qi = lax.broadcasted_iota(jnp.int32, (bq, 2 * bq), 0)
    kj = lax.broadcasted_iota(jnp.int32, (bq, 2 * bq), 1)
    lane = lax.broadcasted_iota(jnp.int32, (bq, DIL_GROUP_DIM), 1)
    head_masks = [(lane >= h * HEAD_DIM) & (lane < (h + 1) * HEAD_DIM) for h in range(n_heads)]
    units = [(c, bb) for c in range(cps) for bb in range(nbs)]

    qs, kcats, vcats, masks, starts = [], [], [], [], []
    for c, bb in units:
        i = jb * nbs + bb
        cols = slice(c * DIL_GROUP_DIM, (c + 1) * DIL_GROUP_DIM)
        p0 = pl.multiple_of(jnp.maximum(i - 1, 0) * bq, bq)
        c0 = pl.multiple_of(i * bq, bq)
        qs.append(q_ref[0, bb * bq:(bb + 1) * bq, cols])
        kcats.append(jnp.concatenate([k_ref[0, pl.ds(p0, bq), cols], k_ref[0, pl.ds(c0, bq), cols]], axis=0))
        vcats.append(jnp.concatenate([v_ref[0, pl.ds(p0, bq), cols], v_ref[0, pl.ds(c0, bq), cols]], axis=0))
        first = (1 - jnp.minimum(i, 1)) * (2 * bq)
        masks.append(((kj < bq) & (kj >= qi + first)) | ((kj >= bq) & ((kj - bq) <= qi)))
        starts.append((cg * cps + c) + dilation * bq * i)
    pairs = [(u, h) for u in range(len(units)) for h in range(n_heads)]
    sc = [jnp.where(masks[u], _dot_nt(jnp.where(head_masks[h], qs[u], jnp.zeros_like(qs[u])), kcats[u]), NEG_BIG)
          for u, h in pairs]
    mx = [jnp.max(x, axis=-1, keepdims=True) for x in sc]
    pr = [jnp.exp(sc[n] - mx[n]) for n in range(len(pairs))]
    den = [jnp.sum(x, axis=-1, keepdims=True) for x in pr]
    ov = [_dot(pr[n].astype(BF16), vcats[pairs[n][0]]) / den[n] for n in range(len(pairs))]
    for u in range(len(units)):
        acc = jnp.zeros((bq, DIL_GROUP_DIM), F32)
        lacc = jnp.zeros((bq, DIL_GROUP_DIM), F32)
        for h in range(n_heads):
            n = u * n_heads + h
            acc = jnp.where(head_masks[h], ov[n], acc)
            lacc = jnp.where(head_masks[h], mx[n] + jnp.log(den[n]), lacc)
        rows = pl.ds(starts[u], bq, stride=dilation) if dilation > 1 else pl.ds(pl.multiple_of(starts[u], bq), bq)
        for half in range(DIL_GROUP_DIM // LANES):
            o_ref[0, half, rows, :] = acc[:, half * LANES:(half + 1) * LANES]
            l_ref[0, half, rows, :] = lacc[:, half * LANES:(half + 1) * LANES]


def _dil_group(q, k, v, b, dilation):
    n = q.shape[0] // b
    s = n * dilation
    nb = n // ATTN_BLOCK
    nbs = min(nb, DIL_UNITS)
    cps = DIL_UNITS // nbs
    halves = DIL_GROUP_DIM // LANES
    view = lambda a: a.reshape(b, n, dilation * DIL_GROUP_DIM)
    qmap = lambda bi, ci, ji: (bi, ji, ci)
    kmap = lambda bi, ci, ji: (bi, 0, ci)
    omap = lambda bi, ci, ji: (bi, 0, 0, 0)
    return pl.pallas_call(
        functools.partial(_dil_kernel, dilation=dilation, cps=cps, nbs=nbs),
        grid=(b, dilation // cps, nb // nbs),
        in_specs=[pl.BlockSpec((1, nbs * ATTN_BLOCK, cps * DIL_GROUP_DIM), qmap),
                  pl.BlockSpec((1, n, cps * DIL_GROUP_DIM), kmap),
                  pl.BlockSpec((1, n, cps * DIL_GROUP_DIM), kmap)],
        out_specs=[pl.BlockSpec((1, halves, s, LANES), omap),
                   pl.BlockSpec((1, halves, s, LANES), omap)],
        out_shape=[jax.ShapeDtypeStruct((b, halves, s, LANES), F32),
                   jax.ShapeDtypeStruct((b, halves, s, LANES), F32)],
        compiler_params=_cparams(("parallel", "arbitrary", "arbitrary")),
        name=f"dil_attn_d{dilation}",
    )(view(q), view(k), view(v))


def _mem_kv_kernel(m_ref, ln_ref, w_ref, kn_ref, k_ref, v_ref):
    x = m_ref[0]
    h = x * lax.rsqrt(jnp.mean(x * x, axis=-1, keepdims=True) + NORM_EPS) * ln_ref[...]
    kv = _dot(h.astype(BF16), w_ref[...])
    for hd in range(MEM_DIM // MEM_HEAD_DIM):
        sl = slice(hd * MEM_HEAD_DIM, (hd + 1) * MEM_HEAD_DIM)
        kh = kv[:, sl]
        kh = kh * lax.rsqrt(jnp.mean(kh * kh, axis=-1, keepdims=True) + NORM_EPS) * kn_ref[...]
        k_ref[0, :, sl] = kh.astype(BF16)
    v_ref[0] = kv[:, MEM_DIM:].astype(BF16)


def _mem_kv(mem, ln_mem, w_mem_kv, mem_k_norm):
    b, m, d = mem.shape
    return pl.pallas_call(
        _mem_kv_kernel,
        grid=(b,),
        in_specs=[pl.BlockSpec((1, m, d), lambda i: (i, 0, 0)),
                  pl.BlockSpec((1, d), lambda i: (0, 0)),
                  pl.BlockSpec((d, 2 * MEM_DIM), lambda i: (0, 0)),
                  pl.BlockSpec((1, MEM_HEAD_DIM), lambda i: (0, 0))],
        out_specs=[pl.BlockSpec((1, m, MEM_DIM), lambda i: (i, 0, 0)),
                   pl.BlockSpec((1, m, MEM_DIM), lambda i: (i, 0, 0))],
        out_shape=[jax.ShapeDtypeStruct((b, m, MEM_DIM), BF16),
                   jax.ShapeDtypeStruct((b, m, MEM_DIM), BF16)],
        compiler_params=_cparams(("parallel",)),
        name="mem_kv",
    )(mem, ln_mem.reshape(1, d), w_mem_kv.astype(BF16), mem_k_norm.reshape(1, MEM_HEAD_DIM))


def _mem_attn_kernel(q_ref, k_ref, v_ref, o_ref):
    for hd in range(MEM_DIM // MEM_HEAD_DIM):
        sl = slice(hd * MEM_HEAD_DIM, (hd + 1) * MEM_HEAD_DIM)
        sc = _dot_nt(q_ref[0, :, sl], k_ref[0, :, sl]) * (MEM_HEAD_DIM ** -0.5)
        m = jnp.max(sc, axis=-1, keepdims=True)
        p = jnp.exp(sc - m)
        den = jnp.sum(p, axis=-1, keepdims=True)
        o_ref[0, :, sl] = (_dot((p / den).astype(BF16), v_ref[0, :, sl])).astype(o_ref.dtype)


def _mem_attn(mq, mk, mv, bm=512):
    b, s, _ = mq.shape
    m = mk.shape[1]
    return pl.pallas_call(
        _mem_attn_kernel,
        grid=(b, s // bm),
        in_specs=[pl.BlockSpec((1, bm, MEM_DIM), lambda bi, si: (bi, si, 0)),
                  pl.BlockSpec((1, m, MEM_DIM), lambda bi, si: (bi, 0, 0)),
                  pl.BlockSpec((1, m, MEM_DIM), lambda bi, si: (bi, 0, 0))],
        out_specs=pl.BlockSpec((1, bm, MEM_DIM), lambda bi, si: (bi, si, 0)),
        out_shape=jax.ShapeDtypeStruct((b, s, MEM_DIM), BF16),
        compiler_params=_cparams(("parallel", "parallel")),
        name="mem_attn",
    )(mq, mk, mv)


def _mix_kernel(x_ref, yr_ref, o0_ref, o1_ref, o2_ref, l0_ref, l1_ref, l2_ref, ym_ref, gt_ref,
                pr_ref, pd_ref, pm_ref, wo_ref, ln2_ref, wr_ref, br_ref,
                x1_ref, h2_ref, route_ref, cnt_ref):
    d = x_ref.shape[1]
    wide = lambda ref: jnp.concatenate([ref[0, hf] for hf in range(DIL_GROUP_DIM // LANES)], axis=1)
    l0, l1, l2 = wide(l0_ref), wide(l1_ref), wide(l2_ref)
    m = jnp.maximum(jnp.maximum(l0, l1), l2)
    e0, e1, e2 = jnp.exp(l0 - m), jnp.exp(l1 - m), jnp.exp(l2 - m)
    y_dil = (e0 * wide(o0_ref) + e1 * wide(o1_ref) + e2 * wide(o2_ref)) / (e0 + e1 + e2)
    mixed = (gt_ref[:, 0:d].astype(F32) * _dot(yr_ref[...], pr_ref[...])
             + gt_ref[:, d:2 * d].astype(F32) * _dot(y_dil.astype(BF16), pd_ref[...])
             + gt_ref[:, 2 * d:3 * d].astype(F32) * _dot(ym_ref[...], pm_ref[...]))
    x1 = x_ref[...] + _dot(mixed.astype(BF16), wo_ref[...])
    x1_ref[...] = x1
    h2 = x1 * lax.rsqrt(jnp.mean(x1 * x1, axis=-1, keepdims=True) + NORM_EPS) * ln2_ref[...]
    h2_ref[...] = h2
    logits = _dot_x3(h2, wr_ref[...]) + br_ref[...]
    lane = lax.broadcasted_iota(jnp.int32, logits.shape, 1)
    lane_f = lane.astype(F32)
    route = jnp.zeros(logits.shape, F32)
    onehot = jnp.zeros(logits.shape, F32)
    vals = []
    for kq in range(TOP_K):
        mx = jnp.max(logits, axis=-1, keepdims=True)
        idx = jnp.min(jnp.where(logits == mx, lane_f, float(LANES)), axis=-1, keepdims=True)
        hit = lane_f == idx
        vals.append(mx)
        route = jnp.where(lane == kq, idx, route)
        onehot = jnp.where(hit, 1.0, onehot)
        logits = jnp.where(hit, -jnp.inf, logits)
    ex = [jnp.exp(vq - vals[0]) for vq in vals]
    den = ex[0] + ex[1] + ex[2] + ex[3]
    for kq in range(TOP_K):
        route = jnp.where(lane == TOP_K + kq, ex[kq] / den, route)
    route_ref[...] = route

    @pl.when(pl.program_id(0) == 0)
    def _():
        cnt_ref[...] = jnp.zeros_like(cnt_ref)

    cnt_ref[...] += jnp.sum(onehot, axis=0, keepdims=True)


def _mix(x2, y_rwkv, outs, lses, y_mem, gates, p_rwkv, p_dil, p_mem, w_out, ln2, w_router, b_router, bm=512):
    t, d = x2.shape
    wr = jnp.zeros((d, LANES), F32).at[:, :N_EXPERTS].set(w_router)
    br = jnp.full((1, LANES), -jnp.inf, F32).at[0, :N_EXPERTS].set(b_router)
    row = lambda w: pl.BlockSpec((bm, w), lambda i: (i, 0))
    const = lambda a: pl.BlockSpec(a.shape, lambda i: (0,) * a.ndim)
    tiles_per_seq = outs[0].shape[2] // bm
    dil = pl.BlockSpec((1, DIL_GROUP_DIM // LANES, bm, LANES),
                       lambda i: (i // tiles_per_seq, 0, i % tiles_per_seq, 0))
    weights = [p_rwkv.astype(BF16), p_dil.astype(BF16), p_mem.astype(BF16), w_out.astype(BF16),
               ln2.reshape(1, d), wr, br]
    return pl.pallas_call(
        _mix_kernel,
        grid=(t // bm,),
        in_specs=[row(d), row(RWKV_DIM)] + [dil] * 6 + [row(MEM_DIM), row(3 * d)]
                 + [const(a) for a in weights],
        out_specs=[row(d), row(d), row(LANES), pl.BlockSpec((1, LANES), lambda i: (0, 0))],
        out_shape=[jax.ShapeDtypeStruct((t, d), F32), jax.ShapeDtypeStruct((t, d), F32),
                   jax.ShapeDtypeStruct((t, LANES), F32), jax.ShapeDtypeStruct((1, LANES), F32)],
        compiler_params=_cparams(("arbitrary",)),
        name="mix",
    )(x2, y_rwkv, *outs, *lses, y_mem, gates, *weights)


def _route_kernel(route_ref, pstart_ref, tri_ref, dest_ref, carry_ref):
    @pl.when(pl.program_id(0) == 0)
    def _():
        carry_ref[...] = jnp.zeros_like(carry_ref)

    route = route_ref[...]
    lane = lax.broadcasted_iota(jnp.int32, route.shape, 1)
    lane_f = lane.astype(F32)
    hits = [lane_f == route[:, kq:kq + 1] for kq in range(TOP_K)]
    onehot = jnp.zeros(route.shape, F32)
    for hq in hits:
        onehot = jnp.where(hq, 1.0, onehot)
    rank = _dot(tri_ref[...], onehot.astype(BF16)) + carry_ref[...]
    slot = pstart_ref[...] + rank
    dest = jnp.zeros(route.shape, jnp.int32)
    for kq in range(TOP_K):
        dk = jnp.sum(jnp.where(hits[kq], slot, 0.0), axis=-1, keepdims=True)
        dest = jnp.where(lane == kq, dk.astype(jnp.int32), dest)
    dest_ref[...] = dest
    carry_ref[...] += jnp.sum(onehot, axis=0, keepdims=True)


def _route(route, pstart, bm=256):
    t = route.shape[0]
    i = jnp.arange(bm)
    tri = (i[None, :] < i[:, None]).astype(BF16)
    return pl.pallas_call(
        _route_kernel,
        grid=(t // bm,),
        in_specs=[pl.BlockSpec((bm, LANES), lambda i: (i, 0)),
                  pl.BlockSpec((1, LANES), lambda i: (0, 0)),
                  pl.BlockSpec((bm, bm), lambda i: (0, 0))],
        out_specs=pl.BlockSpec((bm, LANES), lambda i: (i, 0)),
        out_shape=jax.ShapeDtypeStruct((t, LANES), jnp.int32),
        scratch_shapes=[pltpu.VMEM((1, LANES), F32)],
        compiler_params=_cparams(("arbitrary",)),
        name="route",
    )(route, pstart, tri)


def _row_copy(src_ref, src_row, dst_ref, dst_row, sem):
    return pltpu.make_async_copy(src_ref.at[pl.ds(src_row, 1)], dst_ref.at[pl.ds(dst_row, 1)], sem)


def _dispatch_kernel(last_ref, dest_ref, h_ref, xs_ref, zero_ref, sem, zsem, *, bm):
    @pl.when(pl.program_id(0) == 0)
    def _():
        zero_ref[...] = jnp.zeros_like(zero_ref)

        def zero_copy(e):
            row = pl.multiple_of(last_ref[e] * EXPERT_ROWS, EXPERT_ROWS)
            return pltpu.make_async_copy(zero_ref, xs_ref.at[pl.ds(row, EXPERT_ROWS)], zsem)

        def z_issue(e, carry):
            @pl.when(last_ref[e] >= 0)
            def _():
                zero_copy(e).start()
            return carry

        def z_drain(e, carry):
            @pl.when(last_ref[e] >= 0)
            def _():
                zero_copy(e).wait()
            return carry

        lax.fori_loop(0, 2 * N_EXPERTS, z_issue, 0)
        lax.fori_loop(0, 2 * N_EXPERTS, z_drain, 0)

    def issue(j, carry):
        for kq in range(TOP_K):
            _row_copy(h_ref, j, xs_ref, dest_ref[j * TOP_K + kq], sem).start()
        return carry

    lax.fori_loop(0, bm, issue, 0)

    def drain(j, carry):
        for kq in range(TOP_K):
            _row_copy(h_ref, j, xs_ref, dest_ref[j * TOP_K + kq], sem).wait()
        return carry

    lax.fori_loop(0, bm, drain, 0)


def _dispatch(last_blk, dest_flat, h2, n_slots, bm=256):
    t, d = h2.shape
    grid_spec = pltpu.PrefetchScalarGridSpec(
        num_scalar_prefetch=1,
        grid=(t // bm,),
        in_specs=[pl.BlockSpec((bm * TOP_K,), lambda i, lb: (i,), memory_space=pltpu.SMEM),
                  pl.BlockSpec((bm, d), lambda i, lb: (i, 0))],
        out_specs=pl.BlockSpec(memory_space=pl.ANY),
        scratch_shapes=[pltpu.VMEM((EXPERT_ROWS, d), h2.dtype), pltpu.SemaphoreType.DMA, pltpu.SemaphoreType.DMA],
    )
    return pl.pallas_call(
        functools.partial(_dispatch_kernel, bm=bm),
        grid_spec=grid_spec,
        out_shape=jax.ShapeDtypeStruct((n_slots, d), h2.dtype),
        compiler_params=_cparams(("arbitrary",)),
        name="dispatch",
    )(last_blk, dest_flat, h2)


FF_CHUNK = 256
EXPERT_ROWS = 256


def _expert_kernel(be_ref, par_ref, nxt_ref, nu_ref, xs_ref, w1_ref, b1_ref, w2_ref, b2_ref, sel_ref, ys_ref,
                   w1f_ref, w2f_ref, w1p_ref, w2b_ref, act_ref, sem):
    i = pl.program_id(0)
    e = be_ref[i]
    slot = par_ref[i]
    fresh = jnp.logical_or(i == 0, be_ref[jnp.maximum(i - 1, 0)] != e)
    used = i < nu_ref[0]
    d_ff2 = w1_ref.shape[2]
    half = FF_CHUNK // 2

    def weight_copies(expert, buf):
        return (pltpu.make_async_copy(w1_ref.at[expert], w1f_ref.at[buf], sem.at[0, buf]),
                pltpu.make_async_copy(w2_ref.at[expert], w2f_ref.at[buf], sem.at[1, buf]))

    @pl.when(i == 0)
    def _():
        for cp in weight_copies(e, slot):
            cp.start()

    @pl.when(jnp.logical_and(fresh, used))
    def _():
        for cp in weight_copies(e, slot):
            cp.wait()
        nxt = nxt_ref[i]

        @pl.when(nxt >= 0)
        def _():
            for cp in weight_copies(nxt, 1 - slot):
                cp.start()

        for c in range(d_ff2 // FF_CHUNK):
            sl = slice(c * FF_CHUNK, (c + 1) * FF_CHUNK)
            w1p_ref[:, sl] = _dot(w1f_ref[slot, :, sl].astype(BF16), sel_ref[...]).astype(BF16)
        w2b_ref[...] = w2f_ref[slot].astype(BF16)

    @pl.when(used)
    def _():
        x = xs_ref[...].astype(BF16)
        for c in range(d_ff2 // (2 * FF_CHUNK)):
            sl = slice(2 * c * FF_CHUNK, 2 * (c + 1) * FF_CHUNK)
            hb = _dot(x, w1p_ref[:, sl]) + b1_ref[0, :, sl]
            for j in range(2):
                x_glu = jnp.minimum(hb[:, j * FF_CHUNK:j * FF_CHUNK + half], SWIGLU_LIMIT)
                x_lin = jnp.clip(hb[:, j * FF_CHUNK + half:(j + 1) * FF_CHUNK], -SWIGLU_LIMIT, SWIGLU_LIMIT)
                act = x_glu * _sigmoid(SWIGLU_ALPHA * x_glu) * (x_lin + 1.0)
                act_ref[:, (2 * c + j) * half:(2 * c + j + 1) * half] = act.astype(BF16)
        ys_ref[...] = _dot(act_ref[...], w2b_ref[...]) + b2_ref[0]

    @pl.when(jnp.logical_not(used))
    def _():
        ys_ref[...] = jnp.zeros_like(ys_ref)


def _chunk_deinterleave(a):
    lead = a.shape[:-1]
    a = a.reshape(lead + (a.shape[-1] // FF_CHUNK, FF_CHUNK // 2, 2))
    return jnp.swapaxes(a, -1, -2).reshape(lead + (-1,))


def _experts(block_e, parity, next_e, n_used, xs, w1, b1, w2, b2):
    n_slots, d = xs.shape
    n_e, _, d_ff2 = w1.shape
    n_blocks = n_slots // EXPERT_ROWS
    i = jnp.arange(FF_CHUNK)
    src = jnp.where(i < FF_CHUNK // 2, 2 * i, 2 * (i - FF_CHUNK // 2) + 1)
    sel = (jnp.arange(FF_CHUNK)[:, None] == src[None, :]).astype(BF16)
    b1p = _chunk_deinterleave(b1).reshape(n_e, 1, d_ff2)
    imap = lambda i, be, par, nxt, nu: (i, 0)
    xmap = lambda i, be, par, nxt, nu: (jnp.minimum(i, nu[0] - 1), 0)
    emap = lambda i, be, par, nxt, nu: (be[i], 0, 0)
    grid_spec = pltpu.PrefetchScalarGridSpec(
        num_scalar_prefetch=4,
        grid=(n_blocks,),
        in_specs=[pl.BlockSpec((EXPERT_ROWS, d), xmap),
                  pl.BlockSpec(memory_space=pl.ANY),
                  pl.BlockSpec((1, 1, d_ff2), emap),
                  pl.BlockSpec(memory_space=pl.ANY),
                  pl.BlockSpec((1, 1, d), emap),
                  pl.BlockSpec((FF_CHUNK, FF_CHUNK), lambda i, be, par, nxt, nu: (0, 0))],
        out_specs=pl.BlockSpec((EXPERT_ROWS, d), imap),
        scratch_shapes=[pltpu.VMEM((2, d, d_ff2), F32), pltpu.VMEM((2, d_ff2 // 2, d), F32),
                        pltpu.VMEM((d, d_ff2), BF16), pltpu.VMEM((d_ff2 // 2, d), BF16),
                        pltpu.VMEM((EXPERT_ROWS, d_ff2 // 2), BF16),
                        pltpu.SemaphoreType.DMA((2, 2))],
    )
    return pl.pallas_call(
        _expert_kernel,
        grid_spec=grid_spec,
        out_shape=jax.ShapeDtypeStruct((n_slots, d), F32),
        compiler_params=_cparams(("arbitrary",), vmem=EXPERT_VMEM_LIMIT),
        name="experts",
    )(block_e, parity, next_e, n_used, xs, w1, b1p, w2, b2.reshape(n_e, 1, d), sel)


def _combine_kernel(dest_ref, x1_ref, route_ref, ys_ref, o_ref, buf_ref, sem, *, bm):
    def issue(j, carry):
        for kq in range(TOP_K):
            _row_copy(ys_ref, dest_ref[j * TOP_K + kq], buf_ref.at[kq], j, sem).start()
        return carry

    lax.fori_loop(0, bm, issue, 0)

    def drain(j, carry):
        for kq in range(TOP_K):
            _row_copy(ys_ref, dest_ref[j * TOP_K + kq], buf_ref.at[kq], j, sem).wait()
        return carry

    lax.fori_loop(0, bm, drain, 0)
    acc = x1_ref[...]
    for kq in range(TOP_K):
        acc = acc + route_ref[:, TOP_K + kq:TOP_K + kq + 1] * buf_ref[kq]
    o_ref[...] = acc


def _combine(dest_flat, x1, route, ys, bm=256):
    t, d = x1.shape
    return pl.pallas_call(
        functools.partial(_combine_kernel, bm=bm),
        grid=(t // bm,),
        in_specs=[pl.BlockSpec((bm * TOP_K,), lambda i: (i,), memory_space=pltpu.SMEM),
                  pl.BlockSpec((bm, d), lambda i: (i, 0)),
                  pl.BlockSpec((bm, LANES), lambda i: (i, 0)),
                  pl.BlockSpec(memory_space=pl.ANY)],
        out_specs=pl.BlockSpec((bm, d), lambda i: (i, 0)),
        out_shape=jax.ShapeDtypeStruct((t, d), F32),
        scratch_shapes=[pltpu.VMEM((TOP_K, bm, d), F32), pltpu.SemaphoreType.DMA],
        compiler_params=_cparams(("arbitrary",)),
        name="combine",
    )(dest_flat, x1, route, ys)


def _layer(x, mem, ln1, w_in, token_mu, rwkv_w0, rwkv_w2, rwkv_a0, rwkv_a2, rwkv_g2,
           rwkv_k_k, rwkv_k_a, rwkv_r_k, rwkv_ln_w, rwkv_ln_b, dil_q_norm, dil_k_norm,
           ln_mem, w_mem_kv, mem_q_norm, mem_k_norm, p_rwkv, p_dil, p_mem, w_out,
           ln2, w_router, b_router, w1, b1, w2, b2):
    b, s, d = x.shape
    t = b * s
    x2 = x.reshape(t, d)
    zr, dil_qkv, mq, gates = _in_proj(x2, ln1, w_in, dil_q_norm, dil_k_norm, mem_q_norm)
    y_rwkv = _rwkv(zr.reshape(b, s, RWKV_IN), token_mu, rwkv_w0, rwkv_w2, rwkv_a0, rwkv_a2, rwkv_g2,
                   rwkv_k_k, rwkv_k_a, rwkv_r_k.reshape(-1), rwkv_ln_w, rwkv_ln_b)
    outs, lses = [], []
    for g, (_, dilation) in enumerate(DIL_PATTERNS):
        o, l = _dil_group(*dil_qkv[3 * g:3 * g + 3], b, dilation)
        outs.append(o)
        lses.append(l)
    mk, mv = _mem_kv(mem, ln_mem, w_mem_kv, mem_k_norm)
    y_mem = _mem_attn(mq.reshape(b, s, MEM_DIM), mk, mv)
    x1, h2, route, counts = _mix(x2, y_rwkv.reshape(t, RWKV_DIM), outs, lses, y_mem.reshape(t, MEM_DIM), gates,
                                 p_rwkv, p_dil, p_mem, w_out, ln2, w_router, b_router)

    counts = counts[0, :N_EXPERTS].astype(jnp.int32)
    nblk = (counts + EXPERT_ROWS - 1) // EXPERT_ROWS
    bend = jnp.cumsum(nblk)
    pstart = ((bend - nblk) * EXPERT_ROWS).astype(F32)
    pstart = jnp.zeros((1, LANES), F32).at[0, :N_EXPERTS].set(pstart)
    n_blocks = (t * TOP_K) // EXPERT_ROWS + N_EXPERTS
    block_e = jnp.sum(bend[None, :] <= jnp.arange(n_blocks, dtype=jnp.int32)[:, None], axis=1)
    block_e = jnp.minimum(block_e, N_EXPERTS - 1).astype(jnp.int32)
    n_used = bend[-1:].astype(jnp.int32)
    used = nblk > 0
    eids = jnp.arange(N_EXPERTS, dtype=jnp.int32)
    ordinal = jnp.cumsum(used.astype(jnp.int32)) - 1
    later = jnp.where(used[None, :] & (eids[None, :] > eids[:, None]), eids[None, :], N_EXPERTS)
    next_used = jnp.min(later, axis=1)
    next_used = jnp.where(next_used < N_EXPERTS, next_used, -1).astype(jnp.int32)
    parity = (ordinal[block_e] % 2).astype(jnp.int32)
    next_e = next_used[block_e]
    tail = n_used[0] + eids
    last_blk = jnp.concatenate([jnp.where(used, bend - 1, -1), jnp.where(tail < n_blocks, tail, -1)]).astype(jnp.int32)

    dest = _route(route, pstart)[:, :TOP_K].reshape(-1)
    xs = _dispatch(last_blk, dest, h2, n_blocks * EXPERT_ROWS)
    ys = _experts(block_e, parity, next_e, n_used, xs, w1, b1, w2, b2)
    out = _combine(dest, x1, route, ys)
    return out.reshape(b, s, d)


def kernel(x, mem, ln1, w_in, token_mu, rwkv_w0, rwkv_w2, rwkv_a0, rwkv_a2, rwkv_g2, rwkv_k_k, rwkv_k_a, rwkv_r_k, rwkv_ln_w, rwkv_ln_b, dil_q_norm, dil_k_norm, ln_mem, w_mem_kv, mem_q_norm, mem_k_norm, p_rwkv, p_dil, p_mem, w_out, ln2, w_router, b_router, w1, b1, w2, b2):
    params = (ln1, w_in, token_mu, rwkv_w0, rwkv_w2, rwkv_a0, rwkv_a2, rwkv_g2, rwkv_k_k, rwkv_k_a,
              rwkv_r_k, rwkv_ln_w, rwkv_ln_b, dil_q_norm, dil_k_norm, ln_mem, w_mem_kv, mem_q_norm,
              mem_k_norm, p_rwkv, p_dil, p_mem, w_out, ln2, w_router, b_router, w1, b1, w2, b2)
    for l in range(ln1.shape[0]):
        x = _layer(x, mem, *[p[l] for p in params])
    return x
```

```python
import functools

import jax
import jax.numpy as jnp
from jax import lax
from jax.experimental import pallas as pl
from jax.experimental.pallas import tpu as pltpu

F32 = jnp.float32
BF16 = jnp.bfloat16

NORM_EPS = 1e-5
HEAD_DIM = 64
RWKV_DIM = 512
RWKV_IN = 1792
GN_EPS = HEAD_DIM * 1e-5
DIL_DIM = 768
DIL_GROUP_DIM = 256
DIL_PATTERNS = ((128, 1), (512, 4), (2048, 16))
ATTN_BLOCK = 128
MEM_DIM = 512
MEM_HEAD_DIM = 128
N_EXPERTS = 32
TOP_K = 4
SWIGLU_ALPHA = 1.702
SWIGLU_LIMIT = 7.0
LANES = 128
CHUNK = 64
NEG_BIG = -1e30
VMEM_LIMIT = 48 * 1024 * 1024
EXPERT_VMEM_LIMIT = 56 * 1024 * 1024


def _dot(a, b):
    return jnp.dot(a, b, preferred_element_type=F32)


def _dot_nt(a, b):
    return lax.dot_general(a, b, (((1,), (1,)), ((), ())), preferred_element_type=F32)


def _split2(x):
    hi = x.astype(BF16)
    lo = (x - hi.astype(F32)).astype(BF16)
    return hi, lo


def _split3(x):
    hi = x.astype(BF16)
    r = x - hi.astype(F32)
    lo = r.astype(BF16)
    lo2 = (r - lo.astype(F32)).astype(BF16)
    return hi, lo, lo2


def _dot_rhs3(w_bf16, x):
    hi, lo, lo2 = _split3(x)
    return _dot(w_bf16, hi) + _dot(w_bf16, lo) + _dot(w_bf16, lo2)


def _dot_x3(a, b):
    ah, al = _split2(a)
    bh, bl = _split2(b)
    return _dot(ah, bh) + _dot(al, bh) + _dot(ah, bl)


def _sigmoid(x):
    return 1.0 / (1.0 + jnp.exp(-x))


def _cparams(sem, vmem=VMEM_LIMIT):
    return pltpu.CompilerParams(dimension_semantics=sem, vmem_limit_bytes=vmem)


IN_CHUNK = 256
PERM_ROWS = 256
N_ZR = RWKV_IN // IN_CHUNK
N_DIL = 3 * DIL_DIM // IN_CHUNK
N_MQ = MEM_DIM // IN_CHUNK
N_GATE = 3 * 1024 // IN_CHUNK


def _in_proj_kernel(x_ref, ln_ref, w_ref, qg_ref, kg_ref, mg_ref, p4_ref, p16_ref,
                    zr_ref, *rest, bm):
    dil_refs, (mq_ref, gt_ref) = rest[:N_DIL], rest[N_DIL:]
    perm_refs = (None, p4_ref, p16_ref)
    x = x_ref[...]
    h = x * lax.rsqrt(jnp.mean(x * x, axis=-1, keepdims=True) + NORM_EPS) * ln_ref[...]
    hb = h.astype(BF16)

    lane = lax.broadcasted_iota(jnp.int32, (x.shape[0], LANES), 1)

    def seg_rms(z, seg, gain):
        z2 = z * z
        cols = []
        for cb in range(IN_CHUNK // LANES):
            blk = z2[:, cb * LANES:(cb + 1) * LANES]
            ss = jnp.zeros_like(blk)
            for sg in range(LANES // seg):
                m = (lane >= sg * seg) & (lane < (sg + 1) * seg)
                ss = jnp.where(m, jnp.sum(jnp.where(m, blk, 0.0), axis=-1, keepdims=True), ss)
            cols.append(ss)
        ss = jnp.concatenate(cols, axis=1)
        return z * lax.rsqrt(ss * (1.0 / seg) + NORM_EPS) * gain

    for c in range(N_ZR + N_DIL + N_MQ + N_GATE):
        z = _dot(hb, w_ref[:, c * IN_CHUNK:(c + 1) * IN_CHUNK])
        if c < N_ZR:
            zr_ref[:, c * IN_CHUNK:(c + 1) * IN_CHUNK] = z
            continue
        d = c - N_ZR
        if d < N_DIL:
            which, g = divmod(d, 3)
            if which == 0:
                z = seg_rms(z, HEAD_DIM, qg_ref[...])
            elif which == 1:
                z = seg_rms(z, HEAD_DIM, kg_ref[...])
            zb = z.astype(BF16)
            o_ref = dil_refs[g * 3 + which]
            dilation = DIL_PATTERNS[g][1]
            if dilation == 1:
                o_ref[...] = zb
            else:
                rows = PERM_ROWS // dilation
                for sub in range(bm // PERM_ROWS):
                    zp = _dot(perm_refs[g][...], zb[sub * PERM_ROWS:(sub + 1) * PERM_ROWS, :]).astype(BF16)
                    for r in range(dilation):
                        o_ref[sub * rows:(sub + 1) * rows, r * IN_CHUNK:(r + 1) * IN_CHUNK] = zp[r * rows:(r + 1) * rows, :]
            continue
        d -= N_DIL
        if d < N_MQ:
            mq_ref[:, d * IN_CHUNK:(d + 1) * IN_CHUNK] = seg_rms(z, MEM_HEAD_DIM, mg_ref[...]).astype(BF16)
            continue
        d -= N_MQ
        gt_ref[:, d * IN_CHUNK:(d + 1) * IN_CHUNK] = _sigmoid(z).astype(BF16)


def _class_perm(bm, dilation):
    i = jnp.arange(bm)
    src = (i % (bm // dilation)) * dilation + i // (bm // dilation)
    return (src[:, None] == i[None, :]).astype(BF16)


def _in_proj(x2, ln1, w_in, dil_q_norm, dil_k_norm, mem_q_norm, bm=512):
    t, d = x2.shape
    n_cols = w_in.shape[1]
    wb = w_in.astype(BF16)
    qg = (jnp.tile(dil_q_norm, IN_CHUNK // HEAD_DIM) * (HEAD_DIM ** -0.5)).reshape(1, IN_CHUNK)
    kg = jnp.tile(dil_k_norm, IN_CHUNK // HEAD_DIM).reshape(1, IN_CHUNK)
    mg = jnp.tile(mem_q_norm, IN_CHUNK // MEM_HEAD_DIM).reshape(1, IN_CHUNK)
    p4 = _class_perm(PERM_ROWS, DIL_PATTERNS[1][1])
    p16 = _class_perm(PERM_ROWS, DIL_PATTERNS[2][1])
    row = lambda w: pl.BlockSpec((bm, w), lambda i: (i, 0))
    const = lambda a: pl.BlockSpec(a.shape, lambda i: (0,) * a.ndim)
    dil_specs, dil_shapes = [], []
    for _, dilation in DIL_PATTERNS:
        for _ in range(3):
            dil_specs.append(pl.BlockSpec((bm // dilation, dilation * DIL_GROUP_DIM), lambda i: (i, 0)))
            dil_shapes.append(jax.ShapeDtypeStruct((t // dilation, dilation * DIL_GROUP_DIM), BF16))
    outs = pl.pallas_call(
        functools.partial(_in_proj_kernel, bm=bm),
        grid=(t // bm,),
        in_specs=[row(d), pl.BlockSpec((1, d), lambda i: (0, 0)),
                  pl.BlockSpec((d, n_cols), lambda i: (0, 0), pipeline_mode=pl.Buffered(1)),
                  const(qg), const(kg), const(mg), const(p4), const(p16)],
        out_specs=[row(RWKV_IN)] + dil_specs + [row(MEM_DIM), row(3 * d)],
        out_shape=[jax.ShapeDtypeStruct((t, RWKV_IN), F32)] + dil_shapes
                  + [jax.ShapeDtypeStruct((t, MEM_DIM), BF16), jax.ShapeDtypeStruct((t, 3 * d), BF16)],
        compiler_params=_cparams(("parallel",)),
        name="in_proj",
    )(x2, ln1.reshape(1, d), wb, qg, kg, mg, p4, p16)
    return outs[0], outs[1:1 + N_DIL], outs[1 + N_DIL], outs[2 + N_DIL]


def _rwkv_kernel(z_ref, mu_ref, w0_ref, a0_ref, wl_ref, kk_ref, ka_ref, rk_ref,
                 lnw_ref, lnb_ref, tri_ref, y_ref, carry_ref, state_ref, ybuf_ref, *, tt):
    s_idx = pl.program_id(1)

    @pl.when(s_idx == 0)
    def _():
        carry_ref[...] = jnp.zeros_like(carry_ref)
        state_ref[...] = jnp.zeros_like(state_ref)

    z = z_ref[0]
    rows = lax.broadcasted_iota(jnp.int32, z.shape, 0)
    prev = jnp.where(rows == 0, carry_ref[...], pltpu.roll(z, 1, axis=0))
    carry_ref[...] = z[tt - 1:tt, :]
    zl = z + (prev - z) * mu_ref[...]
    r = zl[:, 0:RWKV_DIM]
    k = zl[:, RWKV_DIM:2 * RWKV_DIM]
    v = zl[:, 2 * RWKV_DIM:3 * RWKV_DIM]
    zlo = zl[:, 3 * RWKV_DIM:RWKV_IN]
    lcol = lax.broadcasted_iota(jnp.int32, zlo.shape, 1)
    feat = jnp.where(lcol < 64, jnp.tanh(zlo), jnp.where(lcol < 128, zlo, _sigmoid(zlo)))
    lora = _dot(feat.astype(BF16), wl_ref[...])
    nu = -(w0_ref[...] + lora[:, 0:RWKV_DIM])
    softplus = jnp.maximum(nu, 0.0) + jnp.log(1.0 + jnp.exp(-jnp.abs(nu)))
    lw = -jnp.exp(-softplus - 0.5)
    a = _sigmoid(a0_ref[...] + lora[:, RWKV_DIM:2 * RWKV_DIM])
    g = lora[:, 2 * RWKV_DIM:3 * RWKV_DIM]

    n_pairs = RWKV_DIM // LANES
    n_chunks = tt // CHUNK
    head_lo = lax.broadcasted_iota(jnp.int32, (tt, LANES), 1) < HEAD_DIM

    def seg_sum(x):
        parts = []
        for p in range(n_pairs):
            xp = x[:, p * LANES:(p + 1) * LANES]
            lo = jnp.sum(jnp.where(head_lo, xp, 0.0), axis=-1, keepdims=True)
            hi = jnp.sum(jnp.where(head_lo, 0.0, xp), axis=-1, keepdims=True)
            parts.append(jnp.where(head_lo, lo, hi))
        return jnp.concatenate(parts, axis=1)

    kk = k * kk_ref[...]
    kk = kk / jnp.maximum(jnp.sqrt(seg_sum(kk * kk)), 1e-12)
    k2 = k * (1.0 + (a - 1.0) * ka_ref[...])
    a_s = -kk
    b_s = kk * a

    cum = _dot_rhs3(tri_ref[...], lw)
    tot = jnp.concatenate(
        [jnp.broadcast_to(cum[(c + 1) * CHUNK - 1:(c + 1) * CHUNK, :], (CHUNK, RWKV_DIM)) for c in range(n_chunks)],
        axis=0)
    e_neg = jnp.exp(-cum)
    e_end = jnp.exp(tot - cum)
    at = (a_s * jnp.exp(cum - lw)).astype(BF16)
    rt = (r * jnp.exp(cum)).astype(BF16)
    bt = (b_s * e_neg).astype(BF16)
    kt = (k2 * e_neg).astype(BF16)
    be = (b_s * e_end).astype(BF16)
    ke = (k2 * e_end).astype(BF16)
    vb = v.astype(BF16)
    e_tot = jnp.exp(tot)

    r128 = lax.broadcasted_iota(jnp.int32, (LANES, LANES), 0)
    c128 = lax.broadcasted_iota(jnp.int32, (LANES, LANES), 1)
    lane_lo = lax.broadcasted_iota(jnp.int32, (CHUNK, LANES), 1) < HEAD_DIM
    stril = c128 < r128
    tril = c128 <= r128
    eye = (c128 == r128).astype(F32)
    same16 = (r128 // 16) == (c128 // 16)
    same32 = (r128 // 32) == (c128 // 32)
    off16 = same32 & jnp.logical_not(same16)
    off32 = jnp.logical_not(same32)

    def bdiag(xp):
        zero = jnp.zeros_like(xp)
        return jnp.concatenate([jnp.where(lane_lo, xp, zero), jnp.where(lane_lo, zero, xp)], axis=0)

    units = [(ci, p) for ci in range(n_chunks) for p in range(n_pairs)]
    pick = lambda arr, u: arr[u[0] * CHUNK:(u[0] + 1) * CHUNK, u[1] * LANES:(u[1] + 1) * LANES]
    cat0 = lambda xs: jnp.concatenate(xs, axis=0)
    cat1 = lambda xs: jnp.concatenate(xs, axis=1)
    zero_b = jnp.zeros((LANES, LANES), BF16)

    at_b = [bdiag(pick(at, u)) for u in units]
    v_b = [bdiag(pick(vb, u)) for u in units]
    be_b = [bdiag(pick(be, u)) for u in units]
    ke_b = [bdiag(pick(ke, u)) for u in units]
    rt_b = [bdiag(pick(rt, u)) for u in units]
    a_ab, a_ak, a_rbk = [], [], []
    for i, u in enumerate(units):
        mq = _dot_nt(cat0([at_b[i], rt_b[i]]), cat0([bdiag(pick(bt, u)), bdiag(pick(kt, u))]))
        a_ab.append(jnp.where(stril, mq[:LANES, :LANES], 0.0).astype(BF16))
        a_ak.append(jnp.where(stril, mq[:LANES, LANES:], 0.0).astype(BF16))
        a_rbk.append(cat1([jnp.where(tril, mq[LANES:, :LANES], 0.0), jnp.where(tril, mq[LANES:, LANES:], 0.0)]).astype(BF16))
    w_b = [_dot(a_ak[i], v_b[i]).astype(BF16) for i in range(len(units))]

    d1 = [jnp.where(same16, x, jnp.zeros_like(x)) for x in a_ab]
    xs = [eye + d.astype(F32) for d in d1]
    d2 = [_dot(d, d).astype(BF16) for d in d1]
    t_ = [_dot(d2[i], cat1([xs[i].astype(BF16), d2[i]])) for i in range(len(units))]
    xs = [xs[i] + t_[i][:, :LANES] for i in range(len(units))]
    d4 = [t[:, LANES:].astype(BF16) for t in t_]
    t_ = [_dot(d4[i], cat1([xs[i].astype(BF16), d4[i]])) for i in range(len(units))]
    xs = [xs[i] + t_[i][:, :LANES] for i in range(len(units))]
    d8 = [t[:, LANES:].astype(BF16) for t in t_]
    xs = [xs[i] + _dot(d8[i], xs[i].astype(BF16)) for i in range(len(units))]
    for off in (off16, off32):
        xb = [x.astype(BF16) for x in xs]
        g_ = [_dot(jnp.where(off, a_ab[i], jnp.zeros_like(a_ab[i])), xb[i]).astype(BF16) for i in range(len(units))]
        xs = [xs[i] + _dot(xb[i], g_[i]) for i in range(len(units))]

    pq = [_dot(xs[i].astype(BF16), cat1([at_b[i], w_b[i]])) for i in range(len(units))]
    ry = [_dot(a_rbk[i], cat0([pq[i].astype(BF16), cat1([zero_b, v_b[i]])])) for i in range(len(units))]
    r2 = [(rt_b[i].astype(F32) + ry[i][:, :LANES]).astype(BF16) for i in range(len(units))]
    m_c = [_dot(pq[i][:, :LANES].T.astype(BF16), be_b[i]) for i in range(len(units))]
    n_c = [_dot(cat1([pq[i][:, LANES:].T.astype(BF16), v_b[i].astype(F32).T.astype(BF16)]), cat0([be_b[i], ke_b[i]]))
           for i in range(len(units))]

    state = [state_ref[p] for p in range(n_pairs)]
    for i, (ci, p) in enumerate(units):
        s0 = state[p]
        s_hi, s_lo = _split2(s0)
        y = _dot_nt(r2[i], s_hi) + ry[i][:, LANES:]
        mcb = m_c[i].astype(BF16)
        e_row = e_tot[ci * CHUNK:ci * CHUNK + 1, p * LANES:(p + 1) * LANES]
        state[p] = s0 * e_row + (_dot(s_hi, mcb) + _dot(s_lo, mcb)) + n_c[i]
        ybuf_ref[ci * CHUNK:(ci + 1) * CHUNK, p * LANES:(p + 1) * LANES] = jnp.where(lane_lo, y[:CHUNK], y[CHUNK:])
    for p in range(n_pairs):
        state_ref[p] = state[p]

    y = ybuf_ref[...]
    mean = seg_sum(y) * (1.0 / HEAD_DIM)
    yc = y - mean
    var = seg_sum(yc * yc) * (1.0 / HEAD_DIM)
    yn = yc * lax.rsqrt(var + GN_EPS) * lnw_ref[...] + lnb_ref[...]
    bonus = seg_sum(r * k2 * rk_ref[...]) * v
    y_ref[0] = ((yn + bonus) * g).astype(y_ref.dtype)


def _rwkv(zr, token_mu, w0, w2, a0, a2, g2, k_k, k_a, r_k, ln_w, ln_b, tt=256):
    b, s, _ = zr.shape
    i = jnp.arange(tt)
    same = (i[:, None] // CHUNK) == (i[None, :] // CHUNK)
    tri = (same & (i[None, :] <= i[:, None])).astype(BF16)
    vec = lambda a: a.reshape(1, -1)
    wl = jnp.zeros((RWKV_IN - 3 * RWKV_DIM, 3 * RWKV_DIM), F32)
    wl = wl.at[0:64, 0:RWKV_DIM].set(w2).at[64:128, RWKV_DIM:2 * RWKV_DIM].set(a2)
    wl = wl.at[128:256, 2 * RWKV_DIM:3 * RWKV_DIM].set(g2).astype(BF16)
    params = [vec(token_mu), vec(w0), vec(a0), wl, vec(k_k), vec(k_a), vec(r_k),
              vec(ln_w), vec(ln_b), tri]
    const = lambda a: pl.BlockSpec(a.shape, lambda bi, si: (0,) * a.ndim)
    return pl.pallas_call(
        functools.partial(_rwkv_kernel, tt=tt),
        grid=(b, s // tt),
        in_specs=[pl.BlockSpec((1, tt, RWKV_IN), lambda bi, si: (bi, si, 0))] + [const(a) for a in params],
        out_specs=pl.BlockSpec((1, tt, RWKV_DIM), lambda bi, si: (bi, si, 0)),
        out_shape=jax.ShapeDtypeStruct((b, s, RWKV_DIM), BF16),
        scratch_shapes=[pltpu.VMEM((1, RWKV_IN), F32),
                        pltpu.VMEM((RWKV_DIM // LANES, LANES, LANES), F32),
                        pltpu.VMEM((tt, RWKV_DIM), F32)],
        compiler_params=_cparams(("parallel", "arbitrary")),
        name="rwkv",
    )(zr, *params)


DIL_UNITS = 4


def _dil_kernel(q_ref, k_ref, v_ref, o_ref, l_ref, *, dilation, cps, nbs):
    cg = pl.program_id(1)
    jb = pl.program_id(2)
    bq = ATTN_BLOCK
    n_heads = DIL_GROUP_DIM // HEAD_DIM
    qi = lax.broadcasted_iota(jnp.int32, (bq, 2 * bq), 0)
    kj = lax.broadcasted_iota(jnp.int32, (bq, 2 * bq), 1)
    lane = lax.broadcasted_iota(jnp.int32, (bq, DIL_GROUP_DIM), 1)
    head_masks = [(lane >= h * HEAD_DIM) & (lane < (h + 1) * HEAD_DIM) for h in range(n_heads)]
    units = [(c, bb) for c in range(cps) for bb in range(nbs)]

    qs, kcats, vcats, masks, starts = [], [], [], [], []
    for c, bb in units:
        i = jb * nbs + bb
        cols = slice(c * DIL_GROUP_DIM, (c + 1) * DIL_GROUP_DIM)
        p0 = pl.multiple_of(jnp.maximum(i - 1, 0) * bq, bq)
        c0 = pl.multiple_of(i * bq, bq)
        qs.append(q_ref[0, bb * bq:(bb + 1) * bq, cols])
        kcats.append(jnp.concatenate([k_ref[0, pl.ds(p0, bq), cols], k_ref[0, pl.ds(c0, bq), cols]], axis=0))
        vcats.append(jnp.concatenate([v_ref[0, pl.ds(p0, bq), cols], v_ref[0, pl.ds(c0, bq), cols]], axis=0))
        first = (1 - jnp.minimum(i, 1)) * (2 * bq)
        masks.append(((kj < bq) & (kj >= qi + first)) | ((kj >= bq) & ((kj - bq) <= qi)))
        starts.append((cg * cps + c) + dilation * bq * i)
    pairs = [(u, h) for u in range(len(units)) for h in range(n_heads)]
    sc = [jnp.where(masks[u], _dot_nt(jnp.where(head_masks[h], qs[u], jnp.zeros_like(qs[u])), kcats[u]), NEG_BIG)
          for u, h in pairs]
    mx = [jnp.max(x, axis=-1, keepdims=True) for x in sc]
    pr = [jnp.exp(sc[n] - mx[n]) for n in range(len(pairs))]
    den = [jnp.sum(x, axis=-1, keepdims=True) for x in pr]
    ov = [_dot(pr[n].astype(BF16), vcats[pairs[n][0]]) / den[n] for n in range(len(pairs))]
    for u in range(len(units)):
        acc = jnp.zeros((bq, DIL_GROUP_DIM), F32)
        lacc = jnp.zeros((bq, DIL_GROUP_DIM), F32)
        for h in range(n_heads):
            n = u * n_heads + h
            acc = jnp.where(head_masks[h], ov[n], acc)
            lacc = jnp.where(head_masks[h], mx[n] + jnp.log(den[n]), lacc)
        rows = pl.ds(starts[u], bq, stride=dilation) if dilation > 1 else pl.ds(pl.multiple_of(starts[u], bq), bq)
        for half in range(DIL_GROUP_DIM // LANES):
            o_ref[0, half, rows, :] = acc[:, half * LANES:(half + 1) * LANES]
            l_ref[0, half, rows, :] = lacc[:, half * LANES:(half + 1) * LANES]


def _dil_group(q, k, v, b, dilation):
    n = q.shape[0] // b
    s = n * dilation
    nb = n // ATTN_BLOCK
    nbs = min(nb, DIL_UNITS)
    cps = DIL_UNITS // nbs
    halves = DIL_GROUP_DIM // LANES
    view = lambda a: a.reshape(b, n, dilation * DIL_GROUP_DIM)
    qmap = lambda bi, ci, ji: (bi, ji, ci)
    kmap = lambda bi, ci, ji: (bi, 0, ci)
    omap = lambda bi, ci, ji: (bi, 0, 0, 0)
    return pl.pallas_call(
        functools.partial(_dil_kernel, dilation=dilation, cps=cps, nbs=nbs),
        grid=(b, dilation // cps, nb // nbs),
        in_specs=[pl.BlockSpec((1, nbs * ATTN_BLOCK, cps * DIL_GROUP_DIM), qmap),
                  pl.BlockSpec((1, n, cps * DIL_GROUP_DIM), kmap),
                  pl.BlockSpec((1, n, cps * DIL_GROUP_DIM), kmap)],
        out_specs=[pl.BlockSpec((1, halves, s, LANES), omap),
                   pl.BlockSpec((1, halves, s, LANES), omap)],
        out_shape=[jax.ShapeDtypeStruct((b, halves, s, LANES), F32),
                   jax.ShapeDtypeStruct((b, halves, s, LANES), F32)],
        compiler_params=_cparams(("parallel", "arbitrary", "arbitrary")),
        name=f"dil_attn_d{dilation}",
    )(view(q), view(k), view(v))


def _mem_kv_kernel(m_ref, ln_ref, w_ref, kn_ref, k_ref, v_ref):
    x = m_ref[0]
    h = x * lax.rsqrt(jnp.mean(x * x, axis=-1, keepdims=True) + NORM_EPS) * ln_ref[...]
    kv = _dot(h.astype(BF16), w_ref[...])
    for hd in range(MEM_DIM // MEM_HEAD_DIM):
        sl = slice(hd * MEM_HEAD_DIM, (hd + 1) * MEM_HEAD_DIM)
        kh = kv[:, sl]
        kh = kh * lax.rsqrt(jnp.mean(kh * kh, axis=-1, keepdims=True) + NORM_EPS) * kn_ref[...]
        k_ref[0, :, sl] = kh.astype(BF16)
    v_ref[0] = kv[:, MEM_DIM:].astype(BF16)


def _mem_kv(mem, ln_mem, w_mem_kv, mem_k_norm):
    b, m, d = mem.shape
    return pl.pallas_call(
        _mem_kv_kernel,
        grid=(b,),
        in_specs=[pl.BlockSpec((1, m, d), lambda i: (i, 0, 0)),
                  pl.BlockSpec((1, d), lambda i: (0, 0)),
                  pl.BlockSpec((d, 2 * MEM_DIM), lambda i: (0, 0)),
                  pl.BlockSpec((1, MEM_HEAD_DIM), lambda i: (0, 0))],
        out_specs=[pl.BlockSpec((1, m, MEM_DIM), lambda i: (i, 0, 0)),
                   pl.BlockSpec((1, m, MEM_DIM), lambda i: (i, 0, 0))],
        out_shape=[jax.ShapeDtypeStruct((b, m, MEM_DIM), BF16),
                   jax.ShapeDtypeStruct((b, m, MEM_DIM), BF16)],
        compiler_params=_cparams(("parallel",)),
        name="mem_kv",
    )(mem, ln_mem.reshape(1, d), w_mem_kv.astype(BF16), mem_k_norm.reshape(1, MEM_HEAD_DIM))


def _mem_attn_kernel(q_ref, k_ref, v_ref, o_ref):
    for hd in range(MEM_DIM // MEM_HEAD_DIM):
        sl = slice(hd * MEM_HEAD_DIM, (hd + 1) * MEM_HEAD_DIM)
        sc = _dot_nt(q_ref[0, :, sl], k_ref[0, :, sl]) * (MEM_HEAD_DIM ** -0.5)
        m = jnp.max(sc, axis=-1, keepdims=True)
        p = jnp.exp(sc - m)
        den = jnp.sum(p, axis=-1, keepdims=True)
        o_ref[0, :, sl] = (_dot((p / den).astype(BF16), v_ref[0, :, sl])).astype(o_ref.dtype)


def _mem_attn(mq, mk, mv, bm=512):
    b, s, _ = mq.shape
    m = mk.shape[1]
    return pl.pallas_call(
        _mem_attn_kernel,
        grid=(b, s // bm),
        in_specs=[pl.BlockSpec((1, bm, MEM_DIM), lambda bi, si: (bi, si, 0)),
                  pl.BlockSpec((1, m, MEM_DIM), lambda bi, si: (bi, 0, 0)),
                  pl.BlockSpec((1, m, MEM_DIM), lambda bi, si: (bi, 0, 0))],
        out_specs=pl.BlockSpec((1, bm, MEM_DIM), lambda bi, si: (bi, si, 0)),
        out_shape=jax.ShapeDtypeStruct((b, s, MEM_DIM), BF16),
        compiler_params=_cparams(("parallel", "parallel")),
        name="mem_attn",
    )(mq, mk, mv)


def _mix_kernel(x_ref, yr_ref, o0_ref, o1_ref, o2_ref, l0_ref, l1_ref, l2_ref, ym_ref, gt_ref,
                pr_ref, pd_ref, pm_ref, wo_ref, ln2_ref, wr_ref, br_ref,
                x1_ref, h2_ref, route_ref, cnt_ref):
    d = x_ref.shape[1]
    wide = lambda ref: jnp.concatenate([ref[0, hf] for hf in range(DIL_GROUP_DIM // LANES)], axis=1)
    l0, l1, l2 = wide(l0_ref), wide(l1_ref), wide(l2_ref)
    m = jnp.maximum(jnp.maximum(l0, l1), l2)
    e0, e1, e2 = jnp.exp(l0 - m), jnp.exp(l1 - m), jnp.exp(l2 - m)
    y_dil = (e0 * wide(o0_ref) + e1 * wide(o1_ref) + e2 * wide(o2_ref)) / (e0 + e1 + e2)
    mixed = (gt_ref[:, 0:d].astype(F32) * _dot(yr_ref[...], pr_ref[...])
             + gt_ref[:, d:2 * d].astype(F32) * _dot(y_dil.astype(BF16), pd_ref[...])
             + gt_ref[:, 2 * d:3 * d].astype(F32) * _dot(ym_ref[...], pm_ref[...]))
    x1 = x_ref[...] + _dot(mixed.astype(BF16), wo_ref[...])
    x1_ref[...] = x1
    h2 = x1 * lax.rsqrt(jnp.mean(x1 * x1, axis=-1, keepdims=True) + NORM_EPS) * ln2_ref[...]
    h2_ref[...] = h2
    logits = _dot_x3(h2, wr_ref[...]) + br_ref[...]
    lane = lax.broadcasted_iota(jnp.int32, logits.shape, 1)
    lane_f = lane.astype(F32)
    route = jnp.zeros(logits.shape, F32)
    onehot = jnp.zeros(logits.shape, F32)
    vals = []
    for kq in range(TOP_K):
        mx = jnp.max(logits, axis=-1, keepdims=True)
        idx = jnp.min(jnp.where(logits == mx, lane_f, float(LANES)), axis=-1, keepdims=True)
        hit = lane_f == idx
        vals.append(mx)
        route = jnp.where(lane == kq, idx, route)
        onehot = jnp.where(hit, 1.0, onehot)
        logits = jnp.where(hit, -jnp.inf, logits)
    ex = [jnp.exp(vq - vals[0]) for vq in vals]
    den = ex[0] + ex[1] + ex[2] + ex[3]
    for kq in range(TOP_K):
        route = jnp.where(lane == TOP_K + kq, ex[kq] / den, route)
    route_ref[...] = route

    @pl.when(pl.program_id(0) == 0)
    def _():
        cnt_ref[...] = jnp.zeros_like(cnt_ref)

    cnt_ref[...] += jnp.sum(onehot, axis=0, keepdims=True)


def _mix(x2, y_rwkv, outs, lses, y_mem, gates, p_rwkv, p_dil, p_mem, w_out, ln2, w_router, b_router, bm=512):
    t, d = x2.shape
    wr = jnp.zeros((d, LANES), F32).at[:, :N_EXPERTS].set(w_router)
    br = jnp.full((1, LANES), -jnp.inf, F32).at[0, :N_EXPERTS].set(b_router)
    row = lambda w: pl.BlockSpec((bm, w), lambda i: (i, 0))
    const = lambda a: pl.BlockSpec(a.shape, lambda i: (0,) * a.ndim)
    tiles_per_seq = outs[0].shape[2] // bm
    dil = pl.BlockSpec((1, DIL_GROUP_DIM // LANES, bm, LANES),
                       lambda i: (i // tiles_per_seq, 0, i % tiles_per_seq, 0))
    weights = [p_rwkv.astype(BF16), p_dil.astype(BF16), p_mem.astype(BF16), w_out.astype(BF16),
               ln2.reshape(1, d), wr, br]
    return pl.pallas_call(
        _mix_kernel,
        grid=(t // bm,),
        in_specs=[row(d), row(RWKV_DIM)] + [dil] * 6 + [row(MEM_DIM), row(3 * d)]
                 + [const(a) for a in weights],
        out_specs=[row(d), row(d), row(LANES), pl.BlockSpec((1, LANES), lambda i: (0, 0))],
        out_shape=[jax.ShapeDtypeStruct((t, d), F32), jax.ShapeDtypeStruct((t, d), F32),
                   jax.ShapeDtypeStruct((t, LANES), F32), jax.ShapeDtypeStruct((1, LANES), F32)],
        compiler_params=_cparams(("arbitrary",)),
        name="mix",
    )(x2, y_rwkv, *outs, *lses, y_mem, gates, *weights)


def _route_kernel(route_ref, pstart_ref, tri_ref, dest_ref, carry_ref):
    @pl.when(pl.program_id(0) == 0)
    def _():
        carry_ref[...] = jnp.zeros_like(carry_ref)

    route = route_ref[...]
    lane = lax.broadcasted_iota(jnp.int32, route.shape, 1)
    lane_f = lane.astype(F32)
    hits = [lane_f == route[:, kq:kq + 1] for kq in range(TOP_K)]
    onehot = jnp.zeros(route.shape, F32)
    for hq in hits:
        onehot = jnp.where(hq, 1.0, onehot)
    rank = _dot(tri_ref[...], onehot.astype(BF16)) + carry_ref[...]
    slot = pstart_ref[...] + rank
    dest = jnp.zeros(route.shape, jnp.int32)
    for kq in range(TOP_K):
        dk = jnp.sum(jnp.where(hits[kq], slot, 0.0), axis=-1, keepdims=True)
        dest = jnp.where(lane == kq, dk.astype(jnp.int32), dest)
    dest_ref[...] = dest
    carry_ref[...] += jnp.sum(onehot, axis=0, keepdims=True)


def _route(route, pstart, bm=1024):
    t = route.shape[0]
    i = jnp.arange(bm)
    tri = (i[None, :] < i[:, None]).astype(BF16)
    return pl.pallas_call(
        _route_kernel,
        grid=(t // bm,),
        in_specs=[pl.BlockSpec((bm, LANES), lambda i: (i, 0)),
                  pl.BlockSpec((1, LANES), lambda i: (0, 0)),
                  pl.BlockSpec((bm, bm), lambda i: (0, 0))],
        out_specs=pl.BlockSpec((bm, LANES), lambda i: (i, 0)),
        out_shape=jax.ShapeDtypeStruct((t, LANES), jnp.int32),
        scratch_shapes=[pltpu.VMEM((1, LANES), F32)],
        compiler_params=_cparams(("arbitrary",)),
        name="route",
    )(route, pstart, tri)


def _row_copy(src_ref, src_row, dst_ref, dst_row, sem):
    return pltpu.make_async_copy(src_ref.at[pl.ds(src_row, 1)], dst_ref.at[pl.ds(dst_row, 1)], sem)


def _dispatch_kernel(last_ref, dest_ref, h_ref, xs_ref, zero_ref, sem, zsem, *, bm):
    @pl.when(pl.program_id(0) == 0)
    def _():
        zero_ref[...] = jnp.zeros_like(zero_ref)

        def zero_copy(e):
            row = pl.multiple_of(last_ref[e] * EXPERT_ROWS, EXPERT_ROWS)
            return pltpu.make_async_copy(zero_ref, xs_ref.at[pl.ds(row, EXPERT_ROWS)], zsem)

        def z_issue(e, carry):
            @pl.when(last_ref[e] >= 0)
            def _():
                zero_copy(e).start()
            return carry

        def z_drain(e, carry):
            @pl.when(last_ref[e] >= 0)
            def _():
                zero_copy(e).wait()
            return carry

        lax.fori_loop(0, 2 * N_EXPERTS, z_issue, 0)
        lax.fori_loop(0, 2 * N_EXPERTS, z_drain, 0)

    def issue(j, carry):
        for kq in range(TOP_K):
            _row_copy(h_ref, j, xs_ref, dest_ref[j * TOP_K + kq], sem).start(priority=kq % 2)
        return carry

    lax.fori_loop(0, bm, issue, 0)
    for kq in range(TOP_K):
        pltpu.make_async_copy(h_ref, xs_ref.at[pl.ds(0, bm)], sem).wait()


def _dispatch(last_blk, dest_flat, h2, n_slots, bm=256):
    t, d = h2.shape
    grid_spec = pltpu.PrefetchScalarGridSpec(
        num_scalar_prefetch=1,
        grid=(t // bm,),
        in_specs=[pl.BlockSpec((bm * TOP_K,), lambda i, lb: (i,), memory_space=pltpu.SMEM),
                  pl.BlockSpec((bm, d), lambda i, lb: (i, 0))],
        out_specs=pl.BlockSpec(memory_space=pl.ANY),
        scratch_shapes=[pltpu.VMEM((EXPERT_ROWS, d), h2.dtype), pltpu.SemaphoreType.DMA, pltpu.SemaphoreType.DMA],
    )
    return pl.pallas_call(
        functools.partial(_dispatch_kernel, bm=bm),
        grid_spec=grid_spec,
        out_shape=jax.ShapeDtypeStruct((n_slots, d), h2.dtype),
        compiler_params=_cparams(("arbitrary",)),
        name="dispatch",
    )(last_blk, dest_flat, h2)


FF_CHUNK = 256
EXPERT_ROWS = 256


def _expert_kernel(be_ref, par_ref, nxt_ref, nu_ref, xs_ref, w1_ref, b1_ref, w2_ref, b2_ref, sel_ref, ys_ref,
                   w1f_ref, w2f_ref, w1p_ref, w2b_ref, act_ref, sem):
    i = pl.program_id(0)
    e = be_ref[i]
    slot = par_ref[i]
    fresh = jnp.logical_or(i == 0, be_ref[jnp.maximum(i - 1, 0)] != e)
    used = i < nu_ref[0]
    d_ff2 = w1_ref.shape[2]
    half = FF_CHUNK // 2

    def weight_copies(expert, buf):
        return (pltpu.make_async_copy(w1_ref.at[expert], w1f_ref.at[buf], sem.at[0, buf]),
                pltpu.make_async_copy(w2_ref.at[expert], w2f_ref.at[buf], sem.at[1, buf]))

    @pl.when(i == 0)
    def _():
        for cp in weight_copies(e, slot):
            cp.start()

    @pl.when(jnp.logical_and(fresh, used))
    def _():
        for cp in weight_copies(e, slot):
            cp.wait()
        nxt = nxt_ref[i]

        @pl.when(nxt >= 0)
        def _():
            for cp in weight_copies(nxt, 1 - slot):
                cp.start()

        for c in range(d_ff2 // FF_CHUNK):
            sl = slice(c * FF_CHUNK, (c + 1) * FF_CHUNK)
            w1p_ref[:, sl] = _dot(w1f_ref[slot, :, sl].astype(BF16), sel_ref[...]).astype(BF16)
        w2b_ref[...] = w2f_ref[slot].astype(BF16)

    @pl.when(used)
    def _():
        x = xs_ref[...].astype(BF16)
        for c in range(d_ff2 // (2 * FF_CHUNK)):
            sl = slice(2 * c * FF_CHUNK, 2 * (c + 1) * FF_CHUNK)
            hb = _dot(x, w1p_ref[:, sl]) + b1_ref[0, :, sl]
            for j in range(2):
                x_glu = jnp.minimum(hb[:, j * FF_CHUNK:j * FF_CHUNK + half], SWIGLU_LIMIT)
                x_lin = jnp.clip(hb[:, j * FF_CHUNK + half:(j + 1) * FF_CHUNK], -SWIGLU_LIMIT, SWIGLU_LIMIT)
                act = x_glu * _sigmoid(SWIGLU_ALPHA * x_glu) * (x_lin + 1.0)
                act_ref[:, (2 * c + j) * half:(2 * c + j + 1) * half] = act.astype(BF16)
        ys_ref[...] = _dot(act_ref[...], w2b_ref[...]) + b2_ref[0]

    @pl.when(jnp.logical_not(used))
    def _():
        ys_ref[...] = jnp.zeros_like(ys_ref)


def _chunk_deinterleave(a):
    lead = a.shape[:-1]
    a = a.reshape(lead + (a.shape[-1] // FF_CHUNK, FF_CHUNK // 2, 2))
    return jnp.swapaxes(a, -1, -2).reshape(lead + (-1,))


def _experts(block_e, parity, next_e, n_used, xs, w1, b1, w2, b2):
    n_slots, d = xs.shape
    n_e, _, d_ff2 = w1.shape
    n_blocks = n_slots // EXPERT_ROWS
    i = jnp.arange(FF_CHUNK)
    src = jnp.where(i < FF_CHUNK // 2, 2 * i, 2 * (i - FF_CHUNK // 2) + 1)
    sel = (jnp.arange(FF_CHUNK)[:, None] == src[None, :]).astype(BF16)
    b1p = _chunk_deinterleave(b1).reshape(n_e, 1, d_ff2)
    imap = lambda i, be, par, nxt, nu: (i, 0)
    xmap = lambda i, be, par, nxt, nu: (jnp.minimum(i, nu[0] - 1), 0)
    emap = lambda i, be, par, nxt, nu: (be[i], 0, 0)
    grid_spec = pltpu.PrefetchScalarGridSpec(
        num_scalar_prefetch=4,
        grid=(n_blocks,),
        in_specs=[pl.BlockSpec((EXPERT_ROWS, d), xmap),
                  pl.BlockSpec(memory_space=pl.ANY),
                  pl.BlockSpec((1, 1, d_ff2), emap),
                  pl.BlockSpec(memory_space=pl.ANY),
                  pl.BlockSpec((1, 1, d), emap),
                  pl.BlockSpec((FF_CHUNK, FF_CHUNK), lambda i, be, par, nxt, nu: (0, 0))],
        out_specs=pl.BlockSpec((EXPERT_ROWS, d), imap),
        scratch_shapes=[pltpu.VMEM((2, d, d_ff2), F32), pltpu.VMEM((2, d_ff2 // 2, d), F32),
                        pltpu.VMEM((d, d_ff2), BF16), pltpu.VMEM((d_ff2 // 2, d), BF16),
                        pltpu.VMEM((EXPERT_ROWS, d_ff2 // 2), BF16),
                        pltpu.SemaphoreType.DMA((2, 2))],
    )
    return pl.pallas_call(
        _expert_kernel,
        grid_spec=grid_spec,
        out_shape=jax.ShapeDtypeStruct((n_slots, d), F32),
        compiler_params=_cparams(("arbitrary",), vmem=EXPERT_VMEM_LIMIT),
        name="experts",
    )(block_e, parity, next_e, n_used, xs, w1, b1p, w2, b2.reshape(n_e, 1, d), sel)


def _combine_kernel(dest_ref, dnext_ref, x1_ref, route_ref, ys_ref, o_ref, buf_ref, sem, *, bm):
    i = pl.program_id(0)
    n = pl.num_programs(0)
    slot = i % 2

    def gather(d_ref, buf):
        def issue(j, carry):
            for kq in range(TOP_K):
                _row_copy(ys_ref, d_ref[j * TOP_K + kq], buf_ref.at[buf, kq], j, sem.at[buf]).start(priority=kq % 2)
            return carry

        lax.fori_loop(0, bm, issue, 0)

    @pl.when(i == 0)
    def _():
        gather(dest_ref, slot)

    @pl.when(i + 1 < n)
    def _():
        gather(dnext_ref, 1 - slot)

    for kq in range(TOP_K):
        pltpu.make_async_copy(ys_ref.at[pl.ds(0, bm)], buf_ref.at[slot, kq], sem.at[slot]).wait()
    acc = x1_ref[...]
    for kq in range(TOP_K):
        acc = acc + route_ref[:, TOP_K + kq:TOP_K + kq + 1] * buf_ref[slot, kq]
    o_ref[...] = acc


def _combine(dest_flat, x1, route, ys, bm=256):
    t, d = x1.shape
    n = t // bm
    return pl.pallas_call(
        functools.partial(_combine_kernel, bm=bm),
        grid=(n,),
        in_specs=[pl.BlockSpec((bm * TOP_K,), lambda i: (i,), memory_space=pltpu.SMEM),
                  pl.BlockSpec((bm * TOP_K,), lambda i: (jnp.minimum(i + 1, n - 1),), memory_space=pltpu.SMEM),
                  pl.BlockSpec((bm, d), lambda i: (i, 0)),
                  pl.BlockSpec((bm, LANES), lambda i: (i, 0)),
                  pl.BlockSpec(memory_space=pl.ANY)],
        out_specs=pl.BlockSpec((bm, d), lambda i: (i, 0)),
        out_shape=jax.ShapeDtypeStruct((t, d), F32),
        scratch_shapes=[pltpu.VMEM((2, TOP_K, bm, d), F32), pltpu.SemaphoreType.DMA((2,))],
        compiler_params=_cparams(("arbitrary",)),
        name="combine",
    )(dest_flat, dest_flat, x1, route, ys)


def _layer(x, mem, ln1, w_in, token_mu, rwkv_w0, rwkv_w2, rwkv_a0, rwkv_a2, rwkv_g2,
           rwkv_k_k, rwkv_k_a, rwkv_r_k, rwkv_ln_w, rwkv_ln_b, dil_q_norm, dil_k_norm,
           ln_mem, w_mem_kv, mem_q_norm, mem_k_norm, p_rwkv, p_dil, p_mem, w_out,
           ln2, w_router, b_router, w1, b1, w2, b2):
    b, s, d = x.shape
    t = b * s
    x2 = x.reshape(t, d)
    zr, dil_qkv, mq, gates = _in_proj(x2, ln1, w_in, dil_q_norm, dil_k_norm, mem_q_norm)
    y_rwkv = _rwkv(zr.reshape(b, s, RWKV_IN), token_mu, rwkv_w0, rwkv_w2, rwkv_a0, rwkv_a2, rwkv_g2,
                   rwkv_k_k, rwkv_k_a, rwkv_r_k.reshape(-1), rwkv_ln_w, rwkv_ln_b)
    outs, lses = [], []
    for g, (_, dilation) in enumerate(DIL_PATTERNS):
        o, l = _dil_group(*dil_qkv[3 * g:3 * g + 3], b, dilation)
        outs.append(o)
        lses.append(l)
    mk, mv = _mem_kv(mem, ln_mem, w_mem_kv, mem_k_norm)
    y_mem = _mem_attn(mq.reshape(b, s, MEM_DIM), mk, mv)
    x1, h2, route, counts = _mix(x2, y_rwkv.reshape(t, RWKV_DIM), outs, lses, y_mem.reshape(t, MEM_DIM), gates,
                                 p_rwkv, p_dil, p_mem, w_out, ln2, w_router, b_router)

    counts = counts[0, :N_EXPERTS].astype(jnp.int32)
    nblk = (counts + EXPERT_ROWS - 1) // EXPERT_ROWS
    bend = jnp.cumsum(nblk)
    pstart = ((bend - nblk) * EXPERT_ROWS).astype(F32)
    pstart = jnp.zeros((1, LANES), F32).at[0, :N_EXPERTS].set(pstart)
    n_blocks = (t * TOP_K) // EXPERT_ROWS + N_EXPERTS
    block_e = jnp.sum(bend[None, :] <= jnp.arange(n_blocks, dtype=jnp.int32)[:, None], axis=1)
    block_e = jnp.minimum(block_e, N_EXPERTS - 1).astype(jnp.int32)
    n_used = bend[-1:].astype(jnp.int32)
    used = nblk > 0
    eids = jnp.arange(N_EXPERTS, dtype=jnp.int32)
    ordinal = jnp.cumsum(used.astype(jnp.int32)) - 1
    later = jnp.where(used[None, :] & (eids[None, :] > eids[:, None]), eids[None, :], N_EXPERTS)
    next_used = jnp.min(later, axis=1)
    next_used = jnp.where(next_used < N_EXPERTS, next_used, -1).astype(jnp.int32)
    parity = (ordinal[block_e] % 2).astype(jnp.int32)
    next_e = next_used[block_e]
    tail = n_used[0] + eids
    last_blk = jnp.concatenate([jnp.where(used, bend - 1, -1), jnp.where(tail < n_blocks, tail, -1)]).astype(jnp.int32)

    dest = _route(route, pstart)[:, :TOP_K].reshape(-1)
    xs = _dispatch(last_blk, dest, h2, n_blocks * EXPERT_ROWS)
    ys = _experts(block_e, parity, next_e, n_used, xs, w1, b1, w2, b2)
    out = _combine(dest, x1, route, ys)
    return out.reshape(b, s, d)


def kernel(x, mem, ln1, w_in, token_mu, rwkv_w0, rwkv_w2, rwkv_a0, rwkv_a2, rwkv_g2, rwkv_k_k, rwkv_k_a, rwkv_r_k, rwkv_ln_w, rwkv_ln_b, dil_q_norm, dil_k_norm, ln_mem, w_mem_kv, mem_q_norm, mem_k_norm, p_rwkv, p_dil, p_mem, w_out, ln2, w_router, b_router, w1, b1, w2, b2):
    params = (ln1, w_in, token_mu, rwkv_w0, rwkv_w2, rwkv_a0, rwkv_a2, rwkv_g2, rwkv_k_k, rwkv_k_a,
              rwkv_r_k, rwkv_ln_w, rwkv_ln_b, dil_q_norm, dil_k_norm, ln_mem, w_mem_kv, mem_q_norm,
              mem_k_norm, p_rwkv, p_dil, p_mem, w_out, ln2, w_router, b_router, w1, b1, w2, b2)
    for l in range(ln1.shape[0]):
        x = _layer(x, mem, *[p[l] for p in params])
    return x
```

```python
import functools

import jax
import jax.numpy as jnp
from jax import lax
from jax.experimental import pallas as pl
from jax.experimental.pallas import tpu as pltpu

F32 = jnp.float32
BF16 = jnp.bfloat16

NORM_EPS = 1e-5
HEAD_DIM = 64
RWKV_DIM = 512
RWKV_IN = 1792
GN_EPS = HEAD_DIM * 1e-5
DIL_DIM = 768
DIL_GROUP_DIM = 256
DIL_PATTERNS = ((128, 1), (512, 4), (2048, 16))
ATTN_BLOCK = 128
MEM_DIM = 512
MEM_HEAD_DIM = 128
N_EXPERTS = 32
TOP_K = 4
SWIGLU_ALPHA = 1.702
SWIGLU_LIMIT = 7.0
LANES = 128
SUBLANES = 8
CHUNK = 64
NEG_BIG = -1e30
VMEM_LIMIT = 48 * 1024 * 1024
EXPERT_VMEM_LIMIT = 56 * 1024 * 1024


def _dot(a, b):
    return jnp.dot(a, b, preferred_element_type=F32)


def _dot_nt(a, b):
    return lax.dot_general(a, b, (((1,), (1,)), ((), ())), preferred_element_type=F32)


def _split2(x):
    hi = x.astype(BF16)
    lo = (x - hi.astype(F32)).astype(BF16)
    return hi, lo


def _split3(x):
    hi = x.astype(BF16)
    r = x - hi.astype(F32)
    lo = r.astype(BF16)
    lo2 = (r - lo.astype(F32)).astype(BF16)
    return hi, lo, lo2


def _dot_rhs3(w_bf16, x):
    hi, lo, lo2 = _split3(x)
    return _dot(w_bf16, hi) + _dot(w_bf16, lo) + _dot(w_bf16, lo2)


def _dot_x3(a, b):
    ah, al = _split2(a)
    bh, bl = _split2(b)
    return _dot(ah, bh) + _dot(al, bh) + _dot(ah, bl)


def _sigmoid(x):
    return 1.0 / (1.0 + jnp.exp(-x))


def _cparams(sem, vmem=VMEM_LIMIT):
    return pltpu.CompilerParams(dimension_semantics=sem, vmem_limit_bytes=vmem)


IN_CHUNK = 256
PERM_ROWS = 256
N_ZR = RWKV_IN // IN_CHUNK
N_DIL = 3 * DIL_DIM // IN_CHUNK
N_MQ = MEM_DIM // IN_CHUNK
N_GATE = 3 * 1024 // IN_CHUNK


def _in_proj_kernel(x_ref, ln_ref, w_ref, qg_ref, kg_ref, mg_ref, p4_ref, p16_ref,
                    zr_ref, *rest, bm):
    dil_refs, (mq_ref, gt_ref) = rest[:N_DIL], rest[N_DIL:]
    perm_refs = (None, p4_ref, p16_ref)
    x = x_ref[...]
    h = x * lax.rsqrt(jnp.mean(x * x, axis=-1, keepdims=True) + NORM_EPS) * ln_ref[...]
    hb = h.astype(BF16)

    lane = lax.broadcasted_iota(jnp.int32, (x.shape[0], LANES), 1)

    def seg_rms(z, seg, gain):
        z2 = z * z
        cols = []
        for cb in range(IN_CHUNK // LANES):
            blk = z2[:, cb * LANES:(cb + 1) * LANES]
            ss = jnp.zeros_like(blk)
            for sg in range(LANES // seg):
                m = (lane >= sg * seg) & (lane < (sg + 1) * seg)
                ss = jnp.where(m, jnp.sum(jnp.where(m, blk, 0.0), axis=-1, keepdims=True), ss)
            cols.append(ss)
        ss = jnp.concatenate(cols, axis=1)
        return z * lax.rsqrt(ss * (1.0 / seg) + NORM_EPS) * gain

    for c in range(N_ZR + N_DIL + N_MQ + N_GATE):
        z = _dot(hb, w_ref[:, c * IN_CHUNK:(c + 1) * IN_CHUNK])
        if c < N_ZR:
            zr_ref[:, c * IN_CHUNK:(c + 1) * IN_CHUNK] = z
            continue
        d = c - N_ZR
        if d < N_DIL:
            which, g = divmod(d, 3)
            if which == 0:
                z = seg_rms(z, HEAD_DIM, qg_ref[...])
            elif which == 1:
                z = seg_rms(z, HEAD_DIM, kg_ref[...])
            zb = z.astype(BF16)
            o_ref = dil_refs[g * 3 + which]
            dilation = DIL_PATTERNS[g][1]
            if dilation == 1:
                o_ref[...] = zb
            else:
                rows = PERM_ROWS // dilation
                for sub in range(bm // PERM_ROWS):
                    zp = _dot(perm_refs[g][...], zb[sub * PERM_ROWS:(sub + 1) * PERM_ROWS, :]).astype(BF16)
                    for r in range(dilation):
                        o_ref[sub * rows:(sub + 1) * rows, r * IN_CHUNK:(r + 1) * IN_CHUNK] = zp[r * rows:(r + 1) * rows, :]
            continue
        d -= N_DIL
        if d < N_MQ:
            mq_ref[:, d * IN_CHUNK:(d + 1) * IN_CHUNK] = seg_rms(z, MEM_HEAD_DIM, mg_ref[...]).astype(BF16)
            continue
        d -= N_MQ
        gt_ref[:, d * IN_CHUNK:(d + 1) * IN_CHUNK] = _sigmoid(z).astype(BF16)


def _class_perm(bm, dilation):
    i = jnp.arange(bm)
    src = (i % (bm // dilation)) * dilation + i // (bm // dilation)
    return (src[:, None] == i[None, :]).astype(BF16)


def _in_proj(x2, ln1, w_in, dil_q_norm, dil_k_norm, mem_q_norm, bm=512):
    t, d = x2.shape
    n_cols = w_in.shape[1]
    wb = w_in.astype(BF16)
    qg = (jnp.tile(dil_q_norm, IN_CHUNK // HEAD_DIM) * (HEAD_DIM ** -0.5)).reshape(1, IN_CHUNK)
    kg = jnp.tile(dil_k_norm, IN_CHUNK // HEAD_DIM).reshape(1, IN_CHUNK)
    mg = jnp.tile(mem_q_norm, IN_CHUNK // MEM_HEAD_DIM).reshape(1, IN_CHUNK)
    p4 = _class_perm(PERM_ROWS, DIL_PATTERNS[1][1])
    p16 = _class_perm(PERM_ROWS, DIL_PATTERNS[2][1])
    row = lambda w: pl.BlockSpec((bm, w), lambda i: (i, 0))
    const = lambda a: pl.BlockSpec(a.shape, lambda i: (0,) * a.ndim)
    dil_specs, dil_shapes = [], []
    for _, dilation in DIL_PATTERNS:
        for _ in range(3):
            dil_specs.append(pl.BlockSpec((bm // dilation, dilation * DIL_GROUP_DIM), lambda i: (i, 0)))
            dil_shapes.append(jax.ShapeDtypeStruct((t // dilation, dilation * DIL_GROUP_DIM), BF16))
    outs = pl.pallas_call(
        functools.partial(_in_proj_kernel, bm=bm),
        grid=(t // bm,),
        in_specs=[row(d), pl.BlockSpec((1, d), lambda i: (0, 0)),
                  pl.BlockSpec((d, n_cols), lambda i: (0, 0), pipeline_mode=pl.Buffered(1)),
                  const(qg), const(kg), const(mg), const(p4), const(p16)],
        out_specs=[row(RWKV_IN)] + dil_specs + [row(MEM_DIM), row(3 * d)],
        out_shape=[jax.ShapeDtypeStruct((t, RWKV_IN), F32)] + dil_shapes
                  + [jax.ShapeDtypeStruct((t, MEM_DIM), BF16), jax.ShapeDtypeStruct((t, 3 * d), BF16)],
        compiler_params=_cparams(("parallel",)),
        name="in_proj",
    )(x2, ln1.reshape(1, d), wb, qg, kg, mg, p4, p16)
    return outs[0], outs[1:1 + N_DIL], outs[1 + N_DIL], outs[2 + N_DIL]


def _rwkv_kernel(z_ref, mu_ref, w0_ref, a0_ref, wl_ref, kk_ref, ka_ref, rk_ref,
                 lnw_ref, lnb_ref, tri_ref, y_ref, carry_ref, state_ref, ybuf_ref, *, tt):
    s_idx = pl.program_id(1)

    @pl.when(s_idx == 0)
    def _():
        carry_ref[...] = jnp.zeros_like(carry_ref)
        state_ref[...] = jnp.zeros_like(state_ref)

    z = z_ref[0]
    rows = lax.broadcasted_iota(jnp.int32, z.shape, 0)
    prev = jnp.where(rows == 0, carry_ref[...], pltpu.roll(z, 1, axis=0))
    carry_ref[...] = z[tt - 1:tt, :]
    zl = z + (prev - z) * mu_ref[...]
    r = zl[:, 0:RWKV_DIM]
    k = zl[:, RWKV_DIM:2 * RWKV_DIM]
    v = zl[:, 2 * RWKV_DIM:3 * RWKV_DIM]
    zlo = zl[:, 3 * RWKV_DIM:RWKV_IN]
    lcol = lax.broadcasted_iota(jnp.int32, zlo.shape, 1)
    feat = jnp.where(lcol < 64, jnp.tanh(zlo), jnp.where(lcol < 128, zlo, _sigmoid(zlo)))
    lora = _dot(feat.astype(BF16), wl_ref[...])
    nu = -(w0_ref[...] + lora[:, 0:RWKV_DIM])
    softplus = jnp.maximum(nu, 0.0) + jnp.log(1.0 + jnp.exp(-jnp.abs(nu)))
    lw = -jnp.exp(-softplus - 0.5)
    a = _sigmoid(a0_ref[...] + lora[:, RWKV_DIM:2 * RWKV_DIM])
    g = lora[:, 2 * RWKV_DIM:3 * RWKV_DIM]

    n_pairs = RWKV_DIM // LANES
    n_chunks = tt // CHUNK
    head_lo = lax.broadcasted_iota(jnp.int32, (tt, LANES), 1) < HEAD_DIM

    def seg_sum(x):
        parts = []
        for p in range(n_pairs):
            xp = x[:, p * LANES:(p + 1) * LANES]
            lo = jnp.sum(jnp.where(head_lo, xp, 0.0), axis=-1, keepdims=True)
            hi = jnp.sum(jnp.where(head_lo, 0.0, xp), axis=-1, keepdims=True)
            parts.append(jnp.where(head_lo, lo, hi))
        return jnp.concatenate(parts, axis=1)

    kk = k * kk_ref[...]
    kk = kk / jnp.maximum(jnp.sqrt(seg_sum(kk * kk)), 1e-12)
    k2 = k * (1.0 + (a - 1.0) * ka_ref[...])
    a_s = -kk
    b_s = kk * a

    cum = _dot_rhs3(tri_ref[...], lw)
    tot = jnp.concatenate(
        [jnp.broadcast_to(cum[(c + 1) * CHUNK - 1:(c + 1) * CHUNK, :], (CHUNK, RWKV_DIM)) for c in range(n_chunks)],
        axis=0)
    e_neg = jnp.exp(-cum)
    e_end = jnp.exp(tot - cum)
    at = (a_s * jnp.exp(cum - lw)).astype(BF16)
    rt = (r * jnp.exp(cum)).astype(BF16)
    bt = (b_s * e_neg).astype(BF16)
    kt = (k2 * e_neg).astype(BF16)
    be = (b_s * e_end).astype(BF16)
    ke = (k2 * e_end).astype(BF16)
    vb = v.astype(BF16)
    e_tot = jnp.exp(tot)

    r128 = lax.broadcasted_iota(jnp.int32, (LANES, LANES), 0)
    c128 = lax.broadcasted_iota(jnp.int32, (LANES, LANES), 1)
    lane_lo = lax.broadcasted_iota(jnp.int32, (CHUNK, LANES), 1) < HEAD_DIM
    stril = c128 < r128
    tril = c128 <= r128
    eye = (c128 == r128).astype(F32)
    same16 = (r128 // 16) == (c128 // 16)
    same32 = (r128 // 32) == (c128 // 32)
    off16 = same32 & jnp.logical_not(same16)
    off32 = jnp.logical_not(same32)

    def bdiag(xp):
        zero = jnp.zeros_like(xp)
        return jnp.concatenate([jnp.where(lane_lo, xp, zero), jnp.where(lane_lo, zero, xp)], axis=0)

    units = [(ci, p) for ci in range(n_chunks) for p in range(n_pairs)]
    pick = lambda arr, u: arr[u[0] * CHUNK:(u[0] + 1) * CHUNK, u[1] * LANES:(u[1] + 1) * LANES]
    cat0 = lambda xs: jnp.concatenate(xs, axis=0)
    cat1 = lambda xs: jnp.concatenate(xs, axis=1)
    zero_b = jnp.zeros((LANES, LANES), BF16)

    at_b = [bdiag(pick(at, u)) for u in units]
    v_b = [bdiag(pick(vb, u)) for u in units]
    be_b = [bdiag(pick(be, u)) for u in units]
    ke_b = [bdiag(pick(ke, u)) for u in units]
    rt_b = [bdiag(pick(rt, u)) for u in units]
    a_ab, a_ak, a_rbk = [], [], []
    for i, u in enumerate(units):
        mq = _dot_nt(cat0([at_b[i], rt_b[i]]), cat0([bdiag(pick(bt, u)), bdiag(pick(kt, u))]))
        a_ab.append(jnp.where(stril, mq[:LANES, :LANES], 0.0).astype(BF16))
        a_ak.append(jnp.where(stril, mq[:LANES, LANES:], 0.0).astype(BF16))
        a_rbk.append(cat1([jnp.where(tril, mq[LANES:, :LANES], 0.0), jnp.where(tril, mq[LANES:, LANES:], 0.0)]).astype(BF16))
    w_b = [_dot(a_ak[i], v_b[i]).astype(BF16) for i in range(len(units))]

    d1 = [jnp.where(same16, x, jnp.zeros_like(x)) for x in a_ab]
    xs = [eye + d.astype(F32) for d in d1]
    d2 = [_dot(d, d).astype(BF16) for d in d1]
    t_ = [_dot(d2[i], cat1([xs[i].astype(BF16), d2[i]])) for i in range(len(units))]
    xs = [xs[i] + t_[i][:, :LANES] for i in range(len(units))]
    d4 = [t[:, LANES:].astype(BF16) for t in t_]
    t_ = [_dot(d4[i], cat1([xs[i].astype(BF16), d4[i]])) for i in range(len(units))]
    xs = [xs[i] + t_[i][:, :LANES] for i in range(len(units))]
    d8 = [t[:, LANES:].astype(BF16) for t in t_]
    xs = [xs[i] + _dot(d8[i], xs[i].astype(BF16)) for i in range(len(units))]
    for off in (off16, off32):
        xb = [x.astype(BF16) for x in xs]
        g_ = [_dot(jnp.where(off, a_ab[i], jnp.zeros_like(a_ab[i])), xb[i]).astype(BF16) for i in range(len(units))]
        xs = [xs[i] + _dot(xb[i], g_[i]) for i in range(len(units))]

    pq = [_dot(xs[i].astype(BF16), cat1([at_b[i], w_b[i]])) for i in range(len(units))]
    ry = [_dot(a_rbk[i], cat0([pq[i].astype(BF16), cat1([zero_b, v_b[i]])])) for i in range(len(units))]
    r2 = [(rt_b[i].astype(F32) + ry[i][:, :LANES]).astype(BF16) for i in range(len(units))]
    m_c = [_dot(pq[i][:, :LANES].T.astype(BF16), be_b[i]) for i in range(len(units))]
    n_c = [_dot(cat1([pq[i][:, LANES:].T.astype(BF16), v_b[i].astype(F32).T.astype(BF16)]), cat0([be_b[i], ke_b[i]]))
           for i in range(len(units))]

    state = [state_ref[p] for p in range(n_pairs)]
    for i, (ci, p) in enumerate(units):
        s0 = state[p]
        s_hi, s_lo = _split2(s0)
        y = _dot_nt(r2[i], s_hi) + ry[i][:, LANES:]
        mcb = m_c[i].astype(BF16)
        e_row = e_tot[ci * CHUNK:ci * CHUNK + 1, p * LANES:(p + 1) * LANES]
        state[p] = s0 * e_row + (_dot(s_hi, mcb) + _dot(s_lo, mcb)) + n_c[i]
        ybuf_ref[ci * CHUNK:(ci + 1) * CHUNK, p * LANES:(p + 1) * LANES] = jnp.where(lane_lo, y[:CHUNK], y[CHUNK:])
    for p in range(n_pairs):
        state_ref[p] = state[p]

    y = ybuf_ref[...]
    mean = seg_sum(y) * (1.0 / HEAD_DIM)
    yc = y - mean
    var = seg_sum(yc * yc) * (1.0 / HEAD_DIM)
    yn = yc * lax.rsqrt(var + GN_EPS) * lnw_ref[...] + lnb_ref[...]
    bonus = seg_sum(r * k2 * rk_ref[...]) * v
    y_ref[0] = ((yn + bonus) * g).astype(y_ref.dtype)


def _rwkv(zr, token_mu, w0, w2, a0, a2, g2, k_k, k_a, r_k, ln_w, ln_b, tt=256):
    b, s, _ = zr.shape
    i = jnp.arange(tt)
    same = (i[:, None] // CHUNK) == (i[None, :] // CHUNK)
    tri = (same & (i[None, :] <= i[:, None])).astype(BF16)
    vec = lambda a: a.reshape(1, -1)
    wl = jnp.zeros((RWKV_IN - 3 * RWKV_DIM, 3 * RWKV_DIM), F32)
    wl = wl.at[0:64, 0:RWKV_DIM].set(w2).at[64:128, RWKV_DIM:2 * RWKV_DIM].set(a2)
    wl = wl.at[128:256, 2 * RWKV_DIM:3 * RWKV_DIM].set(g2).astype(BF16)
    params = [vec(token_mu), vec(w0), vec(a0), wl, vec(k_k), vec(k_a), vec(r_k),
              vec(ln_w), vec(ln_b), tri]
    const = lambda a: pl.BlockSpec(a.shape, lambda bi, si: (0,) * a.ndim)
    return pl.pallas_call(
        functools.partial(_rwkv_kernel, tt=tt),
        grid=(b, s // tt),
        in_specs=[pl.BlockSpec((1, tt, RWKV_IN), lambda bi, si: (bi, si, 0))] + [const(a) for a in params],
        out_specs=pl.BlockSpec((1, tt, RWKV_DIM), lambda bi, si: (bi, si, 0)),
        out_shape=jax.ShapeDtypeStruct((b, s, RWKV_DIM), BF16),
        scratch_shapes=[pltpu.VMEM((1, RWKV_IN), F32),
                        pltpu.VMEM((RWKV_DIM // LANES, LANES, LANES), F32),
                        pltpu.VMEM((tt, RWKV_DIM), F32)],
        compiler_params=_cparams(("parallel", "arbitrary")),
        name="rwkv",
    )(zr, *params)


DIL_UNITS = 4


def _dil_kernel(q_ref, k_ref, v_ref, o_ref, l_ref, *, dilation, cps, nbs):
    cg = pl.program_id(1)
    jb = pl.program_id(2)
    bq = ATTN_BLOCK
    n_heads = DIL_GROUP_DIM // HEAD_DIM
    qi = lax.broadcasted_iota(jnp.int32, (bq, 2 * bq), 0)
    kj = lax.broadcasted_iota(jnp.int32, (bq, 2 * bq), 1)
    lane = lax.broadcasted_iota(jnp.int32, (bq, DIL_GROUP_DIM), 1)
    head_masks = [(lane >= h * HEAD_DIM) & (lane < (h + 1) * HEAD_DIM) for h in range(n_heads)]
    units = [(c, bb) for c in range(cps) for bb in range(nbs)]

    qs, kcats, vcats, masks, starts = [], [], [], [], []
    for c, bb in units:
        i = jb * nbs + bb
        cols = slice(c * DIL_GROUP_DIM, (c + 1) * DIL_GROUP_DIM)
        p0 = pl.multiple_of(jnp.maximum(i - 1, 0) * bq, bq)
        c0 = pl.multiple_of(i * bq, bq)
        qs.append(q_ref[0, bb * bq:(bb + 1) * bq, cols])
        kcats.append(jnp.concatenate([k_ref[0, pl.ds(p0, bq), cols], k_ref[0, pl.ds(c0, bq), cols]], axis=0))
        vcats.append(jnp.concatenate([v_ref[0, pl.ds(p0, bq), cols], v_ref[0, pl.ds(c0, bq), cols]], axis=0))
        first = (1 - jnp.minimum(i, 1)) * (2 * bq)
        masks.append(((kj < bq) & (kj >= qi + first)) | ((kj >= bq) & ((kj - bq) <= qi)))
        starts.append((cg * cps + c) + dilation * bq * i)
    pairs = [(u, h) for u in range(len(units)) for h in range(n_heads)]
    sc = [jnp.where(masks[u], _dot_nt(jnp.where(head_masks[h], qs[u], jnp.zeros_like(qs[u])), kcats[u]), NEG_BIG)
          for u, h in pairs]
    mx = [jnp.max(x, axis=-1, keepdims=True) for x in sc]
    pr = [jnp.exp(sc[n] - mx[n]) for n in range(len(pairs))]
    den = [jnp.sum(x, axis=-1, keepdims=True) for x in pr]
    ov = [_dot(pr[n].astype(BF16), vcats[pairs[n][0]]) / den[n] for n in range(len(pairs))]
    for u in range(len(units)):
        acc = jnp.zeros((bq, DIL_GROUP_DIM), F32)
        lacc = jnp.zeros((bq, DIL_GROUP_DIM), F32)
        for h in range(n_heads):
            n = u * n_heads + h
            acc = jnp.where(head_masks[h], ov[n], acc)
            lacc = jnp.where(head_masks[h], mx[n] + jnp.log(den[n]), lacc)
        rows = pl.ds(starts[u], bq, stride=dilation) if dilation > 1 else pl.ds(pl.multiple_of(starts[u], bq), bq)
        for half in range(DIL_GROUP_DIM // LANES):
            o_ref[0, half, rows, :] = acc[:, half * LANES:(half + 1) * LANES]
            l_ref[0, half, rows, :] = lacc[:, half * LANES:(half + 1) * LANES]


def _dil_group(q, k, v, b, dilation):
    n = q.shape[0] // b
    s = n * dilation
    nb = n // ATTN_BLOCK
    nbs = min(nb, DIL_UNITS)
    cps = DIL_UNITS // nbs
    halves = DIL_GROUP_DIM // LANES
    view = lambda a: a.reshape(b, n, dilation * DIL_GROUP_DIM)
    qmap = lambda bi, ci, ji: (bi, ji, ci)
    kmap = lambda bi, ci, ji: (bi, 0, ci)
    omap = lambda bi, ci, ji: (bi, 0, 0, 0)
    return pl.pallas_call(
        functools.partial(_dil_kernel, dilation=dilation, cps=cps, nbs=nbs),
        grid=(b, dilation // cps, nb // nbs),
        in_specs=[pl.BlockSpec((1, nbs * ATTN_BLOCK, cps * DIL_GROUP_DIM), qmap),
                  pl.BlockSpec((1, n, cps * DIL_GROUP_DIM), kmap),
                  pl.BlockSpec((1, n, cps * DIL_GROUP_DIM), kmap)],
        out_specs=[pl.BlockSpec((1, halves, s, LANES), omap),
                   pl.BlockSpec((1, halves, s, LANES), omap)],
        out_shape=[jax.ShapeDtypeStruct((b, halves, s, LANES), F32),
                   jax.ShapeDtypeStruct((b, halves, s, LANES), F32)],
        compiler_params=_cparams(("parallel", "arbitrary", "arbitrary")),
        name=f"dil_attn_d{dilation}",
    )(view(q), view(k), view(v))


def _mem_kv_kernel(m_ref, ln_ref, w_ref, kn_ref, k_ref, v_ref):
    x = m_ref[0]
    h = x * lax.rsqrt(jnp.mean(x * x, axis=-1, keepdims=True) + NORM_EPS) * ln_ref[...]
    kv = _dot(h.astype(BF16), w_ref[...])
    for hd in range(MEM_DIM // MEM_HEAD_DIM):
        sl = slice(hd * MEM_HEAD_DIM, (hd + 1) * MEM_HEAD_DIM)
        kh = kv[:, sl]
        kh = kh * lax.rsqrt(jnp.mean(kh * kh, axis=-1, keepdims=True) + NORM_EPS) * kn_ref[...]
        k_ref[0, :, sl] = kh.astype(BF16)
    v_ref[0] = kv[:, MEM_DIM:].astype(BF16)


def _mem_kv(mem, ln_mem, w_mem_kv, mem_k_norm):
    b, m, d = mem.shape
    return pl.pallas_call(
        _mem_kv_kernel,
        grid=(b,),
        in_specs=[pl.BlockSpec((1, m, d), lambda i: (i, 0, 0)),
                  pl.BlockSpec((1, d), lambda i: (0, 0)),
                  pl.BlockSpec((d, 2 * MEM_DIM), lambda i: (0, 0)),
                  pl.BlockSpec((1, MEM_HEAD_DIM), lambda i: (0, 0))],
        out_specs=[pl.BlockSpec((1, m, MEM_DIM), lambda i: (i, 0, 0)),
                   pl.BlockSpec((1, m, MEM_DIM), lambda i: (i, 0, 0))],
        out_shape=[jax.ShapeDtypeStruct((b, m, MEM_DIM), BF16),
                   jax.ShapeDtypeStruct((b, m, MEM_DIM), BF16)],
        compiler_params=_cparams(("parallel",)),
        name="mem_kv",
    )(mem, ln_mem.reshape(1, d), w_mem_kv.astype(BF16), mem_k_norm.reshape(1, MEM_HEAD_DIM))


def _mem_attn_kernel(q_ref, k_ref, v_ref, o_ref):
    for hd in range(MEM_DIM // MEM_HEAD_DIM):
        sl = slice(hd * MEM_HEAD_DIM, (hd + 1) * MEM_HEAD_DIM)
        sc = _dot_nt(q_ref[0, :, sl], k_ref[0, :, sl]) * (MEM_HEAD_DIM ** -0.5)
        m = jnp.max(sc, axis=-1, keepdims=True)
        p = jnp.exp(sc - m)
        den = jnp.sum(p, axis=-1, keepdims=True)
        o_ref[0, :, sl] = (_dot((p / den).astype(BF16), v_ref[0, :, sl])).astype(o_ref.dtype)


def _mem_attn(mq, mk, mv, bm=512):
    b, s, _ = mq.shape
    m = mk.shape[1]
    return pl.pallas_call(
        _mem_attn_kernel,
        grid=(b, s // bm),
        in_specs=[pl.BlockSpec((1, bm, MEM_DIM), lambda bi, si: (bi, si, 0)),
                  pl.BlockSpec((1, m, MEM_DIM), lambda bi, si: (bi, 0, 0)),
                  pl.BlockSpec((1, m, MEM_DIM), lambda bi, si: (bi, 0, 0))],
        out_specs=pl.BlockSpec((1, bm, MEM_DIM), lambda bi, si: (bi, si, 0)),
        out_shape=jax.ShapeDtypeStruct((b, s, MEM_DIM), BF16),
        compiler_params=_cparams(("parallel", "parallel")),
        name="mem_attn",
    )(mq, mk, mv)


def _mix_kernel(x_ref, yr_ref, o0_ref, o1_ref, o2_ref, l0_ref, l1_ref, l2_ref, ym_ref, gt_ref,
                pr_ref, pd_ref, pm_ref, wo_ref, ln2_ref, wr_ref, br_ref,
                x1_ref, h2_ref, route_ref, cnt_ref):
    d = x_ref.shape[1]
    wide = lambda ref: jnp.concatenate([ref[0, hf] for hf in range(DIL_GROUP_DIM // LANES)], axis=1)
    l0, l1, l2 = wide(l0_ref), wide(l1_ref), wide(l2_ref)
    m = jnp.maximum(jnp.maximum(l0, l1), l2)
    e0, e1, e2 = jnp.exp(l0 - m), jnp.exp(l1 - m), jnp.exp(l2 - m)
    y_dil = (e0 * wide(o0_ref) + e1 * wide(o1_ref) + e2 * wide(o2_ref)) / (e0 + e1 + e2)
    mixed = (gt_ref[:, 0:d].astype(F32) * _dot(yr_ref[...], pr_ref[...])
             + gt_ref[:, d:2 * d].astype(F32) * _dot(y_dil.astype(BF16), pd_ref[...])
             + gt_ref[:, 2 * d:3 * d].astype(F32) * _dot(ym_ref[...], pm_ref[...]))
    x1 = x_ref[...] + _dot(mixed.astype(BF16), wo_ref[...])
    x1_ref[...] = x1
    h2 = x1 * lax.rsqrt(jnp.mean(x1 * x1, axis=-1, keepdims=True) + NORM_EPS) * ln2_ref[...]
    h2_ref[...] = h2
    logits = _dot_x3(h2, wr_ref[...]) + br_ref[...]
    lane = lax.broadcasted_iota(jnp.int32, logits.shape, 1)
    lane_f = lane.astype(F32)
    route = jnp.zeros(logits.shape, F32)
    onehot = jnp.zeros(logits.shape, F32)
    vals = []
    for kq in range(TOP_K):
        mx = jnp.max(logits, axis=-1, keepdims=True)
        idx = jnp.min(jnp.where(logits == mx, lane_f, float(LANES)), axis=-1, keepdims=True)
        hit = lane_f == idx
        vals.append(mx)
        route = jnp.where(lane == kq, idx, route)
        onehot = jnp.where(hit, 1.0, onehot)
        logits = jnp.where(hit, -jnp.inf, logits)
    ex = [jnp.exp(vq - vals[0]) for vq in vals]
    den = ex[0] + ex[1] + ex[2] + ex[3]
    for kq in range(TOP_K):
        route = jnp.where(lane == TOP_K + kq, ex[kq] / den, route)
    route_ref[...] = route

    @pl.when(pl.program_id(0) == 0)
    def _():
        cnt_ref[...] = jnp.zeros_like(cnt_ref)

    cnt_ref[...] += jnp.sum(onehot, axis=0, keepdims=True)


def _mix(x2, y_rwkv, outs, lses, y_mem, gates, p_rwkv, p_dil, p_mem, w_out, ln2, w_router, b_router, bm=512):
    t, d = x2.shape
    wr = jnp.zeros((d, LANES), F32).at[:, :N_EXPERTS].set(w_router)
    br = jnp.full((1, LANES), -jnp.inf, F32).at[0, :N_EXPERTS].set(b_router)
    row = lambda w: pl.BlockSpec((bm, w), lambda i: (i, 0))
    const = lambda a: pl.BlockSpec(a.shape, lambda i: (0,) * a.ndim)
    tiles_per_seq = outs[0].shape[2] // bm
    dil = pl.BlockSpec((1, DIL_GROUP_DIM // LANES, bm, LANES),
                       lambda i: (i // tiles_per_seq, 0, i % tiles_per_seq, 0))
    weights = [p_rwkv.astype(BF16), p_dil.astype(BF16), p_mem.astype(BF16), w_out.astype(BF16),
               ln2.reshape(1, d), wr, br]
    return pl.pallas_call(
        _mix_kernel,
        grid=(t // bm,),
        in_specs=[row(d), row(RWKV_DIM)] + [dil] * 6 + [row(MEM_DIM), row(3 * d)]
                 + [const(a) for a in weights],
        out_specs=[row(d), row(d), row(LANES), pl.BlockSpec((1, LANES), lambda i: (0, 0))],
        out_shape=[jax.ShapeDtypeStruct((t, d), F32), jax.ShapeDtypeStruct((t, d), F32),
                   jax.ShapeDtypeStruct((t, LANES), F32), jax.ShapeDtypeStruct((1, LANES), F32)],
        compiler_params=_cparams(("arbitrary",)),
        name="mix",
    )(x2, y_rwkv, *outs, *lses, y_mem, gates, *weights)


def _route_kernel(route_ref, pstart_ref, tri_ref, dest_ref, carry_ref):
    @pl.when(pl.program_id(0) == 0)
    def _():
        carry_ref[...] = jnp.zeros_like(carry_ref)

    route = route_ref[...]
    lane = lax.broadcasted_iota(jnp.int32, route.shape, 1)
    lane_f = lane.astype(F32)
    hits = [lane_f == route[:, kq:kq + 1] for kq in range(TOP_K)]
    onehot = jnp.zeros(route.shape, F32)
    for hq in hits:
        onehot = jnp.where(hq, 1.0, onehot)
    rank = _dot(tri_ref[...], onehot.astype(BF16)) + carry_ref[...]
    slot = pstart_ref[...] + rank
    dest = jnp.zeros(route.shape, jnp.int32)
    for kq in range(TOP_K):
        dk = jnp.sum(jnp.where(hits[kq], slot, 0.0), axis=-1, keepdims=True)
        dest = jnp.where(lane == kq, dk.astype(jnp.int32), dest)
    dest_ref[...] = dest
    carry_ref[...] += jnp.sum(onehot, axis=0, keepdims=True)


def _route(route, pstart, bm=1024):
    t = route.shape[0]
    i = jnp.arange(bm)
    tri = (i[None, :] < i[:, None]).astype(BF16)
    return pl.pallas_call(
        _route_kernel,
        grid=(t // bm,),
        in_specs=[pl.BlockSpec((bm, LANES), lambda i: (i, 0)),
                  pl.BlockSpec((1, LANES), lambda i: (0, 0)),
                  pl.BlockSpec((bm, bm), lambda i: (0, 0))],
        out_specs=pl.BlockSpec((bm, LANES), lambda i: (i, 0)),
        out_shape=jax.ShapeDtypeStruct((t, LANES), jnp.int32),
        scratch_shapes=[pltpu.VMEM((1, LANES), F32)],
        compiler_params=_cparams(("arbitrary",)),
        name="route",
    )(route, pstart, tri)


def _row_copy(src_ref, src_row, dst_ref, dst_row, sem):
    return pltpu.make_async_copy(src_ref.at[pl.ds(src_row, 1)], dst_ref.at[pl.ds(dst_row, 1)], sem)


def _dispatch_kernel(last_ref, dest_ref, h_ref, xs_ref, zero_ref, hbuf_ref, sem_s, sem_l, zsem, *, bm):
    i = pl.program_id(0)
    n = pl.num_programs(0)

    def load(tile, b):
        rows = pl.ds(pl.multiple_of(tile * bm, bm), bm)
        return pltpu.make_async_copy(h_ref.at[rows], hbuf_ref.at[b], sem_l.at[b])

    def wait_scatter(b):
        for kq in range(TOP_K):
            pltpu.make_async_copy(hbuf_ref.at[0], xs_ref.at[pl.ds(0, bm)], sem_s.at[b]).wait()

    @pl.when(i == 0)
    def _():
        load(0, 0).start()
        zero_ref[...] = jnp.zeros_like(zero_ref)

        def zero_copy(e):
            row = pl.multiple_of(last_ref[e] * EXPERT_ROWS, EXPERT_ROWS)
            return pltpu.make_async_copy(zero_ref, xs_ref.at[pl.ds(row, EXPERT_ROWS)], zsem)

        def z_issue(e, carry):
            @pl.when(last_ref[e] >= 0)
            def _():
                zero_copy(e).start()
            return carry

        def z_drain(e, carry):
            @pl.when(last_ref[e] >= 0)
            def _():
                zero_copy(e).wait()
            return carry

        lax.fori_loop(0, 2 * N_EXPERTS, z_issue, 0)
        lax.fori_loop(0, 2 * N_EXPERTS, z_drain, 0)

    @pl.when(i + 1 < n)
    def _():
        load(i + 1, (i + 1) % 3).start()

    cur = i % 3
    load(i, cur).wait()

    def issue(j, carry):
        for kq in range(TOP_K):
            _row_copy(hbuf_ref.at[cur], j, xs_ref, dest_ref[j * TOP_K + kq], sem_s.at[i % 2]).start(priority=kq % 2)
        return carry

    lax.fori_loop(0, bm, issue, 0)

    @pl.when(i > 0)
    def _():
        wait_scatter((i - 1) % 2)

    @pl.when(i == n - 1)
    def _():
        wait_scatter(i % 2)


def _dispatch(last_blk, dest_flat, h2, n_slots, bm=256):
    t, d = h2.shape
    grid_spec = pltpu.PrefetchScalarGridSpec(
        num_scalar_prefetch=1,
        grid=(t // bm,),
        in_specs=[pl.BlockSpec((bm * TOP_K,), lambda i, lb: (i,), memory_space=pltpu.SMEM),
                  pl.BlockSpec(memory_space=pl.ANY)],
        out_specs=pl.BlockSpec(memory_space=pl.ANY),
        scratch_shapes=[pltpu.VMEM((EXPERT_ROWS, d), h2.dtype), pltpu.VMEM((3, bm, d), h2.dtype),
                        pltpu.SemaphoreType.DMA((2,)), pltpu.SemaphoreType.DMA((3,)), pltpu.SemaphoreType.DMA],
    )
    return pl.pallas_call(
        functools.partial(_dispatch_kernel, bm=bm),
        grid_spec=grid_spec,
        out_shape=jax.ShapeDtypeStruct((n_slots, d), h2.dtype),
        compiler_params=_cparams(("arbitrary",)),
        name="dispatch",
    )(last_blk, dest_flat, h2)


FF_CHUNK = 256
EXPERT_ROWS = 256
BLOCKS_PER_STEP = 2


def _expert_kernel(be_ref, par_ref, nxt_ref, nu_ref, xs_ref, w1_ref, b1_ref, w2_ref, b2_ref, sel_ref, ys_ref,
                   w1f_ref, w2f_ref, w1p_ref, w2b_ref, act_ref, sem):
    i = pl.program_id(0)
    d_ff2 = w1_ref.shape[2]
    half = FF_CHUNK // 2

    def weight_copies(expert, buf):
        return (pltpu.make_async_copy(w1_ref.at[expert], w1f_ref.at[buf], sem.at[0, buf]),
                pltpu.make_async_copy(w2_ref.at[expert], w2f_ref.at[buf], sem.at[1, buf]))

    @pl.when(i == 0)
    def _():
        for cp in weight_copies(be_ref[0], par_ref[0]):
            cp.start()

    for hb_i in range(BLOCKS_PER_STEP):
        blk = i * BLOCKS_PER_STEP + hb_i
        rows = slice(hb_i * EXPERT_ROWS, (hb_i + 1) * EXPERT_ROWS)
        e = be_ref[blk]
        slot = par_ref[blk]
        fresh = jnp.logical_or(blk == 0, be_ref[jnp.maximum(blk - 1, 0)] != e)
        used = blk < nu_ref[0]

        @pl.when(jnp.logical_and(fresh, used))
        def _():
            for cp in weight_copies(e, slot):
                cp.wait()
            nxt = nxt_ref[blk]

            @pl.when(nxt >= 0)
            def _():
                for cp in weight_copies(nxt, 1 - slot):
                    cp.start()

            for c in range(d_ff2 // FF_CHUNK):
                sl = slice(c * FF_CHUNK, (c + 1) * FF_CHUNK)
                w1p_ref[:, sl] = _dot(w1f_ref[slot, :, sl].astype(BF16), sel_ref[...]).astype(BF16)
            w2b_ref[...] = w2f_ref[slot].astype(BF16)

        @pl.when(used)
        def _():
            x = xs_ref[rows, :].astype(BF16)
            for c in range(d_ff2 // (2 * FF_CHUNK)):
                sl = slice(2 * c * FF_CHUNK, 2 * (c + 1) * FF_CHUNK)
                hb = _dot(x, w1p_ref[:, sl]) + b1_ref[e, :, sl]
                for j in range(2):
                    x_glu = jnp.minimum(hb[:, j * FF_CHUNK:j * FF_CHUNK + half], SWIGLU_LIMIT)
                    x_lin = jnp.clip(hb[:, j * FF_CHUNK + half:(j + 1) * FF_CHUNK], -SWIGLU_LIMIT, SWIGLU_LIMIT)
                    act = x_glu * _sigmoid(SWIGLU_ALPHA * x_glu) * (x_lin + 1.0)
                    act_ref[:, (2 * c + j) * half:(2 * c + j + 1) * half] = act.astype(BF16)
            ys_ref[rows, :] = _dot(act_ref[...], w2b_ref[...]) + b2_ref[e]

        @pl.when(jnp.logical_not(used))
        def _():
            ys_ref[rows, :] = jnp.zeros((EXPERT_ROWS, ys_ref.shape[1]), F32)


def _chunk_deinterleave(a):
    lead = a.shape[:-1]
    a = a.reshape(lead + (a.shape[-1] // FF_CHUNK, FF_CHUNK // 2, 2))
    return jnp.swapaxes(a, -1, -2).reshape(lead + (-1,))


def _experts(block_e, parity, next_e, n_used, xs, w1, b1, w2, b2):
    n_slots, d = xs.shape
    n_e, _, d_ff2 = w1.shape
    n_blocks = n_slots // EXPERT_ROWS
    i = jnp.arange(FF_CHUNK)
    src = jnp.where(i < FF_CHUNK // 2, 2 * i, 2 * (i - FF_CHUNK // 2) + 1)
    sel = (jnp.arange(FF_CHUNK)[:, None] == src[None, :]).astype(BF16)
    b1p = _chunk_deinterleave(b1).reshape(n_e, 1, d_ff2)
    step_rows = BLOCKS_PER_STEP * EXPERT_ROWS
    imap = lambda i, be, par, nxt, nu: (i, 0)
    xmap = lambda i, be, par, nxt, nu: (jnp.minimum(i, (nu[0] - 1) // BLOCKS_PER_STEP), 0)
    cmap = lambda i, be, par, nxt, nu: (0, 0, 0)
    grid_spec = pltpu.PrefetchScalarGridSpec(
        num_scalar_prefetch=4,
        grid=(n_blocks // BLOCKS_PER_STEP,),
        in_specs=[pl.BlockSpec((step_rows, d), xmap),
                  pl.BlockSpec(memory_space=pl.ANY),
                  pl.BlockSpec((n_e, 1, d_ff2), cmap),
                  pl.BlockSpec(memory_space=pl.ANY),
                  pl.BlockSpec((n_e, 1, d), cmap),
                  pl.BlockSpec((FF_CHUNK, FF_CHUNK), lambda i, be, par, nxt, nu: (0, 0))],
        out_specs=pl.BlockSpec((step_rows, d), imap),
        scratch_shapes=[pltpu.VMEM((2, d, d_ff2), F32), pltpu.VMEM((2, d_ff2 // 2, d), F32),
                        pltpu.VMEM((d, d_ff2), BF16), pltpu.VMEM((d_ff2 // 2, d), BF16),
                        pltpu.VMEM((EXPERT_ROWS, d_ff2 // 2), BF16),
                        pltpu.SemaphoreType.DMA((2, 2))],
    )
    return pl.pallas_call(
        _expert_kernel,
        grid_spec=grid_spec,
        out_shape=jax.ShapeDtypeStruct((n_slots, d), F32),
        compiler_params=_cparams(("arbitrary",), vmem=EXPERT_VMEM_LIMIT),
        name="experts",
    )(block_e, parity, next_e, n_used, xs, w1, b1p, w2, b2.reshape(n_e, 1, d), sel)


def _combine_kernel(dest_ref, dnext_ref, x1_ref, route_ref, ys_ref, o_ref, buf_ref, sem, *, bm):
    i = pl.program_id(0)
    n = pl.num_programs(0)
    slot = i % 2

    def gather(d_ref, buf):
        def issue(g, carry):
            base = pl.multiple_of(g * SUBLANES, SUBLANES)
            for r in range(SUBLANES):
                for kq in range(TOP_K):
                    _row_copy(ys_ref, d_ref[(base + r) * TOP_K + kq], buf_ref.at[buf, kq], base + r,
                              sem.at[buf]).start(priority=kq % 2)
            return carry

        lax.fori_loop(0, bm // SUBLANES, issue, 0)

    @pl.when(i == 0)
    def _():
        gather(dest_ref, slot)

    @pl.when(i + 1 < n)
    def _():
        gather(dnext_ref, 1 - slot)

    for kq in range(TOP_K):
        pltpu.make_async_copy(ys_ref.at[pl.ds(0, bm)], buf_ref.at[slot, kq], sem.at[slot]).wait()
    acc = x1_ref[...]
    for kq in range(TOP_K):
        acc = acc + route_ref[:, TOP_K + kq:TOP_K + kq + 1] * buf_ref[slot, kq]
    o_ref[...] = acc


def _combine(dest_flat, x1, route, ys, bm=256):
    t, d = x1.shape
    n = t // bm
    return pl.pallas_call(
        functools.partial(_combine_kernel, bm=bm),
        grid=(n,),
        in_specs=[pl.BlockSpec((bm * TOP_K,), lambda i: (i,), memory_space=pltpu.SMEM),
                  pl.BlockSpec((bm * TOP_K,), lambda i: (jnp.minimum(i + 1, n - 1),), memory_space=pltpu.SMEM),
                  pl.BlockSpec((bm, d), lambda i: (i, 0)),
                  pl.BlockSpec((bm, LANES), lambda i: (i, 0)),
                  pl.BlockSpec(memory_space=pl.ANY)],
        out_specs=pl.BlockSpec((bm, d), lambda i: (i, 0)),
        out_shape=jax.ShapeDtypeStruct((t, d), F32),
        scratch_shapes=[pltpu.VMEM((2, TOP_K, bm, d), F32), pltpu.SemaphoreType.DMA((2,))],
        compiler_params=_cparams(("arbitrary",)),
        name="combine",
    )(dest_flat, dest_flat, x1, route, ys)


def _layer(x, mem, ln1, w_in, token_mu, rwkv_w0, rwkv_w2, rwkv_a0, rwkv_a2, rwkv_g2,
           rwkv_k_k, rwkv_k_a, rwkv_r_k, rwkv_ln_w, rwkv_ln_b, dil_q_norm, dil_k_norm,
           ln_mem, w_mem_kv, mem_q_norm, mem_k_norm, p_rwkv, p_dil, p_mem, w_out,
           ln2, w_router, b_router, w1, b1, w2, b2):
    b, s, d = x.shape
    t = b * s
    x2 = x.reshape(t, d)
    zr, dil_qkv, mq, gates = _in_proj(x2, ln1, w_in, dil_q_norm, dil_k_norm, mem_q_norm)
    y_rwkv = _rwkv(zr.reshape(b, s, RWKV_IN), token_mu, rwkv_w0, rwkv_w2, rwkv_a0, rwkv_a2, rwkv_g2,
                   rwkv_k_k, rwkv_k_a, rwkv_r_k.reshape(-1), rwkv_ln_w, rwkv_ln_b)
    outs, lses = [], []
    for g, (_, dilation) in enumerate(DIL_PATTERNS):
        o, l = _dil_group(*dil_qkv[3 * g:3 * g + 3], b, dilation)
        outs.append(o)
        lses.append(l)
    mk, mv = _mem_kv(mem, ln_mem, w_mem_kv, mem_k_norm)
    y_mem = _mem_attn(mq.reshape(b, s, MEM_DIM), mk, mv)
    x1, h2, route, counts = _mix(x2, y_rwkv.reshape(t, RWKV_DIM), outs, lses, y_mem.reshape(t, MEM_DIM), gates,
                                 p_rwkv, p_dil, p_mem, w_out, ln2, w_router, b_router)

    counts = counts[0, :N_EXPERTS].astype(jnp.int32)
    nblk = (counts + EXPERT_ROWS - 1) // EXPERT_ROWS
    bend = jnp.cumsum(nblk)
    pstart = ((bend - nblk) * EXPERT_ROWS).astype(F32)
    pstart = jnp.zeros((1, LANES), F32).at[0, :N_EXPERTS].set(pstart)
    n_blocks = (t * TOP_K) // EXPERT_ROWS + N_EXPERTS
    block_e = jnp.sum(bend[None, :] <= jnp.arange(n_blocks, dtype=jnp.int32)[:, None], axis=1)
    block_e = jnp.minimum(block_e, N_EXPERTS - 1).astype(jnp.int32)
    n_used = bend[-1:].astype(jnp.int32)
    used = nblk > 0
    eids = jnp.arange(N_EXPERTS, dtype=jnp.int32)
    ordinal = jnp.cumsum(used.astype(jnp.int32)) - 1
    later = jnp.where(used[None, :] & (eids[None, :] > eids[:, None]), eids[None, :], N_EXPERTS)
    next_used = jnp.min(later, axis=1)
    next_used = jnp.where(next_used < N_EXPERTS, next_used, -1).astype(jnp.int32)
    parity = (ordinal[block_e] % 2).astype(jnp.int32)
    next_e = next_used[block_e]
    tail = n_used[0] + eids
    last_blk = jnp.concatenate([jnp.where(used, bend - 1, -1), jnp.where(tail < n_blocks, tail, -1)]).astype(jnp.int32)

    dest = _route(route, pstart)[:, :TOP_K].reshape(-1)
    xs = _dispatch(last_blk, dest, h2, n_blocks * EXPERT_ROWS)
    ys = _experts(block_e, parity, next_e, n_used, xs, w1, b1, w2, b2)
    out = _combine(dest, x1, route, ys)
    return out.reshape(b, s, d)


def kernel(x, mem, ln1, w_in, token_mu, rwkv_w0, rwkv_w2, rwkv_a0, rwkv_a2, rwkv_g2, rwkv_k_k, rwkv_k_a, rwkv_r_k, rwkv_ln_w, rwkv_ln_b, dil_q_norm, dil_k_norm, ln_mem, w_mem_kv, mem_q_norm, mem_k_norm, p_rwkv, p_dil, p_mem, w_out, ln2, w_router, b_router, w1, b1, w2, b2):
    params = (ln1, w_in, token_mu, rwkv_w0, rwkv_w2, rwkv_a0, rwkv_a2, rwkv_g2, rwkv_k_k, rwkv_k_a,
              rwkv_r_k, rwkv_ln_w, rwkv_ln_b, dil_q_norm, dil_k_norm, ln_mem, w_mem_kv, mem_q_norm,
              mem_k_norm, p_rwkv, p_dil, p_mem, w_out, ln2, w_router, b_router, w1, b1, w2, b2)
    for l in range(ln1.shape[0]):
        x = _layer(x, mem, *[p[l] for p in params])
    return x
```

```python
import functools

import jax
import jax.numpy as jnp
from jax import lax
from jax.experimental import pallas as pl
from jax.experimental.pallas import tpu as pltpu

F32 = jnp.float32
BF16 = jnp.bfloat16

NORM_EPS = 1e-5
HEAD_DIM = 64
RWKV_DIM = 512
RWKV_IN = 1792
GN_EPS = HEAD_DIM * 1e-5
DIL_DIM = 768
DIL_GROUP_DIM = 256
DIL_PATTERNS = ((128, 1), (512, 4), (2048, 16))
ATTN_BLOCK = 128
MEM_DIM = 512
MEM_HEAD_DIM = 128
N_EXPERTS = 32
TOP_K = 4
SWIGLU_ALPHA = 1.702
SWIGLU_LIMIT = 7.0
LANES = 128
SUBLANES = 8
CHUNK = 64
NEG_BIG = -1e30
VMEM_LIMIT = 48 * 1024 * 1024
EXPERT_VMEM_LIMIT = 56 * 1024 * 1024


def _dot(a, b):
    return jnp.dot(a, b, preferred_element_type=F32)


def _dot_nt(a, b):
    return lax.dot_general(a, b, (((1,), (1,)), ((), ())), preferred_element_type=F32)


def _split2(x):
    hi = x.astype(BF16)
    lo = (x - hi.astype(F32)).astype(BF16)
    return hi, lo


def _split3(x):
    hi = x.astype(BF16)
    r = x - hi.astype(F32)
    lo = r.astype(BF16)
    lo2 = (r - lo.astype(F32)).astype(BF16)
    return hi, lo, lo2


def _dot_rhs3(w_bf16, x):
    hi, lo, lo2 = _split3(x)
    return _dot(w_bf16, hi) + _dot(w_bf16, lo) + _dot(w_bf16, lo2)


def _dot_x3(a, b):
    ah, al = _split2(a)
    bh, bl = _split2(b)
    return _dot(ah, bh) + _dot(al, bh) + _dot(ah, bl)


def _sigmoid(x):
    return 1.0 / (1.0 + jnp.exp(-x))


def _cparams(sem, vmem=VMEM_LIMIT):
    return pltpu.CompilerParams(dimension_semantics=sem, vmem_limit_bytes=vmem)


IN_CHUNK = 256
PERM_ROWS = 256
N_ZR = RWKV_IN // IN_CHUNK
N_DIL = 3 * DIL_DIM // IN_CHUNK
N_MQ = MEM_DIM // IN_CHUNK
N_GATE = 3 * 1024 // IN_CHUNK


def _in_proj_kernel(x_ref, ln_ref, w_ref, qg_ref, kg_ref, mg_ref, p4_ref, p16_ref,
                    zr_ref, *rest, bm):
    dil_refs, (mq_ref, gt_ref) = rest[:N_DIL], rest[N_DIL:]
    perm_refs = (None, p4_ref, p16_ref)
    x = x_ref[...]
    h = x * lax.rsqrt(jnp.mean(x * x, axis=-1, keepdims=True) + NORM_EPS) * ln_ref[...]
    hb = h.astype(BF16)

    lane = lax.broadcasted_iota(jnp.int32, (x.shape[0], LANES), 1)

    def seg_rms(z, seg, gain):
        z2 = z * z
        cols = []
        for cb in range(IN_CHUNK // LANES):
            blk = z2[:, cb * LANES:(cb + 1) * LANES]
            ss = jnp.zeros_like(blk)
            for sg in range(LANES // seg):
                m = (lane >= sg * seg) & (lane < (sg + 1) * seg)
                ss = jnp.where(m, jnp.sum(jnp.where(m, blk, 0.0), axis=-1, keepdims=True), ss)
            cols.append(ss)
        ss = jnp.concatenate(cols, axis=1)
        return z * lax.rsqrt(ss * (1.0 / seg) + NORM_EPS) * gain

    for c in range(N_ZR + N_DIL + N_MQ + N_GATE):
        z = _dot(hb, w_ref[:, c * IN_CHUNK:(c + 1) * IN_CHUNK])
        if c < N_ZR:
            zr_ref[:, c * IN_CHUNK:(c + 1) * IN_CHUNK] = z
            continue
        d = c - N_ZR
        if d < N_DIL:
            which, g = divmod(d, 3)
            if which == 0:
                z = seg_rms(z, HEAD_DIM, qg_ref[...])
            elif which == 1:
                z = seg_rms(z, HEAD_DIM, kg_ref[...])
            zb = z.astype(BF16)
            o_ref = dil_refs[g * 3 + which]
            dilation = DIL_PATTERNS[g][1]
            if dilation == 1:
                o_ref[...] = zb
            else:
                rows = PERM_ROWS // dilation
                for sub in range(bm // PERM_ROWS):
                    zp = _dot(perm_refs[g][...], zb[sub * PERM_ROWS:(sub + 1) * PERM_ROWS, :]).astype(BF16)
                    for r in range(dilation):
                        o_ref[sub * rows:(sub + 1) * rows, r * IN_CHUNK:(r + 1) * IN_CHUNK] = zp[r * rows:(r + 1) * rows, :]
            continue
        d -= N_DIL
        if d < N_MQ:
            mq_ref[:, d * IN_CHUNK:(d + 1) * IN_CHUNK] = seg_rms(z, MEM_HEAD_DIM, mg_ref[...]).astype(BF16)
            continue
        d -= N_MQ
        gt_ref[:, d * IN_CHUNK:(d + 1) * IN_CHUNK] = _sigmoid(z).astype(BF16)


def _class_perm(bm, dilation):
    i = jnp.arange(bm)
    src = (i % (bm // dilation)) * dilation + i // (bm // dilation)
    return (src[:, None] == i[None, :]).astype(BF16)


def _in_proj(x2, ln1, w_in, dil_q_norm, dil_k_norm, mem_q_norm, bm=512):
    t, d = x2.shape
    n_cols = w_in.shape[1]
    wb = w_in.astype(BF16)
    qg = (jnp.tile(dil_q_norm, IN_CHUNK // HEAD_DIM) * (HEAD_DIM ** -0.5)).reshape(1, IN_CHUNK)
    kg = jnp.tile(dil_k_norm, IN_CHUNK // HEAD_DIM).reshape(1, IN_CHUNK)
    mg = jnp.tile(mem_q_norm, IN_CHUNK // MEM_HEAD_DIM).reshape(1, IN_CHUNK)
    p4 = _class_perm(PERM_ROWS, DIL_PATTERNS[1][1])
    p16 = _class_perm(PERM_ROWS, DIL_PATTERNS[2][1])
    row = lambda w: pl.BlockSpec((bm, w), lambda i: (i, 0))
    const = lambda a: pl.BlockSpec(a.shape, lambda i: (0,) * a.ndim)
    dil_specs, dil_shapes = [], []
    for _, dilation in DIL_PATTERNS:
        for _ in range(3):
            dil_specs.append(pl.BlockSpec((bm // dilation, dilation * DIL_GROUP_DIM), lambda i: (i, 0)))
            dil_shapes.append(jax.ShapeDtypeStruct((t // dilation, dilation * DIL_GROUP_DIM), BF16))
    outs = pl.pallas_call(
        functools.partial(_in_proj_kernel, bm=bm),
        grid=(t // bm,),
        in_specs=[row(d), pl.BlockSpec((1, d), lambda i: (0, 0)),
                  pl.BlockSpec((d, n_cols), lambda i: (0, 0), pipeline_mode=pl.Buffered(1)),
                  const(qg), const(kg), const(mg), const(p4), const(p16)],
        out_specs=[row(RWKV_IN)] + dil_specs + [row(MEM_DIM), row(3 * d)],
        out_shape=[jax.ShapeDtypeStruct((t, RWKV_IN), F32)] + dil_shapes
                  + [jax.ShapeDtypeStruct((t, MEM_DIM), BF16), jax.ShapeDtypeStruct((t, 3 * d), BF16)],
        compiler_params=_cparams(("parallel",)),
        name="in_proj",
    )(x2, ln1.reshape(1, d), wb, qg, kg, mg, p4, p16)
    return outs[0], outs[1:1 + N_DIL], outs[1 + N_DIL], outs[2 + N_DIL]


def _rwkv_kernel(z_ref, mu_ref, w0_ref, a0_ref, wl_ref, kk_ref, ka_ref, rk_ref,
                 lnw_ref, lnb_ref, tri_ref, y_ref, carry_ref, state_ref, ybuf_ref, *, tt):
    s_idx = pl.program_id(1)

    @pl.when(s_idx == 0)
    def _():
        carry_ref[...] = jnp.zeros_like(carry_ref)
        state_ref[...] = jnp.zeros_like(state_ref)

    z = z_ref[0]
    rows = lax.broadcasted_iota(jnp.int32, z.shape, 0)
    prev = jnp.where(rows == 0, carry_ref[...], pltpu.roll(z, 1, axis=0))
    carry_ref[...] = z[tt - 1:tt, :]
    zl = z + (prev - z) * mu_ref[...]
    r = zl[:, 0:RWKV_DIM]
    k = zl[:, RWKV_DIM:2 * RWKV_DIM]
    v = zl[:, 2 * RWKV_DIM:3 * RWKV_DIM]
    zlo = zl[:, 3 * RWKV_DIM:RWKV_IN]
    lcol = lax.broadcasted_iota(jnp.int32, zlo.shape, 1)
    feat = jnp.where(lcol < 64, jnp.tanh(zlo), jnp.where(lcol < 128, zlo, _sigmoid(zlo)))
    lora = _dot(feat.astype(BF16), wl_ref[...])
    nu = -(w0_ref[...] + lora[:, 0:RWKV_DIM])
    softplus = jnp.maximum(nu, 0.0) + jnp.log(1.0 + jnp.exp(-jnp.abs(nu)))
    lw = -jnp.exp(-softplus - 0.5)
    a = _sigmoid(a0_ref[...] + lora[:, RWKV_DIM:2 * RWKV_DIM])
    g = lora[:, 2 * RWKV_DIM:3 * RWKV_DIM]

    n_pairs = RWKV_DIM // LANES
    n_chunks = tt // CHUNK
    head_lo = lax.broadcasted_iota(jnp.int32, (tt, LANES), 1) < HEAD_DIM

    def seg_sum(x):
        parts = []
        for p in range(n_pairs):
            xp = x[:, p * LANES:(p + 1) * LANES]
            lo = jnp.sum(jnp.where(head_lo, xp, 0.0), axis=-1, keepdims=True)
            hi = jnp.sum(jnp.where(head_lo, 0.0, xp), axis=-1, keepdims=True)
            parts.append(jnp.where(head_lo, lo, hi))
        return jnp.concatenate(parts, axis=1)

    kk = k * kk_ref[...]
    kk = kk / jnp.maximum(jnp.sqrt(seg_sum(kk * kk)), 1e-12)
    k2 = k * (1.0 + (a - 1.0) * ka_ref[...])
    a_s = -kk
    b_s = kk * a

    cum = _dot_rhs3(tri_ref[...], lw)
    tot = jnp.concatenate(
        [jnp.broadcast_to(cum[(c + 1) * CHUNK - 1:(c + 1) * CHUNK, :], (CHUNK, RWKV_DIM)) for c in range(n_chunks)],
        axis=0)
    e_neg = jnp.exp(-cum)
    e_end = jnp.exp(tot - cum)
    at = (a_s * jnp.exp(cum - lw)).astype(BF16)
    rt = (r * jnp.exp(cum)).astype(BF16)
    bt = (b_s * e_neg).astype(BF16)
    kt = (k2 * e_neg).astype(BF16)
    be = (b_s * e_end).astype(BF16)
    ke = (k2 * e_end).astype(BF16)
    vb = v.astype(BF16)
    e_tot = jnp.exp(tot)

    r128 = lax.broadcasted_iota(jnp.int32, (LANES, LANES), 0)
    c128 = lax.broadcasted_iota(jnp.int32, (LANES, LANES), 1)
    lane_lo = lax.broadcasted_iota(jnp.int32, (CHUNK, LANES), 1) < HEAD_DIM
    stril = c128 < r128
    tril = c128 <= r128
    eye = (c128 == r128).astype(F32)
    same16 = (r128 // 16) == (c128 // 16)
    same32 = (r128 // 32) == (c128 // 32)
    off16 = same32 & jnp.logical_not(same16)
    off32 = jnp.logical_not(same32)

    def bdiag(xp):
        zero = jnp.zeros_like(xp)
        return jnp.concatenate([jnp.where(lane_lo, xp, zero), jnp.where(lane_lo, zero, xp)], axis=0)

    units = [(ci, p) for ci in range(n_chunks) for p in range(n_pairs)]
    pick = lambda arr, u: arr[u[0] * CHUNK:(u[0] + 1) * CHUNK, u[1] * LANES:(u[1] + 1) * LANES]
    cat0 = lambda xs: jnp.concatenate(xs, axis=0)
    cat1 = lambda xs: jnp.concatenate(xs, axis=1)
    zero_b = jnp.zeros((LANES, LANES), BF16)

    at_b = [bdiag(pick(at, u)) for u in units]
    v_b = [bdiag(pick(vb, u)) for u in units]
    be_b = [bdiag(pick(be, u)) for u in units]
    ke_b = [bdiag(pick(ke, u)) for u in units]
    rt_b = [bdiag(pick(rt, u)) for u in units]
    a_ab, a_ak, a_rbk = [], [], []
    for i, u in enumerate(units):
        mq = _dot_nt(cat0([at_b[i], rt_b[i]]), cat0([bdiag(pick(bt, u)), bdiag(pick(kt, u))]))
        a_ab.append(jnp.where(stril, mq[:LANES, :LANES], 0.0).astype(BF16))
        a_ak.append(jnp.where(stril, mq[:LANES, LANES:], 0.0).astype(BF16))
        a_rbk.append(cat1([jnp.where(tril, mq[LANES:, :LANES], 0.0), jnp.where(tril, mq[LANES:, LANES:], 0.0)]).astype(BF16))
    w_b = [_dot(a_ak[i], v_b[i]).astype(BF16) for i in range(len(units))]

    d1 = [jnp.where(same16, x, jnp.zeros_like(x)) for x in a_ab]
    xs = [eye + d.astype(F32) for d in d1]
    d2 = [_dot(d, d).astype(BF16) for d in d1]
    t_ = [_dot(d2[i], cat1([xs[i].astype(BF16), d2[i]])) for i in range(len(units))]
    xs = [xs[i] + t_[i][:, :LANES] for i in range(len(units))]
    d4 = [t[:, LANES:].astype(BF16) for t in t_]
    t_ = [_dot(d4[i], cat1([xs[i].astype(BF16), d4[i]])) for i in range(len(units))]
    xs = [xs[i] + t_[i][:, :LANES] for i in range(len(units))]
    d8 = [t[:, LANES:].astype(BF16) for t in t_]
    xs = [xs[i] + _dot(d8[i], xs[i].astype(BF16)) for i in range(len(units))]
    for off in (off16, off32):
        xb = [x.astype(BF16) for x in xs]
        g_ = [_dot(jnp.where(off, a_ab[i], jnp.zeros_like(a_ab[i])), xb[i]).astype(BF16) for i in range(len(units))]
        xs = [xs[i] + _dot(xb[i], g_[i]) for i in range(len(units))]

    pq = [_dot(xs[i].astype(BF16), cat1([at_b[i], w_b[i]])) for i in range(len(units))]
    ry = [_dot(a_rbk[i], cat0([pq[i].astype(BF16), cat1([zero_b, v_b[i]])])) for i in range(len(units))]
    r2 = [(rt_b[i].astype(F32) + ry[i][:, :LANES]).astype(BF16) for i in range(len(units))]
    m_c = [_dot(pq[i][:, :LANES].T.astype(BF16), be_b[i]) for i in range(len(units))]
    n_c = [_dot(cat1([pq[i][:, LANES:].T.astype(BF16), v_b[i].astype(F32).T.astype(BF16)]), cat0([be_b[i], ke_b[i]]))
           for i in range(len(units))]

    state = [state_ref[p] for p in range(n_pairs)]
    for i, (ci, p) in enumerate(units):
        s0 = state[p]
        s_hi, s_lo = _split2(s0)
        y = _dot_nt(r2[i], s_hi) + ry[i][:, LANES:]
        mcb = m_c[i].astype(BF16)
        e_row = e_tot[ci * CHUNK:ci * CHUNK + 1, p * LANES:(p + 1) * LANES]
        state[p] = s0 * e_row + (_dot(s_hi, mcb) + _dot(s_lo, mcb)) + n_c[i]
        ybuf_ref[ci * CHUNK:(ci + 1) * CHUNK, p * LANES:(p + 1) * LANES] = jnp.where(lane_lo, y[:CHUNK], y[CHUNK:])
    for p in range(n_pairs):
        state_ref[p] = state[p]

    y = ybuf_ref[...]
    mean = seg_sum(y) * (1.0 / HEAD_DIM)
    yc = y - mean
    var = seg_sum(yc * yc) * (1.0 / HEAD_DIM)
    yn = yc * lax.rsqrt(var + GN_EPS) * lnw_ref[...] + lnb_ref[...]
    bonus = seg_sum(r * k2 * rk_ref[...]) * v
    y_ref[0] = ((yn + bonus) * g).astype(y_ref.dtype)


def _rwkv(zr, token_mu, w0, w2, a0, a2, g2, k_k, k_a, r_k, ln_w, ln_b, tt=256):
    b, s, _ = zr.shape
    i = jnp.arange(tt)
    same = (i[:, None] // CHUNK) == (i[None, :] // CHUNK)
    tri = (same & (i[None, :] <= i[:, None])).astype(BF16)
    vec = lambda a: a.reshape(1, -1)
    wl = jnp.zeros((RWKV_IN - 3 * RWKV_DIM, 3 * RWKV_DIM), F32)
    wl = wl.at[0:64, 0:RWKV_DIM].set(w2).at[64:128, RWKV_DIM:2 * RWKV_DIM].set(a2)
    wl = wl.at[128:256, 2 * RWKV_DIM:3 * RWKV_DIM].set(g2).astype(BF16)
    params = [vec(token_mu), vec(w0), vec(a0), wl, vec(k_k), vec(k_a), vec(r_k),
              vec(ln_w), vec(ln_b), tri]
    const = lambda a: pl.BlockSpec(a.shape, lambda bi, si: (0,) * a.ndim)
    return pl.pallas_call(
        functools.partial(_rwkv_kernel, tt=tt),
        grid=(b, s // tt),
        in_specs=[pl.BlockSpec((1, tt, RWKV_IN), lambda bi, si: (bi, si, 0))] + [const(a) for a in params],
        out_specs=pl.BlockSpec((1, tt, RWKV_DIM), lambda bi, si: (bi, si, 0)),
        out_shape=jax.ShapeDtypeStruct((b, s, RWKV_DIM), BF16),
        scratch_shapes=[pltpu.VMEM((1, RWKV_IN), F32),
                        pltpu.VMEM((RWKV_DIM // LANES, LANES, LANES), F32),
                        pltpu.VMEM((tt, RWKV_DIM), F32)],
        compiler_params=_cparams(("parallel", "arbitrary")),
        name="rwkv",
    )(zr, *params)


DIL_UNITS = 8


def _dil_kernel(q_ref, k_ref, v_ref, o_ref, l_ref, *, dilation, cps, nbs):
    cg = pl.program_id(1)
    jb = pl.program_id(2)
    bq = ATTN_BLOCK
    n_heads = DIL_GROUP_DIM // HEAD_DIM
    qi = lax.broadcasted_iota(jnp.int32, (bq, 2 * bq), 0)
    kj = lax.broadcasted_iota(jnp.int32, (bq, 2 * bq), 1)
    lane = lax.broadcasted_iota(jnp.int32, (bq, DIL_GROUP_DIM), 1)
    head_masks = [(lane >= h * HEAD_DIM) & (lane < (h + 1) * HEAD_DIM) for h in range(n_heads)]
    units = [(c, bb) for c in range(cps) for bb in range(nbs)]

    qs, kcats, vcats, masks, starts = [], [], [], [], []
    for c, bb in units:
        i = jb * nbs + bb
        cols = slice(c * DIL_GROUP_DIM, (c + 1) * DIL_GROUP_DIM)
        p0 = pl.multiple_of(jnp.maximum(i - 1, 0) * bq, bq)
        c0 = pl.multiple_of(i * bq, bq)
        qs.append(q_ref[0, bb * bq:(bb + 1) * bq, cols])
        kcats.append(jnp.concatenate([k_ref[0, pl.ds(p0, bq), cols], k_ref[0, pl.ds(c0, bq), cols]], axis=0))
        vcats.append(jnp.concatenate([v_ref[0, pl.ds(p0, bq), cols], v_ref[0, pl.ds(c0, bq), cols]], axis=0))
        first = (1 - jnp.minimum(i, 1)) * (2 * bq)
        masks.append(((kj < bq) & (kj >= qi + first)) | ((kj >= bq) & ((kj - bq) <= qi)))
        starts.append((cg * cps + c) + dilation * bq * i)
    pairs = [(u, h) for u in range(len(units)) for h in range(n_heads)]
    sc = [jnp.where(masks[u], _dot_nt(jnp.where(head_masks[h], qs[u], jnp.zeros_like(qs[u])), kcats[u]), NEG_BIG)
          for u, h in pairs]
    mx = [jnp.max(x, axis=-1, keepdims=True) for x in sc]
    pr = [jnp.exp(sc[n] - mx[n]) for n in range(len(pairs))]
    den = [jnp.sum(x, axis=-1, keepdims=True) for x in pr]
    ov = [_dot(pr[n].astype(BF16), vcats[pairs[n][0]]) / den[n] for n in range(len(pairs))]
    for u in range(len(units)):
        acc = jnp.zeros((bq, DIL_GROUP_DIM), F32)
        lacc = jnp.zeros((bq, DIL_GROUP_DIM), F32)
        for h in range(n_heads):
            n = u * n_heads + h
            acc = jnp.where(head_masks[h], ov[n], acc)
            lacc = jnp.where(head_masks[h], mx[n] + jnp.log(den[n]), lacc)
        rows = pl.ds(starts[u], bq, stride=dilation) if dilation > 1 else pl.ds(pl.multiple_of(starts[u], bq), bq)
        for half in range(DIL_GROUP_DIM // LANES):
            o_ref[0, half, rows, :] = acc[:, half * LANES:(half + 1) * LANES]
            l_ref[0, half, rows, :] = lacc[:, half * LANES:(half + 1) * LANES]


def _dil_group(q, k, v, b, dilation):
    n = q.shape[0] // b
    s = n * dilation
    nb = n // ATTN_BLOCK
    nbs = min(nb, DIL_UNITS)
    cps = DIL_UNITS // nbs
    halves = DIL_GROUP_DIM // LANES
    view = lambda a: a.reshape(b, n, dilation * DIL_GROUP_DIM)
    qmap = lambda bi, ci, ji: (bi, ji, ci)
    kmap = lambda bi, ci, ji: (bi, 0, ci)
    omap = lambda bi, ci, ji: (bi, 0, 0, 0)
    return pl.pallas_call(
        functools.partial(_dil_kernel, dilation=dilation, cps=cps, nbs=nbs),
        grid=(b, dilation // cps, nb // nbs),
        in_specs=[pl.BlockSpec((1, nbs * ATTN_BLOCK, cps * DIL_GROUP_DIM), qmap),
                  pl.BlockSpec((1, n, cps * DIL_GROUP_DIM), kmap),
                  pl.BlockSpec((1, n, cps * DIL_GROUP_DIM), kmap)],
        out_specs=[pl.BlockSpec((1, halves, s, LANES), omap),
                   pl.BlockSpec((1, halves, s, LANES), omap)],
        out_shape=[jax.ShapeDtypeStruct((b, halves, s, LANES), F32),
                   jax.ShapeDtypeStruct((b, halves, s, LANES), F32)],
        compiler_params=_cparams(("parallel", "arbitrary", "arbitrary")),
        name=f"dil_attn_d{dilation}",
    )(view(q), view(k), view(v))


def _mem_kv_kernel(m_ref, ln_ref, w_ref, kn_ref, k_ref, v_ref):
    x = m_ref[0]
    h = x * lax.rsqrt(jnp.mean(x * x, axis=-1, keepdims=True) + NORM_EPS) * ln_ref[...]
    kv = _dot(h.astype(BF16), w_ref[...])
    for hd in range(MEM_DIM // MEM_HEAD_DIM):
        sl = slice(hd * MEM_HEAD_DIM, (hd + 1) * MEM_HEAD_DIM)
        kh = kv[:, sl]
        kh = kh * lax.rsqrt(jnp.mean(kh * kh, axis=-1, keepdims=True) + NORM_EPS) * kn_ref[...]
        k_ref[0, :, sl] = kh.astype(BF16)
    v_ref[0] = kv[:, MEM_DIM:].astype(BF16)


def _mem_kv(mem, ln_mem, w_mem_kv, mem_k_norm):
    b, m, d = mem.shape
    return pl.pallas_call(
        _mem_kv_kernel,
        grid=(b,),
        in_specs=[pl.BlockSpec((1, m, d), lambda i: (i, 0, 0)),
                  pl.BlockSpec((1, d), lambda i: (0, 0)),
                  pl.BlockSpec((d, 2 * MEM_DIM), lambda i: (0, 0)),
                  pl.BlockSpec((1, MEM_HEAD_DIM), lambda i: (0, 0))],
        out_specs=[pl.BlockSpec((1, m, MEM_DIM), lambda i: (i, 0, 0)),
                   pl.BlockSpec((1, m, MEM_DIM), lambda i: (i, 0, 0))],
        out_shape=[jax.ShapeDtypeStruct((b, m, MEM_DIM), BF16),
                   jax.ShapeDtypeStruct((b, m, MEM_DIM), BF16)],
        compiler_params=_cparams(("parallel",)),
        name="mem_kv",
    )(mem, ln_mem.reshape(1, d), w_mem_kv.astype(BF16), mem_k_norm.reshape(1, MEM_HEAD_DIM))


def _mem_attn_kernel(q_ref, k_ref, v_ref, o_ref):
    for hd in range(MEM_DIM // MEM_HEAD_DIM):
        sl = slice(hd * MEM_HEAD_DIM, (hd + 1) * MEM_HEAD_DIM)
        sc = _dot_nt(q_ref[0, :, sl], k_ref[0, :, sl]) * (MEM_HEAD_DIM ** -0.5)
        m = jnp.max(sc, axis=-1, keepdims=True)
        p = jnp.exp(sc - m)
        den = jnp.sum(p, axis=-1, keepdims=True)
        o_ref[0, :, sl] = (_dot((p / den).astype(BF16), v_ref[0, :, sl])).astype(o_ref.dtype)


def _mem_attn(mq, mk, mv, bm=512):
    b, s, _ = mq.shape
    m = mk.shape[1]
    return pl.pallas_call(
        _mem_attn_kernel,
        grid=(b, s // bm),
        in_specs=[pl.BlockSpec((1, bm, MEM_DIM), lambda bi, si: (bi, si, 0)),
                  pl.BlockSpec((1, m, MEM_DIM), lambda bi, si: (bi, 0, 0)),
                  pl.BlockSpec((1, m, MEM_DIM), lambda bi, si: (bi, 0, 0))],
        out_specs=pl.BlockSpec((1, bm, MEM_DIM), lambda bi, si: (bi, si, 0)),
        out_shape=jax.ShapeDtypeStruct((b, s, MEM_DIM), BF16),
        compiler_params=_cparams(("parallel", "parallel")),
        name="mem_attn",
    )(mq, mk, mv)


def _mix_kernel(x_ref, yr_ref, o0_ref, o1_ref, o2_ref, l0_ref, l1_ref, l2_ref, ym_ref, gt_ref,
                pr_ref, pd_ref, pm_ref, wo_ref, ln2_ref, wr_ref, br_ref,
                x1_ref, h2_ref, route_ref, cnt_ref):
    d = x_ref.shape[1]
    wide = lambda ref: jnp.concatenate([ref[0, hf] for hf in range(DIL_GROUP_DIM // LANES)], axis=1)
    l0, l1, l2 = wide(l0_ref), wide(l1_ref), wide(l2_ref)
    m = jnp.maximum(jnp.maximum(l0, l1), l2)
    e0, e1, e2 = jnp.exp(l0 - m), jnp.exp(l1 - m), jnp.exp(l2 - m)
    y_dil = (e0 * wide(o0_ref) + e1 * wide(o1_ref) + e2 * wide(o2_ref)) / (e0 + e1 + e2)
    mixed = (gt_ref[:, 0:d].astype(F32) * _dot(yr_ref[...], pr_ref[...])
             + gt_ref[:, d:2 * d].astype(F32) * _dot(y_dil.astype(BF16), pd_ref[...])
             + gt_ref[:, 2 * d:3 * d].astype(F32) * _dot(ym_ref[...], pm_ref[...]))
    x1 = x_ref[...] + _dot(mixed.astype(BF16), wo_ref[...])
    x1_ref[...] = x1
    h2 = x1 * lax.rsqrt(jnp.mean(x1 * x1, axis=-1, keepdims=True) + NORM_EPS) * ln2_ref[...]
    h2_ref[...] = h2
    logits = _dot_x3(h2, wr_ref[...]) + br_ref[...]
    lane = lax.broadcasted_iota(jnp.int32, logits.shape, 1)
    lane_f = lane.astype(F32)
    route = jnp.zeros(logits.shape, F32)
    onehot = jnp.zeros(logits.shape, F32)
    vals = []
    for kq in range(TOP_K):
        mx = jnp.max(logits, axis=-1, keepdims=True)
        idx = jnp.min(jnp.where(logits == mx, lane_f, float(LANES)), axis=-1, keepdims=True)
        hit = lane_f == idx
        vals.append(mx)
        route = jnp.where(lane == kq, idx, route)
        onehot = jnp.where(hit, 1.0, onehot)
        logits = jnp.where(hit, -jnp.inf, logits)
    ex = [jnp.exp(vq - vals[0]) for vq in vals]
    den = ex[0] + ex[1] + ex[2] + ex[3]
    for kq in range(TOP_K):
        route = jnp.where(lane == TOP_K + kq, ex[kq] / den, route)
    route_ref[...] = route

    @pl.when(pl.program_id(0) == 0)
    def _():
        cnt_ref[...] = jnp.zeros_like(cnt_ref)

    cnt_ref[...] += jnp.sum(onehot, axis=0, keepdims=True)


def _mix(x2, y_rwkv, outs, lses, y_mem, gates, p_rwkv, p_dil, p_mem, w_out, ln2, w_router, b_router, bm=512):
    t, d = x2.shape
    wr = jnp.zeros((d, LANES), F32).at[:, :N_EXPERTS].set(w_router)
    br = jnp.full((1, LANES), -jnp.inf, F32).at[0, :N_EXPERTS].set(b_router)
    row = lambda w: pl.BlockSpec((bm, w), lambda i: (i, 0))
    const = lambda a: pl.BlockSpec(a.shape, lambda i: (0,) * a.ndim)
    tiles_per_seq = outs[0].shape[2] // bm
    dil = pl.BlockSpec((1, DIL_GROUP_DIM // LANES, bm, LANES),
                       lambda i: (i // tiles_per_seq, 0, i % tiles_per_seq, 0))
    weights = [p_rwkv.astype(BF16), p_dil.astype(BF16), p_mem.astype(BF16), w_out.astype(BF16),
               ln2.reshape(1, d), wr, br]
    return pl.pallas_call(
        _mix_kernel,
        grid=(t // bm,),
        in_specs=[row(d), row(RWKV_DIM)] + [dil] * 6 + [row(MEM_DIM), row(3 * d)]
                 + [const(a) for a in weights],
        out_specs=[row(d), row(d), row(LANES), pl.BlockSpec((1, LANES), lambda i: (0, 0))],
        out_shape=[jax.ShapeDtypeStruct((t, d), F32), jax.ShapeDtypeStruct((t, d), F32),
                   jax.ShapeDtypeStruct((t, LANES), F32), jax.ShapeDtypeStruct((1, LANES), F32)],
        compiler_params=_cparams(("arbitrary",)),
        name="mix",
    )(x2, y_rwkv, *outs, *lses, y_mem, gates, *weights)


def _route_kernel(route_ref, pstart_ref, tri_ref, dest_ref, carry_ref):
    @pl.when(pl.program_id(0) == 0)
    def _():
        carry_ref[...] = jnp.zeros_like(carry_ref)

    route = route_ref[...]
    lane = lax.broadcasted_iota(jnp.int32, route.shape, 1)
    lane_f = lane.astype(F32)
    hits = [lane_f == route[:, kq:kq + 1] for kq in range(TOP_K)]
    onehot = jnp.zeros(route.shape, F32)
    for hq in hits:
        onehot = jnp.where(hq, 1.0, onehot)
    rank = _dot(tri_ref[...], onehot.astype(BF16)) + carry_ref[...]
    slot = pstart_ref[...] + rank
    dest = jnp.zeros(route.shape, jnp.int32)
    for kq in range(TOP_K):
        dk = jnp.sum(jnp.where(hits[kq], slot, 0.0), axis=-1, keepdims=True)
        dest = jnp.where(lane == kq, dk.astype(jnp.int32), dest)
    dest_ref[...] = dest
    carry_ref[...] += jnp.sum(onehot, axis=0, keepdims=True)


def _route(route, pstart, bm=1024):
    t = route.shape[0]
    i = jnp.arange(bm)
    tri = (i[None, :] < i[:, None]).astype(BF16)
    return pl.pallas_call(
        _route_kernel,
        grid=(t // bm,),
        in_specs=[pl.BlockSpec((bm, LANES), lambda i: (i, 0)),
                  pl.BlockSpec((1, LANES), lambda i: (0, 0)),
                  pl.BlockSpec((bm, bm), lambda i: (0, 0))],
        out_specs=pl.BlockSpec((bm, LANES), lambda i: (i, 0)),
        out_shape=jax.ShapeDtypeStruct((t, LANES), jnp.int32),
        scratch_shapes=[pltpu.VMEM((1, LANES), F32)],
        compiler_params=_cparams(("arbitrary",)),
        name="route",
    )(route, pstart, tri)


def _row_copy(src_ref, src_row, dst_ref, dst_row, sem):
    return pltpu.make_async_copy(src_ref.at[pl.ds(src_row, 1)], dst_ref.at[pl.ds(dst_row, 1)], sem)


def _dispatch_kernel(last_ref, dest_ref, h_ref, xs_ref, zero_ref, hbuf_ref, sem_s, sem_l, zsem, *, bm):
    i = pl.program_id(0)
    n = pl.num_programs(0)

    def load(tile, b):
        rows = pl.ds(pl.multiple_of(tile * bm, bm), bm)
        return pltpu.make_async_copy(h_ref.at[rows], hbuf_ref.at[b], sem_l.at[b])

    def wait_scatter(b):
        for kq in range(TOP_K):
            pltpu.make_async_copy(hbuf_ref.at[0], xs_ref.at[pl.ds(0, bm)], sem_s.at[b]).wait()

    @pl.when(i == 0)
    def _():
        load(0, 0).start()
        zero_ref[...] = jnp.zeros_like(zero_ref)

        def zero_copy(e):
            row = pl.multiple_of(last_ref[e] * EXPERT_ROWS, EXPERT_ROWS)
            return pltpu.make_async_copy(zero_ref, xs_ref.at[pl.ds(row, EXPERT_ROWS)], zsem)

        def z_issue(e, carry):
            @pl.when(last_ref[e] >= 0)
            def _():
                zero_copy(e).start()
            return carry

        def z_drain(e, carry):
            @pl.when(last_ref[e] >= 0)
            def _():
                zero_copy(e).wait()
            return carry

        lax.fori_loop(0, 2 * N_EXPERTS, z_issue, 0)
        lax.fori_loop(0, 2 * N_EXPERTS, z_drain, 0)

    @pl.when(i + 1 < n)
    def _():
        load(i + 1, (i + 1) % 3).start()

    load(i, i % 3).wait()

    for phase in range(6):
        @pl.when(i % 6 == phase)
        def _(phase=phase):
            def issue(j, carry):
                for kq in range(TOP_K):
                    _row_copy(hbuf_ref.at[phase % 3], j, xs_ref, dest_ref[j * TOP_K + kq],
                              sem_s.at[phase % 2]).start(priority=kq % 2)
                return carry

            lax.fori_loop(0, bm, issue, 0)

    @pl.when(i > 0)
    def _():
        wait_scatter((i - 1) % 2)

    @pl.when(i == n - 1)
    def _():
        wait_scatter(i % 2)


def _dispatch(last_blk, dest_flat, h2, n_slots, bm=512):
    t, d = h2.shape
    grid_spec = pltpu.PrefetchScalarGridSpec(
        num_scalar_prefetch=1,
        grid=(t // bm,),
        in_specs=[pl.BlockSpec((bm * TOP_K,), lambda i, lb: (i,), memory_space=pltpu.SMEM),
                  pl.BlockSpec(memory_space=pl.ANY)],
        out_specs=pl.BlockSpec(memory_space=pl.ANY),
        scratch_shapes=[pltpu.VMEM((EXPERT_ROWS, d), h2.dtype), pltpu.VMEM((3, bm, d), h2.dtype),
                        pltpu.SemaphoreType.DMA((2,)), pltpu.SemaphoreType.DMA((3,)), pltpu.SemaphoreType.DMA],
    )
    return pl.pallas_call(
        functools.partial(_dispatch_kernel, bm=bm),
        grid_spec=grid_spec,
        out_shape=jax.ShapeDtypeStruct((n_slots, d), h2.dtype),
        compiler_params=_cparams(("arbitrary",)),
        name="dispatch",
    )(last_blk, dest_flat, h2)


FF_CHUNK = 256
EXPERT_ROWS = 256
BLOCKS_PER_STEP = 2


def _expert_kernel(be_ref, par_ref, nxt_ref, nu_ref, xs_ref, w1_ref, b1_ref, w2_ref, b2_ref, sel_ref, ys_ref,
                   w1f_ref, w2f_ref, w1p_ref, w2b_ref, act_ref, sem):
    i = pl.program_id(0)
    d_ff2 = w1_ref.shape[2]
    half = FF_CHUNK // 2

    def weight_copies(expert, buf):
        return (pltpu.make_async_copy(w1_ref.at[expert], w1f_ref.at[buf], sem.at[0, buf]),
                pltpu.make_async_copy(w2_ref.at[expert], w2f_ref.at[buf], sem.at[1, buf]))

    @pl.when(i == 0)
    def _():
        for cp in weight_copies(be_ref[0], par_ref[0]):
            cp.start()

    for hb_i in range(BLOCKS_PER_STEP):
        blk = i * BLOCKS_PER_STEP + hb_i
        rows = slice(hb_i * EXPERT_ROWS, (hb_i + 1) * EXPERT_ROWS)
        e = be_ref[blk]
        slot = par_ref[blk]
        fresh = jnp.logical_or(blk == 0, be_ref[jnp.maximum(blk - 1, 0)] != e)
        used = blk < nu_ref[0]

        @pl.when(jnp.logical_and(fresh, used))
        def _():
            for cp in weight_copies(e, slot):
                cp.wait()
            nxt = nxt_ref[blk]

            @pl.when(nxt >= 0)
            def _():
                for cp in weight_copies(nxt, 1 - slot):
                    cp.start()

            for c in range(d_ff2 // FF_CHUNK):
                sl = slice(c * FF_CHUNK, (c + 1) * FF_CHUNK)
                w1p_ref[:, sl] = _dot(w1f_ref[slot, :, sl].astype(BF16), sel_ref[...]).astype(BF16)
            w2b_ref[...] = w2f_ref[slot].astype(BF16)

        @pl.when(used)
        def _():
            x = xs_ref[rows, :].astype(BF16)
            for c in range(d_ff2 // (2 * FF_CHUNK)):
                sl = slice(2 * c * FF_CHUNK, 2 * (c + 1) * FF_CHUNK)
                hb = _dot(x, w1p_ref[:, sl]) + b1_ref[e, :, sl]
                for j in range(2):
                    x_glu = jnp.minimum(hb[:, j * FF_CHUNK:j * FF_CHUNK + half], SWIGLU_LIMIT)
                    x_lin = jnp.clip(hb[:, j * FF_CHUNK + half:(j + 1) * FF_CHUNK], -SWIGLU_LIMIT, SWIGLU_LIMIT)
                    act = x_glu * _sigmoid(SWIGLU_ALPHA * x_glu) * (x_lin + 1.0)
                    act_ref[:, (2 * c + j) * half:(2 * c + j + 1) * half] = act.astype(BF16)
            ys_ref[rows, :] = _dot(act_ref[...], w2b_ref[...]) + b2_ref[e]

        @pl.when(jnp.logical_not(used))
        def _():
            ys_ref[rows, :] = jnp.zeros((EXPERT_ROWS, ys_ref.shape[1]), F32)


def _chunk_deinterleave(a):
    lead = a.shape[:-1]
    a = a.reshape(lead + (a.shape[-1] // FF_CHUNK, FF_CHUNK // 2, 2))
    return jnp.swapaxes(a, -1, -2).reshape(lead + (-1,))


def _experts(block_e, parity, next_e, n_used, xs, w1, b1, w2, b2):
    n_slots, d = xs.shape
    n_e, _, d_ff2 = w1.shape
    n_blocks = n_slots // EXPERT_ROWS
    i = jnp.arange(FF_CHUNK)
    src = jnp.where(i < FF_CHUNK // 2, 2 * i, 2 * (i - FF_CHUNK // 2) + 1)
    sel = (jnp.arange(FF_CHUNK)[:, None] == src[None, :]).astype(BF16)
    b1p = _chunk_deinterleave(b1).reshape(n_e, 1, d_ff2)
    step_rows = BLOCKS_PER_STEP * EXPERT_ROWS
    imap = lambda i, be, par, nxt, nu: (i, 0)
    xmap = lambda i, be, par, nxt, nu: (jnp.minimum(i, (nu[0] - 1) // BLOCKS_PER_STEP), 0)
    cmap = lambda i, be, par, nxt, nu: (0, 0, 0)
    grid_spec = pltpu.PrefetchScalarGridSpec(
        num_scalar_prefetch=4,
        grid=(n_blocks // BLOCKS_PER_STEP,),
        in_specs=[pl.BlockSpec((step_rows, d), xmap),
                  pl.BlockSpec(memory_space=pl.ANY),
                  pl.BlockSpec((n_e, 1, d_ff2), cmap),
                  pl.BlockSpec(memory_space=pl.ANY),
                  pl.BlockSpec((n_e, 1, d), cmap),
                  pl.BlockSpec((FF_CHUNK, FF_CHUNK), lambda i, be, par, nxt, nu: (0, 0))],
        out_specs=pl.BlockSpec((step_rows, d), imap),
        scratch_shapes=[pltpu.VMEM((2, d, d_ff2), F32), pltpu.VMEM((2, d_ff2 // 2, d), F32),
                        pltpu.VMEM((d, d_ff2), BF16), pltpu.VMEM((d_ff2 // 2, d), BF16),
                        pltpu.VMEM((EXPERT_ROWS, d_ff2 // 2), BF16),
                        pltpu.SemaphoreType.DMA((2, 2))],
    )
    return pl.pallas_call(
        _expert_kernel,
        grid_spec=grid_spec,
        out_shape=jax.ShapeDtypeStruct((n_slots, d), F32),
        compiler_params=_cparams(("arbitrary",), vmem=EXPERT_VMEM_LIMIT),
        name="experts",
    )(block_e, parity, next_e, n_used, xs, w1, b1p, w2, b2.reshape(n_e, 1, d), sel)


def _combine_kernel(dest_ref, dnext_ref, x1_ref, route_ref, ys_ref, o_ref, buf_ref, sem, *, bm):
    i = pl.program_id(0)
    n = pl.num_programs(0)
    slot = i % 2

    def gather(d_ref, buf):
        def issue(g, carry):
            base = pl.multiple_of(g * SUBLANES, SUBLANES)
            for r in range(SUBLANES):
                for kq in range(TOP_K):
                    _row_copy(ys_ref, d_ref[(base + r) * TOP_K + kq], buf_ref.at[buf, kq], base + r,
                              sem.at[buf]).start(priority=kq % 2)
            return carry

        lax.fori_loop(0, bm // SUBLANES, issue, 0)

    @pl.when(i == 0)
    def _():
        gather(dest_ref, 0)

    for nxt in range(2):
        @pl.when(jnp.logical_and(i + 1 < n, (i + 1) % 2 == nxt))
        def _(nxt=nxt):
            gather(dnext_ref, nxt)

    for kq in range(TOP_K):
        pltpu.make_async_copy(ys_ref.at[pl.ds(0, bm)], buf_ref.at[slot, kq], sem.at[slot]).wait()
    acc = x1_ref[...]
    for kq in range(TOP_K):
        acc = acc + route_ref[:, TOP_K + kq:TOP_K + kq + 1] * buf_ref[slot, kq]
    o_ref[...] = acc


def _combine(dest_flat, x1, route, ys, bm=512):
    t, d = x1.shape
    n = t // bm
    return pl.pallas_call(
        functools.partial(_combine_kernel, bm=bm),
        grid=(n,),
        in_specs=[pl.BlockSpec((bm * TOP_K,), lambda i: (i,), memory_space=pltpu.SMEM),
                  pl.BlockSpec((bm * TOP_K,), lambda i: (jnp.minimum(i + 1, n - 1),), memory_space=pltpu.SMEM),
                  pl.BlockSpec((bm, d), lambda i: (i, 0)),
                  pl.BlockSpec((bm, LANES), lambda i: (i, 0)),
                  pl.BlockSpec(memory_space=pl.ANY)],
        out_specs=pl.BlockSpec((bm, d), lambda i: (i, 0)),
        out_shape=jax.ShapeDtypeStruct((t, d), F32),
        scratch_shapes=[pltpu.VMEM((2, TOP_K, bm, d), F32), pltpu.SemaphoreType.DMA((2,))],
        compiler_params=_cparams(("arbitrary",)),
        name="combine",
    )(dest_flat, dest_flat, x1, route, ys)


def _layer(x, mem, ln1, w_in, token_mu, rwkv_w0, rwkv_w2, rwkv_a0, rwkv_a2, rwkv_g2,
           rwkv_k_k, rwkv_k_a, rwkv_r_k, rwkv_ln_w, rwkv_ln_b, dil_q_norm, dil_k_norm,
           ln_mem, w_mem_kv, mem_q_norm, mem_k_norm, p_rwkv, p_dil, p_mem, w_out,
           ln2, w_router, b_router, w1, b1, w2, b2):
    b, s, d = x.shape
    t = b * s
    x2 = x.reshape(t, d)
    zr, dil_qkv, mq, gates = _in_proj(x2, ln1, w_in, dil_q_norm, dil_k_norm, mem_q_norm)
    y_rwkv = _rwkv(zr.reshape(b, s, RWKV_IN), token_mu, rwkv_w0, rwkv_w2, rwkv_a0, rwkv_a2, rwkv_g2,
                   rwkv_k_k, rwkv_k_a, rwkv_r_k.reshape(-1), rwkv_ln_w, rwkv_ln_b)
    outs, lses = [], []
    for g, (_, dilation) in enumerate(DIL_PATTERNS):
        o, l = _dil_group(*dil_qkv[3 * g:3 * g + 3], b, dilation)
        outs.append(o)
        lses.append(l)
    mk, mv = _mem_kv(mem, ln_mem, w_mem_kv, mem_k_norm)
    y_mem = _mem_attn(mq.reshape(b, s, MEM_DIM), mk, mv)
    x1, h2, route, counts = _mix(x2, y_rwkv.reshape(t, RWKV_DIM), outs, lses, y_mem.reshape(t, MEM_DIM), gates,
                                 p_rwkv, p_dil, p_mem, w_out, ln2, w_router, b_router)

    counts = counts[0, :N_EXPERTS].astype(jnp.int32)
    nblk = (counts + EXPERT_ROWS - 1) // EXPERT_ROWS
    bend = jnp.cumsum(nblk)
    pstart = ((bend - nblk) * EXPERT_ROWS).astype(F32)
    pstart = jnp.zeros((1, LANES), F32).at[0, :N_EXPERTS].set(pstart)
    n_blocks = (t * TOP_K) // EXPERT_ROWS + N_EXPERTS
    block_e = jnp.sum(bend[None, :] <= jnp.arange(n_blocks, dtype=jnp.int32)[:, None], axis=1)
    block_e = jnp.minimum(block_e, N_EXPERTS - 1).astype(jnp.int32)
    n_used = bend[-1:].astype(jnp.int32)
    used = nblk > 0
    eids = jnp.arange(N_EXPERTS, dtype=jnp.int32)
    ordinal = jnp.cumsum(used.astype(jnp.int32)) - 1
    later = jnp.where(used[None, :] & (eids[None, :] > eids[:, None]), eids[None, :], N_EXPERTS)
    next_used = jnp.min(later, axis=1)
    next_used = jnp.where(next_used < N_EXPERTS, next_used, -1).astype(jnp.int32)
    parity = (ordinal[block_e] % 2).astype(jnp.int32)
    next_e = next_used[block_e]
    tail = n_used[0] + eids
    last_blk = jnp.concatenate([jnp.where(used, bend - 1, -1), jnp.where(tail < n_blocks, tail, -1)]).astype(jnp.int32)

    dest = _route(route, pstart)[:, :TOP_K].reshape(-1)
    xs = _dispatch(last_blk, dest, h2, n_blocks * EXPERT_ROWS)
    ys = _experts(block_e, parity, next_e, n_used, xs, w1, b1, w2, b2)
    out = _combine(dest, x1, route, ys)
    return out.reshape(b, s, d)


def kernel(x, mem, ln1, w_in, token_mu, rwkv_w0, rwkv_w2, rwkv_a0, rwkv_a2, rwkv_g2, rwkv_k_k, rwkv_k_a, rwkv_r_k, rwkv_ln_w, rwkv_ln_b, dil_q_norm, dil_k_norm, ln_mem, w_mem_kv, mem_q_norm, mem_k_norm, p_rwkv, p_dil, p_mem, w_out, ln2, w_router, b_router, w1, b1, w2, b2):
    params = (ln1, w_in, token_mu, rwkv_w0, rwkv_w2, rwkv_a0, rwkv_a2, rwkv_g2, rwkv_k_k, rwkv_k_a,
              rwkv_r_k, rwkv_ln_w, rwkv_ln_b, dil_q_norm, dil_k_norm, ln_mem, w_mem_kv, mem_q_norm,
              mem_k_norm, p_rwkv, p_dil, p_mem, w_out, ln2, w_router, b_router, w1, b1, w2, b2)
    for l in range(ln1.shape[0]):
        x = _layer(x, mem, *[p[l] for p in params])
    return x
```

```python
import functools

import jax
import jax.numpy as jnp
from jax import lax
from jax.experimental import pallas as pl
from jax.experimental.pallas import tpu as pltpu

F32 = jnp.float32
BF16 = jnp.bfloat16

NORM_EPS = 1e-5
HEAD_DIM = 64
RWKV_DIM = 512
RWKV_IN = 1792
GN_EPS = HEAD_DIM * 1e-5
DIL_DIM = 768
DIL_GROUP_DIM = 256
DIL_PATTERNS = ((128, 1), (512, 4), (2048, 16))
ATTN_BLOCK = 128
MEM_DIM = 512
MEM_HEAD_DIM = 128
N_EXPERTS = 32
TOP_K = 4
SWIGLU_ALPHA = 1.702
SWIGLU_LIMIT = 7.0
LANES = 128
SUBLANES = 8
CHUNK = 64
NEG_BIG = -1e30
VMEM_LIMIT = 48 * 1024 * 1024
EXPERT_VMEM_LIMIT = 56 * 1024 * 1024


def _dot(a, b):
    return jnp.dot(a, b, preferred_element_type=F32)


def _dot_nt(a, b):
    return lax.dot_general(a, b, (((1,), (1,)), ((), ())), preferred_element_type=F32)


def _split2(x):
    hi = x.astype(BF16)
    lo = (x - hi.astype(F32)).astype(BF16)
    return hi, lo


def _split3(x):
    hi = x.astype(BF16)
    r = x - hi.astype(F32)
    lo = r.astype(BF16)
    lo2 = (r - lo.astype(F32)).astype(BF16)
    return hi, lo, lo2


def _dot_rhs3(w_bf16, x):
    hi, lo, lo2 = _split3(x)
    return _dot(w_bf16, hi) + _dot(w_bf16, lo) + _dot(w_bf16, lo2)


def _dot_x3(a, b):
    ah, al = _split2(a)
    bh, bl = _split2(b)
    return _dot(ah, bh) + _dot(al, bh) + _dot(ah, bl)


def _sigmoid(x):
    return 1.0 / (1.0 + jnp.exp(-x))


def _cparams(sem, vmem=VMEM_LIMIT):
    return pltpu.CompilerParams(dimension_semantics=sem, vmem_limit_bytes=vmem)


IN_CHUNK = 256
PERM_ROWS = 256
N_ZR = RWKV_IN // IN_CHUNK
N_DIL = 3 * DIL_DIM // IN_CHUNK
N_MQ = MEM_DIM // IN_CHUNK
N_GATE = 3 * 1024 // IN_CHUNK


def _in_proj_kernel(x_ref, ln_ref, w_ref, qg_ref, kg_ref, mg_ref, p4_ref, p16_ref,
                    zr_ref, *rest, bm):
    dil_refs, (mq_ref, gt_ref) = rest[:N_DIL], rest[N_DIL:]
    perm_refs = (None, p4_ref, p16_ref)
    x = x_ref[...]
    h = x * lax.rsqrt(jnp.mean(x * x, axis=-1, keepdims=True) + NORM_EPS) * ln_ref[...]
    hb = h.astype(BF16)

    lane = lax.broadcasted_iota(jnp.int32, (x.shape[0], LANES), 1)

    def seg_rms(z, seg, gain):
        z2 = z * z
        cols = []
        for cb in range(IN_CHUNK // LANES):
            blk = z2[:, cb * LANES:(cb + 1) * LANES]
            ss = jnp.zeros_like(blk)
            for sg in range(LANES // seg):
                m = (lane >= sg * seg) & (lane < (sg + 1) * seg)
                ss = jnp.where(m, jnp.sum(jnp.where(m, blk, 0.0), axis=-1, keepdims=True), ss)
            cols.append(ss)
        ss = jnp.concatenate(cols, axis=1)
        return z * lax.rsqrt(ss * (1.0 / seg) + NORM_EPS) * gain

    for c in range(N_ZR + N_DIL + N_MQ + N_GATE):
        z = _dot(hb, w_ref[:, c * IN_CHUNK:(c + 1) * IN_CHUNK])
        if c < N_ZR:
            zr_ref[:, c * IN_CHUNK:(c + 1) * IN_CHUNK] = z
            continue
        d = c - N_ZR
        if d < N_DIL:
            which, g = divmod(d, 3)
            if which == 0:
                z = seg_rms(z, HEAD_DIM, qg_ref[...])
            elif which == 1:
                z = seg_rms(z, HEAD_DIM, kg_ref[...])
            zb = z.astype(BF16)
            o_ref = dil_refs[g * 3 + which]
            dilation = DIL_PATTERNS[g][1]
            if dilation == 1:
                o_ref[...] = zb
            else:
                rows = PERM_ROWS // dilation
                for sub in range(bm // PERM_ROWS):
                    zp = _dot(perm_refs[g][...], zb[sub * PERM_ROWS:(sub + 1) * PERM_ROWS, :]).astype(BF16)
                    for r in range(dilation):
                        o_ref[sub * rows:(sub + 1) * rows, r * IN_CHUNK:(r + 1) * IN_CHUNK] = zp[r * rows:(r + 1) * rows, :]
            continue
        d -= N_DIL
        if d < N_MQ:
            mq_ref[:, d * IN_CHUNK:(d + 1) * IN_CHUNK] = seg_rms(z, MEM_HEAD_DIM, mg_ref[...]).astype(BF16)
            continue
        d -= N_MQ
        gt_ref[:, d * IN_CHUNK:(d + 1) * IN_CHUNK] = _sigmoid(z).astype(BF16)


def _class_perm(bm, dilation):
    i = jnp.arange(bm)
    src = (i % (bm // dilation)) * dilation + i // (bm // dilation)
    return (src[:, None] == i[None, :]).astype(BF16)


def _in_proj(x2, ln1, w_in, dil_q_norm, dil_k_norm, mem_q_norm, bm=512):
    t, d = x2.shape
    n_cols = w_in.shape[1]
    wb = w_in.astype(BF16)
    qg = (jnp.tile(dil_q_norm, IN_CHUNK // HEAD_DIM) * (HEAD_DIM ** -0.5)).reshape(1, IN_CHUNK)
    kg = jnp.tile(dil_k_norm, IN_CHUNK // HEAD_DIM).reshape(1, IN_CHUNK)
    mg = jnp.tile(mem_q_norm, IN_CHUNK // MEM_HEAD_DIM).reshape(1, IN_CHUNK)
    p4 = _class_perm(PERM_ROWS, DIL_PATTERNS[1][1])
    p16 = _class_perm(PERM_ROWS, DIL_PATTERNS[2][1])
    row = lambda w: pl.BlockSpec((bm, w), lambda i: (i, 0))
    const = lambda a: pl.BlockSpec(a.shape, lambda i: (0,) * a.ndim)
    dil_specs, dil_shapes = [], []
    for _, dilation in DIL_PATTERNS:
        for _ in range(3):
            dil_specs.append(pl.BlockSpec((bm // dilation, dilation * DIL_GROUP_DIM), lambda i: (i, 0)))
            dil_shapes.append(jax.ShapeDtypeStruct((t // dilation, dilation * DIL_GROUP_DIM), BF16))
    outs = pl.pallas_call(
        functools.partial(_in_proj_kernel, bm=bm),
        grid=(t // bm,),
        in_specs=[row(d), pl.BlockSpec((1, d), lambda i: (0, 0)),
                  pl.BlockSpec((d, n_cols), lambda i: (0, 0), pipeline_mode=pl.Buffered(1)),
                  const(qg), const(kg), const(mg), const(p4), const(p16)],
        out_specs=[row(RWKV_IN)] + dil_specs + [row(MEM_DIM), row(3 * d)],
        out_shape=[jax.ShapeDtypeStruct((t, RWKV_IN), F32)] + dil_shapes
                  + [jax.ShapeDtypeStruct((t, MEM_DIM), BF16), jax.ShapeDtypeStruct((t, 3 * d), BF16)],
        compiler_params=_cparams(("parallel",)),
        name="in_proj",
    )(x2, ln1.reshape(1, d), wb, qg, kg, mg, p4, p16)
    return outs[0], outs[1:1 + N_DIL], outs[1 + N_DIL], outs[2 + N_DIL]


def _rwkv_kernel(z_ref, mu_ref, w0_ref, a0_ref, wl_ref, kk_ref, ka_ref, rk_ref,
                 lnw_ref, lnb_ref, tri_ref, y_ref, carry_ref, state_ref, ybuf_ref, *, tt):
    s_idx = pl.program_id(1)

    @pl.when(s_idx == 0)
    def _():
        carry_ref[...] = jnp.zeros_like(carry_ref)
        state_ref[...] = jnp.zeros_like(state_ref)

    z = z_ref[0]
    rows = lax.broadcasted_iota(jnp.int32, z.shape, 0)
    prev = jnp.where(rows == 0, carry_ref[...], pltpu.roll(z, 1, axis=0))
    carry_ref[...] = z[tt - 1:tt, :]
    zl = z + (prev - z) * mu_ref[...]
    r = zl[:, 0:RWKV_DIM]
    k = zl[:, RWKV_DIM:2 * RWKV_DIM]
    v = zl[:, 2 * RWKV_DIM:3 * RWKV_DIM]
    zlo = zl[:, 3 * RWKV_DIM:RWKV_IN]
    lcol = lax.broadcasted_iota(jnp.int32, zlo.shape, 1)
    feat = jnp.where(lcol < 64, jnp.tanh(zlo), jnp.where(lcol < 128, zlo, _sigmoid(zlo)))
    lora = _dot(feat.astype(BF16), wl_ref[...])
    nu = -(w0_ref[...] + lora[:, 0:RWKV_DIM])
    softplus = jnp.maximum(nu, 0.0) + jnp.log(1.0 + jnp.exp(-jnp.abs(nu)))
    lw = -jnp.exp(-softplus - 0.5)
    a = _sigmoid(a0_ref[...] + lora[:, RWKV_DIM:2 * RWKV_DIM])
    g = lora[:, 2 * RWKV_DIM:3 * RWKV_DIM]

    n_pairs = RWKV_DIM // LANES
    n_chunks = tt // CHUNK
    head_lo = lax.broadcasted_iota(jnp.int32, (tt, LANES), 1) < HEAD_DIM

    def seg_sum(x):
        parts = []
        for p in range(n_pairs):
            xp = x[:, p * LANES:(p + 1) * LANES]
            lo = jnp.sum(jnp.where(head_lo, xp, 0.0), axis=-1, keepdims=True)
            hi = jnp.sum(jnp.where(head_lo, 0.0, xp), axis=-1, keepdims=True)
            parts.append(jnp.where(head_lo, lo, hi))
        return jnp.concatenate(parts, axis=1)

    kk = k * kk_ref[...]
    kk = kk / jnp.maximum(jnp.sqrt(seg_sum(kk * kk)), 1e-12)
    k2 = k * (1.0 + (a - 1.0) * ka_ref[...])
    a_s = -kk
    b_s = kk * a

    cum = _dot_rhs3(tri_ref[...], lw)
    tot = jnp.concatenate(
        [jnp.broadcast_to(cum[(c + 1) * CHUNK - 1:(c + 1) * CHUNK, :], (CHUNK, RWKV_DIM)) for c in range(n_chunks)],
        axis=0)
    e_neg = jnp.exp(-cum)
    e_end = jnp.exp(tot - cum)
    at = (a_s * jnp.exp(cum - lw)).astype(BF16)
    rt = (r * jnp.exp(cum)).astype(BF16)
    bt = (b_s * e_neg).astype(BF16)
    kt = (k2 * e_neg).astype(BF16)
    be = (b_s * e_end).astype(BF16)
    ke = (k2 * e_end).astype(BF16)
    vb = v.astype(BF16)
    e_tot = jnp.exp(tot)

    r128 = lax.broadcasted_iota(jnp.int32, (LANES, LANES), 0)
    c128 = lax.broadcasted_iota(jnp.int32, (LANES, LANES), 1)
    lane_lo = lax.broadcasted_iota(jnp.int32, (CHUNK, LANES), 1) < HEAD_DIM
    stril = c128 < r128
    tril = c128 <= r128
    eye = (c128 == r128).astype(F32)
    same16 = (r128 // 16) == (c128 // 16)
    same32 = (r128 // 32) == (c128 // 32)
    off16 = same32 & jnp.logical_not(same16)
    off32 = jnp.logical_not(same32)

    def bdiag(xp):
        zero = jnp.zeros_like(xp)
        return jnp.concatenate([jnp.where(lane_lo, xp, zero), jnp.where(lane_lo, zero, xp)], axis=0)

    units = [(ci, p) for ci in range(n_chunks) for p in range(n_pairs)]
    pick = lambda arr, u: arr[u[0] * CHUNK:(u[0] + 1) * CHUNK, u[1] * LANES:(u[1] + 1) * LANES]
    cat0 = lambda xs: jnp.concatenate(xs, axis=0)
    cat1 = lambda xs: jnp.concatenate(xs, axis=1)
    zero_b = jnp.zeros((LANES, LANES), BF16)

    at_b = [bdiag(pick(at, u)) for u in units]
    v_b = [bdiag(pick(vb, u)) for u in units]
    be_b = [bdiag(pick(be, u)) for u in units]
    ke_b = [bdiag(pick(ke, u)) for u in units]
    rt_b = [bdiag(pick(rt, u)) for u in units]
    a_ab, a_ak, a_rbk = [], [], []
    for i, u in enumerate(units):
        mq = _dot_nt(cat0([at_b[i], rt_b[i]]), cat0([bdiag(pick(bt, u)), bdiag(pick(kt, u))]))
        a_ab.append(jnp.where(stril, mq[:LANES, :LANES], 0.0).astype(BF16))
        a_ak.append(jnp.where(stril, mq[:LANES, LANES:], 0.0).astype(BF16))
        a_rbk.append(cat1([jnp.where(tril, mq[LANES:, :LANES], 0.0), jnp.where(tril, mq[LANES:, LANES:], 0.0)]).astype(BF16))
    w_b = [_dot(a_ak[i], v_b[i]).astype(BF16) for i in range(len(units))]

    d1 = [jnp.where(same16, x, jnp.zeros_like(x)) for x in a_ab]
    xs = [eye + d.astype(F32) for d in d1]
    d2 = [_dot(d, d).astype(BF16) for d in d1]
    t_ = [_dot(d2[i], cat1([xs[i].astype(BF16), d2[i]])) for i in range(len(units))]
    xs = [xs[i] + t_[i][:, :LANES] for i in range(len(units))]
    d4 = [t[:, LANES:].astype(BF16) for t in t_]
    t_ = [_dot(d4[i], cat1([xs[i].astype(BF16), d4[i]])) for i in range(len(units))]
    xs = [xs[i] + t_[i][:, :LANES] for i in range(len(units))]
    d8 = [t[:, LANES:].astype(BF16) for t in t_]
    xs = [xs[i] + _dot(d8[i], xs[i].astype(BF16)) for i in range(len(units))]
    for off in (off16, off32):
        xb = [x.astype(BF16) for x in xs]
        g_ = [_dot(jnp.where(off, a_ab[i], jnp.zeros_like(a_ab[i])), xb[i]).astype(BF16) for i in range(len(units))]
        xs = [xs[i] + _dot(xb[i], g_[i]) for i in range(len(units))]

    pq = [_dot(xs[i].astype(BF16), cat1([at_b[i], w_b[i]])) for i in range(len(units))]
    ry = [_dot(a_rbk[i], cat0([pq[i].astype(BF16), cat1([zero_b, v_b[i]])])) for i in range(len(units))]
    r2 = [(rt_b[i].astype(F32) + ry[i][:, :LANES]).astype(BF16) for i in range(len(units))]
    m_c = [_dot(pq[i][:, :LANES].T.astype(BF16), be_b[i]) for i in range(len(units))]
    n_c = [_dot(cat1([pq[i][:, LANES:].T.astype(BF16), v_b[i].astype(F32).T.astype(BF16)]), cat0([be_b[i], ke_b[i]]))
           for i in range(len(units))]

    state = [state_ref[p] for p in range(n_pairs)]
    for i, (ci, p) in enumerate(units):
        s0 = state[p]
        s_hi, s_lo = _split2(s0)
        y = _dot_nt(r2[i], s_hi) + ry[i][:, LANES:]
        mcb = m_c[i].astype(BF16)
        e_row = e_tot[ci * CHUNK:ci * CHUNK + 1, p * LANES:(p + 1) * LANES]
        state[p] = s0 * e_row + (_dot(s_hi, mcb) + _dot(s_lo, mcb)) + n_c[i]
        ybuf_ref[ci * CHUNK:(ci + 1) * CHUNK, p * LANES:(p + 1) * LANES] = jnp.where(lane_lo, y[:CHUNK], y[CHUNK:])
    for p in range(n_pairs):
        state_ref[p] = state[p]

    y = ybuf_ref[...]
    mean = seg_sum(y) * (1.0 / HEAD_DIM)
    yc = y - mean
    var = seg_sum(yc * yc) * (1.0 / HEAD_DIM)
    yn = yc * lax.rsqrt(var + GN_EPS) * lnw_ref[...] + lnb_ref[...]
    bonus = seg_sum(r * k2 * rk_ref[...]) * v
    y_ref[0] = ((yn + bonus) * g).astype(y_ref.dtype)


def _rwkv(zr, token_mu, w0, w2, a0, a2, g2, k_k, k_a, r_k, ln_w, ln_b, tt=256):
    b, s, _ = zr.shape
    i = jnp.arange(tt)
    same = (i[:, None] // CHUNK) == (i[None, :] // CHUNK)
    tri = (same & (i[None, :] <= i[:, None])).astype(BF16)
    vec = lambda a: a.reshape(1, -1)
    wl = jnp.zeros((RWKV_IN - 3 * RWKV_DIM, 3 * RWKV_DIM), F32)
    wl = wl.at[0:64, 0:RWKV_DIM].set(w2).at[64:128, RWKV_DIM:2 * RWKV_DIM].set(a2)
    wl = wl.at[128:256, 2 * RWKV_DIM:3 * RWKV_DIM].set(g2).astype(BF16)
    params = [vec(token_mu), vec(w0), vec(a0), wl, vec(k_k), vec(k_a), vec(r_k),
              vec(ln_w), vec(ln_b), tri]
    const = lambda a: pl.BlockSpec(a.shape, lambda bi, si: (0,) * a.ndim)
    return pl.pallas_call(
        functools.partial(_rwkv_kernel, tt=tt),
        grid=(b, s // tt),
        in_specs=[pl.BlockSpec((1, tt, RWKV_IN), lambda bi, si: (bi, si, 0))] + [const(a) for a in params],
        out_specs=pl.BlockSpec((1, tt, RWKV_DIM), lambda bi, si: (bi, si, 0)),
        out_shape=jax.ShapeDtypeStruct((b, s, RWKV_DIM), BF16),
        scratch_shapes=[pltpu.VMEM((1, RWKV_IN), F32),
                        pltpu.VMEM((RWKV_DIM // LANES, LANES, LANES), F32),
                        pltpu.VMEM((tt, RWKV_DIM), F32)],
        compiler_params=_cparams(("parallel", "arbitrary")),
        name="rwkv",
    )(zr, *params)


DIL_UNITS = 8


def _dil_kernel(q_ref, k_ref, v_ref, o_ref, l_ref, *, dilation, cps, nbs):
    cg = pl.program_id(1)
    jb = pl.program_id(2)
    bq = ATTN_BLOCK
    n_heads = DIL_GROUP_DIM // HEAD_DIM
    qi = lax.broadcasted_iota(jnp.int32, (bq, 2 * bq), 0)
    kj = lax.broadcasted_iota(jnp.int32, (bq, 2 * bq), 1)
    lane = lax.broadcasted_iota(jnp.int32, (bq, DIL_GROUP_DIM), 1)
    head_masks = [(lane >= h * HEAD_DIM) & (lane < (h + 1) * HEAD_DIM) for h in range(n_heads)]
    units = [(c, bb) for c in range(cps) for bb in range(nbs)]

    qs, kcats, vcats, masks, starts = [], [], [], [], []
    for c, bb in units:
        i = jb * nbs + bb
        cols = slice(c * DIL_GROUP_DIM, (c + 1) * DIL_GROUP_DIM)
        p0 = pl.multiple_of(jnp.maximum(i - 1, 0) * bq, bq)
        c0 = pl.multiple_of(i * bq, bq)
        qs.append(q_ref[0, bb * bq:(bb + 1) * bq, cols])
        kcats.append(jnp.concatenate([k_ref[0, pl.ds(p0, bq), cols], k_ref[0, pl.ds(c0, bq), cols]], axis=0))
        vcats.append(jnp.concatenate([v_ref[0, pl.ds(p0, bq), cols], v_ref[0, pl.ds(c0, bq), cols]], axis=0))
        first = (1 - jnp.minimum(i, 1)) * (2 * bq)
        masks.append(((kj < bq) & (kj >= qi + first)) | ((kj >= bq) & ((kj - bq) <= qi)))
        starts.append((cg * cps + c) + dilation * bq * i)
    pairs = [(u, h) for u in range(len(units)) for h in range(n_heads)]
    sc = [jnp.where(masks[u], _dot_nt(jnp.where(head_masks[h], qs[u], jnp.zeros_like(qs[u])), kcats[u]), NEG_BIG)
          for u, h in pairs]
    mx = [jnp.max(x, axis=-1, keepdims=True) for x in sc]
    pr = [jnp.exp(sc[n] - mx[n]) for n in range(len(pairs))]
    den = [jnp.sum(x, axis=-1, keepdims=True) for x in pr]
    ov = [_dot(pr[n].astype(BF16), vcats[pairs[n][0]]) / den[n] for n in range(len(pairs))]
    for u in range(len(units)):
        acc = jnp.zeros((bq, DIL_GROUP_DIM), F32)
        lacc = jnp.zeros((bq, DIL_GROUP_DIM), F32)
        for h in range(n_heads):
            n = u * n_heads + h
            acc = jnp.where(head_masks[h], ov[n], acc)
            lacc = jnp.where(head_masks[h], mx[n] + jnp.log(den[n]), lacc)
        rows = pl.ds(starts[u], bq, stride=dilation) if dilation > 1 else pl.ds(pl.multiple_of(starts[u], bq), bq)
        for half in range(DIL_GROUP_DIM // LANES):
            o_ref[0, half, rows, :] = acc[:, half * LANES:(half + 1) * LANES]
            l_ref[0, half, rows, :] = lacc[:, half * LANES:(half + 1) * LANES]


def _dil_group(q, k, v, b, dilation):
    n = q.shape[0] // b
    s = n * dilation
    nb = n // ATTN_BLOCK
    nbs = min(nb, DIL_UNITS)
    cps = DIL_UNITS // nbs
    halves = DIL_GROUP_DIM // LANES
    view = lambda a: a.reshape(b, n, dilation * DIL_GROUP_DIM)
    qmap = lambda bi, ci, ji: (bi, ji, ci)
    kmap = lambda bi, ci, ji: (bi, 0, ci)
    omap = lambda bi, ci, ji: (bi, 0, 0, 0)
    return pl.pallas_call(
        functools.partial(_dil_kernel, dilation=dilation, cps=cps, nbs=nbs),
        grid=(b, dilation // cps, nb // nbs),
        in_specs=[pl.BlockSpec((1, nbs * ATTN_BLOCK, cps * DIL_GROUP_DIM), qmap),
                  pl.BlockSpec((1, n, cps * DIL_GROUP_DIM), kmap),
                  pl.BlockSpec((1, n, cps * DIL_GROUP_DIM), kmap)],
        out_specs=[pl.BlockSpec((1, halves, s, LANES), omap),
                   pl.BlockSpec((1, halves, s, LANES), omap)],
        out_shape=[jax.ShapeDtypeStruct((b, halves, s, LANES), F32),
                   jax.ShapeDtypeStruct((b, halves, s, LANES), F32)],
        compiler_params=_cparams(("parallel", "arbitrary", "arbitrary")),
        name=f"dil_attn_d{dilation}",
    )(view(q), view(k), view(v))


def _mem_kv_kernel(m_ref, ln_ref, w_ref, kn_ref, k_ref, v_ref):
    x = m_ref[0]
    h = x * lax.rsqrt(jnp.mean(x * x, axis=-1, keepdims=True) + NORM_EPS) * ln_ref[...]
    kv = _dot(h.astype(BF16), w_ref[...])
    for hd in range(MEM_DIM // MEM_HEAD_DIM):
        sl = slice(hd * MEM_HEAD_DIM, (hd + 1) * MEM_HEAD_DIM)
        kh = kv[:, sl]
        kh = kh * lax.rsqrt(jnp.mean(kh * kh, axis=-1, keepdims=True) + NORM_EPS) * kn_ref[...]
        k_ref[0, :, sl] = kh.astype(BF16)
    v_ref[0] = kv[:, MEM_DIM:].astype(BF16)


def _mem_kv(mem, ln_mem, w_mem_kv, mem_k_norm):
    b, m, d = mem.shape
    return pl.pallas_call(
        _mem_kv_kernel,
        grid=(b,),
        in_specs=[pl.BlockSpec((1, m, d), lambda i: (i, 0, 0)),
                  pl.BlockSpec((1, d), lambda i: (0, 0)),
                  pl.BlockSpec((d, 2 * MEM_DIM), lambda i: (0, 0)),
                  pl.BlockSpec((1, MEM_HEAD_DIM), lambda i: (0, 0))],
        out_specs=[pl.BlockSpec((1, m, MEM_DIM), lambda i: (i, 0, 0)),
                   pl.BlockSpec((1, m, MEM_DIM), lambda i: (i, 0, 0))],
        out_shape=[jax.ShapeDtypeStruct((b, m, MEM_DIM), BF16),
                   jax.ShapeDtypeStruct((b, m, MEM_DIM), BF16)],
        compiler_params=_cparams(("parallel",)),
        name="mem_kv",
    )(mem, ln_mem.reshape(1, d), w_mem_kv.astype(BF16), mem_k_norm.reshape(1, MEM_HEAD_DIM))


def _mem_attn_kernel(q_ref, k_ref, v_ref, o_ref):
    for hd in range(MEM_DIM // MEM_HEAD_DIM):
        sl = slice(hd * MEM_HEAD_DIM, (hd + 1) * MEM_HEAD_DIM)
        sc = _dot_nt(q_ref[0, :, sl], k_ref[0, :, sl]) * (MEM_HEAD_DIM ** -0.5)
        m = jnp.max(sc, axis=-1, keepdims=True)
        p = jnp.exp(sc - m)
        den = jnp.sum(p, axis=-1, keepdims=True)
        o_ref[0, :, sl] = (_dot((p / den).astype(BF16), v_ref[0, :, sl])).astype(o_ref.dtype)


def _mem_attn(mq, mk, mv, bm=512):
    b, s, _ = mq.shape
    m = mk.shape[1]
    return pl.pallas_call(
        _mem_attn_kernel,
        grid=(b, s // bm),
        in_specs=[pl.BlockSpec((1, bm, MEM_DIM), lambda bi, si: (bi, si, 0)),
                  pl.BlockSpec((1, m, MEM_DIM), lambda bi, si: (bi, 0, 0)),
                  pl.BlockSpec((1, m, MEM_DIM), lambda bi, si: (bi, 0, 0))],
        out_specs=pl.BlockSpec((1, bm, MEM_DIM), lambda bi, si: (bi, si, 0)),
        out_shape=jax.ShapeDtypeStruct((b, s, MEM_DIM), BF16),
        compiler_params=_cparams(("parallel", "parallel")),
        name="mem_attn",
    )(mq, mk, mv)


def _mix_kernel(x_ref, yr_ref, o0_ref, o1_ref, o2_ref, l0_ref, l1_ref, l2_ref, ym_ref, gt_ref,
                pr_ref, pd_ref, pm_ref, wo_ref, ln2_ref, wr_ref, br_ref,
                x1_ref, h2_ref, route_ref, cnt_ref):
    d = x_ref.shape[1]
    wide = lambda ref: jnp.concatenate([ref[0, hf] for hf in range(DIL_GROUP_DIM // LANES)], axis=1)
    l0, l1, l2 = wide(l0_ref), wide(l1_ref), wide(l2_ref)
    m = jnp.maximum(jnp.maximum(l0, l1), l2)
    e0, e1, e2 = jnp.exp(l0 - m), jnp.exp(l1 - m), jnp.exp(l2 - m)
    y_dil = (e0 * wide(o0_ref) + e1 * wide(o1_ref) + e2 * wide(o2_ref)) / (e0 + e1 + e2)
    mixed = (gt_ref[:, 0:d].astype(F32) * _dot(yr_ref[...], pr_ref[...])
             + gt_ref[:, d:2 * d].astype(F32) * _dot(y_dil.astype(BF16), pd_ref[...])
             + gt_ref[:, 2 * d:3 * d].astype(F32) * _dot(ym_ref[...], pm_ref[...]))
    x1 = x_ref[...] + _dot(mixed.astype(BF16), wo_ref[...])
    x1_ref[...] = x1
    h2 = x1 * lax.rsqrt(jnp.mean(x1 * x1, axis=-1, keepdims=True) + NORM_EPS) * ln2_ref[...]
    h2_ref[...] = h2
    logits = _dot_x3(h2, wr_ref[...]) + br_ref[...]
    lane = lax.broadcasted_iota(jnp.int32, logits.shape, 1)
    lane_f = lane.astype(F32)
    route = jnp.zeros(logits.shape, F32)
    onehot = jnp.zeros(logits.shape, F32)
    vals = []
    for kq in range(TOP_K):
        mx = jnp.max(logits, axis=-1, keepdims=True)
        idx = jnp.min(jnp.where(logits == mx, lane_f, float(LANES)), axis=-1, keepdims=True)
        hit = lane_f == idx
        vals.append(mx)
        route = jnp.where(lane == kq, idx, route)
        onehot = jnp.where(hit, 1.0, onehot)
        logits = jnp.where(hit, -jnp.inf, logits)
    ex = [jnp.exp(vq - vals[0]) for vq in vals]
    den = ex[0] + ex[1] + ex[2] + ex[3]
    for kq in range(TOP_K):
        route = jnp.where(lane == TOP_K + kq, ex[kq] / den, route)
    route_ref[...] = route

    @pl.when(pl.program_id(0) == 0)
    def _():
        cnt_ref[...] = jnp.zeros_like(cnt_ref)

    cnt_ref[...] += jnp.sum(onehot, axis=0, keepdims=True)


def _mix(x2, y_rwkv, outs, lses, y_mem, gates, p_rwkv, p_dil, p_mem, w_out, ln2, w_router, b_router, bm=512):
    t, d = x2.shape
    wr = jnp.zeros((d, LANES), F32).at[:, :N_EXPERTS].set(w_router)
    br = jnp.full((1, LANES), -jnp.inf, F32).at[0, :N_EXPERTS].set(b_router)
    row = lambda w: pl.BlockSpec((bm, w), lambda i: (i, 0))
    const = lambda a: pl.BlockSpec(a.shape, lambda i: (0,) * a.ndim)
    tiles_per_seq = outs[0].shape[2] // bm
    dil = pl.BlockSpec((1, DIL_GROUP_DIM // LANES, bm, LANES),
                       lambda i: (i // tiles_per_seq, 0, i % tiles_per_seq, 0))
    weights = [p_rwkv.astype(BF16), p_dil.astype(BF16), p_mem.astype(BF16), w_out.astype(BF16),
               ln2.reshape(1, d), wr, br]
    return pl.pallas_call(
        _mix_kernel,
        grid=(t // bm,),
        in_specs=[row(d), row(RWKV_DIM)] + [dil] * 6 + [row(MEM_DIM), row(3 * d)]
                 + [const(a) for a in weights],
        out_specs=[row(d), row(d), row(LANES), pl.BlockSpec((1, LANES), lambda i: (0, 0))],
        out_shape=[jax.ShapeDtypeStruct((t, d), F32), jax.ShapeDtypeStruct((t, d), F32),
                   jax.ShapeDtypeStruct((t, LANES), F32), jax.ShapeDtypeStruct((1, LANES), F32)],
        compiler_params=_cparams(("arbitrary",)),
        name="mix",
    )(x2, y_rwkv, *outs, *lses, y_mem, gates, *weights)


def _route_kernel(route_ref, pstart_ref, tri_ref, dest_ref, carry_ref):
    @pl.when(pl.program_id(0) == 0)
    def _():
        carry_ref[...] = jnp.zeros_like(carry_ref)

    route = route_ref[...]
    lane = lax.broadcasted_iota(jnp.int32, route.shape, 1)
    lane_f = lane.astype(F32)
    hits = [lane_f == route[:, kq:kq + 1] for kq in range(TOP_K)]
    onehot = jnp.zeros(route.shape, F32)
    for hq in hits:
        onehot = jnp.where(hq, 1.0, onehot)
    rank = _dot(tri_ref[...], onehot.astype(BF16)) + carry_ref[...]
    slot = pstart_ref[...] + rank
    dest = jnp.zeros(route.shape, jnp.int32)
    for kq in range(TOP_K):
        dk = jnp.sum(jnp.where(hits[kq], slot, 0.0), axis=-1, keepdims=True)
        dest = jnp.where(lane == kq, dk.astype(jnp.int32), dest)
    dest_ref[...] = dest
    carry_ref[...] += jnp.sum(onehot, axis=0, keepdims=True)


def _route(route, pstart, bm=1024):
    t = route.shape[0]
    i = jnp.arange(bm)
    tri = (i[None, :] < i[:, None]).astype(BF16)
    return pl.pallas_call(
        _route_kernel,
        grid=(t // bm,),
        in_specs=[pl.BlockSpec((bm, LANES), lambda i: (i, 0)),
                  pl.BlockSpec((1, LANES), lambda i: (0, 0)),
                  pl.BlockSpec((bm, bm), lambda i: (0, 0))],
        out_specs=pl.BlockSpec((bm, LANES), lambda i: (i, 0)),
        out_shape=jax.ShapeDtypeStruct((t, LANES), jnp.int32),
        scratch_shapes=[pltpu.VMEM((1, LANES), F32)],
        compiler_params=_cparams(("arbitrary",)),
        name="route",
    )(route, pstart, tri)


def _row_copy(src_ref, src_row, dst_ref, dst_row, sem):
    return pltpu.make_async_copy(src_ref.at[pl.ds(src_row, 1)], dst_ref.at[pl.ds(dst_row, 1)], sem)


def _dispatch_kernel(last_ref, dest_ref, h_ref, xs_ref, zero_ref, hbuf_ref, sem_s, sem_l, zsem, *, bm):
    i = pl.program_id(0)
    n = pl.num_programs(0)

    def load(tile, b):
        rows = pl.ds(pl.multiple_of(tile * bm, bm), bm)
        return pltpu.make_async_copy(h_ref.at[rows], hbuf_ref.at[b], sem_l.at[b])

    def wait_scatter(b):
        for kq in range(TOP_K):
            pltpu.make_async_copy(hbuf_ref.at[0], xs_ref.at[pl.ds(0, bm)], sem_s.at[b]).wait()

    @pl.when(i == 0)
    def _():
        load(0, 0).start()
        zero_ref[...] = jnp.zeros_like(zero_ref)

        def zero_copy(e):
            row = pl.multiple_of(last_ref[e] * EXPERT_ROWS, EXPERT_ROWS)
            return pltpu.make_async_copy(zero_ref, xs_ref.at[pl.ds(row, EXPERT_ROWS)], zsem)

        def z_issue(e, carry):
            @pl.when(last_ref[e] >= 0)
            def _():
                zero_copy(e).start()
            return carry

        def z_drain(e, carry):
            @pl.when(last_ref[e] >= 0)
            def _():
                zero_copy(e).wait()
            return carry

        lax.fori_loop(0, 2 * N_EXPERTS, z_issue, 0)
        lax.fori_loop(0, 2 * N_EXPERTS, z_drain, 0)

    @pl.when(i + 1 < n)
    def _():
        load(i + 1, (i + 1) % 3).start()

    load(i, i % 3).wait()

    for phase in range(6):
        @pl.when(i % 6 == phase)
        def _(phase=phase):
            def issue(j, carry):
                for kq in range(TOP_K):
                    _row_copy(hbuf_ref.at[phase % 3], j, xs_ref, dest_ref[j * TOP_K + kq],
                              sem_s.at[phase % 2]).start(priority=kq % 2)
                return carry

            lax.fori_loop(0, bm, issue, 0)

    @pl.when(i > 0)
    def _():
        wait_scatter((i - 1) % 2)

    @pl.when(i == n - 1)
    def _():
        wait_scatter(i % 2)


def _dispatch(last_blk, dest_flat, h2, n_slots, bm=512):
    t, d = h2.shape
    grid_spec = pltpu.PrefetchScalarGridSpec(
        num_scalar_prefetch=1,
        grid=(t // bm,),
        in_specs=[pl.BlockSpec((bm * TOP_K,), lambda i, lb: (i,), memory_space=pltpu.SMEM),
                  pl.BlockSpec(memory_space=pl.ANY)],
        out_specs=pl.BlockSpec(memory_space=pl.ANY),
        scratch_shapes=[pltpu.VMEM((EXPERT_ROWS, d), h2.dtype), pltpu.VMEM((3, bm, d), h2.dtype),
                        pltpu.SemaphoreType.DMA((2,)), pltpu.SemaphoreType.DMA((3,)), pltpu.SemaphoreType.DMA],
    )
    return pl.pallas_call(
        functools.partial(_dispatch_kernel, bm=bm),
        grid_spec=grid_spec,
        out_shape=jax.ShapeDtypeStruct((n_slots, d), h2.dtype),
        compiler_params=_cparams(("arbitrary",)),
        name="dispatch",
    )(last_blk, dest_flat, h2)


FF_CHUNK = 256
EXPERT_ROWS = 256
BLOCKS_PER_STEP = 2


def _expert_kernel(be_ref, par_ref, nxt_ref, nu_ref, xs_ref, w1_ref, b1_ref, w2_ref, b2_ref, sel_ref, ys_ref,
                   w1f_ref, w2f_ref, w1p_ref, w2b_ref, act_ref, sem):
    i = pl.program_id(0)
    d_ff2 = w1_ref.shape[2]
    half = FF_CHUNK // 2

    def weight_copies(expert, buf):
        return (pltpu.make_async_copy(w1_ref.at[expert], w1f_ref.at[buf], sem.at[0, buf]),
                pltpu.make_async_copy(w2_ref.at[expert], w2f_ref.at[buf], sem.at[1, buf]))

    @pl.when(i == 0)
    def _():
        for cp in weight_copies(be_ref[0], par_ref[0]):
            cp.start()

    for hb_i in range(BLOCKS_PER_STEP):
        blk = i * BLOCKS_PER_STEP + hb_i
        rows = slice(hb_i * EXPERT_ROWS, (hb_i + 1) * EXPERT_ROWS)
        e = be_ref[blk]
        slot = par_ref[blk]
        fresh = jnp.logical_or(blk == 0, be_ref[jnp.maximum(blk - 1, 0)] != e)
        used = blk < nu_ref[0]

        @pl.when(jnp.logical_and(fresh, used))
        def _():
            for cp in weight_copies(e, slot):
                cp.wait()
            nxt = nxt_ref[blk]

            @pl.when(nxt >= 0)
            def _():
                for cp in weight_copies(nxt, 1 - slot):
                    cp.start(priority=1)

            for c in range(d_ff2 // FF_CHUNK):
                sl = slice(c * FF_CHUNK, (c + 1) * FF_CHUNK)
                w1p_ref[:, sl] = _dot(w1f_ref[slot, :, sl].astype(BF16), sel_ref[...]).astype(BF16)
            w2b_ref[...] = w2f_ref[slot].astype(BF16)

        @pl.when(used)
        def _():
            x = xs_ref[rows, :].astype(BF16)
            for c in range(d_ff2 // (2 * FF_CHUNK)):
                sl = slice(2 * c * FF_CHUNK, 2 * (c + 1) * FF_CHUNK)
                hb = _dot(x, w1p_ref[:, sl]) + b1_ref[e, :, sl]
                for j in range(2):
                    x_glu = jnp.minimum(hb[:, j * FF_CHUNK:j * FF_CHUNK + half], SWIGLU_LIMIT)
                    x_lin = jnp.clip(hb[:, j * FF_CHUNK + half:(j + 1) * FF_CHUNK], -SWIGLU_LIMIT, SWIGLU_LIMIT)
                    act = x_glu * _sigmoid(SWIGLU_ALPHA * x_glu) * (x_lin + 1.0)
                    act_ref[:, (2 * c + j) * half:(2 * c + j + 1) * half] = act.astype(BF16)
            ys_ref[rows, :] = _dot(act_ref[...], w2b_ref[...]) + b2_ref[e]

        @pl.when(jnp.logical_not(used))
        def _():
            ys_ref[rows, :] = jnp.zeros((EXPERT_ROWS, ys_ref.shape[1]), F32)


def _chunk_deinterleave(a):
    lead = a.shape[:-1]
    a = a.reshape(lead + (a.shape[-1] // FF_CHUNK, FF_CHUNK // 2, 2))
    return jnp.swapaxes(a, -1, -2).reshape(lead + (-1,))


def _experts(block_e, parity, next_e, n_used, xs, w1, b1, w2, b2):
    n_slots, d = xs.shape
    n_e, _, d_ff2 = w1.shape
    n_blocks = n_slots // EXPERT_ROWS
    i = jnp.arange(FF_CHUNK)
    src = jnp.where(i < FF_CHUNK // 2, 2 * i, 2 * (i - FF_CHUNK // 2) + 1)
    sel = (jnp.arange(FF_CHUNK)[:, None] == src[None, :]).astype(BF16)
    b1p = _chunk_deinterleave(b1).reshape(n_e, 1, d_ff2)
    step_rows = BLOCKS_PER_STEP * EXPERT_ROWS
    imap = lambda i, be, par, nxt, nu: (i, 0)
    xmap = lambda i, be, par, nxt, nu: (jnp.minimum(i, (nu[0] - 1) // BLOCKS_PER_STEP), 0)
    cmap = lambda i, be, par, nxt, nu: (0, 0, 0)
    grid_spec = pltpu.PrefetchScalarGridSpec(
        num_scalar_prefetch=4,
        grid=(n_blocks // BLOCKS_PER_STEP,),
        in_specs=[pl.BlockSpec((step_rows, d), xmap),
                  pl.BlockSpec(memory_space=pl.ANY),
                  pl.BlockSpec((n_e, 1, d_ff2), cmap),
                  pl.BlockSpec(memory_space=pl.ANY),
                  pl.BlockSpec((n_e, 1, d), cmap),
                  pl.BlockSpec((FF_CHUNK, FF_CHUNK), lambda i, be, par, nxt, nu: (0, 0))],
        out_specs=pl.BlockSpec((step_rows, d), imap),
        scratch_shapes=[pltpu.VMEM((2, d, d_ff2), F32), pltpu.VMEM((2, d_ff2 // 2, d), F32),
                        pltpu.VMEM((d, d_ff2), BF16), pltpu.VMEM((d_ff2 // 2, d), BF16),
                        pltpu.VMEM((EXPERT_ROWS, d_ff2 // 2), BF16),
                        pltpu.SemaphoreType.DMA((2, 2))],
    )
    return pl.pallas_call(
        _expert_kernel,
        grid_spec=grid_spec,
        out_shape=jax.ShapeDtypeStruct((n_slots, d), F32),
        compiler_params=_cparams(("arbitrary",), vmem=EXPERT_VMEM_LIMIT),
        name="experts",
    )(block_e, parity, next_e, n_used, xs, w1, b1p, w2, b2.reshape(n_e, 1, d), sel)


def _combine_kernel(dest_ref, dnext_ref, x1_ref, route_ref, ys_ref, o_ref, buf_ref, sem, *, bm):
    i = pl.program_id(0)
    n = pl.num_programs(0)
    slot = i % 2

    def gather(d_ref, buf):
        def issue(g, carry):
            base = pl.multiple_of(g * SUBLANES, SUBLANES)
            for r in range(SUBLANES):
                for kq in range(TOP_K):
                    _row_copy(ys_ref, d_ref[(base + r) * TOP_K + kq], buf_ref.at[buf, kq], base + r,
                              sem.at[buf]).start(priority=kq % 2)
            return carry

        lax.fori_loop(0, bm // SUBLANES, issue, 0)

    @pl.when(i == 0)
    def _():
        gather(dest_ref, 0)

    for nxt in range(2):
        @pl.when(jnp.logical_and(i + 1 < n, (i + 1) % 2 == nxt))
        def _(nxt=nxt):
            gather(dnext_ref, nxt)

    for kq in range(TOP_K):
        pltpu.make_async_copy(ys_ref.at[pl.ds(0, bm)], buf_ref.at[slot, kq], sem.at[slot]).wait()
    acc = x1_ref[...]
    for kq in range(TOP_K):
        acc = acc + route_ref[:, TOP_K + kq:TOP_K + kq + 1] * buf_ref[slot, kq]
    o_ref[...] = acc


def _combine(dest_flat, x1, route, ys, bm=512):
    t, d = x1.shape
    n = t // bm
    return pl.pallas_call(
        functools.partial(_combine_kernel, bm=bm),
        grid=(n,),
        in_specs=[pl.BlockSpec((bm * TOP_K,), lambda i: (i,), memory_space=pltpu.SMEM),
                  pl.BlockSpec((bm * TOP_K,), lambda i: (jnp.minimum(i + 1, n - 1),), memory_space=pltpu.SMEM),
                  pl.BlockSpec((bm, d), lambda i: (i, 0)),
                  pl.BlockSpec((bm, LANES), lambda i: (i, 0)),
                  pl.BlockSpec(memory_space=pl.ANY)],
        out_specs=pl.BlockSpec((bm, d), lambda i: (i, 0)),
        out_shape=jax.ShapeDtypeStruct((t, d), F32),
        scratch_shapes=[pltpu.VMEM((2, TOP_K, bm, d), F32), pltpu.SemaphoreType.DMA((2,))],
        compiler_params=_cparams(("arbitrary",)),
        name="combine",
    )(dest_flat, dest_flat, x1, route, ys)


def _layer(x, mem, ln1, w_in, token_mu, rwkv_w0, rwkv_w2, rwkv_a0, rwkv_a2, rwkv_g2,
           rwkv_k_k, rwkv_k_a, rwkv_r_k, rwkv_ln_w, rwkv_ln_b, dil_q_norm, dil_k_norm,
           ln_mem, w_mem_kv, mem_q_norm, mem_k_norm, p_rwkv, p_dil, p_mem, w_out,
           ln2, w_router, b_router, w1, b1, w2, b2):
    b, s, d = x.shape
    t = b * s
    x2 = x.reshape(t, d)
    zr, dil_qkv, mq, gates = _in_proj(x2, ln1, w_in, dil_q_norm, dil_k_norm, mem_q_norm)
    y_rwkv = _rwkv(zr.reshape(b, s, RWKV_IN), token_mu, rwkv_w0, rwkv_w2, rwkv_a0, rwkv_a2, rwkv_g2,
                   rwkv_k_k, rwkv_k_a, rwkv_r_k.reshape(-1), rwkv_ln_w, rwkv_ln_b)
    outs, lses = [], []
    for g, (_, dilation) in enumerate(DIL_PATTERNS):
        o, l = _dil_group(*dil_qkv[3 * g:3 * g + 3], b, dilation)
        outs.append(o)
        lses.append(l)
    mk, mv = _mem_kv(mem, ln_mem, w_mem_kv, mem_k_norm)
    y_mem = _mem_attn(mq.reshape(b, s, MEM_DIM), mk, mv)
    x1, h2, route, counts = _mix(x2, y_rwkv.reshape(t, RWKV_DIM), outs, lses, y_mem.reshape(t, MEM_DIM), gates,
                                 p_rwkv, p_dil, p_mem, w_out, ln2, w_router, b_router)

    counts = counts[0, :N_EXPERTS].astype(jnp.int32)
    nblk = (counts + EXPERT_ROWS - 1) // EXPERT_ROWS
    bend = jnp.cumsum(nblk)
    pstart = ((bend - nblk) * EXPERT_ROWS).astype(F32)
    pstart = jnp.zeros((1, LANES), F32).at[0, :N_EXPERTS].set(pstart)
    n_blocks = (t * TOP_K) // EXPERT_ROWS + N_EXPERTS
    block_e = jnp.sum(bend[None, :] <= jnp.arange(n_blocks, dtype=jnp.int32)[:, None], axis=1)
    block_e = jnp.minimum(block_e, N_EXPERTS - 1).astype(jnp.int32)
    n_used = bend[-1:].astype(jnp.int32)
    used = nblk > 0
    eids = jnp.arange(N_EXPERTS, dtype=jnp.int32)
    ordinal = jnp.cumsum(used.astype(jnp.int32)) - 1
    later = jnp.where(used[None, :] & (eids[None, :] > eids[:, None]), eids[None, :], N_EXPERTS)
    next_used = jnp.min(later, axis=1)
    next_used = jnp.where(next_used < N_EXPERTS, next_used, -1).astype(jnp.int32)
    is_e = (block_e[:, None] == eids[None, :]).astype(jnp.int32)
    parity = (jnp.sum(is_e * ordinal[None, :], axis=1) % 2).astype(jnp.int32)
    next_e = jnp.sum(is_e * next_used[None, :], axis=1).astype(jnp.int32)
    tail = n_used[0] + eids
    last_blk = jnp.concatenate([jnp.where(used, bend - 1, -1), jnp.where(tail < n_blocks, tail, -1)]).astype(jnp.int32)

    dest = _route(route, pstart)[:, :TOP_K].reshape(-1)
    xs = _dispatch(last_blk, dest, h2, n_blocks * EXPERT_ROWS)
    ys = _experts(block_e, parity, next_e, n_used, xs, w1, b1, w2, b2)
    out = _combine(dest, x1, route, ys)
    return out.reshape(b, s, d)


def kernel(x, mem, ln1, w_in, token_mu, rwkv_w0, rwkv_w2, rwkv_a0, rwkv_a2, rwkv_g2, rwkv_k_k, rwkv_k_a, rwkv_r_k, rwkv_ln_w, rwkv_ln_b, dil_q_norm, dil_k_norm, ln_mem, w_mem_kv, mem_q_norm, mem_k_norm, p_rwkv, p_dil, p_mem, w_out, ln2, w_router, b_router, w1, b1, w2, b2):
    params = (ln1, w_in, token_mu, rwkv_w0, rwkv_w2, rwkv_a0, rwkv_a2, rwkv_g2, rwkv_k_k, rwkv_k_a,
              rwkv_r_k, rwkv_ln_w, rwkv_ln_b, dil_q_norm, dil_k_norm, ln_mem, w_mem_kv, mem_q_norm,
              mem_k_norm, p_rwkv, p_dil, p_mem, w_out, ln2, w_router, b_router, w1, b1, w2, b2)
    for l in range(ln1.shape[0]):
        x = _layer(x, mem, *[p[l] for p in params])
    return x
```

```python
import functools

import jax
import jax.numpy as jnp
from jax import lax
from jax.experimental import pallas as pl
from jax.experimental.pallas import tpu as pltpu

F32 = jnp.float32
BF16 = jnp.bfloat16

NORM_EPS = 1e-5
HEAD_DIM = 64
RWKV_DIM = 512
RWKV_IN = 1792
GN_EPS = HEAD_DIM * 1e-5
DIL_DIM = 768
DIL_GROUP_DIM = 256
DIL_PATTERNS = ((128, 1), (512, 4), (2048, 16))
ATTN_BLOCK = 128
MEM_DIM = 512
MEM_HEAD_DIM = 128
N_EXPERTS = 32
TOP_K = 4
SWIGLU_ALPHA = 1.702
SWIGLU_LIMIT = 7.0
LANES = 128
SUBLANES = 8
CHUNK = 64
NEG_BIG = -1e30
VMEM_LIMIT = 48 * 1024 * 1024
EXPERT_VMEM_LIMIT = 56 * 1024 * 1024


def _dot(a, b):
    return jnp.dot(a, b, preferred_element_type=F32)


def _dot_nt(a, b):
    return lax.dot_general(a, b, (((1,), (1,)), ((), ())), preferred_element_type=F32)


def _split2(x):
    hi = x.astype(BF16)
    lo = (x - hi.astype(F32)).astype(BF16)
    return hi, lo


def _split3(x):
    hi = x.astype(BF16)
    r = x - hi.astype(F32)
    lo = r.astype(BF16)
    lo2 = (r - lo.astype(F32)).astype(BF16)
    return hi, lo, lo2


def _dot_rhs3(w_bf16, x):
    hi, lo, lo2 = _split3(x)
    return _dot(w_bf16, hi) + _dot(w_bf16, lo) + _dot(w_bf16, lo2)


def _dot_x3(a, b):
    ah, al = _split2(a)
    bh, bl = _split2(b)
    return _dot(ah, bh) + _dot(al, bh) + _dot(ah, bl)


def _sigmoid(x):
    return 1.0 / (1.0 + jnp.exp(-x))


def _pack_bf16_pairs(x):
    half = x.shape[1] // 2
    bits = lambda v: pltpu.bitcast(v.astype(BF16).astype(F32), jnp.uint32)
    return (bits(x[:, :half]) >> 16) | (bits(x[:, half:]) & jnp.uint32(0xFFFF0000))


def _unpack_bf16_pairs(w):
    lo = pltpu.bitcast(w << 16, F32)
    hi = pltpu.bitcast(w & jnp.uint32(0xFFFF0000), F32)
    return jnp.concatenate([lo, hi], axis=1).astype(BF16)


def _cparams(sem, vmem=VMEM_LIMIT):
    return pltpu.CompilerParams(dimension_semantics=sem, vmem_limit_bytes=vmem)


IN_CHUNK = 256
PERM_ROWS = 256
N_ZR = RWKV_IN // IN_CHUNK
N_DIL = 3 * DIL_DIM // IN_CHUNK
N_MQ = MEM_DIM // IN_CHUNK
N_GATE = 3 * 1024 // IN_CHUNK


def _in_proj_kernel(x_ref, ln_ref, w_ref, qg_ref, kg_ref, mg_ref, p4_ref, p16_ref,
                    zr_ref, *rest, bm):
    dil_refs, (mq_ref, gt_ref) = rest[:N_DIL], rest[N_DIL:]
    perm_refs = (None, p4_ref, p16_ref)
    x = x_ref[...]
    h = x * lax.rsqrt(jnp.mean(x * x, axis=-1, keepdims=True) + NORM_EPS) * ln_ref[...]
    hb = h.astype(BF16)

    lane = lax.broadcasted_iota(jnp.int32, (x.shape[0], LANES), 1)

    def seg_rms(z, seg, gain):
        z2 = z * z
        cols = []
        for cb in range(IN_CHUNK // LANES):
            blk = z2[:, cb * LANES:(cb + 1) * LANES]
            ss = jnp.zeros_like(blk)
            for sg in range(LANES // seg):
                m = (lane >= sg * seg) & (lane < (sg + 1) * seg)
                ss = jnp.where(m, jnp.sum(jnp.where(m, blk, 0.0), axis=-1, keepdims=True), ss)
            cols.append(ss)
        ss = jnp.concatenate(cols, axis=1)
        return z * lax.rsqrt(ss * (1.0 / seg) + NORM_EPS) * gain

    for c in range(N_ZR + N_DIL + N_MQ + N_GATE):
        z = _dot(hb, w_ref[:, c * IN_CHUNK:(c + 1) * IN_CHUNK])
        if c < N_ZR:
            zr_ref[:, c * IN_CHUNK:(c + 1) * IN_CHUNK] = z
            continue
        d = c - N_ZR
        if d < N_DIL:
            which, g = divmod(d, 3)
            if which == 0:
                z = seg_rms(z, HEAD_DIM, qg_ref[...])
            elif which == 1:
                z = seg_rms(z, HEAD_DIM, kg_ref[...])
            zb = z.astype(BF16)
            o_ref = dil_refs[g * 3 + which]
            dilation = DIL_PATTERNS[g][1]
            if dilation == 1:
                o_ref[...] = zb
            else:
                rows = PERM_ROWS // dilation
                for sub in range(bm // PERM_ROWS):
                    zp = _dot(perm_refs[g][...], zb[sub * PERM_ROWS:(sub + 1) * PERM_ROWS, :]).astype(BF16)
                    for r in range(dilation):
                        o_ref[sub * rows:(sub + 1) * rows, r * IN_CHUNK:(r + 1) * IN_CHUNK] = zp[r * rows:(r + 1) * rows, :]
            continue
        d -= N_DIL
        if d < N_MQ:
            mq_ref[:, d * IN_CHUNK:(d + 1) * IN_CHUNK] = seg_rms(z, MEM_HEAD_DIM, mg_ref[...]).astype(BF16)
            continue
        d -= N_MQ
        gt_ref[:, d * IN_CHUNK:(d + 1) * IN_CHUNK] = _sigmoid(z).astype(BF16)


def _class_perm(bm, dilation):
    i = jnp.arange(bm)
    src = (i % (bm // dilation)) * dilation + i // (bm // dilation)
    return (src[:, None] == i[None, :]).astype(BF16)


def _in_proj(x2, ln1, w_in, dil_q_norm, dil_k_norm, mem_q_norm, bm=512):
    t, d = x2.shape
    n_cols = w_in.shape[1]
    wb = w_in.astype(BF16)
    qg = (jnp.tile(dil_q_norm, IN_CHUNK // HEAD_DIM) * (HEAD_DIM ** -0.5)).reshape(1, IN_CHUNK)
    kg = jnp.tile(dil_k_norm, IN_CHUNK // HEAD_DIM).reshape(1, IN_CHUNK)
    mg = jnp.tile(mem_q_norm, IN_CHUNK // MEM_HEAD_DIM).reshape(1, IN_CHUNK)
    p4 = _class_perm(PERM_ROWS, DIL_PATTERNS[1][1])
    p16 = _class_perm(PERM_ROWS, DIL_PATTERNS[2][1])
    row = lambda w: pl.BlockSpec((bm, w), lambda i: (i, 0))
    const = lambda a: pl.BlockSpec(a.shape, lambda i: (0,) * a.ndim)
    dil_specs, dil_shapes = [], []
    for _, dilation in DIL_PATTERNS:
        for _ in range(3):
            dil_specs.append(pl.BlockSpec((bm // dilation, dilation * DIL_GROUP_DIM), lambda i: (i, 0)))
            dil_shapes.append(jax.ShapeDtypeStruct((t // dilation, dilation * DIL_GROUP_DIM), BF16))
    outs = pl.pallas_call(
        functools.partial(_in_proj_kernel, bm=bm),
        grid=(t // bm,),
        in_specs=[row(d), pl.BlockSpec((1, d), lambda i: (0, 0)),
                  pl.BlockSpec((d, n_cols), lambda i: (0, 0), pipeline_mode=pl.Buffered(1)),
                  const(qg), const(kg), const(mg), const(p4), const(p16)],
        out_specs=[row(RWKV_IN)] + dil_specs + [row(MEM_DIM), row(3 * d)],
        out_shape=[jax.ShapeDtypeStruct((t, RWKV_IN), F32)] + dil_shapes
                  + [jax.ShapeDtypeStruct((t, MEM_DIM), BF16), jax.ShapeDtypeStruct((t, 3 * d), BF16)],
        compiler_params=_cparams(("parallel",)),
        name="in_proj",
    )(x2, ln1.reshape(1, d), wb, qg, kg, mg, p4, p16)
    return outs[0], outs[1:1 + N_DIL], outs[1 + N_DIL], outs[2 + N_DIL]


def _rwkv_kernel(z_ref, mu_ref, w0_ref, a0_ref, wl_ref, kk_ref, ka_ref, rk_ref,
                 lnw_ref, lnb_ref, tri_ref, y_ref, carry_ref, state_ref, ybuf_ref, *, tt):
    s_idx = pl.program_id(1)

    @pl.when(s_idx == 0)
    def _():
        carry_ref[...] = jnp.zeros_like(carry_ref)
        state_ref[...] = jnp.zeros_like(state_ref)

    z = z_ref[0]
    rows = lax.broadcasted_iota(jnp.int32, z.shape, 0)
    prev = jnp.where(rows == 0, carry_ref[...], pltpu.roll(z, 1, axis=0))
    carry_ref[...] = z[tt - 1:tt, :]
    zl = z + (prev - z) * mu_ref[...]
    r = zl[:, 0:RWKV_DIM]
    k = zl[:, RWKV_DIM:2 * RWKV_DIM]
    v = zl[:, 2 * RWKV_DIM:3 * RWKV_DIM]
    zlo = zl[:, 3 * RWKV_DIM:RWKV_IN]
    lcol = lax.broadcasted_iota(jnp.int32, zlo.shape, 1)
    feat = jnp.where(lcol < 64, jnp.tanh(zlo), jnp.where(lcol < 128, zlo, _sigmoid(zlo)))
    lora = _dot(feat.astype(BF16), wl_ref[...])
    nu = -(w0_ref[...] + lora[:, 0:RWKV_DIM])
    softplus = jnp.maximum(nu, 0.0) + jnp.log(1.0 + jnp.exp(-jnp.abs(nu)))
    lw = -jnp.exp(-softplus - 0.5)
    a = _sigmoid(a0_ref[...] + lora[:, RWKV_DIM:2 * RWKV_DIM])
    g = lora[:, 2 * RWKV_DIM:3 * RWKV_DIM]

    n_pairs = RWKV_DIM // LANES
    n_chunks = tt // CHUNK
    head_lo = lax.broadcasted_iota(jnp.int32, (tt, LANES), 1) < HEAD_DIM

    def seg_sum(x):
        parts = []
        for p in range(n_pairs):
            xp = x[:, p * LANES:(p + 1) * LANES]
            lo = jnp.sum(jnp.where(head_lo, xp, 0.0), axis=-1, keepdims=True)
            hi = jnp.sum(jnp.where(head_lo, 0.0, xp), axis=-1, keepdims=True)
            parts.append(jnp.where(head_lo, lo, hi))
        return jnp.concatenate(parts, axis=1)

    kk = k * kk_ref[...]
    kk = kk / jnp.maximum(jnp.sqrt(seg_sum(kk * kk)), 1e-12)
    k2 = k * (1.0 + (a - 1.0) * ka_ref[...])
    a_s = -kk
    b_s = kk * a

    cum = _dot_rhs3(tri_ref[...], lw)
    tot = jnp.concatenate(
        [jnp.broadcast_to(cum[(c + 1) * CHUNK - 1:(c + 1) * CHUNK, :], (CHUNK, RWKV_DIM)) for c in range(n_chunks)],
        axis=0)
    e_neg = jnp.exp(-cum)
    e_end = jnp.exp(tot - cum)
    at = (a_s * jnp.exp(cum - lw)).astype(BF16)
    rt = (r * jnp.exp(cum)).astype(BF16)
    bt = (b_s * e_neg).astype(BF16)
    kt = (k2 * e_neg).astype(BF16)
    be = (b_s * e_end).astype(BF16)
    ke = (k2 * e_end).astype(BF16)
    vb = v.astype(BF16)
    e_tot = jnp.exp(tot)

    r128 = lax.broadcasted_iota(jnp.int32, (LANES, LANES), 0)
    c128 = lax.broadcasted_iota(jnp.int32, (LANES, LANES), 1)
    lane_lo = lax.broadcasted_iota(jnp.int32, (CHUNK, LANES), 1) < HEAD_DIM
    stril = c128 < r128
    tril = c128 <= r128
    eye = (c128 == r128).astype(F32)
    same16 = (r128 // 16) == (c128 // 16)
    same32 = (r128 // 32) == (c128 // 32)
    off16 = same32 & jnp.logical_not(same16)
    off32 = jnp.logical_not(same32)

    def bdiag(xp):
        zero = jnp.zeros_like(xp)
        return jnp.concatenate([jnp.where(lane_lo, xp, zero), jnp.where(lane_lo, zero, xp)], axis=0)

    units = [(ci, p) for ci in range(n_chunks) for p in range(n_pairs)]
    pick = lambda arr, u: arr[u[0] * CHUNK:(u[0] + 1) * CHUNK, u[1] * LANES:(u[1] + 1) * LANES]
    cat0 = lambda xs: jnp.concatenate(xs, axis=0)
    cat1 = lambda xs: jnp.concatenate(xs, axis=1)
    zero_b = jnp.zeros((LANES, LANES), BF16)

    at_b = [bdiag(pick(at, u)) for u in units]
    v_b = [bdiag(pick(vb, u)) for u in units]
    be_b = [bdiag(pick(be, u)) for u in units]
    ke_b = [bdiag(pick(ke, u)) for u in units]
    rt_b = [bdiag(pick(rt, u)) for u in units]
    a_ab, a_ak, a_rbk = [], [], []
    for i, u in enumerate(units):
        mq = _dot_nt(cat0([at_b[i], rt_b[i]]), cat0([bdiag(pick(bt, u)), bdiag(pick(kt, u))]))
        a_ab.append(jnp.where(stril, mq[:LANES, :LANES], 0.0).astype(BF16))
        a_ak.append(jnp.where(stril, mq[:LANES, LANES:], 0.0).astype(BF16))
        a_rbk.append(cat1([jnp.where(tril, mq[LANES:, :LANES], 0.0), jnp.where(tril, mq[LANES:, LANES:], 0.0)]).astype(BF16))
    w_b = [_dot(a_ak[i], v_b[i]).astype(BF16) for i in range(len(units))]

    d1 = [jnp.where(same16, x, jnp.zeros_like(x)) for x in a_ab]
    xs = [eye + d.astype(F32) for d in d1]
    d2 = [_dot(d, d).astype(BF16) for d in d1]
    t_ = [_dot(d2[i], cat1([xs[i].astype(BF16), d2[i]])) for i in range(len(units))]
    xs = [xs[i] + t_[i][:, :LANES] for i in range(len(units))]
    d4 = [t[:, LANES:].astype(BF16) for t in t_]
    t_ = [_dot(d4[i], cat1([xs[i].astype(BF16), d4[i]])) for i in range(len(units))]
    xs = [xs[i] + t_[i][:, :LANES] for i in range(len(units))]
    d8 = [t[:, LANES:].astype(BF16) for t in t_]
    xs = [xs[i] + _dot(d8[i], xs[i].astype(BF16)) for i in range(len(units))]
    for off in (off16, off32):
        xb = [x.astype(BF16) for x in xs]
        g_ = [_dot(jnp.where(off, a_ab[i], jnp.zeros_like(a_ab[i])), xb[i]).astype(BF16) for i in range(len(units))]
        xs = [xs[i] + _dot(xb[i], g_[i]) for i in range(len(units))]

    pq = [_dot(xs[i].astype(BF16), cat1([at_b[i], w_b[i]])) for i in range(len(units))]
    ry = [_dot(a_rbk[i], cat0([pq[i].astype(BF16), cat1([zero_b, v_b[i]])])) for i in range(len(units))]
    r2 = [(rt_b[i].astype(F32) + ry[i][:, :LANES]).astype(BF16) for i in range(len(units))]
    m_c = [_dot(pq[i][:, :LANES].T.astype(BF16), be_b[i]) for i in range(len(units))]
    n_c = [_dot(cat1([pq[i][:, LANES:].T.astype(BF16), v_b[i].astype(F32).T.astype(BF16)]), cat0([be_b[i], ke_b[i]]))
           for i in range(len(units))]

    state = [state_ref[p] for p in range(n_pairs)]
    for i, (ci, p) in enumerate(units):
        s0 = state[p]
        s_b = s0.astype(BF16)
        y = _dot_nt(r2[i], s_b) + ry[i][:, LANES:]
        e_row = e_tot[ci * CHUNK:ci * CHUNK + 1, p * LANES:(p + 1) * LANES]
        state[p] = s0 * e_row + _dot(s_b, m_c[i].astype(BF16)) + n_c[i]
        ybuf_ref[ci * CHUNK:(ci + 1) * CHUNK, p * LANES:(p + 1) * LANES] = jnp.where(lane_lo, y[:CHUNK], y[CHUNK:])
    for p in range(n_pairs):
        state_ref[p] = state[p]

    y = ybuf_ref[...]
    mean = seg_sum(y) * (1.0 / HEAD_DIM)
    yc = y - mean
    var = seg_sum(yc * yc) * (1.0 / HEAD_DIM)
    yn = yc * lax.rsqrt(var + GN_EPS) * lnw_ref[...] + lnb_ref[...]
    bonus = seg_sum(r * k2 * rk_ref[...]) * v
    y_ref[0] = ((yn + bonus) * g).astype(y_ref.dtype)


def _rwkv(zr, token_mu, w0, w2, a0, a2, g2, k_k, k_a, r_k, ln_w, ln_b, tt=256):
    b, s, _ = zr.shape
    i = jnp.arange(tt)
    same = (i[:, None] // CHUNK) == (i[None, :] // CHUNK)
    tri = (same & (i[None, :] <= i[:, None])).astype(BF16)
    vec = lambda a: a.reshape(1, -1)
    wl = jnp.zeros((RWKV_IN - 3 * RWKV_DIM, 3 * RWKV_DIM), F32)
    wl = wl.at[0:64, 0:RWKV_DIM].set(w2).at[64:128, RWKV_DIM:2 * RWKV_DIM].set(a2)
    wl = wl.at[128:256, 2 * RWKV_DIM:3 * RWKV_DIM].set(g2).astype(BF16)
    params = [vec(token_mu), vec(w0), vec(a0), wl, vec(k_k), vec(k_a), vec(r_k),
              vec(ln_w), vec(ln_b), tri]
    const = lambda a: pl.BlockSpec(a.shape, lambda bi, si: (0,) * a.ndim)
    return pl.pallas_call(
        functools.partial(_rwkv_kernel, tt=tt),
        grid=(b, s // tt),
        in_specs=[pl.BlockSpec((1, tt, RWKV_IN), lambda bi, si: (bi, si, 0))] + [const(a) for a in params],
        out_specs=pl.BlockSpec((1, tt, RWKV_DIM), lambda bi, si: (bi, si, 0)),
        out_shape=jax.ShapeDtypeStruct((b, s, RWKV_DIM), BF16),
        scratch_shapes=[pltpu.VMEM((1, RWKV_IN), F32),
                        pltpu.VMEM((RWKV_DIM // LANES, LANES, LANES), F32),
                        pltpu.VMEM((tt, RWKV_DIM), F32)],
        compiler_params=_cparams(("parallel", "arbitrary")),
        name="rwkv",
    )(zr, *params)


DIL_UNITS = 8


def _dil_kernel(q_ref, k_ref, v_ref, o_ref, l_ref, *, dilation, cps, nbs):
    cg = pl.program_id(1)
    jb = pl.program_id(2)
    bq = ATTN_BLOCK
    n_heads = DIL_GROUP_DIM // HEAD_DIM
    qi = lax.broadcasted_iota(jnp.int32, (bq, 2 * bq), 0)
    kj = lax.broadcasted_iota(jnp.int32, (bq, 2 * bq), 1)
    lane = lax.broadcasted_iota(jnp.int32, (bq, DIL_GROUP_DIM), 1)
    head_masks = [(lane >= h * HEAD_DIM) & (lane < (h + 1) * HEAD_DIM) for h in range(n_heads)]
    units = [(c, bb) for c in range(cps) for bb in range(nbs)]

    qs, kcats, vcats, masks, starts = [], [], [], [], []
    for c, bb in units:
        i = jb * nbs + bb
        cols = slice(c * DIL_GROUP_DIM, (c + 1) * DIL_GROUP_DIM)
        p0 = pl.multiple_of(jnp.maximum(i - 1, 0) * bq, bq)
        c0 = pl.multiple_of(i * bq, bq)
        qs.append(q_ref[0, bb * bq:(bb + 1) * bq, cols])
        kcats.append(jnp.concatenate([k_ref[0, pl.ds(p0, bq), cols], k_ref[0, pl.ds(c0, bq), cols]], axis=0))
        vcats.append(jnp.concatenate([v_ref[0, pl.ds(p0, bq), cols], v_ref[0, pl.ds(c0, bq), cols]], axis=0))
        first = (1 - jnp.minimum(i, 1)) * (2 * bq)
        masks.append(((kj < bq) & (kj >= qi + first)) | ((kj >= bq) & ((kj - bq) <= qi)))
        starts.append((cg * cps + c) + dilation * bq * i)
    pairs = [(u, h) for u in range(len(units)) for h in range(n_heads)]
    sc = [jnp.where(masks[u], _dot_nt(jnp.where(head_masks[h], qs[u], jnp.zeros_like(qs[u])), kcats[u]), NEG_BIG)
          for u, h in pairs]
    mx = [jnp.max(x, axis=-1, keepdims=True) for x in sc]
    pr = [jnp.exp(sc[n] - mx[n]) for n in range(len(pairs))]
    den = [jnp.sum(x, axis=-1, keepdims=True) for x in pr]
    ov = [_dot(pr[n].astype(BF16), vcats[pairs[n][0]]) / den[n] for n in range(len(pairs))]
    for u in range(len(units)):
        acc = jnp.zeros((bq, DIL_GROUP_DIM), F32)
        lacc = jnp.zeros((bq, DIL_GROUP_DIM), F32)
        for h in range(n_heads):
            n = u * n_heads + h
            acc = jnp.where(head_masks[h], ov[n], acc)
            lacc = jnp.where(head_masks[h], mx[n] + jnp.log(den[n]), lacc)
        rows = pl.ds(starts[u], bq, stride=dilation) if dilation > 1 else pl.ds(pl.multiple_of(starts[u], bq), bq)
        for half in range(DIL_GROUP_DIM // LANES):
            o_ref[0, half, rows, :] = acc[:, half * LANES:(half + 1) * LANES]
            l_ref[0, half, rows, :] = lacc[:, half * LANES:(half + 1) * LANES]


def _dil_group(q, k, v, b, dilation):
    n = q.shape[0] // b
    s = n * dilation
    nb = n // ATTN_BLOCK
    nbs = min(nb, DIL_UNITS)
    cps = DIL_UNITS // nbs
    halves = DIL_GROUP_DIM // LANES
    view = lambda a: a.reshape(b, n, dilation * DIL_GROUP_DIM)
    qmap = lambda bi, ci, ji: (bi, ji, ci)
    kmap = lambda bi, ci, ji: (bi, 0, ci)
    omap = lambda bi, ci, ji: (bi, 0, 0, 0)
    return pl.pallas_call(
        functools.partial(_dil_kernel, dilation=dilation, cps=cps, nbs=nbs),
        grid=(b, dilation // cps, nb // nbs),
        in_specs=[pl.BlockSpec((1, nbs * ATTN_BLOCK, cps * DIL_GROUP_DIM), qmap),
                  pl.BlockSpec((1, n, cps * DIL_GROUP_DIM), kmap),
                  pl.BlockSpec((1, n, cps * DIL_GROUP_DIM), kmap)],
        out_specs=[pl.BlockSpec((1, halves, s, LANES), omap),
                   pl.BlockSpec((1, halves, s, LANES), omap)],
        out_shape=[jax.ShapeDtypeStruct((b, halves, s, LANES), F32),
                   jax.ShapeDtypeStruct((b, halves, s, LANES), F32)],
        compiler_params=_cparams(("parallel", "arbitrary", "arbitrary")),
        name=f"dil_attn_d{dilation}",
    )(view(q), view(k), view(v))


MEM_ROWS = 256

def _mem_kv_kernel(m_ref, ln_ref, w_ref, kn_ref, k_ref, v_ref):
    x = m_ref[0]
    h = x * lax.rsqrt(jnp.mean(x * x, axis=-1, keepdims=True) + NORM_EPS) * ln_ref[...]
    kv = _dot(h.astype(BF16), w_ref[...])
    for hd in range(MEM_DIM // MEM_HEAD_DIM):
        sl = slice(hd * MEM_HEAD_DIM, (hd + 1) * MEM_HEAD_DIM)
        kh = kv[:, sl]
        kh = kh * lax.rsqrt(jnp.mean(kh * kh, axis=-1, keepdims=True) + NORM_EPS) * kn_ref[...]
        k_ref[0, :, sl] = kh.astype(BF16)
    v_ref[0] = kv[:, MEM_DIM:].astype(BF16)


def _mem_kv(mem, ln_mem, w_mem_kv, mem_k_norm):
    b, m, d = mem.shape
    return pl.pallas_call(
        _mem_kv_kernel,
        grid=(b,),
        in_specs=[pl.BlockSpec((1, m, d), lambda i: (i, 0, 0)),
                  pl.BlockSpec((1, d), lambda i: (0, 0)),
                  pl.BlockSpec((d, 2 * MEM_DIM), lambda i: (0, 0)),
                  pl.BlockSpec((1, MEM_HEAD_DIM), lambda i: (0, 0))],
        out_specs=[pl.BlockSpec((1, m, MEM_DIM), lambda i: (i, 0, 0)),
                   pl.BlockSpec((1, m, MEM_DIM), lambda i: (i, 0, 0))],
        out_shape=[jax.ShapeDtypeStruct((b, m, MEM_DIM), BF16),
                   jax.ShapeDtypeStruct((b, m, MEM_DIM), BF16)],
        compiler_params=_cparams(("parallel",)),
        name="mem_kv",
    )(mem, ln_mem.reshape(1, d), w_mem_kv.astype(BF16), mem_k_norm.reshape(1, MEM_HEAD_DIM))


def _mem_attn_kernel(q_ref, k_ref, v_ref, o_ref):
    bm = q_ref.shape[1]
    units = [(r0, hd) for r0 in range(0, bm, MEM_ROWS) for hd in range(MEM_DIM // MEM_HEAD_DIM)]
    cols = lambda hd: slice(hd * MEM_HEAD_DIM, (hd + 1) * MEM_HEAD_DIM)
    sc = [_dot_nt(q_ref[0, r0:r0 + MEM_ROWS, cols(hd)], k_ref[0, :, cols(hd)]) * (MEM_HEAD_DIM ** -0.5)
          for r0, hd in units]
    mx = [jnp.max(x, axis=-1, keepdims=True) for x in sc]
    pr = [jnp.exp(sc[n] - mx[n]) for n in range(len(units))]
    den = [jnp.sum(x, axis=-1, keepdims=True) for x in pr]
    for n, (r0, hd) in enumerate(units):
        o_ref[0, r0:r0 + MEM_ROWS, cols(hd)] = _dot((pr[n] / den[n]).astype(BF16), v_ref[0, :, cols(hd)]).astype(o_ref.dtype)


def _mem_attn(mq, mk, mv, bm=512):
    b, s, _ = mq.shape
    m = mk.shape[1]
    return pl.pallas_call(
        _mem_attn_kernel,
        grid=(b, s // bm),
        in_specs=[pl.BlockSpec((1, bm, MEM_DIM), lambda bi, si: (bi, si, 0)),
                  pl.BlockSpec((1, m, MEM_DIM), lambda bi, si: (bi, 0, 0)),
                  pl.BlockSpec((1, m, MEM_DIM), lambda bi, si: (bi, 0, 0))],
        out_specs=pl.BlockSpec((1, bm, MEM_DIM), lambda bi, si: (bi, si, 0)),
        out_shape=jax.ShapeDtypeStruct((b, s, MEM_DIM), BF16),
        compiler_params=_cparams(("parallel", "parallel")),
        name="mem_attn",
    )(mq, mk, mv)


def _mix_kernel(x_ref, yr_ref, o0_ref, o1_ref, o2_ref, l0_ref, l1_ref, l2_ref, ym_ref, gt_ref,
                pr_ref, pd_ref, pm_ref, wo_ref, ln2_ref, wr_ref, br_ref,
                x1_ref, h2_ref, route_ref, cnt_ref):
    d = x_ref.shape[1]
    wide = lambda ref: jnp.concatenate([ref[0, hf] for hf in range(DIL_GROUP_DIM // LANES)], axis=1)
    l0, l1, l2 = wide(l0_ref), wide(l1_ref), wide(l2_ref)
    m = jnp.maximum(jnp.maximum(l0, l1), l2)
    e0, e1, e2 = jnp.exp(l0 - m), jnp.exp(l1 - m), jnp.exp(l2 - m)
    y_dil = (e0 * wide(o0_ref) + e1 * wide(o1_ref) + e2 * wide(o2_ref)) / (e0 + e1 + e2)
    mixed = (gt_ref[:, 0:d].astype(F32) * _dot(yr_ref[...], pr_ref[...])
             + gt_ref[:, d:2 * d].astype(F32) * _dot(y_dil.astype(BF16), pd_ref[...])
             + gt_ref[:, 2 * d:3 * d].astype(F32) * _dot(ym_ref[...], pm_ref[...]))
    x1 = x_ref[...] + _dot(mixed.astype(BF16), wo_ref[...])
    x1_ref[...] = x1
    h2 = x1 * lax.rsqrt(jnp.mean(x1 * x1, axis=-1, keepdims=True) + NORM_EPS) * ln2_ref[...]
    h2_ref[...] = _pack_bf16_pairs(h2)
    logits = _dot_x3(h2, wr_ref[...]) + br_ref[...]
    lane = lax.broadcasted_iota(jnp.int32, logits.shape, 1)
    lane_f = lane.astype(F32)
    route = jnp.zeros(logits.shape, F32)
    onehot = jnp.zeros(logits.shape, F32)
    vals = []
    for kq in range(TOP_K):
        mx = jnp.max(logits, axis=-1, keepdims=True)
        idx = jnp.min(jnp.where(logits == mx, lane_f, float(LANES)), axis=-1, keepdims=True)
        hit = lane_f == idx
        vals.append(mx)
        route = jnp.where(lane == kq, idx, route)
        onehot = jnp.where(hit, 1.0, onehot)
        logits = jnp.where(hit, -jnp.inf, logits)
    ex = [jnp.exp(vq - vals[0]) for vq in vals]
    den = ex[0] + ex[1] + ex[2] + ex[3]
    for kq in range(TOP_K):
        route = jnp.where(lane == TOP_K + kq, ex[kq] / den, route)
    route_ref[...] = route

    @pl.when(pl.program_id(0) == 0)
    def _():
        cnt_ref[...] = jnp.zeros_like(cnt_ref)

    cnt_ref[...] += jnp.sum(onehot, axis=0, keepdims=True)


def _mix(x2, y_rwkv, outs, lses, y_mem, gates, p_rwkv, p_dil, p_mem, w_out, ln2, w_router, b_router, bm=512):
    t, d = x2.shape
    wr = jnp.zeros((d, LANES), F32).at[:, :N_EXPERTS].set(w_router)
    br = jnp.full((1, LANES), -jnp.inf, F32).at[0, :N_EXPERTS].set(b_router)
    row = lambda w: pl.BlockSpec((bm, w), lambda i: (i, 0))
    const = lambda a: pl.BlockSpec(a.shape, lambda i: (0,) * a.ndim)
    tiles_per_seq = outs[0].shape[2] // bm
    dil = pl.BlockSpec((1, DIL_GROUP_DIM // LANES, bm, LANES),
                       lambda i: (i // tiles_per_seq, 0, i % tiles_per_seq, 0))
    weights = [p_rwkv.astype(BF16), p_dil.astype(BF16), p_mem.astype(BF16), w_out.astype(BF16),
               ln2.reshape(1, d), wr, br]
    return pl.pallas_call(
        _mix_kernel,
        grid=(t // bm,),
        in_specs=[row(d), row(RWKV_DIM)] + [dil] * 6 + [row(MEM_DIM), row(3 * d)]
                 + [const(a) for a in weights],
        out_specs=[row(d), row(d // 2), row(LANES), pl.BlockSpec((1, LANES), lambda i: (0, 0))],
        out_shape=[jax.ShapeDtypeStruct((t, d), F32), jax.ShapeDtypeStruct((t, d // 2), jnp.uint32),
                   jax.ShapeDtypeStruct((t, LANES), F32), jax.ShapeDtypeStruct((1, LANES), F32)],
        compiler_params=_cparams(("arbitrary",)),
        name="mix",
    )(x2, y_rwkv, *outs, *lses, y_mem, gates, *weights)


def _route_kernel(route_ref, pstart_ref, tri_ref, dest_ref, carry_ref):
    @pl.when(pl.program_id(0) == 0)
    def _():
        carry_ref[...] = jnp.zeros_like(carry_ref)

    route = route_ref[...]
    lane = lax.broadcasted_iota(jnp.int32, route.shape, 1)
    lane_f = lane.astype(F32)
    hits = [lane_f == route[:, kq:kq + 1] for kq in range(TOP_K)]
    onehot = jnp.zeros(route.shape, F32)
    for hq in hits:
        onehot = jnp.where(hq, 1.0, onehot)
    rank = _dot(tri_ref[...], onehot.astype(BF16)) + carry_ref[...]
    slot = pstart_ref[...] + rank
    dest = jnp.zeros(route.shape, jnp.int32)
    for kq in range(TOP_K):
        dk = jnp.sum(jnp.where(hits[kq], slot, 0.0), axis=-1, keepdims=True)
        dest = jnp.where(lane == kq, dk.astype(jnp.int32), dest)
    dest_ref[...] = dest
    carry_ref[...] += jnp.sum(onehot, axis=0, keepdims=True)


def _route(route, pstart, bm=1024):
    t = route.shape[0]
    i = jnp.arange(bm)
    tri = (i[None, :] < i[:, None]).astype(BF16)
    return pl.pallas_call(
        _route_kernel,
        grid=(t // bm,),
        in_specs=[pl.BlockSpec((bm, LANES), lambda i: (i, 0)),
                  pl.BlockSpec((1, LANES), lambda i: (0, 0)),
                  pl.BlockSpec((bm, bm), lambda i: (0, 0))],
        out_specs=pl.BlockSpec((bm, LANES), lambda i: (i, 0)),
        out_shape=jax.ShapeDtypeStruct((t, LANES), jnp.int32),
        scratch_shapes=[pltpu.VMEM((1, LANES), F32)],
        compiler_params=_cparams(("arbitrary",)),
        name="route",
    )(route, pstart, tri)


def _row_copy(src_ref, src_row, dst_ref, dst_row, sem):
    return pltpu.make_async_copy(src_ref.at[pl.ds(src_row, 1)], dst_ref.at[pl.ds(dst_row, 1)], sem)


def _dispatch_kernel(last_ref, dest_ref, h_ref, xs_ref, zero_ref, hbuf_ref, sem_s, sem_l, zsem, *, bm):
    i = pl.program_id(0)
    n = pl.num_programs(0)

    def load(tile, b):
        rows = pl.ds(pl.multiple_of(tile * bm, bm), bm)
        return pltpu.make_async_copy(h_ref.at[rows], hbuf_ref.at[b], sem_l.at[b])

    def wait_scatter(b):
        for kq in range(TOP_K):
            pltpu.make_async_copy(hbuf_ref.at[0], xs_ref.at[pl.ds(0, bm)], sem_s.at[b]).wait()

    @pl.when(i == 0)
    def _():
        load(0, 0).start()
        zero_ref[...] = jnp.zeros_like(zero_ref)

        def zero_copy(e):
            row = pl.multiple_of(last_ref[e] * EXPERT_ROWS, EXPERT_ROWS)
            return pltpu.make_async_copy(zero_ref, xs_ref.at[pl.ds(row, EXPERT_ROWS)], zsem)

        def z_issue(e, carry):
            @pl.when(last_ref[e] >= 0)
            def _():
                zero_copy(e).start()
            return carry

        def z_drain(e, carry):
            @pl.when(last_ref[e] >= 0)
            def _():
                zero_copy(e).wait()
            return carry

        lax.fori_loop(0, 2 * N_EXPERTS, z_issue, 0)
        lax.fori_loop(0, 2 * N_EXPERTS, z_drain, 0)

    @pl.when(i + 1 < n)
    def _():
        load(i + 1, (i + 1) % 3).start()

    load(i, i % 3).wait()

    for phase in range(6):
        @pl.when(i % 6 == phase)
        def _(phase=phase):
            def issue(j, carry):
                for kq in range(TOP_K):
                    _row_copy(hbuf_ref.at[phase % 3], j, xs_ref, dest_ref[j * TOP_K + kq],
                              sem_s.at[phase % 2]).start(priority=kq % 2)
                return carry

            lax.fori_loop(0, bm, issue, 0)

    @pl.when(i > 0)
    def _():
        wait_scatter((i - 1) % 2)

    @pl.when(i == n - 1)
    def _():
        wait_scatter(i % 2)


def _dispatch(last_blk, dest_flat, h2, n_slots, bm=512):
    t, d = h2.shape
    grid_spec = pltpu.PrefetchScalarGridSpec(
        num_scalar_prefetch=1,
        grid=(t // bm,),
        in_specs=[pl.BlockSpec((bm * TOP_K,), lambda i, lb: (i,), memory_space=pltpu.SMEM),
                  pl.BlockSpec(memory_space=pl.ANY)],
        out_specs=pl.BlockSpec(memory_space=pl.ANY),
        scratch_shapes=[pltpu.VMEM((EXPERT_ROWS, d), h2.dtype), pltpu.VMEM((3, bm, d), h2.dtype),
                        pltpu.SemaphoreType.DMA((2,)), pltpu.SemaphoreType.DMA((3,)), pltpu.SemaphoreType.DMA],
    )
    return pl.pallas_call(
        functools.partial(_dispatch_kernel, bm=bm),
        grid_spec=grid_spec,
        out_shape=jax.ShapeDtypeStruct((n_slots, d), h2.dtype),
        compiler_params=_cparams(("arbitrary",)),
        name="dispatch",
    )(last_blk, dest_flat, h2)


FF_CHUNK = 256
EXPERT_ROWS = 256
BLOCKS_PER_STEP = 2


def _expert_kernel(be_ref, par_ref, nxt_ref, nu_ref, xs_ref, w1_ref, b1_ref, w2_ref, b2_ref, sel_ref, ys_ref,
                   w1f_ref, w2f_ref, w1p_ref, w2b_ref, act_ref, sem):
    i = pl.program_id(0)
    d_ff2 = w1_ref.shape[2]
    half = FF_CHUNK // 2

    def weight_copies(expert, buf):
        return (pltpu.make_async_copy(w1_ref.at[expert], w1f_ref.at[buf], sem.at[0, buf]),
                pltpu.make_async_copy(w2_ref.at[expert], w2f_ref.at[buf], sem.at[1, buf]))

    @pl.when(i == 0)
    def _():
        for cp in weight_copies(be_ref[0], par_ref[0]):
            cp.start()

    for hb_i in range(BLOCKS_PER_STEP):
        blk = i * BLOCKS_PER_STEP + hb_i
        rows = slice(hb_i * EXPERT_ROWS, (hb_i + 1) * EXPERT_ROWS)
        e = be_ref[blk]
        slot = par_ref[blk]
        fresh = jnp.logical_or(blk == 0, be_ref[jnp.maximum(blk - 1, 0)] != e)
        used = blk < nu_ref[0]

        @pl.when(jnp.logical_and(fresh, used))
        def _():
            for cp in weight_copies(e, slot):
                cp.wait()
            nxt = nxt_ref[blk]

            @pl.when(nxt >= 0)
            def _():
                for cp in weight_copies(nxt, 1 - slot):
                    cp.start(priority=1)

            for c in range(d_ff2 // FF_CHUNK):
                sl = slice(c * FF_CHUNK, (c + 1) * FF_CHUNK)
                w1p_ref[:, sl] = _dot(w1f_ref[slot, :, sl].astype(BF16), sel_ref[...]).astype(BF16)
            w2b_ref[...] = w2f_ref[slot].astype(BF16)

        @pl.when(used)
        def _():
            x = _unpack_bf16_pairs(xs_ref[rows, :])
            for c in range(d_ff2 // (2 * FF_CHUNK)):
                sl = slice(2 * c * FF_CHUNK, 2 * (c + 1) * FF_CHUNK)
                hb = _dot(x, w1p_ref[:, sl]) + b1_ref[e, :, sl]
                for j in range(2):
                    x_glu = jnp.minimum(hb[:, j * FF_CHUNK:j * FF_CHUNK + half], SWIGLU_LIMIT)
                    x_lin = jnp.clip(hb[:, j * FF_CHUNK + half:(j + 1) * FF_CHUNK], -SWIGLU_LIMIT, SWIGLU_LIMIT)
                    act = x_glu * _sigmoid(SWIGLU_ALPHA * x_glu) * (x_lin + 1.0)
                    act_ref[:, (2 * c + j) * half:(2 * c + j + 1) * half] = act.astype(BF16)
            ys_ref[rows, :] = _dot(act_ref[...], w2b_ref[...]) + b2_ref[e]

        @pl.when(jnp.logical_not(used))
        def _():
            ys_ref[rows, :] = jnp.zeros((EXPERT_ROWS, ys_ref.shape[1]), F32)


def _chunk_deinterleave(a):
    lead = a.shape[:-1]
    a = a.reshape(lead + (a.shape[-1] // FF_CHUNK, FF_CHUNK // 2, 2))
    return jnp.swapaxes(a, -1, -2).reshape(lead + (-1,))


def _experts(block_e, parity, next_e, n_used, xs, w1, b1, w2, b2):
    n_slots, dx = xs.shape
    n_e, d, d_ff2 = w1.shape
    n_blocks = n_slots // EXPERT_ROWS
    i = jnp.arange(FF_CHUNK)
    src = jnp.where(i < FF_CHUNK // 2, 2 * i, 2 * (i - FF_CHUNK // 2) + 1)
    sel = (jnp.arange(FF_CHUNK)[:, None] == src[None, :]).astype(BF16)
    b1p = _chunk_deinterleave(b1).reshape(n_e, 1, d_ff2)
    step_rows = BLOCKS_PER_STEP * EXPERT_ROWS
    imap = lambda i, be, par, nxt, nu: (i, 0)
    xmap = lambda i, be, par, nxt, nu: (jnp.minimum(i, (nu[0] - 1) // BLOCKS_PER_STEP), 0)
    cmap = lambda i, be, par, nxt, nu: (0, 0, 0)
    grid_spec = pltpu.PrefetchScalarGridSpec(
        num_scalar_prefetch=4,
        grid=(n_blocks // BLOCKS_PER_STEP,),
        in_specs=[pl.BlockSpec((step_rows, dx), xmap),
                  pl.BlockSpec(memory_space=pl.ANY),
                  pl.BlockSpec((n_e, 1, d_ff2), cmap),
                  pl.BlockSpec(memory_space=pl.ANY),
                  pl.BlockSpec((n_e, 1, d), cmap),
                  pl.BlockSpec((FF_CHUNK, FF_CHUNK), lambda i, be, par, nxt, nu: (0, 0))],
        out_specs=pl.BlockSpec((step_rows, d), imap),
        scratch_shapes=[pltpu.VMEM((2, d, d_ff2), F32), pltpu.VMEM((2, d_ff2 // 2, d), F32),
                        pltpu.VMEM((d, d_ff2), BF16), pltpu.VMEM((d_ff2 // 2, d), BF16),
                        pltpu.VMEM((EXPERT_ROWS, d_ff2 // 2), BF16),
                        pltpu.SemaphoreType.DMA((2, 2))],
    )
    return pl.pallas_call(
        _expert_kernel,
        grid_spec=grid_spec,
        out_shape=jax.ShapeDtypeStruct((n_slots, d), F32),
        compiler_params=_cparams(("arbitrary",), vmem=EXPERT_VMEM_LIMIT),
        name="experts",
    )(block_e, parity, next_e, n_used, xs, w1, b1p, w2, b2.reshape(n_e, 1, d), sel)


def _combine_kernel(dest_ref, dnext_ref, x1_ref, route_ref, ys_ref, o_ref, buf_ref, sem, *, bm):
    i = pl.program_id(0)
    n = pl.num_programs(0)
    slot = i % 2

    def gather(d_ref, buf):
        def issue(g, carry):
            base = pl.multiple_of(g * SUBLANES, SUBLANES)
            for r in range(SUBLANES):
                for kq in range(TOP_K):
                    _row_copy(ys_ref, d_ref[(base + r) * TOP_K + kq], buf_ref.at[buf, kq], base + r,
                              sem.at[buf]).start(priority=kq % 2)
            return carry

        lax.fori_loop(0, bm // SUBLANES, issue, 0)

    @pl.when(i == 0)
    def _():
        gather(dest_ref, 0)

    for nxt in range(2):
        @pl.when(jnp.logical_and(i + 1 < n, (i + 1) % 2 == nxt))
        def _(nxt=nxt):
            gather(dnext_ref, nxt)

    for kq in range(TOP_K):
        pltpu.make_async_copy(ys_ref.at[pl.ds(0, bm)], buf_ref.at[slot, kq], sem.at[slot]).wait()
    acc = x1_ref[...]
    for kq in range(TOP_K):
        acc = acc + route_ref[:, TOP_K + kq:TOP_K + kq + 1] * buf_ref[slot, kq]
    o_ref[...] = acc


def _combine(dest_flat, x1, route, ys, bm=512):
    t, d = x1.shape
    n = t // bm
    return pl.pallas_call(
        functools.partial(_combine_kernel, bm=bm),
        grid=(n,),
        in_specs=[pl.BlockSpec((bm * TOP_K,), lambda i: (i,), memory_space=pltpu.SMEM),
                  pl.BlockSpec((bm * TOP_K,), lambda i: (jnp.minimum(i + 1, n - 1),), memory_space=pltpu.SMEM),
                  pl.BlockSpec((bm, d), lambda i: (i, 0)),
                  pl.BlockSpec((bm, LANES), lambda i: (i, 0)),
                  pl.BlockSpec(memory_space=pl.ANY)],
        out_specs=pl.BlockSpec((bm, d), lambda i: (i, 0)),
        out_shape=jax.ShapeDtypeStruct((t, d), F32),
        scratch_shapes=[pltpu.VMEM((2, TOP_K, bm, d), F32), pltpu.SemaphoreType.DMA((2,))],
        compiler_params=_cparams(("arbitrary",)),
        name="combine",
    )(dest_flat, dest_flat, x1, route, ys)


def _layer(x, mem, ln1, w_in, token_mu, rwkv_w0, rwkv_w2, rwkv_a0, rwkv_a2, rwkv_g2,
           rwkv_k_k, rwkv_k_a, rwkv_r_k, rwkv_ln_w, rwkv_ln_b, dil_q_norm, dil_k_norm,
           ln_mem, w_mem_kv, mem_q_norm, mem_k_norm, p_rwkv, p_dil, p_mem, w_out,
           ln2, w_router, b_router, w1, b1, w2, b2):
    b, s, d = x.shape
    t = b * s
    x2 = x.reshape(t, d)
    zr, dil_qkv, mq, gates = _in_proj(x2, ln1, w_in, dil_q_norm, dil_k_norm, mem_q_norm)
    y_rwkv = _rwkv(zr.reshape(b, s, RWKV_IN), token_mu, rwkv_w0, rwkv_w2, rwkv_a0, rwkv_a2, rwkv_g2,
                   rwkv_k_k, rwkv_k_a, rwkv_r_k.reshape(-1), rwkv_ln_w, rwkv_ln_b)
    outs, lses = [], []
    for g, (_, dilation) in enumerate(DIL_PATTERNS):
        o, l = _dil_group(*dil_qkv[3 * g:3 * g + 3], b, dilation)
        outs.append(o)
        lses.append(l)
    mk, mv = _mem_kv(mem, ln_mem, w_mem_kv, mem_k_norm)
    y_mem = _mem_attn(mq.reshape(b, s, MEM_DIM), mk, mv)
    x1, h2, route, counts = _mix(x2, y_rwkv.reshape(t, RWKV_DIM), outs, lses, y_mem.reshape(t, MEM_DIM), gates,
                                 p_rwkv, p_dil, p_mem, w_out, ln2, w_router, b_router)

    counts = counts[0, :N_EXPERTS].astype(jnp.int32)
    nblk = (counts + EXPERT_ROWS - 1) // EXPERT_ROWS
    bend = jnp.cumsum(nblk)
    pstart = ((bend - nblk) * EXPERT_ROWS).astype(F32)
    pstart = jnp.zeros((1, LANES), F32).at[0, :N_EXPERTS].set(pstart)
    n_blocks = (t * TOP_K) // EXPERT_ROWS + N_EXPERTS
    block_e = jnp.sum(bend[None, :] <= jnp.arange(n_blocks, dtype=jnp.int32)[:, None], axis=1)
    block_e = jnp.minimum(block_e, N_EXPERTS - 1).astype(jnp.int32)
    n_used = bend[-1:].astype(jnp.int32)
    used = nblk > 0
    eids = jnp.arange(N_EXPERTS, dtype=jnp.int32)
    ordinal = jnp.cumsum(used.astype(jnp.int32)) - 1
    later = jnp.where(used[None, :] & (eids[None, :] > eids[:, None]), eids[None, :], N_EXPERTS)
    next_used = jnp.min(later, axis=1)
    next_used = jnp.where(next_used < N_EXPERTS, next_used, -1).astype(jnp.int32)
    is_e = (block_e[:, None] == eids[None, :]).astype(jnp.int32)
    parity = (jnp.sum(is_e * ordinal[None, :], axis=1) % 2).astype(jnp.int32)
    next_e = jnp.sum(is_e * next_used[None, :], axis=1).astype(jnp.int32)
    tail = n_used[0] + eids
    last_blk = jnp.concatenate([jnp.where(used, bend - 1, -1), jnp.where(tail < n_blocks, tail, -1)]).astype(jnp.int32)

    dest = _route(route, pstart)[:, :TOP_K].reshape(-1)
    xs = _dispatch(last_blk, dest, h2, n_blocks * EXPERT_ROWS)
    ys = _experts(block_e, parity, next_e, n_used, xs, w1, b1, w2, b2)
    out = _combine(dest, x1, route, ys)
    return out.reshape(b, s, d)


def kernel(x, mem, ln1, w_in, token_mu, rwkv_w0, rwkv_w2, rwkv_a0, rwkv_a2, rwkv_g2, rwkv_k_k, rwkv_k_a, rwkv_r_k, rwkv_ln_w, rwkv_ln_b, dil_q_norm, dil_k_norm, ln_mem, w_mem_kv, mem_q_norm, mem_k_norm, p_rwkv, p_dil, p_mem, w_out, ln2, w_router, b_router, w1, b1, w2, b2):
    params = (ln1, w_in, token_mu, rwkv_w0, rwkv_w2, rwkv_a0, rwkv_a2, rwkv_g2, rwkv_k_k, rwkv_k_a,
              rwkv_r_k, rwkv_ln_w, rwkv_ln_b, dil_q_norm, dil_k_norm, ln_mem, w_mem_kv, mem_q_norm,
              mem_k_norm, p_rwkv, p_dil, p_mem, w_out, ln2, w_router, b_router, w1, b1, w2, b2)
    for l in range(ln1.shape[0]):
        x = _layer(x, mem, *[p[l] for p in params])
    return x
```

```python
import functools

import jax
import jax.numpy as jnp
from jax import lax
from jax.experimental import pallas as pl
from jax.experimental.pallas import tpu as pltpu

F32 = jnp.float32
BF16 = jnp.bfloat16

NORM_EPS = 1e-5
HEAD_DIM = 64
RWKV_DIM = 512
RWKV_IN = 1792
GN_EPS = HEAD_DIM * 1e-5
DIL_DIM = 768
DIL_GROUP_DIM = 256
DIL_PATTERNS = ((128, 1), (512, 4), (2048, 16))
ATTN_BLOCK = 128
MEM_DIM = 512
MEM_HEAD_DIM = 128
N_EXPERTS = 32
TOP_K = 4
SWIGLU_ALPHA = 1.702
SWIGLU_LIMIT = 7.0
LANES = 128
SUBLANES = 8
CHUNK = 64
NEG_BIG = -1e30
VMEM_LIMIT = 48 * 1024 * 1024
EXPERT_VMEM_LIMIT = 56 * 1024 * 1024


def _dot(a, b):
    return jnp.dot(a, b, preferred_element_type=F32)


def _dot_nt(a, b):
    return lax.dot_general(a, b, (((1,), (1,)), ((), ())), preferred_element_type=F32)


def _split2(x):
    hi = x.astype(BF16)
    lo = (x - hi.astype(F32)).astype(BF16)
    return hi, lo


def _split3(x):
    hi = x.astype(BF16)
    r = x - hi.astype(F32)
    lo = r.astype(BF16)
    lo2 = (r - lo.astype(F32)).astype(BF16)
    return hi, lo, lo2


def _dot_rhs3(w_bf16, x):
    hi, lo, lo2 = _split3(x)
    return _dot(w_bf16, hi) + _dot(w_bf16, lo) + _dot(w_bf16, lo2)


def _dot_x3(a, b):
    ah, al = _split2(a)
    bh, bl = _split2(b)
    return _dot(ah, bh) + _dot(al, bh) + _dot(ah, bl)


def _sigmoid(x):
    return 1.0 / (1.0 + jnp.exp(-x))


def _pack_bf16_pairs(x):
    half = x.shape[1] // 2
    bits = lambda v: pltpu.bitcast(v.astype(BF16).astype(F32), jnp.uint32)
    return (bits(x[:, :half]) >> 16) | (bits(x[:, half:]) & jnp.uint32(0xFFFF0000))


def _unpack_bf16_pairs(w):
    lo = pltpu.bitcast(w << 16, F32)
    hi = pltpu.bitcast(w & jnp.uint32(0xFFFF0000), F32)
    return jnp.concatenate([lo, hi], axis=1).astype(BF16)


def _cparams(sem, vmem=VMEM_LIMIT):
    return pltpu.CompilerParams(dimension_semantics=sem, vmem_limit_bytes=vmem)


IN_CHUNK = 256
PERM_ROWS = 256
N_ZR = RWKV_IN // IN_CHUNK
N_DIL = 3 * DIL_DIM // IN_CHUNK
N_MQ = MEM_DIM // IN_CHUNK
N_GATE = 3 * 1024 // IN_CHUNK


def _in_proj_kernel(x_ref, ln_ref, w_ref, qg_ref, kg_ref, mg_ref, p4_ref, p16_ref,
                    zr_ref, *rest, bm):
    dil_refs, (mq_ref, gt_ref) = rest[:N_DIL], rest[N_DIL:]
    perm_refs = (None, p4_ref, p16_ref)
    x = x_ref[...]
    h = x * lax.rsqrt(jnp.mean(x * x, axis=-1, keepdims=True) + NORM_EPS) * ln_ref[...]
    hb = h.astype(BF16)

    lane = lax.broadcasted_iota(jnp.int32, (x.shape[0], LANES), 1)

    def seg_rms(z, seg, gain):
        z2 = z * z
        cols = []
        for cb in range(IN_CHUNK // LANES):
            blk = z2[:, cb * LANES:(cb + 1) * LANES]
            ss = jnp.zeros_like(blk)
            for sg in range(LANES // seg):
                m = (lane >= sg * seg) & (lane < (sg + 1) * seg)
                ss = jnp.where(m, jnp.sum(jnp.where(m, blk, 0.0), axis=-1, keepdims=True), ss)
            cols.append(ss)
        ss = jnp.concatenate(cols, axis=1)
        return z * lax.rsqrt(ss * (1.0 / seg) + NORM_EPS) * gain

    for c in range(N_ZR + N_DIL + N_MQ + N_GATE):
        z = _dot(hb, w_ref[:, c * IN_CHUNK:(c + 1) * IN_CHUNK])
        if c < N_ZR:
            zr_ref[:, c * IN_CHUNK:(c + 1) * IN_CHUNK] = z
            continue
        d = c - N_ZR
        if d < N_DIL:
            which, g = divmod(d, 3)
            if which == 0:
                z = seg_rms(z, HEAD_DIM, qg_ref[...])
            elif which == 1:
                z = seg_rms(z, HEAD_DIM, kg_ref[...])
            zb = z.astype(BF16)
            o_ref = dil_refs[g * 3 + which]
            dilation = DIL_PATTERNS[g][1]
            if dilation == 1:
                o_ref[...] = zb
            else:
                rows = PERM_ROWS // dilation
                for sub in range(bm // PERM_ROWS):
                    zp = _dot(perm_refs[g][...], zb[sub * PERM_ROWS:(sub + 1) * PERM_ROWS, :]).astype(BF16)
                    for r in range(dilation):
                        o_ref[sub * rows:(sub + 1) * rows, r * IN_CHUNK:(r + 1) * IN_CHUNK] = zp[r * rows:(r + 1) * rows, :]
            continue
        d -= N_DIL
        if d < N_MQ:
            mq_ref[:, d * IN_CHUNK:(d + 1) * IN_CHUNK] = seg_rms(z, MEM_HEAD_DIM, mg_ref[...]).astype(BF16)
            continue
        d -= N_MQ
        gt_ref[:, d * IN_CHUNK:(d + 1) * IN_CHUNK] = _sigmoid(z).astype(BF16)


def _class_perm(bm, dilation):
    i = jnp.arange(bm)
    src = (i % (bm // dilation)) * dilation + i // (bm // dilation)
    return (src[:, None] == i[None, :]).astype(BF16)


def _in_proj(x2, ln1, w_in, dil_q_norm, dil_k_norm, mem_q_norm, bm=512):
    t, d = x2.shape
    n_cols = w_in.shape[1]
    wb = w_in.astype(BF16)
    qg = (jnp.tile(dil_q_norm, IN_CHUNK // HEAD_DIM) * (HEAD_DIM ** -0.5)).reshape(1, IN_CHUNK)
    kg = jnp.tile(dil_k_norm, IN_CHUNK // HEAD_DIM).reshape(1, IN_CHUNK)
    mg = jnp.tile(mem_q_norm, IN_CHUNK // MEM_HEAD_DIM).reshape(1, IN_CHUNK)
    p4 = _class_perm(PERM_ROWS, DIL_PATTERNS[1][1])
    p16 = _class_perm(PERM_ROWS, DIL_PATTERNS[2][1])
    row = lambda w: pl.BlockSpec((bm, w), lambda i: (i, 0))
    const = lambda a: pl.BlockSpec(a.shape, lambda i: (0,) * a.ndim)
    dil_specs, dil_shapes = [], []
    for _, dilation in DIL_PATTERNS:
        for _ in range(3):
            dil_specs.append(pl.BlockSpec((bm // dilation, dilation * DIL_GROUP_DIM), lambda i: (i, 0)))
            dil_shapes.append(jax.ShapeDtypeStruct((t // dilation, dilation * DIL_GROUP_DIM), BF16))
    outs = pl.pallas_call(
        functools.partial(_in_proj_kernel, bm=bm),
        grid=(t // bm,),
        in_specs=[row(d), pl.BlockSpec((1, d), lambda i: (0, 0)),
                  pl.BlockSpec((d, n_cols), lambda i: (0, 0), pipeline_mode=pl.Buffered(1)),
                  const(qg), const(kg), const(mg), const(p4), const(p16)],
        out_specs=[row(RWKV_IN)] + dil_specs + [row(MEM_DIM), row(3 * d)],
        out_shape=[jax.ShapeDtypeStruct((t, RWKV_IN), F32)] + dil_shapes
                  + [jax.ShapeDtypeStruct((t, MEM_DIM), BF16), jax.ShapeDtypeStruct((t, 3 * d), BF16)],
        compiler_params=_cparams(("parallel",)),
        name="in_proj",
    )(x2, ln1.reshape(1, d), wb, qg, kg, mg, p4, p16)
    return outs[0], outs[1:1 + N_DIL], outs[1 + N_DIL], outs[2 + N_DIL]


def _rwkv_kernel(z_ref, mu_ref, w0_ref, a0_ref, wl_ref, kk_ref, ka_ref, rk_ref,
                 lnw_ref, lnb_ref, tri_ref, y_ref, carry_ref, state_ref, ybuf_ref, *, tt):
    s_idx = pl.program_id(1)

    @pl.when(s_idx == 0)
    def _():
        carry_ref[...] = jnp.zeros_like(carry_ref)
        state_ref[...] = jnp.zeros_like(state_ref)

    z = z_ref[0]
    rows = lax.broadcasted_iota(jnp.int32, z.shape, 0)
    prev = jnp.where(rows == 0, carry_ref[...], pltpu.roll(z, 1, axis=0))
    carry_ref[...] = z[tt - 1:tt, :]
    zl = z + (prev - z) * mu_ref[...]
    r = zl[:, 0:RWKV_DIM]
    k = zl[:, RWKV_DIM:2 * RWKV_DIM]
    v = zl[:, 2 * RWKV_DIM:3 * RWKV_DIM]
    zlo = zl[:, 3 * RWKV_DIM:RWKV_IN]
    lcol = lax.broadcasted_iota(jnp.int32, zlo.shape, 1)
    feat = jnp.where(lcol < 64, jnp.tanh(zlo), jnp.where(lcol < 128, zlo, _sigmoid(zlo)))
    lora = _dot(feat.astype(BF16), wl_ref[...])
    nu = -(w0_ref[...] + lora[:, 0:RWKV_DIM])
    softplus = jnp.maximum(nu, 0.0) + jnp.log(1.0 + jnp.exp(-jnp.abs(nu)))
    lw = -jnp.exp(-softplus - 0.5)
    a = _sigmoid(a0_ref[...] + lora[:, RWKV_DIM:2 * RWKV_DIM])
    g = lora[:, 2 * RWKV_DIM:3 * RWKV_DIM]

    n_pairs = RWKV_DIM // LANES
    n_chunks = tt // CHUNK
    head_lo = lax.broadcasted_iota(jnp.int32, (tt, LANES), 1) < HEAD_DIM

    def seg_sum(x):
        parts = []
        for p in range(n_pairs):
            xp = x[:, p * LANES:(p + 1) * LANES]
            lo = jnp.sum(jnp.where(head_lo, xp, 0.0), axis=-1, keepdims=True)
            hi = jnp.sum(jnp.where(head_lo, 0.0, xp), axis=-1, keepdims=True)
            parts.append(jnp.where(head_lo, lo, hi))
        return jnp.concatenate(parts, axis=1)

    kk = k * kk_ref[...]
    kk = kk / jnp.maximum(jnp.sqrt(seg_sum(kk * kk)), 1e-12)
    k2 = k * (1.0 + (a - 1.0) * ka_ref[...])
    a_s = -kk
    b_s = kk * a

    cum = _dot_rhs3(tri_ref[...], lw)
    tot = jnp.concatenate(
        [jnp.broadcast_to(cum[(c + 1) * CHUNK - 1:(c + 1) * CHUNK, :], (CHUNK, RWKV_DIM)) for c in range(n_chunks)],
        axis=0)
    e_neg = jnp.exp(-cum)
    e_end = jnp.exp(tot - cum)
    at = (a_s * jnp.exp(cum - lw)).astype(BF16)
    rt = (r * jnp.exp(cum)).astype(BF16)
    bt = (b_s * e_neg).astype(BF16)
    kt = (k2 * e_neg).astype(BF16)
    be = (b_s * e_end).astype(BF16)
    ke = (k2 * e_end).astype(BF16)
    vb = v.astype(BF16)
    e_tot = jnp.exp(tot)

    r128 = lax.broadcasted_iota(jnp.int32, (LANES, LANES), 0)
    c128 = lax.broadcasted_iota(jnp.int32, (LANES, LANES), 1)
    lane_lo = lax.broadcasted_iota(jnp.int32, (CHUNK, LANES), 1) < HEAD_DIM
    stril = c128 < r128
    tril = c128 <= r128
    eye = (c128 == r128).astype(F32)
    same16 = (r128 // 16) == (c128 // 16)
    same32 = (r128 // 32) == (c128 // 32)
    off16 = same32 & jnp.logical_not(same16)
    off32 = jnp.logical_not(same32)

    def bdiag(xp):
        zero = jnp.zeros_like(xp)
        return jnp.concatenate([jnp.where(lane_lo, xp, zero), jnp.where(lane_lo, zero, xp)], axis=0)

    units = [(ci, p) for ci in range(n_chunks) for p in range(n_pairs)]
    pick = lambda arr, u: arr[u[0] * CHUNK:(u[0] + 1) * CHUNK, u[1] * LANES:(u[1] + 1) * LANES]
    cat0 = lambda xs: jnp.concatenate(xs, axis=0)
    cat1 = lambda xs: jnp.concatenate(xs, axis=1)
    zero_b = jnp.zeros((LANES, LANES), BF16)

    at_b = [bdiag(pick(at, u)) for u in units]
    v_b = [bdiag(pick(vb, u)) for u in units]
    be_b = [bdiag(pick(be, u)) for u in units]
    ke_b = [bdiag(pick(ke, u)) for u in units]
    rt_b = [bdiag(pick(rt, u)) for u in units]
    a_ab, a_ak, a_rbk = [], [], []
    for i, u in enumerate(units):
        mq = _dot_nt(cat0([at_b[i], rt_b[i]]), cat0([bdiag(pick(bt, u)), bdiag(pick(kt, u))]))
        a_ab.append(jnp.where(stril, mq[:LANES, :LANES], 0.0).astype(BF16))
        a_ak.append(jnp.where(stril, mq[:LANES, LANES:], 0.0).astype(BF16))
        a_rbk.append(cat1([jnp.where(tril, mq[LANES:, :LANES], 0.0), jnp.where(tril, mq[LANES:, LANES:], 0.0)]).astype(BF16))
    w_b = [_dot(a_ak[i], v_b[i]).astype(BF16) for i in range(len(units))]

    d1 = [jnp.where(same16, x, jnp.zeros_like(x)) for x in a_ab]
    xs = [eye + d.astype(F32) for d in d1]
    d2 = [_dot(d, d).astype(BF16) for d in d1]
    t_ = [_dot(d2[i], cat1([xs[i].astype(BF16), d2[i]])) for i in range(len(units))]
    xs = [xs[i] + t_[i][:, :LANES] for i in range(len(units))]
    d4 = [t[:, LANES:].astype(BF16) for t in t_]
    t_ = [_dot(d4[i], cat1([xs[i].astype(BF16), d4[i]])) for i in range(len(units))]
    xs = [xs[i] + t_[i][:, :LANES] for i in range(len(units))]
    d8 = [t[:, LANES:].astype(BF16) for t in t_]
    xs = [xs[i] + _dot(d8[i], xs[i].astype(BF16)) for i in range(len(units))]
    for off in (off16, off32):
        xb = [x.astype(BF16) for x in xs]
        g_ = [_dot(jnp.where(off, a_ab[i], jnp.zeros_like(a_ab[i])), xb[i]).astype(BF16) for i in range(len(units))]
        xs = [xs[i] + _dot(xb[i], g_[i]) for i in range(len(units))]

    pq = [_dot(xs[i].astype(BF16), cat1([at_b[i], w_b[i]])) for i in range(len(units))]
    ry = [_dot(a_rbk[i], cat0([pq[i].astype(BF16), cat1([zero_b, v_b[i]])])) for i in range(len(units))]
    r2 = [(rt_b[i].astype(F32) + ry[i][:, :LANES]).astype(BF16) for i in range(len(units))]
    m_c = [_dot(pq[i][:, :LANES].T.astype(BF16), be_b[i]) for i in range(len(units))]
    n_c = [_dot(cat1([pq[i][:, LANES:].T.astype(BF16), v_b[i].astype(F32).T.astype(BF16)]), cat0([be_b[i], ke_b[i]]))
           for i in range(len(units))]

    state = [state_ref[p] for p in range(n_pairs)]
    for i, (ci, p) in enumerate(units):
        s0 = state[p]
        s_b = s0.astype(BF16)
        y = _dot_nt(r2[i], s_b) + ry[i][:, LANES:]
        e_row = e_tot[ci * CHUNK:ci * CHUNK + 1, p * LANES:(p + 1) * LANES]
        state[p] = s0 * e_row + _dot(s_b, m_c[i].astype(BF16)) + n_c[i]
        ybuf_ref[ci * CHUNK:(ci + 1) * CHUNK, p * LANES:(p + 1) * LANES] = jnp.where(lane_lo, y[:CHUNK], y[CHUNK:])
    for p in range(n_pairs):
        state_ref[p] = state[p]

    y = ybuf_ref[...]
    mean = seg_sum(y) * (1.0 / HEAD_DIM)
    yc = y - mean
    var = seg_sum(yc * yc) * (1.0 / HEAD_DIM)
    yn = yc * lax.rsqrt(var + GN_EPS) * lnw_ref[...] + lnb_ref[...]
    bonus = seg_sum(r * k2 * rk_ref[...]) * v
    y_ref[0] = ((yn + bonus) * g).astype(y_ref.dtype)


def _rwkv(zr, token_mu, w0, w2, a0, a2, g2, k_k, k_a, r_k, ln_w, ln_b, tt=256):
    b, s, _ = zr.shape
    i = jnp.arange(tt)
    same = (i[:, None] // CHUNK) == (i[None, :] // CHUNK)
    tri = (same & (i[None, :] <= i[:, None])).astype(BF16)
    vec = lambda a: a.reshape(1, -1)
    wl = jnp.zeros((RWKV_IN - 3 * RWKV_DIM, 3 * RWKV_DIM), F32)
    wl = wl.at[0:64, 0:RWKV_DIM].set(w2).at[64:128, RWKV_DIM:2 * RWKV_DIM].set(a2)
    wl = wl.at[128:256, 2 * RWKV_DIM:3 * RWKV_DIM].set(g2).astype(BF16)
    params = [vec(token_mu), vec(w0), vec(a0), wl, vec(k_k), vec(k_a), vec(r_k),
              vec(ln_w), vec(ln_b), tri]
    const = lambda a: pl.BlockSpec(a.shape, lambda bi, si: (0,) * a.ndim)
    return pl.pallas_call(
        functools.partial(_rwkv_kernel, tt=tt),
        grid=(b, s // tt),
        in_specs=[pl.BlockSpec((1, tt, RWKV_IN), lambda bi, si: (bi, si, 0))] + [const(a) for a in params],
        out_specs=pl.BlockSpec((1, tt, RWKV_DIM), lambda bi, si: (bi, si, 0)),
        out_shape=jax.ShapeDtypeStruct((b, s, RWKV_DIM), BF16),
        scratch_shapes=[pltpu.VMEM((1, RWKV_IN), F32),
                        pltpu.VMEM((RWKV_DIM // LANES, LANES, LANES), F32),
                        pltpu.VMEM((tt, RWKV_DIM), F32)],
        compiler_params=_cparams(("parallel", "arbitrary")),
        name="rwkv",
    )(zr, *params)


DIL_UNITS = 8


def _dil_kernel(q_ref, k_ref, v_ref, o_ref, l_ref, *, dilation, cps, nbs):
    cg = pl.program_id(1)
    jb = pl.program_id(2)
    bq = ATTN_BLOCK
    n_heads = DIL_GROUP_DIM // HEAD_DIM
    qi = lax.broadcasted_iota(jnp.int32, (bq, 2 * bq), 0)
    kj = lax.broadcasted_iota(jnp.int32, (bq, 2 * bq), 1)
    lane = lax.broadcasted_iota(jnp.int32, (bq, DIL_GROUP_DIM), 1)
    head_masks = [(lane >= h * HEAD_DIM) & (lane < (h + 1) * HEAD_DIM) for h in range(n_heads)]
    units = [(c, bb) for c in range(cps) for bb in range(nbs)]

    qs, kcats, vcats, masks, starts = [], [], [], [], []
    for c, bb in units:
        i = jb * nbs + bb
        cols = slice(c * DIL_GROUP_DIM, (c + 1) * DIL_GROUP_DIM)
        p0 = pl.multiple_of(jnp.maximum(i - 1, 0) * bq, bq)
        c0 = pl.multiple_of(i * bq, bq)
        qs.append(q_ref[0, bb * bq:(bb + 1) * bq, cols])
        kcats.append(jnp.concatenate([k_ref[0, pl.ds(p0, bq), cols], k_ref[0, pl.ds(c0, bq), cols]], axis=0))
        vcats.append(jnp.concatenate([v_ref[0, pl.ds(p0, bq), cols], v_ref[0, pl.ds(c0, bq), cols]], axis=0))
        first = (1 - jnp.minimum(i, 1)) * (2 * bq)
        masks.append(((kj < bq) & (kj >= qi + first)) | ((kj >= bq) & ((kj - bq) <= qi)))
        starts.append((cg * cps + c) + dilation * bq * i)
    pairs = [(u, h) for u in range(len(units)) for h in range(n_heads)]
    sc = [jnp.where(masks[u], _dot_nt(jnp.where(head_masks[h], qs[u], jnp.zeros_like(qs[u])), kcats[u]), NEG_BIG)
          for u, h in pairs]
    mx = [jnp.max(x, axis=-1, keepdims=True) for x in sc]
    pr = [jnp.exp(sc[n] - mx[n]) for n in range(len(pairs))]
    den = [jnp.sum(x, axis=-1, keepdims=True) for x in pr]
    ov = [_dot(pr[n].astype(BF16), vcats[pairs[n][0]]) / den[n] for n in range(len(pairs))]
    for u in range(len(units)):
        acc = jnp.zeros((bq, DIL_GROUP_DIM), F32)
        lacc = jnp.zeros((bq, DIL_GROUP_DIM), F32)
        for h in range(n_heads):
            n = u * n_heads + h
            acc = jnp.where(head_masks[h], ov[n], acc)
            lacc = jnp.where(head_masks[h], mx[n] + jnp.log(den[n]), lacc)
        rows = pl.ds(starts[u], bq, stride=dilation) if dilation > 1 else pl.ds(pl.multiple_of(starts[u], bq), bq)
        for half in range(DIL_GROUP_DIM // LANES):
            o_ref[0, half, rows, :] = acc[:, half * LANES:(half + 1) * LANES]
            l_ref[0, half, rows, :] = lacc[:, half * LANES:(half + 1) * LANES]


def _dil_group(q, k, v, b, dilation):
    n = q.shape[0] // b
    s = n * dilation
    nb = n // ATTN_BLOCK
    nbs = min(nb, DIL_UNITS)
    cps = DIL_UNITS // nbs
    halves = DIL_GROUP_DIM // LANES
    view = lambda a: a.reshape(b, n, dilation * DIL_GROUP_DIM)
    qmap = lambda bi, ci, ji: (bi, ji, ci)
    kmap = lambda bi, ci, ji: (bi, 0, ci)
    omap = lambda bi, ci, ji: (bi, 0, 0, 0)
    return pl.pallas_call(
        functools.partial(_dil_kernel, dilation=dilation, cps=cps, nbs=nbs),
        grid=(b, dilation // cps, nb // nbs),
        in_specs=[pl.BlockSpec((1, nbs * ATTN_BLOCK, cps * DIL_GROUP_DIM), qmap),
                  pl.BlockSpec((1, n, cps * DIL_GROUP_DIM), kmap),
                  pl.BlockSpec((1, n, cps * DIL_GROUP_DIM), kmap)],
        out_specs=[pl.BlockSpec((1, halves, s, LANES), omap),
                   pl.BlockSpec((1, halves, s, LANES), omap)],
        out_shape=[jax.ShapeDtypeStruct((b, halves, s, LANES), F32),
                   jax.ShapeDtypeStruct((b, halves, s, LANES), F32)],
        compiler_params=_cparams(("parallel", "arbitrary", "arbitrary")),
        name=f"dil_attn_d{dilation}",
    )(view(q), view(k), view(v))


MEM_ROWS = 256

def _mem_kv_kernel(m_ref, ln_ref, w_ref, kn_ref, k_ref, v_ref):
    x = m_ref[0]
    h = x * lax.rsqrt(jnp.mean(x * x, axis=-1, keepdims=True) + NORM_EPS) * ln_ref[...]
    kv = _dot(h.astype(BF16), w_ref[...])
    for hd in range(MEM_DIM // MEM_HEAD_DIM):
        sl = slice(hd * MEM_HEAD_DIM, (hd + 1) * MEM_HEAD_DIM)
        kh = kv[:, sl]
        kh = kh * lax.rsqrt(jnp.mean(kh * kh, axis=-1, keepdims=True) + NORM_EPS) * kn_ref[...]
        k_ref[0, :, sl] = kh.astype(BF16)
    v_ref[0] = kv[:, MEM_DIM:].astype(BF16)


def _mem_kv(mem, ln_mem, w_mem_kv, mem_k_norm):
    b, m, d = mem.shape
    return pl.pallas_call(
        _mem_kv_kernel,
        grid=(b,),
        in_specs=[pl.BlockSpec((1, m, d), lambda i: (i, 0, 0)),
                  pl.BlockSpec((1, d), lambda i: (0, 0)),
                  pl.BlockSpec((d, 2 * MEM_DIM), lambda i: (0, 0)),
                  pl.BlockSpec((1, MEM_HEAD_DIM), lambda i: (0, 0))],
        out_specs=[pl.BlockSpec((1, m, MEM_DIM), lambda i: (i, 0, 0)),
                   pl.BlockSpec((1, m, MEM_DIM), lambda i: (i, 0, 0))],
        out_shape=[jax.ShapeDtypeStruct((b, m, MEM_DIM), BF16),
                   jax.ShapeDtypeStruct((b, m, MEM_DIM), BF16)],
        compiler_params=_cparams(("parallel",)),
        name="mem_kv",
    )(mem, ln_mem.reshape(1, d), w_mem_kv.astype(BF16), mem_k_norm.reshape(1, MEM_HEAD_DIM))


def _mem_attn_kernel(q_ref, k_ref, v_ref, o_ref):
    bm = q_ref.shape[1]
    units = [(r0, hd) for r0 in range(0, bm, MEM_ROWS) for hd in range(MEM_DIM // MEM_HEAD_DIM)]
    cols = lambda hd: slice(hd * MEM_HEAD_DIM, (hd + 1) * MEM_HEAD_DIM)
    sc = [_dot_nt(q_ref[0, r0:r0 + MEM_ROWS, cols(hd)], k_ref[0, :, cols(hd)]) * (MEM_HEAD_DIM ** -0.5)
          for r0, hd in units]
    mx = [jnp.max(x, axis=-1, keepdims=True) for x in sc]
    pr = [jnp.exp(sc[n] - mx[n]) for n in range(len(units))]
    den = [jnp.sum(x, axis=-1, keepdims=True) for x in pr]
    for n, (r0, hd) in enumerate(units):
        o_ref[0, r0:r0 + MEM_ROWS, cols(hd)] = _dot((pr[n] / den[n]).astype(BF16), v_ref[0, :, cols(hd)]).astype(o_ref.dtype)


def _mem_attn(mq, mk, mv, bm=512):
    b, s, _ = mq.shape
    m = mk.shape[1]
    return pl.pallas_call(
        _mem_attn_kernel,
        grid=(b, s // bm),
        in_specs=[pl.BlockSpec((1, bm, MEM_DIM), lambda bi, si: (bi, si, 0)),
                  pl.BlockSpec((1, m, MEM_DIM), lambda bi, si: (bi, 0, 0)),
                  pl.BlockSpec((1, m, MEM_DIM), lambda bi, si: (bi, 0, 0))],
        out_specs=pl.BlockSpec((1, bm, MEM_DIM), lambda bi, si: (bi, si, 0)),
        out_shape=jax.ShapeDtypeStruct((b, s, MEM_DIM), BF16),
        compiler_params=_cparams(("parallel", "parallel")),
        name="mem_attn",
    )(mq, mk, mv)


def _mix_kernel(x_ref, yr_ref, o0_ref, o1_ref, o2_ref, l0_ref, l1_ref, l2_ref, ym_ref, gt_ref,
                pr_ref, pd_ref, pm_ref, wo_ref, ln2_ref, wr_ref, br_ref,
                x1_ref, h2_ref, route_ref, cnt_ref):
    d = x_ref.shape[1]
    wide = lambda ref: jnp.concatenate([ref[0, hf] for hf in range(DIL_GROUP_DIM // LANES)], axis=1)
    l0, l1, l2 = wide(l0_ref), wide(l1_ref), wide(l2_ref)
    m = jnp.maximum(jnp.maximum(l0, l1), l2)
    e0, e1, e2 = jnp.exp(l0 - m), jnp.exp(l1 - m), jnp.exp(l2 - m)
    y_dil = (e0 * wide(o0_ref) + e1 * wide(o1_ref) + e2 * wide(o2_ref)) / (e0 + e1 + e2)
    mixed = (gt_ref[:, 0:d].astype(F32) * _dot(yr_ref[...], pr_ref[...])
             + gt_ref[:, d:2 * d].astype(F32) * _dot(y_dil.astype(BF16), pd_ref[...])
             + gt_ref[:, 2 * d:3 * d].astype(F32) * _dot(ym_ref[...], pm_ref[...]))
    x1 = x_ref[...] + _dot(mixed.astype(BF16), wo_ref[...])
    x1_ref[...] = x1
    h2 = x1 * lax.rsqrt(jnp.mean(x1 * x1, axis=-1, keepdims=True) + NORM_EPS) * ln2_ref[...]
    h2_ref[...] = _pack_bf16_pairs(h2)
    logits = _dot_x3(h2, wr_ref[...]) + br_ref[...]
    lane = lax.broadcasted_iota(jnp.int32, logits.shape, 1)
    lane_f = lane.astype(F32)
    route = jnp.zeros(logits.shape, F32)
    onehot = jnp.zeros(logits.shape, F32)
    vals = []
    for kq in range(TOP_K):
        mx = jnp.max(logits, axis=-1, keepdims=True)
        idx = jnp.min(jnp.where(logits == mx, lane_f, float(LANES)), axis=-1, keepdims=True)
        hit = lane_f == idx
        vals.append(mx)
        route = jnp.where(lane == kq, idx, route)
        onehot = jnp.where(hit, 1.0, onehot)
        logits = jnp.where(hit, -jnp.inf, logits)
    ex = [jnp.exp(vq - vals[0]) for vq in vals]
    den = ex[0] + ex[1] + ex[2] + ex[3]
    for kq in range(TOP_K):
        route = jnp.where(lane == TOP_K + kq, ex[kq] / den, route)
    route_ref[...] = route

    @pl.when(pl.program_id(0) == 0)
    def _():
        cnt_ref[...] = jnp.zeros_like(cnt_ref)

    cnt_ref[...] += jnp.sum(onehot, axis=0, keepdims=True)


def _mix(x2, y_rwkv, outs, lses, y_mem, gates, p_rwkv, p_dil, p_mem, w_out, ln2, w_router, b_router, bm=512):
    t, d = x2.shape
    wr = jnp.zeros((d, LANES), F32).at[:, :N_EXPERTS].set(w_router)
    br = jnp.full((1, LANES), -jnp.inf, F32).at[0, :N_EXPERTS].set(b_router)
    row = lambda w: pl.BlockSpec((bm, w), lambda i: (i, 0))
    const = lambda a: pl.BlockSpec(a.shape, lambda i: (0,) * a.ndim)
    tiles_per_seq = outs[0].shape[2] // bm
    dil = pl.BlockSpec((1, DIL_GROUP_DIM // LANES, bm, LANES),
                       lambda i: (i // tiles_per_seq, 0, i % tiles_per_seq, 0))
    weights = [p_rwkv.astype(BF16), p_dil.astype(BF16), p_mem.astype(BF16), w_out.astype(BF16),
               ln2.reshape(1, d), wr, br]
    return pl.pallas_call(
        _mix_kernel,
        grid=(t // bm,),
        in_specs=[row(d), row(RWKV_DIM)] + [dil] * 6 + [row(MEM_DIM), row(3 * d)]
                 + [const(a) for a in weights],
        out_specs=[row(d), row(d // 2), row(LANES), pl.BlockSpec((1, LANES), lambda i: (0, 0))],
        out_shape=[jax.ShapeDtypeStruct((t, d), F32), jax.ShapeDtypeStruct((t, d // 2), jnp.uint32),
                   jax.ShapeDtypeStruct((t, LANES), F32), jax.ShapeDtypeStruct((1, LANES), F32)],
        compiler_params=_cparams(("arbitrary",)),
        name="mix",
    )(x2, y_rwkv, *outs, *lses, y_mem, gates, *weights)


def _route_kernel(route_ref, pstart_ref, tri_ref, dest_ref, carry_ref):
    @pl.when(pl.program_id(0) == 0)
    def _():
        carry_ref[...] = jnp.zeros_like(carry_ref)

    route = route_ref[...]
    lane = lax.broadcasted_iota(jnp.int32, route.shape, 1)
    lane_f = lane.astype(F32)
    hits = [lane_f == route[:, kq:kq + 1] for kq in range(TOP_K)]
    onehot = jnp.zeros(route.shape, F32)
    for hq in hits:
        onehot = jnp.where(hq, 1.0, onehot)
    rank = _dot(tri_ref[...], onehot.astype(BF16)) + carry_ref[...]
    slot = pstart_ref[...] + rank
    dest = jnp.zeros(route.shape, jnp.int32)
    for kq in range(TOP_K):
        dk = jnp.sum(jnp.where(hits[kq], slot, 0.0), axis=-1, keepdims=True)
        dest = jnp.where(lane == kq, dk.astype(jnp.int32), dest)
    dest_ref[...] = dest
    carry_ref[...] += jnp.sum(onehot, axis=0, keepdims=True)


def _route(route, pstart, bm=1024):
    t = route.shape[0]
    i = jnp.arange(bm)
    tri = (i[None, :] < i[:, None]).astype(BF16)
    return pl.pallas_call(
        _route_kernel,
        grid=(t // bm,),
        in_specs=[pl.BlockSpec((bm, LANES), lambda i: (i, 0)),
                  pl.BlockSpec((1, LANES), lambda i: (0, 0)),
                  pl.BlockSpec((bm, bm), lambda i: (0, 0))],
        out_specs=pl.BlockSpec((bm, LANES), lambda i: (i, 0)),
        out_shape=jax.ShapeDtypeStruct((t, LANES), jnp.int32),
        scratch_shapes=[pltpu.VMEM((1, LANES), F32)],
        compiler_params=_cparams(("arbitrary",)),
        name="route",
    )(route, pstart, tri)


def _row_copy(src_ref, src_row, dst_ref, dst_row, sem):
    return pltpu.make_async_copy(src_ref.at[pl.ds(src_row, 1)], dst_ref.at[pl.ds(dst_row, 1)], sem)


def _dispatch_kernel(last_ref, dest_ref, h_ref, xs_ref, zero_ref, hbuf_ref, sem_s, sem_l, zsem, *, bm):
    i = pl.program_id(0)
    n = pl.num_programs(0)

    def load(tile, b):
        rows = pl.ds(pl.multiple_of(tile * bm, bm), bm)
        return pltpu.make_async_copy(h_ref.at[rows], hbuf_ref.at[b], sem_l.at[b])

    def wait_scatter(b):
        for kq in range(TOP_K):
            pltpu.make_async_copy(hbuf_ref.at[0], xs_ref.at[pl.ds(0, bm)], sem_s.at[b]).wait()

    @pl.when(i == 0)
    def _():
        load(0, 0).start()
        zero_ref[...] = jnp.zeros_like(zero_ref)

        def zero_copy(e):
            row = pl.multiple_of(last_ref[e] * EXPERT_ROWS, EXPERT_ROWS)
            return pltpu.make_async_copy(zero_ref, xs_ref.at[pl.ds(row, EXPERT_ROWS)], zsem)

        def z_issue(e, carry):
            @pl.when(last_ref[e] >= 0)
            def _():
                zero_copy(e).start()
            return carry

        def z_drain(e, carry):
            @pl.when(last_ref[e] >= 0)
            def _():
                zero_copy(e).wait()
            return carry

        lax.fori_loop(0, 2 * N_EXPERTS, z_issue, 0)
        lax.fori_loop(0, 2 * N_EXPERTS, z_drain, 0)

    @pl.when(i + 1 < n)
    def _():
        load(i + 1, (i + 1) % 3).start()

    load(i, i % 3).wait()

    for phase in range(6):
        @pl.when(i % 6 == phase)
        def _(phase=phase):
            def issue(j, carry):
                for kq in range(TOP_K):
                    _row_copy(hbuf_ref.at[phase % 3], j, xs_ref, dest_ref[j * TOP_K + kq],
                              sem_s.at[phase % 2]).start(priority=kq % 2)
                return carry

            lax.fori_loop(0, bm, issue, 0)

    @pl.when(i > 0)
    def _():
        wait_scatter((i - 1) % 2)

    @pl.when(i == n - 1)
    def _():
        wait_scatter(i % 2)


def _dispatch(last_blk, dest_flat, h2, n_slots, bm=512):
    t, d = h2.shape
    grid_spec = pltpu.PrefetchScalarGridSpec(
        num_scalar_prefetch=1,
        grid=(t // bm,),
        in_specs=[pl.BlockSpec((bm * TOP_K,), lambda i, lb: (i,), memory_space=pltpu.SMEM),
                  pl.BlockSpec(memory_space=pl.ANY)],
        out_specs=pl.BlockSpec(memory_space=pl.ANY),
        scratch_shapes=[pltpu.VMEM((EXPERT_ROWS, d), h2.dtype), pltpu.VMEM((3, bm, d), h2.dtype),
                        pltpu.SemaphoreType.DMA((2,)), pltpu.SemaphoreType.DMA((3,)), pltpu.SemaphoreType.DMA],
    )
    return pl.pallas_call(
        functools.partial(_dispatch_kernel, bm=bm),
        grid_spec=grid_spec,
        out_shape=jax.ShapeDtypeStruct((n_slots, d), h2.dtype),
        compiler_params=_cparams(("arbitrary",)),
        name="dispatch",
    )(last_blk, dest_flat, h2)


FF_CHUNK = 256
EXPERT_ROWS = 256
BLOCKS_PER_STEP = 2


def _expert_kernel(be_ref, par_ref, nxt_ref, nu_ref, xs_ref, w1_ref, b1_ref, w2_ref, b2_ref, sel_ref, ys_ref,
                   w1f_ref, w2f_ref, w1p_ref, w2b_ref, act_ref, sem):
    i = pl.program_id(0)
    d_ff2 = w1_ref.shape[2]
    half = FF_CHUNK // 2

    def weight_copies(expert, buf):
        return (pltpu.make_async_copy(w1_ref.at[expert], w1f_ref.at[buf], sem.at[0, buf]),
                pltpu.make_async_copy(w2_ref.at[expert], w2f_ref.at[buf], sem.at[1, buf]))

    @pl.when(i == 0)
    def _():
        for cp in weight_copies(be_ref[0], par_ref[0]):
            cp.start()

    def convert(blk, e, slot):
        for cp in weight_copies(e, slot):
            cp.wait()
        nxt = nxt_ref[blk]

        @pl.when(nxt >= 0)
        def _():
            for cp in weight_copies(nxt, 1 - slot):
                cp.start(priority=1)

        for c in range(d_ff2 // FF_CHUNK):
            sl = slice(c * FF_CHUNK, (c + 1) * FF_CHUNK)
            w1p_ref[:, sl] = _dot(w1f_ref[slot, :, sl].astype(BF16), sel_ref[...]).astype(BF16)
        w2b_ref[...] = w2f_ref[slot].astype(BF16)

    def compute(rows, e):
        n_rows = rows.stop - rows.start
        x = _unpack_bf16_pairs(xs_ref[rows, :])
        for c in range(d_ff2 // (2 * FF_CHUNK)):
            sl = slice(2 * c * FF_CHUNK, 2 * (c + 1) * FF_CHUNK)
            hb = _dot(x, w1p_ref[:, sl]) + b1_ref[e, :, sl]
            for j in range(2):
                x_glu = jnp.minimum(hb[:, j * FF_CHUNK:j * FF_CHUNK + half], SWIGLU_LIMIT)
                x_lin = jnp.clip(hb[:, j * FF_CHUNK + half:(j + 1) * FF_CHUNK], -SWIGLU_LIMIT, SWIGLU_LIMIT)
                act = x_glu * _sigmoid(SWIGLU_ALPHA * x_glu) * (x_lin + 1.0)
                act_ref[0:n_rows, (2 * c + j) * half:(2 * c + j + 1) * half] = act.astype(BF16)
        ys_ref[rows, :] = _dot(act_ref[0:n_rows, :], w2b_ref[...]) + b2_ref[e]

    blk0 = i * BLOCKS_PER_STEP
    blk1 = blk0 + 1
    e0, e1 = be_ref[blk0], be_ref[blk1]
    used0, used1 = blk0 < nu_ref[0], blk1 < nu_ref[0]
    fresh0 = jnp.logical_or(blk0 == 0, be_ref[jnp.maximum(blk0 - 1, 0)] != e0)
    both = jnp.logical_and(used1, e1 == e0)
    lo_rows, hi_rows = slice(0, EXPERT_ROWS), slice(EXPERT_ROWS, 2 * EXPERT_ROWS)

    @pl.when(jnp.logical_and(fresh0, used0))
    def _():
        convert(blk0, e0, par_ref[blk0])

    @pl.when(jnp.logical_and(used0, both))
    def _():
        compute(slice(0, 2 * EXPERT_ROWS), e0)

    @pl.when(jnp.logical_and(used0, jnp.logical_not(both)))
    def _():
        compute(lo_rows, e0)

    @pl.when(jnp.logical_and(used1, jnp.logical_not(both)))
    def _():
        convert(blk1, e1, par_ref[blk1])
        compute(hi_rows, e1)

    @pl.when(jnp.logical_not(used0))
    def _():
        ys_ref[lo_rows, :] = jnp.zeros((EXPERT_ROWS, ys_ref.shape[1]), F32)

    @pl.when(jnp.logical_not(used1))
    def _():
        ys_ref[hi_rows, :] = jnp.zeros((EXPERT_ROWS, ys_ref.shape[1]), F32)


def _chunk_deinterleave(a):
    lead = a.shape[:-1]
    a = a.reshape(lead + (a.shape[-1] // FF_CHUNK, FF_CHUNK // 2, 2))
    return jnp.swapaxes(a, -1, -2).reshape(lead + (-1,))


def _experts(block_e, parity, next_e, n_used, xs, w1, b1, w2, b2):
    n_slots, dx = xs.shape
    n_e, d, d_ff2 = w1.shape
    n_blocks = n_slots // EXPERT_ROWS
    i = jnp.arange(FF_CHUNK)
    src = jnp.where(i < FF_CHUNK // 2, 2 * i, 2 * (i - FF_CHUNK // 2) + 1)
    sel = (jnp.arange(FF_CHUNK)[:, None] == src[None, :]).astype(BF16)
    b1p = _chunk_deinterleave(b1).reshape(n_e, 1, d_ff2)
    step_rows = BLOCKS_PER_STEP * EXPERT_ROWS
    imap = lambda i, be, par, nxt, nu: (i, 0)
    xmap = lambda i, be, par, nxt, nu: (jnp.minimum(i, (nu[0] - 1) // BLOCKS_PER_STEP), 0)
    cmap = lambda i, be, par, nxt, nu: (0, 0, 0)
    grid_spec = pltpu.PrefetchScalarGridSpec(
        num_scalar_prefetch=4,
        grid=(n_blocks // BLOCKS_PER_STEP,),
        in_specs=[pl.BlockSpec((step_rows, dx), xmap),
                  pl.BlockSpec(memory_space=pl.ANY),
                  pl.BlockSpec((n_e, 1, d_ff2), cmap),
                  pl.BlockSpec(memory_space=pl.ANY),
                  pl.BlockSpec((n_e, 1, d), cmap),
                  pl.BlockSpec((FF_CHUNK, FF_CHUNK), lambda i, be, par, nxt, nu: (0, 0))],
        out_specs=pl.BlockSpec((step_rows, d), imap),
        scratch_shapes=[pltpu.VMEM((2, d, d_ff2), F32), pltpu.VMEM((2, d_ff2 // 2, d), F32),
                        pltpu.VMEM((d, d_ff2), BF16), pltpu.VMEM((d_ff2 // 2, d), BF16),
                        pltpu.VMEM((BLOCKS_PER_STEP * EXPERT_ROWS, d_ff2 // 2), BF16),
                        pltpu.SemaphoreType.DMA((2, 2))],
    )
    return pl.pallas_call(
        _expert_kernel,
        grid_spec=grid_spec,
        out_shape=jax.ShapeDtypeStruct((n_slots, d), F32),
        compiler_params=_cparams(("arbitrary",), vmem=EXPERT_VMEM_LIMIT),
        name="experts",
    )(block_e, parity, next_e, n_used, xs, w1, b1p, w2, b2.reshape(n_e, 1, d), sel)


def _combine_kernel(dest_ref, dnext_ref, x1_ref, route_ref, ys_ref, o_ref, buf_ref, sem, *, bm):
    i = pl.program_id(0)
    n = pl.num_programs(0)
    slot = i % 2

    def gather(d_ref, buf):
        def issue(g, carry):
            base = pl.multiple_of(g * SUBLANES, SUBLANES)
            for r in range(SUBLANES):
                for kq in range(TOP_K):
                    _row_copy(ys_ref, d_ref[(base + r) * TOP_K + kq], buf_ref.at[buf, kq], base + r,
                              sem.at[buf]).start(priority=kq % 2)
            return carry

        lax.fori_loop(0, bm // SUBLANES, issue, 0)

    @pl.when(i == 0)
    def _():
        gather(dest_ref, 0)

    for nxt in range(2):
        @pl.when(jnp.logical_and(i + 1 < n, (i + 1) % 2 == nxt))
        def _(nxt=nxt):
            gather(dnext_ref, nxt)

    for kq in range(TOP_K):
        pltpu.make_async_copy(ys_ref.at[pl.ds(0, bm)], buf_ref.at[slot, kq], sem.at[slot]).wait()
    acc = x1_ref[...]
    for kq in range(TOP_K):
        acc = acc + route_ref[:, TOP_K + kq:TOP_K + kq + 1] * buf_ref[slot, kq]
    o_ref[...] = acc


def _combine(dest_flat, x1, route, ys, bm=512):
    t, d = x1.shape
    n = t // bm
    return pl.pallas_call(
        functools.partial(_combine_kernel, bm=bm),
        grid=(n,),
        in_specs=[pl.BlockSpec((bm * TOP_K,), lambda i: (i,), memory_space=pltpu.SMEM),
                  pl.BlockSpec((bm * TOP_K,), lambda i: (jnp.minimum(i + 1, n - 1),), memory_space=pltpu.SMEM),
                  pl.BlockSpec((bm, d), lambda i: (i, 0)),
                  pl.BlockSpec((bm, LANES), lambda i: (i, 0)),
                  pl.BlockSpec(memory_space=pl.ANY)],
        out_specs=pl.BlockSpec((bm, d), lambda i: (i, 0)),
        out_shape=jax.ShapeDtypeStruct((t, d), F32),
        scratch_shapes=[pltpu.VMEM((2, TOP_K, bm, d), F32), pltpu.SemaphoreType.DMA((2,))],
        compiler_params=_cparams(("arbitrary",)),
        name="combine",
    )(dest_flat, dest_flat, x1, route, ys)


def _layer(x, mem, ln1, w_in, token_mu, rwkv_w0, rwkv_w2, rwkv_a0, rwkv_a2, rwkv_g2,
           rwkv_k_k, rwkv_k_a, rwkv_r_k, rwkv_ln_w, rwkv_ln_b, dil_q_norm, dil_k_norm,
           ln_mem, w_mem_kv, mem_q_norm, mem_k_norm, p_rwkv, p_dil, p_mem, w_out,
           ln2, w_router, b_router, w1, b1, w2, b2):
    b, s, d = x.shape
    t = b * s
    x2 = x.reshape(t, d)
    zr, dil_qkv, mq, gates = _in_proj(x2, ln1, w_in, dil_q_norm, dil_k_norm, mem_q_norm)
    y_rwkv = _rwkv(zr.reshape(b, s, RWKV_IN), token_mu, rwkv_w0, rwkv_w2, rwkv_a0, rwkv_a2, rwkv_g2,
                   rwkv_k_k, rwkv_k_a, rwkv_r_k.reshape(-1), rwkv_ln_w, rwkv_ln_b)
    outs, lses = [], []
    for g, (_, dilation) in enumerate(DIL_PATTERNS):
        o, l = _dil_group(*dil_qkv[3 * g:3 * g + 3], b, dilation)
        outs.append(o)
        lses.append(l)
    mk, mv = _mem_kv(mem, ln_mem, w_mem_kv, mem_k_norm)
    y_mem = _mem_attn(mq.reshape(b, s, MEM_DIM), mk, mv)
    x1, h2, route, counts = _mix(x2, y_rwkv.reshape(t, RWKV_DIM), outs, lses, y_mem.reshape(t, MEM_DIM), gates,
                                 p_rwkv, p_dil, p_mem, w_out, ln2, w_router, b_router)

    counts = counts[0, :N_EXPERTS].astype(jnp.int32)
    nblk = (counts + EXPERT_ROWS - 1) // EXPERT_ROWS
    bend = jnp.cumsum(nblk)
    pstart = ((bend - nblk) * EXPERT_ROWS).astype(F32)
    pstart = jnp.zeros((1, LANES), F32).at[0, :N_EXPERTS].set(pstart)
    n_blocks = (t * TOP_K) // EXPERT_ROWS + N_EXPERTS
    block_e = jnp.sum(bend[None, :] <= jnp.arange(n_blocks, dtype=jnp.int32)[:, None], axis=1)
    block_e = jnp.minimum(block_e, N_EXPERTS - 1).astype(jnp.int32)
    n_used = bend[-1:].astype(jnp.int32)
    used = nblk > 0
    eids = jnp.arange(N_EXPERTS, dtype=jnp.int32)
    ordinal = jnp.cumsum(used.astype(jnp.int32)) - 1
    later = jnp.where(used[None, :] & (eids[None, :] > eids[:, None]), eids[None, :], N_EXPERTS)
    next_used = jnp.min(later, axis=1)
    next_used = jnp.where(next_used < N_EXPERTS, next_used, -1).astype(jnp.int32)
    is_e = (block_e[:, None] == eids[None, :]).astype(jnp.int32)
    parity = (jnp.sum(is_e * ordinal[None, :], axis=1) % 2).astype(jnp.int32)
    next_e = jnp.sum(is_e * next_used[None, :], axis=1).astype(jnp.int32)
    tail = n_used[0] + eids
    last_blk = jnp.concatenate([jnp.where(used, bend - 1, -1), jnp.where(tail < n_blocks, tail, -1)]).astype(jnp.int32)

    dest = _route(route, pstart)[:, :TOP_K].reshape(-1)
    xs = _dispatch(last_blk, dest, h2, n_blocks * EXPERT_ROWS)
    ys = _experts(block_e, parity, next_e, n_used, xs, w1, b1, w2, b2)
    out = _combine(dest, x1, route, ys)
    return out.reshape(b, s, d)


def kernel(x, mem, ln1, w_in, token_mu, rwkv_w0, rwkv_w2, rwkv_a0, rwkv_a2, rwkv_g2, rwkv_k_k, rwkv_k_a, rwkv_r_k, rwkv_ln_w, rwkv_ln_b, dil_q_norm, dil_k_norm, ln_mem, w_mem_kv, mem_q_norm, mem_k_norm, p_rwkv, p_dil, p_mem, w_out, ln2, w_router, b_router, w1, b1, w2, b2):
    params = (ln1, w_in, token_mu, rwkv_w0, rwkv_w2, rwkv_a0, rwkv_a2, rwkv_g2, rwkv_k_k, rwkv_k_a,
              rwkv_r_k, rwkv_ln_w, rwkv_ln_b, dil_q_norm, dil_k_norm, ln_mem, w_mem_kv, mem_q_norm,
              mem_k_norm, p_rwkv, p_dil, p_mem, w_out, ln2, w_router, b_router, w1, b1, w2, b2)
    for l in range(ln1.shape[0]):
        x = _layer(x, mem, *[p[l] for p in params])
    return x
```

```python
import functools

import jax
import jax.numpy as jnp
from jax import lax
from jax.experimental import pallas as pl
from jax.experimental.pallas import tpu as pltpu

F32 = jnp.float32
BF16 = jnp.bfloat16

NORM_EPS = 1e-5
HEAD_DIM = 64
RWKV_DIM = 512
RWKV_IN = 1792
GN_EPS = HEAD_DIM * 1e-5
DIL_DIM = 768
DIL_GROUP_DIM = 256
DIL_PATTERNS = ((128, 1), (512, 4), (2048, 16))
ATTN_BLOCK = 128
MEM_DIM = 512
MEM_HEAD_DIM = 128
N_EXPERTS = 32
TOP_K = 4
SWIGLU_ALPHA = 1.702
SWIGLU_LIMIT = 7.0
LANES = 128
SUBLANES = 8
CHUNK = 64
NEG_BIG = -1e30
VMEM_LIMIT = 48 * 1024 * 1024
EXPERT_VMEM_LIMIT = 56 * 1024 * 1024


def _dot(a, b):
    return jnp.dot(a, b, preferred_element_type=F32)


def _dot_nt(a, b):
    return lax.dot_general(a, b, (((1,), (1,)), ((), ())), preferred_element_type=F32)


def _split2(x):
    hi = x.astype(BF16)
    lo = (x - hi.astype(F32)).astype(BF16)
    return hi, lo


def _split3(x):
    hi = x.astype(BF16)
    r = x - hi.astype(F32)
    lo = r.astype(BF16)
    lo2 = (r - lo.astype(F32)).astype(BF16)
    return hi, lo, lo2


def _dot_rhs3(w_bf16, x):
    hi, lo, lo2 = _split3(x)
    return _dot(w_bf16, hi) + _dot(w_bf16, lo) + _dot(w_bf16, lo2)


def _dot_x3(a, b):
    ah, al = _split2(a)
    bh, bl = _split2(b)
    return _dot(ah, bh) + _dot(al, bh) + _dot(ah, bl)


def _sigmoid(x):
    return 1.0 / (1.0 + jnp.exp(-x))


def _pack_bf16_pairs(x):
    half = x.shape[1] // 2
    bits = lambda v: pltpu.bitcast(v.astype(BF16).astype(F32), jnp.uint32)
    return (bits(x[:, :half]) >> 16) | (bits(x[:, half:]) & jnp.uint32(0xFFFF0000))


def _unpack_bf16_pairs(w):
    lo = pltpu.bitcast(w << 16, F32)
    hi = pltpu.bitcast(w & jnp.uint32(0xFFFF0000), F32)
    return jnp.concatenate([lo, hi], axis=1).astype(BF16)


def _cparams(sem, vmem=VMEM_LIMIT):
    return pltpu.CompilerParams(dimension_semantics=sem, vmem_limit_bytes=vmem)


IN_CHUNK = 256
PERM_ROWS = 256
N_ZR = RWKV_IN // IN_CHUNK
N_DIL = 3 * DIL_DIM // IN_CHUNK
N_MQ = MEM_DIM // IN_CHUNK
N_GATE = 3 * 1024 // IN_CHUNK


def _in_proj_kernel(x_ref, ln_ref, w_ref, qg_ref, kg_ref, mg_ref, p4_ref, p16_ref,
                    zr_ref, *rest, bm):
    dil_refs, (mq_ref, gt_ref) = rest[:N_DIL], rest[N_DIL:]
    perm_refs = (None, p4_ref, p16_ref)
    x = x_ref[...]
    h = x * lax.rsqrt(jnp.mean(x * x, axis=-1, keepdims=True) + NORM_EPS) * ln_ref[...]
    hb = h.astype(BF16)

    lane = lax.broadcasted_iota(jnp.int32, (x.shape[0], LANES), 1)

    def seg_rms(z, seg, gain):
        z2 = z * z
        cols = []
        for cb in range(IN_CHUNK // LANES):
            blk = z2[:, cb * LANES:(cb + 1) * LANES]
            ss = jnp.zeros_like(blk)
            for sg in range(LANES // seg):
                m = (lane >= sg * seg) & (lane < (sg + 1) * seg)
                ss = jnp.where(m, jnp.sum(jnp.where(m, blk, 0.0), axis=-1, keepdims=True), ss)
            cols.append(ss)
        ss = jnp.concatenate(cols, axis=1)
        return z * lax.rsqrt(ss * (1.0 / seg) + NORM_EPS) * gain

    for c in range(N_ZR + N_DIL + N_MQ + N_GATE):
        z = _dot(hb, w_ref[:, c * IN_CHUNK:(c + 1) * IN_CHUNK])
        if c < N_ZR:
            zr_ref[:, c * IN_CHUNK:(c + 1) * IN_CHUNK] = z
            continue
        d = c - N_ZR
        if d < N_DIL:
            which, g = divmod(d, 3)
            if which == 0:
                z = seg_rms(z, HEAD_DIM, qg_ref[...])
            elif which == 1:
                z = seg_rms(z, HEAD_DIM, kg_ref[...])
            zb = z.astype(BF16)
            o_ref = dil_refs[g * 3 + which]
            dilation = DIL_PATTERNS[g][1]
            if dilation == 1:
                o_ref[...] = zb
            else:
                rows = PERM_ROWS // dilation
                for sub in range(bm // PERM_ROWS):
                    zp = _dot(perm_refs[g][...], zb[sub * PERM_ROWS:(sub + 1) * PERM_ROWS, :]).astype(BF16)
                    for r in range(dilation):
                        o_ref[sub * rows:(sub + 1) * rows, r * IN_CHUNK:(r + 1) * IN_CHUNK] = zp[r * rows:(r + 1) * rows, :]
            continue
        d -= N_DIL
        if d < N_MQ:
            mq_ref[:, d * IN_CHUNK:(d + 1) * IN_CHUNK] = seg_rms(z, MEM_HEAD_DIM, mg_ref[...]).astype(BF16)
            continue
        d -= N_MQ
        gt_ref[:, d * IN_CHUNK:(d + 1) * IN_CHUNK] = _sigmoid(z).astype(BF16)


def _class_perm(bm, dilation):
    i = jnp.arange(bm)
    src = (i % (bm // dilation)) * dilation + i // (bm // dilation)
    return (src[:, None] == i[None, :]).astype(BF16)


def _in_proj(x2, ln1, w_in, dil_q_norm, dil_k_norm, mem_q_norm, bm=512):
    t, d = x2.shape
    n_cols = w_in.shape[1]
    wb = w_in.astype(BF16)
    qg = (jnp.tile(dil_q_norm, IN_CHUNK // HEAD_DIM) * (HEAD_DIM ** -0.5)).reshape(1, IN_CHUNK)
    kg = jnp.tile(dil_k_norm, IN_CHUNK // HEAD_DIM).reshape(1, IN_CHUNK)
    mg = jnp.tile(mem_q_norm, IN_CHUNK // MEM_HEAD_DIM).reshape(1, IN_CHUNK)
    p4 = _class_perm(PERM_ROWS, DIL_PATTERNS[1][1])
    p16 = _class_perm(PERM_ROWS, DIL_PATTERNS[2][1])
    row = lambda w: pl.BlockSpec((bm, w), lambda i: (i, 0))
    const = lambda a: pl.BlockSpec(a.shape, lambda i: (0,) * a.ndim)
    dil_specs, dil_shapes = [], []
    for _, dilation in DIL_PATTERNS:
        for _ in range(3):
            dil_specs.append(pl.BlockSpec((bm // dilation, dilation * DIL_GROUP_DIM), lambda i: (i, 0)))
            dil_shapes.append(jax.ShapeDtypeStruct((t // dilation, dilation * DIL_GROUP_DIM), BF16))
    outs = pl.pallas_call(
        functools.partial(_in_proj_kernel, bm=bm),
        grid=(t // bm,),
        in_specs=[row(d), pl.BlockSpec((1, d), lambda i: (0, 0)),
                  pl.BlockSpec((d, n_cols), lambda i: (0, 0), pipeline_mode=pl.Buffered(1)),
                  const(qg), const(kg), const(mg), const(p4), const(p16)],
        out_specs=[row(RWKV_IN)] + dil_specs + [row(MEM_DIM), row(3 * d)],
        out_shape=[jax.ShapeDtypeStruct((t, RWKV_IN), F32)] + dil_shapes
                  + [jax.ShapeDtypeStruct((t, MEM_DIM), BF16), jax.ShapeDtypeStruct((t, 3 * d), BF16)],
        compiler_params=_cparams(("parallel",)),
        name="in_proj",
    )(x2, ln1.reshape(1, d), wb, qg, kg, mg, p4, p16)
    return outs[0], outs[1:1 + N_DIL], outs[1 + N_DIL], outs[2 + N_DIL]


def _rwkv_kernel(z_ref, mu_ref, w0_ref, a0_ref, wl_ref, kk_ref, ka_ref, rk_ref,
                 lnw_ref, lnb_ref, tri_ref, y_ref, carry_ref, state_ref, ybuf_ref, *, tt):
    s_idx = pl.program_id(1)

    @pl.when(s_idx == 0)
    def _():
        carry_ref[...] = jnp.zeros_like(carry_ref)
        state_ref[...] = jnp.zeros_like(state_ref)

    z = z_ref[0]
    rows = lax.broadcasted_iota(jnp.int32, z.shape, 0)
    prev = jnp.where(rows == 0, carry_ref[...], pltpu.roll(z, 1, axis=0))
    carry_ref[...] = z[tt - 1:tt, :]
    zl = z + (prev - z) * mu_ref[...]
    r = zl[:, 0:RWKV_DIM]
    k = zl[:, RWKV_DIM:2 * RWKV_DIM]
    v = zl[:, 2 * RWKV_DIM:3 * RWKV_DIM]
    zlo = zl[:, 3 * RWKV_DIM:RWKV_IN]
    lcol = lax.broadcasted_iota(jnp.int32, zlo.shape, 1)
    feat = jnp.where(lcol < 64, jnp.tanh(zlo), jnp.where(lcol < 128, zlo, _sigmoid(zlo)))
    lora = _dot(feat.astype(BF16), wl_ref[...])
    nu = -(w0_ref[...] + lora[:, 0:RWKV_DIM])
    softplus = jnp.maximum(nu, 0.0) + jnp.log(1.0 + jnp.exp(-jnp.abs(nu)))
    lw = -jnp.exp(-softplus - 0.5)
    a = _sigmoid(a0_ref[...] + lora[:, RWKV_DIM:2 * RWKV_DIM])
    g = lora[:, 2 * RWKV_DIM:3 * RWKV_DIM]

    n_pairs = RWKV_DIM // LANES
    n_chunks = tt // CHUNK
    head_lo = lax.broadcasted_iota(jnp.int32, (tt, LANES), 1) < HEAD_DIM

    def seg_sum(x):
        parts = []
        for p in range(n_pairs):
            xp = x[:, p * LANES:(p + 1) * LANES]
            lo = jnp.sum(jnp.where(head_lo, xp, 0.0), axis=-1, keepdims=True)
            hi = jnp.sum(jnp.where(head_lo, 0.0, xp), axis=-1, keepdims=True)
            parts.append(jnp.where(head_lo, lo, hi))
        return jnp.concatenate(parts, axis=1)

    kk = k * kk_ref[...]
    kk = kk / jnp.maximum(jnp.sqrt(seg_sum(kk * kk)), 1e-12)
    k2 = k * (1.0 + (a - 1.0) * ka_ref[...])
    a_s = -kk
    b_s = kk * a

    cum = _dot_rhs3(tri_ref[...], lw)
    tot = jnp.concatenate(
        [jnp.broadcast_to(cum[(c + 1) * CHUNK - 1:(c + 1) * CHUNK, :], (CHUNK, RWKV_DIM)) for c in range(n_chunks)],
        axis=0)
    e_neg = jnp.exp(-cum)
    e_end = jnp.exp(tot - cum)
    at = (a_s * jnp.exp(cum - lw)).astype(BF16)
    rt = (r * jnp.exp(cum)).astype(BF16)
    bt = (b_s * e_neg).astype(BF16)
    kt = (k2 * e_neg).astype(BF16)
    be = (b_s * e_end).astype(BF16)
    ke = (k2 * e_end).astype(BF16)
    vb = v.astype(BF16)
    e_tot = jnp.exp(tot)

    r128 = lax.broadcasted_iota(jnp.int32, (LANES, LANES), 0)
    c128 = lax.broadcasted_iota(jnp.int32, (LANES, LANES), 1)
    lane_lo = lax.broadcasted_iota(jnp.int32, (CHUNK, LANES), 1) < HEAD_DIM
    stril = c128 < r128
    tril = c128 <= r128
    eye = (c128 == r128).astype(F32)
    same16 = (r128 // 16) == (c128 // 16)
    same32 = (r128 // 32) == (c128 // 32)
    off16 = same32 & jnp.logical_not(same16)
    off32 = jnp.logical_not(same32)

    def bdiag(xp):
        zero = jnp.zeros_like(xp)
        return jnp.concatenate([jnp.where(lane_lo, xp, zero), jnp.where(lane_lo, zero, xp)], axis=0)

    units = [(ci, p) for ci in range(n_chunks) for p in range(n_pairs)]
    pick = lambda arr, u: arr[u[0] * CHUNK:(u[0] + 1) * CHUNK, u[1] * LANES:(u[1] + 1) * LANES]
    cat0 = lambda xs: jnp.concatenate(xs, axis=0)
    cat1 = lambda xs: jnp.concatenate(xs, axis=1)
    zero_b = jnp.zeros((LANES, LANES), BF16)

    at_b = [bdiag(pick(at, u)) for u in units]
    v_b = [bdiag(pick(vb, u)) for u in units]
    be_b = [bdiag(pick(be, u)) for u in units]
    ke_b = [bdiag(pick(ke, u)) for u in units]
    rt_b = [bdiag(pick(rt, u)) for u in units]
    a_ab, a_ak, a_rbk = [], [], []
    for i, u in enumerate(units):
        mq = _dot_nt(cat0([at_b[i], rt_b[i]]), cat0([bdiag(pick(bt, u)), bdiag(pick(kt, u))]))
        a_ab.append(jnp.where(stril, mq[:LANES, :LANES], 0.0).astype(BF16))
        a_ak.append(jnp.where(stril, mq[:LANES, LANES:], 0.0).astype(BF16))
        a_rbk.append(cat1([jnp.where(tril, mq[LANES:, :LANES], 0.0), jnp.where(tril, mq[LANES:, LANES:], 0.0)]).astype(BF16))
    w_b = [_dot(a_ak[i], v_b[i]).astype(BF16) for i in range(len(units))]

    d1 = [jnp.where(same16, x, jnp.zeros_like(x)) for x in a_ab]
    xs = [eye + d.astype(F32) for d in d1]
    d2 = [_dot(d, d).astype(BF16) for d in d1]
    t_ = [_dot(d2[i], cat1([xs[i].astype(BF16), d2[i]])) for i in range(len(units))]
    xs = [xs[i] + t_[i][:, :LANES] for i in range(len(units))]
    d4 = [t[:, LANES:].astype(BF16) for t in t_]
    t_ = [_dot(d4[i], cat1([xs[i].astype(BF16), d4[i]])) for i in range(len(units))]
    xs = [xs[i] + t_[i][:, :LANES] for i in range(len(units))]
    d8 = [t[:, LANES:].astype(BF16) for t in t_]
    xs = [xs[i] + _dot(d8[i], xs[i].astype(BF16)) for i in range(len(units))]
    for off in (off16, off32):
        xb = [x.astype(BF16) for x in xs]
        g_ = [_dot(jnp.where(off, a_ab[i], jnp.zeros_like(a_ab[i])), xb[i]).astype(BF16) for i in range(len(units))]
        xs = [xs[i] + _dot(xb[i], g_[i]) for i in range(len(units))]

    pq = [_dot(xs[i].astype(BF16), cat1([at_b[i], w_b[i]])) for i in range(len(units))]
    ry = [_dot(a_rbk[i], cat0([pq[i].astype(BF16), cat1([zero_b, v_b[i]])])) for i in range(len(units))]
    r2 = [(rt_b[i].astype(F32) + ry[i][:, :LANES]).astype(BF16) for i in range(len(units))]
    m_c = [_dot(pq[i][:, :LANES].T.astype(BF16), be_b[i]) for i in range(len(units))]
    n_c = [_dot(cat1([pq[i][:, LANES:].T.astype(BF16), v_b[i].astype(F32).T.astype(BF16)]), cat0([be_b[i], ke_b[i]]))
           for i in range(len(units))]

    state = [state_ref[p] for p in range(n_pairs)]
    for i, (ci, p) in enumerate(units):
        s0 = state[p]
        s_b = s0.astype(BF16)
        y = _dot_nt(r2[i], s_b) + ry[i][:, LANES:]
        e_row = e_tot[ci * CHUNK:ci * CHUNK + 1, p * LANES:(p + 1) * LANES]
        state[p] = s0 * e_row + _dot(s_b, m_c[i].astype(BF16)) + n_c[i]
        ybuf_ref[ci * CHUNK:(ci + 1) * CHUNK, p * LANES:(p + 1) * LANES] = jnp.where(lane_lo, y[:CHUNK], y[CHUNK:])
    for p in range(n_pairs):
        state_ref[p] = state[p]

    y = ybuf_ref[...]
    mean = seg_sum(y) * (1.0 / HEAD_DIM)
    yc = y - mean
    var = seg_sum(yc * yc) * (1.0 / HEAD_DIM)
    yn = yc * lax.rsqrt(var + GN_EPS) * lnw_ref[...] + lnb_ref[...]
    bonus = seg_sum(r * k2 * rk_ref[...]) * v
    y_ref[0] = ((yn + bonus) * g).astype(y_ref.dtype)


def _rwkv(zr, token_mu, w0, w2, a0, a2, g2, k_k, k_a, r_k, ln_w, ln_b, tt=256):
    b, s, _ = zr.shape
    i = jnp.arange(tt)
    same = (i[:, None] // CHUNK) == (i[None, :] // CHUNK)
    tri = (same & (i[None, :] <= i[:, None])).astype(BF16)
    vec = lambda a: a.reshape(1, -1)
    wl = jnp.zeros((RWKV_IN - 3 * RWKV_DIM, 3 * RWKV_DIM), F32)
    wl = wl.at[0:64, 0:RWKV_DIM].set(w2).at[64:128, RWKV_DIM:2 * RWKV_DIM].set(a2)
    wl = wl.at[128:256, 2 * RWKV_DIM:3 * RWKV_DIM].set(g2).astype(BF16)
    params = [vec(token_mu), vec(w0), vec(a0), wl, vec(k_k), vec(k_a), vec(r_k),
              vec(ln_w), vec(ln_b), tri]
    const = lambda a: pl.BlockSpec(a.shape, lambda bi, si: (0,) * a.ndim)
    return pl.pallas_call(
        functools.partial(_rwkv_kernel, tt=tt),
        grid=(b, s // tt),
        in_specs=[pl.BlockSpec((1, tt, RWKV_IN), lambda bi, si: (bi, si, 0))] + [const(a) for a in params],
        out_specs=pl.BlockSpec((1, tt, RWKV_DIM), lambda bi, si: (bi, si, 0)),
        out_shape=jax.ShapeDtypeStruct((b, s, RWKV_DIM), BF16),
        scratch_shapes=[pltpu.VMEM((1, RWKV_IN), F32),
                        pltpu.VMEM((RWKV_DIM // LANES, LANES, LANES), F32),
                        pltpu.VMEM((tt, RWKV_DIM), F32)],
        compiler_params=_cparams(("parallel", "arbitrary")),
        name="rwkv",
    )(zr, *params)


DIL_UNITS = 8


def _dil_kernel(q_ref, k_ref, v_ref, o_ref, l_ref, *, dilation, cps, nbs):
    cg = pl.program_id(1)
    jb = pl.program_id(2)
    bq = ATTN_BLOCK
    n_heads = DIL_GROUP_DIM // HEAD_DIM
    qi = lax.broadcasted_iota(jnp.int32, (bq, 2 * bq), 0)
    kj = lax.broadcasted_iota(jnp.int32, (bq, 2 * bq), 1)
    lane = lax.broadcasted_iota(jnp.int32, (bq, DIL_GROUP_DIM), 1)
    head_masks = [(lane >= h * HEAD_DIM) & (lane < (h + 1) * HEAD_DIM) for h in range(n_heads)]
    units = [(c, bb) for c in range(cps) for bb in range(nbs)]

    qs, kcats, vcats, masks, starts = [], [], [], [], []
    for c, bb in units:
        i = jb * nbs + bb
        cols = slice(c * DIL_GROUP_DIM, (c + 1) * DIL_GROUP_DIM)
        p0 = pl.multiple_of(jnp.maximum(i - 1, 0) * bq, bq)
        c0 = pl.multiple_of(i * bq, bq)
        qs.append(q_ref[0, bb * bq:(bb + 1) * bq, cols])
        kcats.append(jnp.concatenate([k_ref[0, pl.ds(p0, bq), cols], k_ref[0, pl.ds(c0, bq), cols]], axis=0))
        vcats.append(jnp.concatenate([v_ref[0, pl.ds(p0, bq), cols], v_ref[0, pl.ds(c0, bq), cols]], axis=0))
        first = (1 - jnp.minimum(i, 1)) * (2 * bq)
        masks.append(((kj < bq) & (kj >= qi + first)) | ((kj >= bq) & ((kj - bq) <= qi)))
        starts.append((cg * cps + c) + dilation * bq * i)
    pairs = [(u, h) for u in range(len(units)) for h in range(n_heads)]
    sc = [jnp.where(masks[u], _dot_nt(jnp.where(head_masks[h], qs[u], jnp.zeros_like(qs[u])), kcats[u]), NEG_BIG)
          for u, h in pairs]
    mx = [jnp.max(x, axis=-1, keepdims=True) for x in sc]
    pr = [jnp.exp(sc[n] - mx[n]) for n in range(len(pairs))]
    den = [jnp.sum(x, axis=-1, keepdims=True) for x in pr]
    ov = [_dot(pr[n].astype(BF16), vcats[pairs[n][0]]) / den[n] for n in range(len(pairs))]
    for u in range(len(units)):
        acc = jnp.zeros((bq, DIL_GROUP_DIM), F32)
        lacc = jnp.zeros((bq, DIL_GROUP_DIM), F32)
        for h in range(n_heads):
            n = u * n_heads + h
            acc = jnp.where(head_masks[h], ov[n], acc)
            lacc = jnp.where(head_masks[h], mx[n] + jnp.log(den[n]), lacc)
        rows = pl.ds(starts[u], bq, stride=dilation) if dilation > 1 else pl.ds(pl.multiple_of(starts[u], bq), bq)
        for half in range(DIL_GROUP_DIM // LANES):
            o_ref[0, half, rows, :] = acc[:, half * LANES:(half + 1) * LANES]
            l_ref[0, half, rows, :] = lacc[:, half * LANES:(half + 1) * LANES]


def _dil_group(q, k, v, b, dilation):
    n = q.shape[0] // b
    s = n * dilation
    nb = n // ATTN_BLOCK
    nbs = min(nb, DIL_UNITS)
    cps = DIL_UNITS // nbs
    halves = DIL_GROUP_DIM // LANES
    view = lambda a: a.reshape(b, n, dilation * DIL_GROUP_DIM)
    qmap = lambda bi, ci, ji: (bi, ji, ci)
    kmap = lambda bi, ci, ji: (bi, 0, ci)
    omap = lambda bi, ci, ji: (bi, 0, 0, 0)
    return pl.pallas_call(
        functools.partial(_dil_kernel, dilation=dilation, cps=cps, nbs=nbs),
        grid=(b, dilation // cps, nb // nbs),
        in_specs=[pl.BlockSpec((1, nbs * ATTN_BLOCK, cps * DIL_GROUP_DIM), qmap),
                  pl.BlockSpec((1, n, cps * DIL_GROUP_DIM), kmap),
                  pl.BlockSpec((1, n, cps * DIL_GROUP_DIM), kmap)],
        out_specs=[pl.BlockSpec((1, halves, s, LANES), omap),
                   pl.BlockSpec((1, halves, s, LANES), omap)],
        out_shape=[jax.ShapeDtypeStruct((b, halves, s, LANES), F32),
                   jax.ShapeDtypeStruct((b, halves, s, LANES), F32)],
        compiler_params=_cparams(("parallel", "arbitrary", "arbitrary")),
        name=f"dil_attn_d{dilation}",
    )(view(q), view(k), view(v))


MEM_ROWS = 256

def _mem_kv_kernel(m_ref, ln_ref, w_ref, kn_ref, k_ref, v_ref):
    x = m_ref[0]
    h = x * lax.rsqrt(jnp.mean(x * x, axis=-1, keepdims=True) + NORM_EPS) * ln_ref[...]
    kv = _dot(h.astype(BF16), w_ref[...])
    for hd in range(MEM_DIM // MEM_HEAD_DIM):
        sl = slice(hd * MEM_HEAD_DIM, (hd + 1) * MEM_HEAD_DIM)
        kh = kv[:, sl]
        kh = kh * lax.rsqrt(jnp.mean(kh * kh, axis=-1, keepdims=True) + NORM_EPS) * kn_ref[...]
        k_ref[0, :, sl] = kh.astype(BF16)
    v_ref[0] = kv[:, MEM_DIM:].astype(BF16)


def _mem_kv(mem, ln_mem, w_mem_kv, mem_k_norm):
    b, m, d = mem.shape
    return pl.pallas_call(
        _mem_kv_kernel,
        grid=(b,),
        in_specs=[pl.BlockSpec((1, m, d), lambda i: (i, 0, 0)),
                  pl.BlockSpec((1, d), lambda i: (0, 0)),
                  pl.BlockSpec((d, 2 * MEM_DIM), lambda i: (0, 0)),
                  pl.BlockSpec((1, MEM_HEAD_DIM), lambda i: (0, 0))],
        out_specs=[pl.BlockSpec((1, m, MEM_DIM), lambda i: (i, 0, 0)),
                   pl.BlockSpec((1, m, MEM_DIM), lambda i: (i, 0, 0))],
        out_shape=[jax.ShapeDtypeStruct((b, m, MEM_DIM), BF16),
                   jax.ShapeDtypeStruct((b, m, MEM_DIM), BF16)],
        compiler_params=_cparams(("parallel",)),
        name="mem_kv",
    )(mem, ln_mem.reshape(1, d), w_mem_kv.astype(BF16), mem_k_norm.reshape(1, MEM_HEAD_DIM))


def _mem_attn_kernel(q_ref, k_ref, v_ref, o_ref):
    bm = q_ref.shape[1]
    units = [(r0, hd) for r0 in range(0, bm, MEM_ROWS) for hd in range(MEM_DIM // MEM_HEAD_DIM)]
    cols = lambda hd: slice(hd * MEM_HEAD_DIM, (hd + 1) * MEM_HEAD_DIM)
    sc = [_dot_nt(q_ref[0, r0:r0 + MEM_ROWS, cols(hd)], k_ref[0, :, cols(hd)]) * (MEM_HEAD_DIM ** -0.5)
          for r0, hd in units]
    mx = [jnp.max(x, axis=-1, keepdims=True) for x in sc]
    pr = [jnp.exp(sc[n] - mx[n]) for n in range(len(units))]
    den = [jnp.sum(x, axis=-1, keepdims=True) for x in pr]
    for n, (r0, hd) in enumerate(units):
        o_ref[0, r0:r0 + MEM_ROWS, cols(hd)] = _dot((pr[n] / den[n]).astype(BF16), v_ref[0, :, cols(hd)]).astype(o_ref.dtype)


def _mem_attn(mq, mk, mv, bm=512):
    b, s, _ = mq.shape
    m = mk.shape[1]
    return pl.pallas_call(
        _mem_attn_kernel,
        grid=(b, s // bm),
        in_specs=[pl.BlockSpec((1, bm, MEM_DIM), lambda bi, si: (bi, si, 0)),
                  pl.BlockSpec((1, m, MEM_DIM), lambda bi, si: (bi, 0, 0)),
                  pl.BlockSpec((1, m, MEM_DIM), lambda bi, si: (bi, 0, 0))],
        out_specs=pl.BlockSpec((1, bm, MEM_DIM), lambda bi, si: (bi, si, 0)),
        out_shape=jax.ShapeDtypeStruct((b, s, MEM_DIM), BF16),
        compiler_params=_cparams(("parallel", "parallel")),
        name="mem_attn",
    )(mq, mk, mv)


def _mix_kernel(x_ref, yr_ref, o0_ref, o1_ref, o2_ref, l0_ref, l1_ref, l2_ref, ym_ref, gt_ref,
                pr_ref, pd_ref, pm_ref, wo_ref, ln2_ref, wr_ref, br_ref,
                x1_ref, h2_ref, route_ref, cnt_ref):
    d = x_ref.shape[1]
    wide = lambda ref: jnp.concatenate([ref[0, hf] for hf in range(DIL_GROUP_DIM // LANES)], axis=1)
    l0, l1, l2 = wide(l0_ref), wide(l1_ref), wide(l2_ref)
    m = jnp.maximum(jnp.maximum(l0, l1), l2)
    e0, e1, e2 = jnp.exp(l0 - m), jnp.exp(l1 - m), jnp.exp(l2 - m)
    y_dil = (e0 * wide(o0_ref) + e1 * wide(o1_ref) + e2 * wide(o2_ref)) / (e0 + e1 + e2)
    mixed = (gt_ref[:, 0:d].astype(F32) * _dot(yr_ref[...], pr_ref[...])
             + gt_ref[:, d:2 * d].astype(F32) * _dot(y_dil.astype(BF16), pd_ref[...])
             + gt_ref[:, 2 * d:3 * d].astype(F32) * _dot(ym_ref[...], pm_ref[...]))
    x1 = x_ref[...] + _dot(mixed.astype(BF16), wo_ref[...])
    x1_ref[...] = x1
    h2 = x1 * lax.rsqrt(jnp.mean(x1 * x1, axis=-1, keepdims=True) + NORM_EPS) * ln2_ref[...]
    h2_ref[...] = _pack_bf16_pairs(h2)
    logits = _dot_x3(h2, wr_ref[...]) + br_ref[...]
    lane = lax.broadcasted_iota(jnp.int32, logits.shape, 1)
    lane_f = lane.astype(F32)
    route = jnp.zeros(logits.shape, F32)
    onehot = jnp.zeros(logits.shape, F32)
    vals = []
    for kq in range(TOP_K):
        mx = jnp.max(logits, axis=-1, keepdims=True)
        idx = jnp.min(jnp.where(logits == mx, lane_f, float(LANES)), axis=-1, keepdims=True)
        hit = lane_f == idx
        vals.append(mx)
        route = jnp.where(lane == kq, idx, route)
        onehot = jnp.where(hit, 1.0, onehot)
        logits = jnp.where(hit, -jnp.inf, logits)
    ex = [jnp.exp(vq - vals[0]) for vq in vals]
    den = ex[0] + ex[1] + ex[2] + ex[3]
    for kq in range(TOP_K):
        route = jnp.where(lane == TOP_K + kq, ex[kq] / den, route)
    route_ref[...] = route

    @pl.when(pl.program_id(0) == 0)
    def _():
        cnt_ref[...] = jnp.zeros_like(cnt_ref)

    cnt_ref[...] += jnp.sum(onehot, axis=0, keepdims=True)


def _mix(x2, y_rwkv, outs, lses, y_mem, gates, p_rwkv, p_dil, p_mem, w_out, ln2, w_router, b_router, bm=512):
    t, d = x2.shape
    wr = jnp.zeros((d, LANES), F32).at[:, :N_EXPERTS].set(w_router)
    br = jnp.full((1, LANES), -jnp.inf, F32).at[0, :N_EXPERTS].set(b_router)
    row = lambda w: pl.BlockSpec((bm, w), lambda i: (i, 0))
    const = lambda a: pl.BlockSpec(a.shape, lambda i: (0,) * a.ndim)
    tiles_per_seq = outs[0].shape[2] // bm
    dil = pl.BlockSpec((1, DIL_GROUP_DIM // LANES, bm, LANES),
                       lambda i: (i // tiles_per_seq, 0, i % tiles_per_seq, 0))
    weights = [p_rwkv.astype(BF16), p_dil.astype(BF16), p_mem.astype(BF16), w_out.astype(BF16),
               ln2.reshape(1, d), wr, br]
    return pl.pallas_call(
        _mix_kernel,
        grid=(t // bm,),
        in_specs=[row(d), row(RWKV_DIM)] + [dil] * 6 + [row(MEM_DIM), row(3 * d)]
                 + [const(a) for a in weights],
        out_specs=[row(d), row(d // 2), row(LANES), pl.BlockSpec((1, LANES), lambda i: (0, 0))],
        out_shape=[jax.ShapeDtypeStruct((t, d), F32), jax.ShapeDtypeStruct((t, d // 2), jnp.uint32),
                   jax.ShapeDtypeStruct((t, LANES), F32), jax.ShapeDtypeStruct((1, LANES), F32)],
        compiler_params=_cparams(("arbitrary",)),
        name="mix",
    )(x2, y_rwkv, *outs, *lses, y_mem, gates, *weights)


def _route_kernel(route_ref, pstart_ref, tri_ref, dest_ref, carry_ref):
    @pl.when(pl.program_id(0) == 0)
    def _():
        carry_ref[...] = jnp.zeros_like(carry_ref)

    route = route_ref[...]
    lane = lax.broadcasted_iota(jnp.int32, route.shape, 1)
    lane_f = lane.astype(F32)
    hits = [lane_f == route[:, kq:kq + 1] for kq in range(TOP_K)]
    onehot = jnp.zeros(route.shape, F32)
    for hq in hits:
        onehot = jnp.where(hq, 1.0, onehot)
    rank = _dot(tri_ref[...], onehot.astype(BF16)) + carry_ref[...]
    slot = pstart_ref[...] + rank
    dest = jnp.zeros(route.shape, jnp.int32)
    for kq in range(TOP_K):
        dk = jnp.sum(jnp.where(hits[kq], slot, 0.0), axis=-1, keepdims=True)
        dest = jnp.where(lane == kq, dk.astype(jnp.int32), dest)
    dest_ref[...] = dest
    carry_ref[...] += jnp.sum(onehot, axis=0, keepdims=True)


def _route(route, pstart, bm=1024):
    t = route.shape[0]
    i = jnp.arange(bm)
    tri = (i[None, :] < i[:, None]).astype(BF16)
    return pl.pallas_call(
        _route_kernel,
        grid=(t // bm,),
        in_specs=[pl.BlockSpec((bm, LANES), lambda i: (i, 0)),
                  pl.BlockSpec((1, LANES), lambda i: (0, 0)),
                  pl.BlockSpec((bm, bm), lambda i: (0, 0))],
        out_specs=pl.BlockSpec((bm, LANES), lambda i: (i, 0)),
        out_shape=jax.ShapeDtypeStruct((t, LANES), jnp.int32),
        scratch_shapes=[pltpu.VMEM((1, LANES), F32)],
        compiler_params=_cparams(("arbitrary",)),
        name="route",
    )(route, pstart, tri)


def _row_copy(src_ref, src_row, dst_ref, dst_row, sem):
    return pltpu.make_async_copy(src_ref.at[pl.ds(src_row, 1)], dst_ref.at[pl.ds(dst_row, 1)], sem)


def _dispatch_kernel(last_ref, dest_ref, h_ref, xs_ref, zero_ref, hbuf_ref, sem_s, sem_l, zsem, *, bm):
    i = pl.program_id(0)
    n = pl.num_programs(0)

    def load(tile, b):
        rows = pl.ds(pl.multiple_of(tile * bm, bm), bm)
        return pltpu.make_async_copy(h_ref.at[rows], hbuf_ref.at[b], sem_l.at[b])

    def wait_scatter(b):
        for kq in range(TOP_K):
            pltpu.make_async_copy(hbuf_ref.at[0], xs_ref.at[pl.ds(0, bm)], sem_s.at[b]).wait()

    @pl.when(i == 0)
    def _():
        load(0, 0).start()
        zero_ref[...] = jnp.zeros_like(zero_ref)

        def zero_copy(e):
            row = pl.multiple_of(last_ref[e] * EXPERT_ROWS, EXPERT_ROWS)
            return pltpu.make_async_copy(zero_ref, xs_ref.at[pl.ds(row, EXPERT_ROWS)], zsem)

        def z_issue(e, carry):
            @pl.when(last_ref[e] >= 0)
            def _():
                zero_copy(e).start()
            return carry

        def z_drain(e, carry):
            @pl.when(last_ref[e] >= 0)
            def _():
                zero_copy(e).wait()
            return carry

        lax.fori_loop(0, 2 * N_EXPERTS, z_issue, 0)
        lax.fori_loop(0, 2 * N_EXPERTS, z_drain, 0)

    @pl.when(i + 1 < n)
    def _():
        load(i + 1, (i + 1) % 3).start()

    load(i, i % 3).wait()

    for phase in range(6):
        @pl.when(i % 6 == phase)
        def _(phase=phase):
            def issue(j, carry):
                for kq in range(TOP_K):
                    _row_copy(hbuf_ref.at[phase % 3], j, xs_ref, dest_ref[j * TOP_K + kq],
                              sem_s.at[phase % 2]).start(priority=kq % 2)
                return carry

            lax.fori_loop(0, bm, issue, 0)

    @pl.when(i > 0)
    def _():
        wait_scatter((i - 1) % 2)

    @pl.when(i == n - 1)
    def _():
        wait_scatter(i % 2)


def _dispatch(last_blk, dest_flat, h2, n_slots, bm=512):
    t, d = h2.shape
    grid_spec = pltpu.PrefetchScalarGridSpec(
        num_scalar_prefetch=1,
        grid=(t // bm,),
        in_specs=[pl.BlockSpec((bm * TOP_K,), lambda i, lb: (i,), memory_space=pltpu.SMEM),
                  pl.BlockSpec(memory_space=pl.ANY)],
        out_specs=pl.BlockSpec(memory_space=pl.ANY),
        scratch_shapes=[pltpu.VMEM((EXPERT_ROWS, d), h2.dtype), pltpu.VMEM((3, bm, d), h2.dtype),
                        pltpu.SemaphoreType.DMA((2,)), pltpu.SemaphoreType.DMA((3,)), pltpu.SemaphoreType.DMA],
    )
    return pl.pallas_call(
        functools.partial(_dispatch_kernel, bm=bm),
        grid_spec=grid_spec,
        out_shape=jax.ShapeDtypeStruct((n_slots, d), h2.dtype),
        compiler_params=_cparams(("arbitrary",)),
        name="dispatch",
    )(last_blk, dest_flat, h2)


FF_CHUNK = 256
EXPERT_ROWS = 256
BLOCKS_PER_STEP = 4


def _expert_kernel(be_ref, par_ref, nxt_ref, nu_ref, xs_ref, w1_ref, b1_ref, w2_ref, b2_ref, sel_ref, ys_ref,
                   w1f_ref, w2f_ref, w1p_ref, w2b_ref, act_ref, sem):
    i = pl.program_id(0)
    d_ff2 = w1_ref.shape[2]
    half = FF_CHUNK // 2

    def weight_copies(expert, buf):
        return (pltpu.make_async_copy(w1_ref.at[expert], w1f_ref.at[buf], sem.at[0, buf]),
                pltpu.make_async_copy(w2_ref.at[expert], w2f_ref.at[buf], sem.at[1, buf]))

    @pl.when(i == 0)
    def _():
        for cp in weight_copies(be_ref[0], par_ref[0]):
            cp.start()

    def convert(blk, e, slot):
        for cp in weight_copies(e, slot):
            cp.wait()
        nxt = nxt_ref[blk]

        @pl.when(nxt >= 0)
        def _():
            for cp in weight_copies(nxt, 1 - slot):
                cp.start(priority=1)

        for c in range(d_ff2 // FF_CHUNK):
            sl = slice(c * FF_CHUNK, (c + 1) * FF_CHUNK)
            w1p_ref[:, sl] = _dot(w1f_ref[slot, :, sl].astype(BF16), sel_ref[...]).astype(BF16)
        w2b_ref[...] = w2f_ref[slot].astype(BF16)

    def compute(rows, e):
        n_rows = rows.stop - rows.start
        x = _unpack_bf16_pairs(xs_ref[rows, :])
        for c in range(d_ff2 // (2 * FF_CHUNK)):
            sl = slice(2 * c * FF_CHUNK, 2 * (c + 1) * FF_CHUNK)
            hb = _dot(x, w1p_ref[:, sl]) + b1_ref[e, :, sl]
            for j in range(2):
                x_glu = jnp.minimum(hb[:, j * FF_CHUNK:j * FF_CHUNK + half], SWIGLU_LIMIT)
                x_lin = jnp.clip(hb[:, j * FF_CHUNK + half:(j + 1) * FF_CHUNK], -SWIGLU_LIMIT, SWIGLU_LIMIT)
                act = x_glu * _sigmoid(SWIGLU_ALPHA * x_glu) * (x_lin + 1.0)
                act_ref[0:n_rows, (2 * c + j) * half:(2 * c + j + 1) * half] = act.astype(BF16)
        ys_ref[rows, :] = _dot(act_ref[0:n_rows, :], w2b_ref[...]) + b2_ref[e]

    blk0 = i * BLOCKS_PER_STEP
    n_used = nu_ref[0]

    def run(lo, n):
        first = blk0 + lo
        last = first + n - 1
        e = be_ref[first]
        rows = slice(lo * EXPERT_ROWS, (lo + n) * EXPERT_ROWS)
        fresh = jnp.logical_or(first == 0, be_ref[jnp.maximum(first - 1, 0)] != e)
        whole = jnp.logical_and(be_ref[last] == e, last < n_used)

        @pl.when(whole)
        def _():
            @pl.when(fresh)
            def _():
                convert(first, e, par_ref[first])

            compute(rows, e)

        if n > 1:
            @pl.when(jnp.logical_not(whole))
            def _():
                run(lo, n // 2)
                run(lo + n // 2, n // 2)
        else:
            @pl.when(jnp.logical_not(whole))
            def _():
                ys_ref[rows, :] = jnp.zeros((EXPERT_ROWS, ys_ref.shape[1]), F32)

    run(0, BLOCKS_PER_STEP)


def _chunk_deinterleave(a):
    lead = a.shape[:-1]
    a = a.reshape(lead + (a.shape[-1] // FF_CHUNK, FF_CHUNK // 2, 2))
    return jnp.swapaxes(a, -1, -2).reshape(lead + (-1,))


def _experts(block_e, parity, next_e, n_used, xs, w1, b1, w2, b2):
    n_slots, dx = xs.shape
    n_e, d, d_ff2 = w1.shape
    n_blocks = n_slots // EXPERT_ROWS
    i = jnp.arange(FF_CHUNK)
    src = jnp.where(i < FF_CHUNK // 2, 2 * i, 2 * (i - FF_CHUNK // 2) + 1)
    sel = (jnp.arange(FF_CHUNK)[:, None] == src[None, :]).astype(BF16)
    b1p = _chunk_deinterleave(b1).reshape(n_e, 1, d_ff2)
    step_rows = BLOCKS_PER_STEP * EXPERT_ROWS
    imap = lambda i, be, par, nxt, nu: (i, 0)
    xmap = lambda i, be, par, nxt, nu: (jnp.minimum(i, (nu[0] - 1) // BLOCKS_PER_STEP), 0)
    cmap = lambda i, be, par, nxt, nu: (0, 0, 0)
    grid_spec = pltpu.PrefetchScalarGridSpec(
        num_scalar_prefetch=4,
        grid=(n_blocks // BLOCKS_PER_STEP,),
        in_specs=[pl.BlockSpec((step_rows, dx), xmap),
                  pl.BlockSpec(memory_space=pl.ANY),
                  pl.BlockSpec((n_e, 1, d_ff2), cmap),
                  pl.BlockSpec(memory_space=pl.ANY),
                  pl.BlockSpec((n_e, 1, d), cmap),
                  pl.BlockSpec((FF_CHUNK, FF_CHUNK), lambda i, be, par, nxt, nu: (0, 0))],
        out_specs=pl.BlockSpec((step_rows, d), imap),
        scratch_shapes=[pltpu.VMEM((2, d, d_ff2), F32), pltpu.VMEM((2, d_ff2 // 2, d), F32),
                        pltpu.VMEM((d, d_ff2), BF16), pltpu.VMEM((d_ff2 // 2, d), BF16),
                        pltpu.VMEM((BLOCKS_PER_STEP * EXPERT_ROWS, d_ff2 // 2), BF16),
                        pltpu.SemaphoreType.DMA((2, 2))],
    )
    return pl.pallas_call(
        _expert_kernel,
        grid_spec=grid_spec,
        out_shape=jax.ShapeDtypeStruct((n_slots, d), F32),
        compiler_params=_cparams(("arbitrary",), vmem=EXPERT_VMEM_LIMIT),
        name="experts",
    )(block_e, parity, next_e, n_used, xs, w1, b1p, w2, b2.reshape(n_e, 1, d), sel)


def _combine_kernel(dest_ref, dnext_ref, x1_ref, route_ref, ys_ref, o_ref, buf_ref, sem, *, bm):
    i = pl.program_id(0)
    n = pl.num_programs(0)
    slot = i % 2

    def gather(d_ref, buf):
        def issue(g, carry):
            base = pl.multiple_of(g * SUBLANES, SUBLANES)
            for r in range(SUBLANES):
                for kq in range(TOP_K):
                    _row_copy(ys_ref, d_ref[(base + r) * TOP_K + kq], buf_ref.at[buf, kq], base + r,
                              sem.at[buf]).start(priority=kq % 2)
            return carry

        lax.fori_loop(0, bm // SUBLANES, issue, 0)

    @pl.when(i == 0)
    def _():
        gather(dest_ref, 0)

    for nxt in range(2):
        @pl.when(jnp.logical_and(i + 1 < n, (i + 1) % 2 == nxt))
        def _(nxt=nxt):
            gather(dnext_ref, nxt)

    for kq in range(TOP_K):
        pltpu.make_async_copy(ys_ref.at[pl.ds(0, bm)], buf_ref.at[slot, kq], sem.at[slot]).wait()
    acc = x1_ref[...]
    for kq in range(TOP_K):
        acc = acc + route_ref[:, TOP_K + kq:TOP_K + kq + 1] * buf_ref[slot, kq]
    o_ref[...] = acc


def _combine(dest_flat, x1, route, ys, bm=512):
    t, d = x1.shape
    n = t // bm
    return pl.pallas_call(
        functools.partial(_combine_kernel, bm=bm),
        grid=(n,),
        in_specs=[pl.BlockSpec((bm * TOP_K,), lambda i: (i,), memory_space=pltpu.SMEM),
                  pl.BlockSpec((bm * TOP_K,), lambda i: (jnp.minimum(i + 1, n - 1),), memory_space=pltpu.SMEM),
                  pl.BlockSpec((bm, d), lambda i: (i, 0)),
                  pl.BlockSpec((bm, LANES), lambda i: (i, 0)),
                  pl.BlockSpec(memory_space=pl.ANY)],
        out_specs=pl.BlockSpec((bm, d), lambda i: (i, 0)),
        out_shape=jax.ShapeDtypeStruct((t, d), F32),
        scratch_shapes=[pltpu.VMEM((2, TOP_K, bm, d), F32), pltpu.SemaphoreType.DMA((2,))],
        compiler_params=_cparams(("arbitrary",)),
        name="combine",
    )(dest_flat, dest_flat, x1, route, ys)


def _layer(x, mem, ln1, w_in, token_mu, rwkv_w0, rwkv_w2, rwkv_a0, rwkv_a2, rwkv_g2,
           rwkv_k_k, rwkv_k_a, rwkv_r_k, rwkv_ln_w, rwkv_ln_b, dil_q_norm, dil_k_norm,
           ln_mem, w_mem_kv, mem_q_norm, mem_k_norm, p_rwkv, p_dil, p_mem, w_out,
           ln2, w_router, b_router, w1, b1, w2, b2):
    b, s, d = x.shape
    t = b * s
    x2 = x.reshape(t, d)
    zr, dil_qkv, mq, gates = _in_proj(x2, ln1, w_in, dil_q_norm, dil_k_norm, mem_q_norm)
    y_rwkv = _rwkv(zr.reshape(b, s, RWKV_IN), token_mu, rwkv_w0, rwkv_w2, rwkv_a0, rwkv_a2, rwkv_g2,
                   rwkv_k_k, rwkv_k_a, rwkv_r_k.reshape(-1), rwkv_ln_w, rwkv_ln_b)
    outs, lses = [], []
    for g, (_, dilation) in enumerate(DIL_PATTERNS):
        o, l = _dil_group(*dil_qkv[3 * g:3 * g + 3], b, dilation)
        outs.append(o)
        lses.append(l)
    mk, mv = _mem_kv(mem, ln_mem, w_mem_kv, mem_k_norm)
    y_mem = _mem_attn(mq.reshape(b, s, MEM_DIM), mk, mv)
    x1, h2, route, counts = _mix(x2, y_rwkv.reshape(t, RWKV_DIM), outs, lses, y_mem.reshape(t, MEM_DIM), gates,
                                 p_rwkv, p_dil, p_mem, w_out, ln2, w_router, b_router)

    counts = counts[0, :N_EXPERTS].astype(jnp.int32)
    nblk = (counts + EXPERT_ROWS - 1) // EXPERT_ROWS
    bend = jnp.cumsum(nblk)
    pstart = ((bend - nblk) * EXPERT_ROWS).astype(F32)
    pstart = jnp.zeros((1, LANES), F32).at[0, :N_EXPERTS].set(pstart)
    n_blocks = (t * TOP_K) // EXPERT_ROWS + N_EXPERTS
    block_e = jnp.sum(bend[None, :] <= jnp.arange(n_blocks, dtype=jnp.int32)[:, None], axis=1)
    block_e = jnp.minimum(block_e, N_EXPERTS - 1).astype(jnp.int32)
    n_used = bend[-1:].astype(jnp.int32)
    used = nblk > 0
    eids = jnp.arange(N_EXPERTS, dtype=jnp.int32)
    ordinal = jnp.cumsum(used.astype(jnp.int32)) - 1
    later = jnp.where(used[None, :] & (eids[None, :] > eids[:, None]), eids[None, :], N_EXPERTS)
    next_used = jnp.min(later, axis=1)
    next_used = jnp.where(next_used < N_EXPERTS, next_used, -1).astype(jnp.int32)
    is_e = (block_e[:, None] == eids[None, :]).astype(jnp.int32)
    parity = (jnp.sum(is_e * ordinal[None, :], axis=1) % 2).astype(jnp.int32)
    next_e = jnp.sum(is_e * next_used[None, :], axis=1).astype(jnp.int32)
    tail = n_used[0] + eids
    last_blk = jnp.concatenate([jnp.where(used, bend - 1, -1), jnp.where(tail < n_blocks, tail, -1)]).astype(jnp.int32)

    dest = _route(route, pstart)[:, :TOP_K].reshape(-1)
    xs = _dispatch(last_blk, dest, h2, n_blocks * EXPERT_ROWS)
    ys = _experts(block_e, parity, next_e, n_used, xs, w1, b1, w2, b2)
    out = _combine(dest, x1, route, ys)
    return out.reshape(b, s, d)


def kernel(x, mem, ln1, w_in, token_mu, rwkv_w0, rwkv_w2, rwkv_a0, rwkv_a2, rwkv_g2, rwkv_k_k, rwkv_k_a, rwkv_r_k, rwkv_ln_w, rwkv_ln_b, dil_q_norm, dil_k_norm, ln_mem, w_mem_kv, mem_q_norm, mem_k_norm, p_rwkv, p_dil, p_mem, w_out, ln2, w_router, b_router, w1, b1, w2, b2):
    params = (ln1, w_in, token_mu, rwkv_w0, rwkv_w2, rwkv_a0, rwkv_a2, rwkv_g2, rwkv_k_k, rwkv_k_a,
              rwkv_r_k, rwkv_ln_w, rwkv_ln_b, dil_q_norm, dil_k_norm, ln_mem, w_mem_kv, mem_q_norm,
              mem_k_norm, p_rwkv, p_dil, p_mem, w_out, ln2, w_router, b_router, w1, b1, w2, b2)
    for l in range(ln1.shape[0]):
        x = _layer(x, mem, *[p[l] for p in params])
    return x
```

```python
import functools

import jax
import jax.numpy as jnp
from jax import lax
from jax.experimental import pallas as pl
from jax.experimental.pallas import tpu as pltpu

F32 = jnp.float32
BF16 = jnp.bfloat16

NORM_EPS = 1e-5
HEAD_DIM = 64
RWKV_DIM = 512
RWKV_IN = 1792
GN_EPS = HEAD_DIM * 1e-5
DIL_DIM = 768
DIL_GROUP_DIM = 256
DIL_PATTERNS = ((128, 1), (512, 4), (2048, 16))
ATTN_BLOCK = 128
MEM_DIM = 512
MEM_HEAD_DIM = 128
N_EXPERTS = 32
TOP_K = 4
SWIGLU_ALPHA = 1.702
SWIGLU_LIMIT = 7.0
LANES = 128
SUBLANES = 8
CHUNK = 64
NEG_BIG = -1e30
VMEM_LIMIT = 48 * 1024 * 1024
EXPERT_VMEM_LIMIT = 56 * 1024 * 1024


def _dot(a, b):
    return jnp.dot(a, b, preferred_element_type=F32)


def _dot_nt(a, b):
    return lax.dot_general(a, b, (((1,), (1,)), ((), ())), preferred_element_type=F32)


def _split2(x):
    hi = x.astype(BF16)
    lo = (x - hi.astype(F32)).astype(BF16)
    return hi, lo


def _split3(x):
    hi = x.astype(BF16)
    r = x - hi.astype(F32)
    lo = r.astype(BF16)
    lo2 = (r - lo.astype(F32)).astype(BF16)
    return hi, lo, lo2


def _dot_rhs3(w_bf16, x):
    hi, lo, lo2 = _split3(x)
    return _dot(w_bf16, hi) + _dot(w_bf16, lo) + _dot(w_bf16, lo2)


def _dot_x3(a, b):
    ah, al = _split2(a)
    bh, bl = _split2(b)
    return _dot(ah, bh) + _dot(al, bh) + _dot(ah, bl)


def _sigmoid(x):
    return 1.0 / (1.0 + jnp.exp(-x))


def _pack_bf16_pairs(x):
    half = x.shape[1] // 2
    bits = lambda v: pltpu.bitcast(v.astype(BF16).astype(F32), jnp.uint32)
    return (bits(x[:, :half]) >> 16) | (bits(x[:, half:]) & jnp.uint32(0xFFFF0000))


def _unpack_bf16_pairs(w):
    lo = pltpu.bitcast(w << 16, F32)
    hi = pltpu.bitcast(w & jnp.uint32(0xFFFF0000), F32)
    return jnp.concatenate([lo, hi], axis=1).astype(BF16)


def _cparams(sem, vmem=VMEM_LIMIT):
    return pltpu.CompilerParams(dimension_semantics=sem, vmem_limit_bytes=vmem)


IN_CHUNK = 256
PERM_ROWS = 256
N_ZR = RWKV_IN // IN_CHUNK
N_DIL = 3 * DIL_DIM // IN_CHUNK
N_MQ = MEM_DIM // IN_CHUNK
N_GATE = 3 * 1024 // IN_CHUNK


def _in_proj_kernel(x_ref, ln_ref, w_ref, qg_ref, kg_ref, mg_ref, p4_ref, p16_ref,
                    zr_ref, *rest, bm):
    dil_refs, (mq_ref, gt_ref) = rest[:N_DIL], rest[N_DIL:]
    perm_refs = (None, p4_ref, p16_ref)
    x = x_ref[...]
    h = x * lax.rsqrt(jnp.mean(x * x, axis=-1, keepdims=True) + NORM_EPS) * ln_ref[...]
    hb = h.astype(BF16)

    lane = lax.broadcasted_iota(jnp.int32, (x.shape[0], LANES), 1)

    def seg_rms(z, seg, gain):
        z2 = z * z
        cols = []
        for cb in range(IN_CHUNK // LANES):
            blk = z2[:, cb * LANES:(cb + 1) * LANES]
            ss = jnp.zeros_like(blk)
            for sg in range(LANES // seg):
                m = (lane >= sg * seg) & (lane < (sg + 1) * seg)
                ss = jnp.where(m, jnp.sum(jnp.where(m, blk, 0.0), axis=-1, keepdims=True), ss)
            cols.append(ss)
        ss = jnp.concatenate(cols, axis=1)
        return z * lax.rsqrt(ss * (1.0 / seg) + NORM_EPS) * gain

    for c in range(N_ZR + N_DIL + N_MQ + N_GATE):
        z = _dot(hb, w_ref[:, c * IN_CHUNK:(c + 1) * IN_CHUNK])
        if c < N_ZR:
            zr_ref[:, c * IN_CHUNK:(c + 1) * IN_CHUNK] = z
            continue
        d = c - N_ZR
        if d < N_DIL:
            which, g = divmod(d, 3)
            if which == 0:
                z = seg_rms(z, HEAD_DIM, qg_ref[...])
            elif which == 1:
                z = seg_rms(z, HEAD_DIM, kg_ref[...])
            zb = z.astype(BF16)
            o_ref = dil_refs[g * 3 + which]
            dilation = DIL_PATTERNS[g][1]
            if dilation == 1:
                o_ref[...] = zb
            else:
                rows = PERM_ROWS // dilation
                for sub in range(bm // PERM_ROWS):
                    zp = _dot(perm_refs[g][...], zb[sub * PERM_ROWS:(sub + 1) * PERM_ROWS, :]).astype(BF16)
                    for r in range(dilation):
                        o_ref[sub * rows:(sub + 1) * rows, r * IN_CHUNK:(r + 1) * IN_CHUNK] = zp[r * rows:(r + 1) * rows, :]
            continue
        d -= N_DIL
        if d < N_MQ:
            mq_ref[:, d * IN_CHUNK:(d + 1) * IN_CHUNK] = seg_rms(z, MEM_HEAD_DIM, mg_ref[...]).astype(BF16)
            continue
        d -= N_MQ
        gt_ref[:, d * IN_CHUNK:(d + 1) * IN_CHUNK] = _sigmoid(z).astype(BF16)


def _class_perm(bm, dilation):
    i = jnp.arange(bm)
    src = (i % (bm // dilation)) * dilation + i // (bm // dilation)
    return (src[:, None] == i[None, :]).astype(BF16)


def _in_proj(x2, ln1, w_in, dil_q_norm, dil_k_norm, mem_q_norm, bm=512):
    t, d = x2.shape
    n_cols = w_in.shape[1]
    wb = w_in.astype(BF16)
    qg = (jnp.tile(dil_q_norm, IN_CHUNK // HEAD_DIM) * (HEAD_DIM ** -0.5)).reshape(1, IN_CHUNK)
    kg = jnp.tile(dil_k_norm, IN_CHUNK // HEAD_DIM).reshape(1, IN_CHUNK)
    mg = jnp.tile(mem_q_norm, IN_CHUNK // MEM_HEAD_DIM).reshape(1, IN_CHUNK)
    p4 = _class_perm(PERM_ROWS, DIL_PATTERNS[1][1])
    p16 = _class_perm(PERM_ROWS, DIL_PATTERNS[2][1])
    row = lambda w: pl.BlockSpec((bm, w), lambda i: (i, 0))
    const = lambda a: pl.BlockSpec(a.shape, lambda i: (0,) * a.ndim)
    dil_specs, dil_shapes = [], []
    for _, dilation in DIL_PATTERNS:
        for _ in range(3):
            dil_specs.append(pl.BlockSpec((bm // dilation, dilation * DIL_GROUP_DIM), lambda i: (i, 0)))
            dil_shapes.append(jax.ShapeDtypeStruct((t // dilation, dilation * DIL_GROUP_DIM), BF16))
    outs = pl.pallas_call(
        functools.partial(_in_proj_kernel, bm=bm),
        grid=(t // bm,),
        in_specs=[row(d), pl.BlockSpec((1, d), lambda i: (0, 0)),
                  pl.BlockSpec((d, n_cols), lambda i: (0, 0), pipeline_mode=pl.Buffered(1)),
                  const(qg), const(kg), const(mg), const(p4), const(p16)],
        out_specs=[row(RWKV_IN)] + dil_specs + [row(MEM_DIM), row(3 * d)],
        out_shape=[jax.ShapeDtypeStruct((t, RWKV_IN), F32)] + dil_shapes
                  + [jax.ShapeDtypeStruct((t, MEM_DIM), BF16), jax.ShapeDtypeStruct((t, 3 * d), BF16)],
        compiler_params=_cparams(("parallel",)),
        name="in_proj",
    )(x2, ln1.reshape(1, d), wb, qg, kg, mg, p4, p16)
    return outs[0], outs[1:1 + N_DIL], outs[1 + N_DIL], outs[2 + N_DIL]


def _rwkv_kernel(z_ref, mu_ref, w0_ref, a0_ref, wl_ref, kk_ref, ka_ref, rk_ref,
                 lnw_ref, lnb_ref, tri_ref, y_ref, carry_ref, state_ref, ybuf_ref, *, tt):
    s_idx = pl.program_id(1)

    @pl.when(s_idx == 0)
    def _():
        carry_ref[...] = jnp.zeros_like(carry_ref)
        state_ref[...] = jnp.zeros_like(state_ref)

    z = z_ref[0]
    rows = lax.broadcasted_iota(jnp.int32, z.shape, 0)
    prev = jnp.where(rows == 0, carry_ref[...], pltpu.roll(z, 1, axis=0))
    carry_ref[...] = z[tt - 1:tt, :]
    zl = z + (prev - z) * mu_ref[...]
    r = zl[:, 0:RWKV_DIM]
    k = zl[:, RWKV_DIM:2 * RWKV_DIM]
    v = zl[:, 2 * RWKV_DIM:3 * RWKV_DIM]
    zlo = zl[:, 3 * RWKV_DIM:RWKV_IN]
    lcol = lax.broadcasted_iota(jnp.int32, zlo.shape, 1)
    feat = jnp.where(lcol < 64, jnp.tanh(zlo), jnp.where(lcol < 128, zlo, _sigmoid(zlo)))
    lora = _dot(feat.astype(BF16), wl_ref[...])
    nu = -(w0_ref[...] + lora[:, 0:RWKV_DIM])
    softplus = jnp.maximum(nu, 0.0) + jnp.log(1.0 + jnp.exp(-jnp.abs(nu)))
    lw = -jnp.exp(-softplus - 0.5)
    a = _sigmoid(a0_ref[...] + lora[:, RWKV_DIM:2 * RWKV_DIM])
    g = lora[:, 2 * RWKV_DIM:3 * RWKV_DIM]

    n_pairs = RWKV_DIM // LANES
    n_chunks = tt // CHUNK
    head_lo = lax.broadcasted_iota(jnp.int32, (tt, LANES), 1) < HEAD_DIM

    def seg_sum(x):
        parts = []
        for p in range(n_pairs):
            xp = x[:, p * LANES:(p + 1) * LANES]
            lo = jnp.sum(jnp.where(head_lo, xp, 0.0), axis=-1, keepdims=True)
            hi = jnp.sum(jnp.where(head_lo, 0.0, xp), axis=-1, keepdims=True)
            parts.append(jnp.where(head_lo, lo, hi))
        return jnp.concatenate(parts, axis=1)

    kk = k * kk_ref[...]
    kk = kk / jnp.maximum(jnp.sqrt(seg_sum(kk * kk)), 1e-12)
    k2 = k * (1.0 + (a - 1.0) * ka_ref[...])
    a_s = -kk
    b_s = kk * a

    cum = _dot_rhs3(tri_ref[...], lw)
    tot = jnp.concatenate(
        [jnp.broadcast_to(cum[(c + 1) * CHUNK - 1:(c + 1) * CHUNK, :], (CHUNK, RWKV_DIM)) for c in range(n_chunks)],
        axis=0)
    e_neg = jnp.exp(-cum)
    e_end = jnp.exp(tot - cum)
    at = (a_s * jnp.exp(cum - lw)).astype(BF16)
    rt = (r * jnp.exp(cum)).astype(BF16)
    bt = (b_s * e_neg).astype(BF16)
    kt = (k2 * e_neg).astype(BF16)
    be = (b_s * e_end).astype(BF16)
    ke = (k2 * e_end).astype(BF16)
    vb = v.astype(BF16)
    e_tot = jnp.exp(tot)

    r128 = lax.broadcasted_iota(jnp.int32, (LANES, LANES), 0)
    c128 = lax.broadcasted_iota(jnp.int32, (LANES, LANES), 1)
    lane_lo = lax.broadcasted_iota(jnp.int32, (CHUNK, LANES), 1) < HEAD_DIM
    stril = c128 < r128
    tril = c128 <= r128
    eye = (c128 == r128).astype(F32)
    same16 = (r128 // 16) == (c128 // 16)
    same32 = (r128 // 32) == (c128 // 32)
    off16 = same32 & jnp.logical_not(same16)
    off32 = jnp.logical_not(same32)

    def bdiag(xp):
        zero = jnp.zeros_like(xp)
        return jnp.concatenate([jnp.where(lane_lo, xp, zero), jnp.where(lane_lo, zero, xp)], axis=0)

    units = [(ci, p) for ci in range(n_chunks) for p in range(n_pairs)]
    pick = lambda arr, u: arr[u[0] * CHUNK:(u[0] + 1) * CHUNK, u[1] * LANES:(u[1] + 1) * LANES]
    cat0 = lambda xs: jnp.concatenate(xs, axis=0)
    cat1 = lambda xs: jnp.concatenate(xs, axis=1)
    zero_b = jnp.zeros((LANES, LANES), BF16)

    at_b = [bdiag(pick(at, u)) for u in units]
    v_b = [bdiag(pick(vb, u)) for u in units]
    be_b = [bdiag(pick(be, u)) for u in units]
    ke_b = [bdiag(pick(ke, u)) for u in units]
    rt_b = [bdiag(pick(rt, u)) for u in units]
    a_ab, a_ak, a_rbk = [], [], []
    for i, u in enumerate(units):
        mq = _dot_nt(cat0([at_b[i], rt_b[i]]), cat0([bdiag(pick(bt, u)), bdiag(pick(kt, u))]))
        a_ab.append(jnp.where(stril, mq[:LANES, :LANES], 0.0).astype(BF16))
        a_ak.append(jnp.where(stril, mq[:LANES, LANES:], 0.0).astype(BF16))
        a_rbk.append(cat1([jnp.where(tril, mq[LANES:, :LANES], 0.0), jnp.where(tril, mq[LANES:, LANES:], 0.0)]).astype(BF16))
    w_b = [_dot(a_ak[i], v_b[i]).astype(BF16) for i in range(len(units))]

    d1 = [jnp.where(same16, x, jnp.zeros_like(x)) for x in a_ab]
    xs = [eye + d.astype(F32) for d in d1]
    d2 = [_dot(d, d).astype(BF16) for d in d1]
    t_ = [_dot(d2[i], cat1([xs[i].astype(BF16), d2[i]])) for i in range(len(units))]
    xs = [xs[i] + t_[i][:, :LANES] for i in range(len(units))]
    d4 = [t[:, LANES:].astype(BF16) for t in t_]
    t_ = [_dot(d4[i], cat1([xs[i].astype(BF16), d4[i]])) for i in range(len(units))]
    xs = [xs[i] + t_[i][:, :LANES] for i in range(len(units))]
    d8 = [t[:, LANES:].astype(BF16) for t in t_]
    xs = [xs[i] + _dot(d8[i], xs[i].astype(BF16)) for i in range(len(units))]
    for off in (off16, off32):
        xb = [x.astype(BF16) for x in xs]
        g_ = [_dot(jnp.where(off, a_ab[i], jnp.zeros_like(a_ab[i])), xb[i]).astype(BF16) for i in range(len(units))]
        xs = [xs[i] + _dot(xb[i], g_[i]) for i in range(len(units))]

    pq = [_dot(xs[i].astype(BF16), cat1([at_b[i], w_b[i]])) for i in range(len(units))]
    ry = [_dot(a_rbk[i], cat0([pq[i].astype(BF16), cat1([zero_b, v_b[i]])])) for i in range(len(units))]
    r2 = [(rt_b[i].astype(F32) + ry[i][:, :LANES]).astype(BF16) for i in range(len(units))]
    m_c = [_dot(pq[i][:, :LANES].T.astype(BF16), be_b[i]) for i in range(len(units))]
    n_c = [_dot(cat1([pq[i][:, LANES:].T.astype(BF16), v_b[i].astype(F32).T.astype(BF16)]), cat0([be_b[i], ke_b[i]]))
           for i in range(len(units))]

    state = [state_ref[p] for p in range(n_pairs)]
    for i, (ci, p) in enumerate(units):
        s0 = state[p]
        s_b = s0.astype(BF16)
        y = _dot_nt(r2[i], s_b) + ry[i][:, LANES:]
        e_row = e_tot[ci * CHUNK:ci * CHUNK + 1, p * LANES:(p + 1) * LANES]
        state[p] = s0 * e_row + _dot(s_b, m_c[i].astype(BF16)) + n_c[i]
        ybuf_ref[ci * CHUNK:(ci + 1) * CHUNK, p * LANES:(p + 1) * LANES] = jnp.where(lane_lo, y[:CHUNK], y[CHUNK:])
    for p in range(n_pairs):
        state_ref[p] = state[p]

    y = ybuf_ref[...]
    mean = seg_sum(y) * (1.0 / HEAD_DIM)
    yc = y - mean
    var = seg_sum(yc * yc) * (1.0 / HEAD_DIM)
    yn = yc * lax.rsqrt(var + GN_EPS) * lnw_ref[...] + lnb_ref[...]
    bonus = seg_sum(r * k2 * rk_ref[...]) * v
    y_ref[0] = ((yn + bonus) * g).astype(y_ref.dtype)


def _rwkv(zr, token_mu, w0, w2, a0, a2, g2, k_k, k_a, r_k, ln_w, ln_b, tt=256):
    b, s, _ = zr.shape
    i = jnp.arange(tt)
    same = (i[:, None] // CHUNK) == (i[None, :] // CHUNK)
    tri = (same & (i[None, :] <= i[:, None])).astype(BF16)
    vec = lambda a: a.reshape(1, -1)
    wl = jnp.zeros((RWKV_IN - 3 * RWKV_DIM, 3 * RWKV_DIM), F32)
    wl = wl.at[0:64, 0:RWKV_DIM].set(w2).at[64:128, RWKV_DIM:2 * RWKV_DIM].set(a2)
    wl = wl.at[128:256, 2 * RWKV_DIM:3 * RWKV_DIM].set(g2).astype(BF16)
    params = [vec(token_mu), vec(w0), vec(a0), wl, vec(k_k), vec(k_a), vec(r_k),
              vec(ln_w), vec(ln_b), tri]
    const = lambda a: pl.BlockSpec(a.shape, lambda bi, si: (0,) * a.ndim)
    return pl.pallas_call(
        functools.partial(_rwkv_kernel, tt=tt),
        grid=(b, s // tt),
        in_specs=[pl.BlockSpec((1, tt, RWKV_IN), lambda bi, si: (bi, si, 0))] + [const(a) for a in params],
        out_specs=pl.BlockSpec((1, tt, RWKV_DIM), lambda bi, si: (bi, si, 0)),
        out_shape=jax.ShapeDtypeStruct((b, s, RWKV_DIM), BF16),
        scratch_shapes=[pltpu.VMEM((1, RWKV_IN), F32),
                        pltpu.VMEM((RWKV_DIM // LANES, LANES, LANES), F32),
                        pltpu.VMEM((tt, RWKV_DIM), F32)],
        compiler_params=_cparams(("parallel", "arbitrary")),
        name="rwkv",
    )(zr, *params)


DIL_UNITS = 8


def _dil_kernel(q_ref, k_ref, v_ref, o_ref, l_ref, *, dilation, cps, nbs):
    cg = pl.program_id(1)
    jb = pl.program_id(2)
    bq = ATTN_BLOCK
    n_heads = DIL_GROUP_DIM // HEAD_DIM
    qi = lax.broadcasted_iota(jnp.int32, (bq, 2 * bq), 0)
    kj = lax.broadcasted_iota(jnp.int32, (bq, 2 * bq), 1)
    lane = lax.broadcasted_iota(jnp.int32, (bq, DIL_GROUP_DIM), 1)
    head_masks = [(lane >= h * HEAD_DIM) & (lane < (h + 1) * HEAD_DIM) for h in range(n_heads)]
    units = [(c, bb) for c in range(cps) for bb in range(nbs)]

    qs, kcats, vcats, masks, starts = [], [], [], [], []
    for c, bb in units:
        i = jb * nbs + bb
        cols = slice(c * DIL_GROUP_DIM, (c + 1) * DIL_GROUP_DIM)
        p0 = pl.multiple_of(jnp.maximum(i - 1, 0) * bq, bq)
        c0 = pl.multiple_of(i * bq, bq)
        qs.append(q_ref[0, bb * bq:(bb + 1) * bq, cols])
        kcats.append(jnp.concatenate([k_ref[0, pl.ds(p0, bq), cols], k_ref[0, pl.ds(c0, bq), cols]], axis=0))
        vcats.append(jnp.concatenate([v_ref[0, pl.ds(p0, bq), cols], v_ref[0, pl.ds(c0, bq), cols]], axis=0))
        first = (1 - jnp.minimum(i, 1)) * (2 * bq)
        masks.append(((kj < bq) & (kj >= qi + first)) | ((kj >= bq) & ((kj - bq) <= qi)))
        starts.append((cg * cps + c) + dilation * bq * i)
    pairs = [(u, h) for u in range(len(units)) for h in range(n_heads)]
    sc = [jnp.where(masks[u], _dot_nt(jnp.where(head_masks[h], qs[u], jnp.zeros_like(qs[u])), kcats[u]), NEG_BIG)
          for u, h in pairs]
    mx = [jnp.max(x, axis=-1, keepdims=True) for x in sc]
    pr = [jnp.exp(sc[n] - mx[n]) for n in range(len(pairs))]
    den = [jnp.sum(x, axis=-1, keepdims=True) for x in pr]
    ov = [_dot(pr[n].astype(BF16), vcats[pairs[n][0]]) / den[n] for n in range(len(pairs))]
    for u in range(len(units)):
        acc = jnp.zeros((bq, DIL_GROUP_DIM), F32)
        lacc = jnp.zeros((bq, DIL_GROUP_DIM), F32)
        for h in range(n_heads):
            n = u * n_heads + h
            acc = jnp.where(head_masks[h], ov[n], acc)
            lacc = jnp.where(head_masks[h], mx[n] + jnp.log(den[n]), lacc)
        rows = pl.ds(starts[u], bq, stride=dilation) if dilation > 1 else pl.ds(pl.multiple_of(starts[u], bq), bq)
        for half in range(DIL_GROUP_DIM // LANES):
            o_ref[0, half, rows, :] = acc[:, half * LANES:(half + 1) * LANES]
            l_ref[0, half, rows, :] = lacc[:, half * LANES:(half + 1) * LANES]


def _dil_group(q, k, v, b, dilation):
    n = q.shape[0] // b
    s = n * dilation
    nb = n // ATTN_BLOCK
    nbs = min(nb, DIL_UNITS)
    cps = DIL_UNITS // nbs
    halves = DIL_GROUP_DIM // LANES
    view = lambda a: a.reshape(b, n, dilation * DIL_GROUP_DIM)
    qmap = lambda bi, ci, ji: (bi, ji, ci)
    kmap = lambda bi, ci, ji: (bi, 0, ci)
    omap = lambda bi, ci, ji: (bi, 0, 0, 0)
    return pl.pallas_call(
        functools.partial(_dil_kernel, dilation=dilation, cps=cps, nbs=nbs),
        grid=(b, dilation // cps, nb // nbs),
        in_specs=[pl.BlockSpec((1, nbs * ATTN_BLOCK, cps * DIL_GROUP_DIM), qmap),
                  pl.BlockSpec((1, n, cps * DIL_GROUP_DIM), kmap),
                  pl.BlockSpec((1, n, cps * DIL_GROUP_DIM), kmap)],
        out_specs=[pl.BlockSpec((1, halves, s, LANES), omap),
                   pl.BlockSpec((1, halves, s, LANES), omap)],
        out_shape=[jax.ShapeDtypeStruct((b, halves, s, LANES), F32),
                   jax.ShapeDtypeStruct((b, halves, s, LANES), F32)],
        compiler_params=_cparams(("parallel", "arbitrary", "arbitrary")),
        name=f"dil_attn_d{dilation}",
    )(view(q), view(k), view(v))


MEM_ROWS = 256

def _mem_kv_kernel(m_ref, ln_ref, w_ref, kn_ref, k_ref, v_ref):
    x = m_ref[0]
    h = x * lax.rsqrt(jnp.mean(x * x, axis=-1, keepdims=True) + NORM_EPS) * ln_ref[...]
    kv = _dot(h.astype(BF16), w_ref[...])
    for hd in range(MEM_DIM // MEM_HEAD_DIM):
        sl = slice(hd * MEM_HEAD_DIM, (hd + 1) * MEM_HEAD_DIM)
        kh = kv[:, sl]
        kh = kh * lax.rsqrt(jnp.mean(kh * kh, axis=-1, keepdims=True) + NORM_EPS) * kn_ref[...]
        k_ref[0, :, sl] = kh.astype(BF16)
    v_ref[0] = kv[:, MEM_DIM:].astype(BF16)


def _mem_kv(mem, ln_mem, w_mem_kv, mem_k_norm):
    b, m, d = mem.shape
    return pl.pallas_call(
        _mem_kv_kernel,
        grid=(b,),
        in_specs=[pl.BlockSpec((1, m, d), lambda i: (i, 0, 0)),
                  pl.BlockSpec((1, d), lambda i: (0, 0)),
                  pl.BlockSpec((d, 2 * MEM_DIM), lambda i: (0, 0)),
                  pl.BlockSpec((1, MEM_HEAD_DIM), lambda i: (0, 0))],
        out_specs=[pl.BlockSpec((1, m, MEM_DIM), lambda i: (i, 0, 0)),
                   pl.BlockSpec((1, m, MEM_DIM), lambda i: (i, 0, 0))],
        out_shape=[jax.ShapeDtypeStruct((b, m, MEM_DIM), BF16),
                   jax.ShapeDtypeStruct((b, m, MEM_DIM), BF16)],
        compiler_params=_cparams(("parallel",)),
        name="mem_kv",
    )(mem, ln_mem.reshape(1, d), w_mem_kv.astype(BF16), mem_k_norm.reshape(1, MEM_HEAD_DIM))


def _mem_attn_kernel(q_ref, k_ref, v_ref, o_ref):
    bm = q_ref.shape[1]
    units = [(r0, hd) for r0 in range(0, bm, MEM_ROWS) for hd in range(MEM_DIM // MEM_HEAD_DIM)]
    cols = lambda hd: slice(hd * MEM_HEAD_DIM, (hd + 1) * MEM_HEAD_DIM)
    sc = [_dot_nt(q_ref[0, r0:r0 + MEM_ROWS, cols(hd)], k_ref[0, :, cols(hd)]) * (MEM_HEAD_DIM ** -0.5)
          for r0, hd in units]
    mx = [jnp.max(x, axis=-1, keepdims=True) for x in sc]
    pr = [jnp.exp(sc[n] - mx[n]) for n in range(len(units))]
    den = [jnp.sum(x, axis=-1, keepdims=True) for x in pr]
    for n, (r0, hd) in enumerate(units):
        o_ref[0, r0:r0 + MEM_ROWS, cols(hd)] = _dot((pr[n] / den[n]).astype(BF16), v_ref[0, :, cols(hd)]).astype(o_ref.dtype)


def _mem_attn(mq, mk, mv, bm=512):
    b, s, _ = mq.shape
    m = mk.shape[1]
    return pl.pallas_call(
        _mem_attn_kernel,
        grid=(b, s // bm),
        in_specs=[pl.BlockSpec((1, bm, MEM_DIM), lambda bi, si: (bi, si, 0)),
                  pl.BlockSpec((1, m, MEM_DIM), lambda bi, si: (bi, 0, 0)),
                  pl.BlockSpec((1, m, MEM_DIM), lambda bi, si: (bi, 0, 0))],
        out_specs=pl.BlockSpec((1, bm, MEM_DIM), lambda bi, si: (bi, si, 0)),
        out_shape=jax.ShapeDtypeStruct((b, s, MEM_DIM), BF16),
        compiler_params=_cparams(("parallel", "parallel")),
        name="mem_attn",
    )(mq, mk, mv)


def _mix_kernel(x_ref, yr_ref, o0_ref, o1_ref, o2_ref, l0_ref, l1_ref, l2_ref, ym_ref, gt_ref,
                pr_ref, pd_ref, pm_ref, wo_ref, ln2_ref, wr_ref, br_ref,
                x1_ref, h2_ref, route_ref, cnt_ref):
    d = x_ref.shape[1]
    wide = lambda ref: jnp.concatenate([ref[0, hf] for hf in range(DIL_GROUP_DIM // LANES)], axis=1)
    l0, l1, l2 = wide(l0_ref), wide(l1_ref), wide(l2_ref)
    m = jnp.maximum(jnp.maximum(l0, l1), l2)
    e0, e1, e2 = jnp.exp(l0 - m), jnp.exp(l1 - m), jnp.exp(l2 - m)
    y_dil = (e0 * wide(o0_ref) + e1 * wide(o1_ref) + e2 * wide(o2_ref)) / (e0 + e1 + e2)
    mixed = (gt_ref[:, 0:d].astype(F32) * _dot(yr_ref[...], pr_ref[...])
             + gt_ref[:, d:2 * d].astype(F32) * _dot(y_dil.astype(BF16), pd_ref[...])
             + gt_ref[:, 2 * d:3 * d].astype(F32) * _dot(ym_ref[...], pm_ref[...]))
    x1 = x_ref[...] + _dot(mixed.astype(BF16), wo_ref[...])
    x1_ref[...] = x1
    h2 = x1 * lax.rsqrt(jnp.mean(x1 * x1, axis=-1, keepdims=True) + NORM_EPS) * ln2_ref[...]
    h2_ref[...] = _pack_bf16_pairs(h2)
    logits = _dot_x3(h2, wr_ref[...]) + br_ref[...]
    lane = lax.broadcasted_iota(jnp.int32, logits.shape, 1)
    lane_f = lane.astype(F32)
    route = jnp.zeros(logits.shape, F32)
    onehot = jnp.zeros(logits.shape, F32)
    vals = []
    for kq in range(TOP_K):
        mx = jnp.max(logits, axis=-1, keepdims=True)
        idx = jnp.min(jnp.where(logits == mx, lane_f, float(LANES)), axis=-1, keepdims=True)
        hit = lane_f == idx
        vals.append(mx)
        route = jnp.where(lane == kq, idx, route)
        onehot = jnp.where(hit, 1.0, onehot)
        logits = jnp.where(hit, -jnp.inf, logits)
    ex = [jnp.exp(vq - vals[0]) for vq in vals]
    den = ex[0] + ex[1] + ex[2] + ex[3]
    for kq in range(TOP_K):
        route = jnp.where(lane == TOP_K + kq, ex[kq] / den, route)
    route_ref[...] = route

    @pl.when(pl.program_id(0) == 0)
    def _():
        cnt_ref[...] = jnp.zeros_like(cnt_ref)

    cnt_ref[...] += jnp.sum(onehot, axis=0, keepdims=True)


def _mix(x2, y_rwkv, outs, lses, y_mem, gates, p_rwkv, p_dil, p_mem, w_out, ln2, w_router, b_router, bm=512):
    t, d = x2.shape
    wr = jnp.zeros((d, LANES), F32).at[:, :N_EXPERTS].set(w_router)
    br = jnp.full((1, LANES), -jnp.inf, F32).at[0, :N_EXPERTS].set(b_router)
    row = lambda w: pl.BlockSpec((bm, w), lambda i: (i, 0))
    const = lambda a: pl.BlockSpec(a.shape, lambda i: (0,) * a.ndim)
    tiles_per_seq = outs[0].shape[2] // bm
    dil = pl.BlockSpec((1, DIL_GROUP_DIM // LANES, bm, LANES),
                       lambda i: (i // tiles_per_seq, 0, i % tiles_per_seq, 0))
    weights = [p_rwkv.astype(BF16), p_dil.astype(BF16), p_mem.astype(BF16), w_out.astype(BF16),
               ln2.reshape(1, d), wr, br]
    return pl.pallas_call(
        _mix_kernel,
        grid=(t // bm,),
        in_specs=[row(d), row(RWKV_DIM)] + [dil] * 6 + [row(MEM_DIM), row(3 * d)]
                 + [const(a) for a in weights],
        out_specs=[row(d), row(d // 2), row(LANES), pl.BlockSpec((1, LANES), lambda i: (0, 0))],
        out_shape=[jax.ShapeDtypeStruct((t, d), F32), jax.ShapeDtypeStruct((t, d // 2), jnp.uint32),
                   jax.ShapeDtypeStruct((t, LANES), F32), jax.ShapeDtypeStruct((1, LANES), F32)],
        compiler_params=_cparams(("arbitrary",)),
        name="mix",
    )(x2, y_rwkv, *outs, *lses, y_mem, gates, *weights)


def _route_kernel(route_ref, pstart_ref, tri_ref, dest_ref, carry_ref):
    @pl.when(pl.program_id(0) == 0)
    def _():
        carry_ref[...] = jnp.zeros_like(carry_ref)

    route = route_ref[...]
    lane = lax.broadcasted_iota(jnp.int32, route.shape, 1)
    lane_f = lane.astype(F32)
    hits = [lane_f == route[:, kq:kq + 1] for kq in range(TOP_K)]
    onehot = jnp.zeros(route.shape, F32)
    for hq in hits:
        onehot = jnp.where(hq, 1.0, onehot)
    rank = _dot(tri_ref[...], onehot.astype(BF16)) + carry_ref[...]
    slot = pstart_ref[...] + rank
    dest = jnp.zeros(route.shape, jnp.int32)
    for kq in range(TOP_K):
        dk = jnp.sum(jnp.where(hits[kq], slot, 0.0), axis=-1, keepdims=True)
        dest = jnp.where(lane == kq, dk.astype(jnp.int32), dest)
    dest_ref[...] = dest
    carry_ref[...] += jnp.sum(onehot, axis=0, keepdims=True)


def _route(route, pstart, bm=1024):
    t = route.shape[0]
    i = jnp.arange(bm)
    tri = (i[None, :] < i[:, None]).astype(BF16)
    return pl.pallas_call(
        _route_kernel,
        grid=(t // bm,),
        in_specs=[pl.BlockSpec((bm, LANES), lambda i: (i, 0)),
                  pl.BlockSpec((1, LANES), lambda i: (0, 0)),
                  pl.BlockSpec((bm, bm), lambda i: (0, 0))],
        out_specs=pl.BlockSpec((bm, LANES), lambda i: (i, 0)),
        out_shape=jax.ShapeDtypeStruct((t, LANES), jnp.int32),
        scratch_shapes=[pltpu.VMEM((1, LANES), F32)],
        compiler_params=_cparams(("arbitrary",)),
        name="route",
    )(route, pstart, tri)


def _row_copy(src_ref, src_row, dst_ref, dst_row, sem):
    return pltpu.make_async_copy(src_ref.at[pl.ds(src_row, 1)], dst_ref.at[pl.ds(dst_row, 1)], sem)


def _dispatch_kernel(last_ref, dest_ref, h_ref, xs_ref, zero_ref, hbuf_ref, sem_s, sem_l, zsem, *, bm):
    i = pl.program_id(0)
    n = pl.num_programs(0)

    def load(tile, b):
        rows = pl.ds(pl.multiple_of(tile * bm, bm), bm)
        return pltpu.make_async_copy(h_ref.at[rows], hbuf_ref.at[b], sem_l.at[b])

    def wait_scatter(b):
        for kq in range(TOP_K):
            pltpu.make_async_copy(hbuf_ref.at[0], xs_ref.at[pl.ds(0, bm)], sem_s.at[b]).wait()

    @pl.when(i == 0)
    def _():
        load(0, 0).start()
        zero_ref[...] = jnp.zeros_like(zero_ref)

        def zero_copy(e):
            row = pl.multiple_of(last_ref[e] * EXPERT_ROWS, EXPERT_ROWS)
            return pltpu.make_async_copy(zero_ref, xs_ref.at[pl.ds(row, EXPERT_ROWS)], zsem)

        def z_issue(e, carry):
            @pl.when(last_ref[e] >= 0)
            def _():
                zero_copy(e).start()
            return carry

        def z_drain(e, carry):
            @pl.when(last_ref[e] >= 0)
            def _():
                zero_copy(e).wait()
            return carry

        lax.fori_loop(0, 2 * N_EXPERTS, z_issue, 0)
        lax.fori_loop(0, 2 * N_EXPERTS, z_drain, 0)

    @pl.when(i + 1 < n)
    def _():
        load(i + 1, (i + 1) % 3).start()

    load(i, i % 3).wait()

    for phase in range(6):
        @pl.when(i % 6 == phase)
        def _(phase=phase):
            def issue(j, carry):
                for kq in range(TOP_K):
                    _row_copy(hbuf_ref.at[phase % 3], j, xs_ref, dest_ref[j * TOP_K + kq],
                              sem_s.at[phase % 2]).start(priority=kq % 2)
                return carry

            lax.fori_loop(0, bm, issue, 0)

    @pl.when(i > 0)
    def _():
        wait_scatter((i - 1) % 2)

    @pl.when(i == n - 1)
    def _():
        wait_scatter(i % 2)


def _dispatch(last_blk, dest_flat, h2, n_slots, bm=512):
    t, d = h2.shape
    grid_spec = pltpu.PrefetchScalarGridSpec(
        num_scalar_prefetch=1,
        grid=(t // bm,),
        in_specs=[pl.BlockSpec((bm * TOP_K,), lambda i, lb: (i,), memory_space=pltpu.SMEM),
                  pl.BlockSpec(memory_space=pl.ANY)],
        out_specs=pl.BlockSpec(memory_space=pl.ANY),
        scratch_shapes=[pltpu.VMEM((EXPERT_ROWS, d), h2.dtype), pltpu.VMEM((3, bm, d), h2.dtype),
                        pltpu.SemaphoreType.DMA((2,)), pltpu.SemaphoreType.DMA((3,)), pltpu.SemaphoreType.DMA],
    )
    return pl.pallas_call(
        functools.partial(_dispatch_kernel, bm=bm),
        grid_spec=grid_spec,
        out_shape=jax.ShapeDtypeStruct((n_slots, d), h2.dtype),
        compiler_params=_cparams(("arbitrary",)),
        name="dispatch",
    )(last_blk, dest_flat, h2)


FF_CHUNK = 256
EXPERT_ROWS = 128
BLOCKS_PER_STEP = 8


def _expert_kernel(be_ref, par_ref, nxt_ref, nu_ref, xs_ref, w1_ref, b1_ref, w2_ref, b2_ref, sel_ref, ys_ref,
                   w1f_ref, w2f_ref, w1p_ref, w2b_ref, act_ref, sem):
    i = pl.program_id(0)
    d_ff2 = w1_ref.shape[2]
    half = FF_CHUNK // 2

    def weight_copies(expert, buf):
        return (pltpu.make_async_copy(w1_ref.at[expert], w1f_ref.at[buf], sem.at[0, buf]),
                pltpu.make_async_copy(w2_ref.at[expert], w2f_ref.at[buf], sem.at[1, buf]))

    @pl.when(i == 0)
    def _():
        for cp in weight_copies(be_ref[0], par_ref[0]):
            cp.start()

    def convert(blk, e, slot):
        for cp in weight_copies(e, slot):
            cp.wait()
        nxt = nxt_ref[blk]

        @pl.when(nxt >= 0)
        def _():
            for cp in weight_copies(nxt, 1 - slot):
                cp.start(priority=1)

        for c in range(d_ff2 // FF_CHUNK):
            sl = slice(c * FF_CHUNK, (c + 1) * FF_CHUNK)
            w1p_ref[:, sl] = _dot(w1f_ref[slot, :, sl].astype(BF16), sel_ref[...]).astype(BF16)
        w2b_ref[...] = w2f_ref[slot].astype(BF16)

    def compute(rows, e):
        n_rows = rows.stop - rows.start
        x = _unpack_bf16_pairs(xs_ref[rows, :])
        for c in range(d_ff2 // (2 * FF_CHUNK)):
            sl = slice(2 * c * FF_CHUNK, 2 * (c + 1) * FF_CHUNK)
            hb = _dot(x, w1p_ref[:, sl]) + b1_ref[e, :, sl]
            for j in range(2):
                x_glu = jnp.minimum(hb[:, j * FF_CHUNK:j * FF_CHUNK + half], SWIGLU_LIMIT)
                x_lin = jnp.clip(hb[:, j * FF_CHUNK + half:(j + 1) * FF_CHUNK], -SWIGLU_LIMIT, SWIGLU_LIMIT)
                act = x_glu * _sigmoid(SWIGLU_ALPHA * x_glu) * (x_lin + 1.0)
                act_ref[0:n_rows, (2 * c + j) * half:(2 * c + j + 1) * half] = act.astype(BF16)
        ys_ref[rows, :] = _dot(act_ref[0:n_rows, :], w2b_ref[...]) + b2_ref[e]

    blk0 = i * BLOCKS_PER_STEP
    n_used = nu_ref[0]

    def run(lo, n):
        first = blk0 + lo
        last = first + n - 1
        e = be_ref[first]
        rows = slice(lo * EXPERT_ROWS, (lo + n) * EXPERT_ROWS)
        fresh = jnp.logical_or(first == 0, be_ref[jnp.maximum(first - 1, 0)] != e)
        whole = jnp.logical_and(be_ref[last] == e, last < n_used)

        @pl.when(whole)
        def _():
            @pl.when(fresh)
            def _():
                convert(first, e, par_ref[first])

            compute(rows, e)

        if n > 1:
            @pl.when(jnp.logical_not(whole))
            def _():
                run(lo, n // 2)
                run(lo + n // 2, n // 2)
        else:
            @pl.when(jnp.logical_not(whole))
            def _():
                ys_ref[rows, :] = jnp.zeros((EXPERT_ROWS, ys_ref.shape[1]), F32)

    run(0, BLOCKS_PER_STEP)


def _chunk_deinterleave(a):
    lead = a.shape[:-1]
    a = a.reshape(lead + (a.shape[-1] // FF_CHUNK, FF_CHUNK // 2, 2))
    return jnp.swapaxes(a, -1, -2).reshape(lead + (-1,))


def _experts(block_e, parity, next_e, n_used, xs, w1, b1, w2, b2):
    n_slots, dx = xs.shape
    n_e, d, d_ff2 = w1.shape
    n_blocks = n_slots // EXPERT_ROWS
    i = jnp.arange(FF_CHUNK)
    src = jnp.where(i < FF_CHUNK // 2, 2 * i, 2 * (i - FF_CHUNK // 2) + 1)
    sel = (jnp.arange(FF_CHUNK)[:, None] == src[None, :]).astype(BF16)
    b1p = _chunk_deinterleave(b1).reshape(n_e, 1, d_ff2)
    step_rows = BLOCKS_PER_STEP * EXPERT_ROWS
    imap = lambda i, be, par, nxt, nu: (i, 0)
    xmap = lambda i, be, par, nxt, nu: (jnp.minimum(i, (nu[0] - 1) // BLOCKS_PER_STEP), 0)
    cmap = lambda i, be, par, nxt, nu: (0, 0, 0)
    grid_spec = pltpu.PrefetchScalarGridSpec(
        num_scalar_prefetch=4,
        grid=(n_blocks // BLOCKS_PER_STEP,),
        in_specs=[pl.BlockSpec((step_rows, dx), xmap),
                  pl.BlockSpec(memory_space=pl.ANY),
                  pl.BlockSpec((n_e, 1, d_ff2), cmap),
                  pl.BlockSpec(memory_space=pl.ANY),
                  pl.BlockSpec((n_e, 1, d), cmap),
                  pl.BlockSpec((FF_CHUNK, FF_CHUNK), lambda i, be, par, nxt, nu: (0, 0))],
        out_specs=pl.BlockSpec((step_rows, d), imap),
        scratch_shapes=[pltpu.VMEM((2, d, d_ff2), F32), pltpu.VMEM((2, d_ff2 // 2, d), F32),
                        pltpu.VMEM((d, d_ff2), BF16), pltpu.VMEM((d_ff2 // 2, d), BF16),
                        pltpu.VMEM((BLOCKS_PER_STEP * EXPERT_ROWS, d_ff2 // 2), BF16),
                        pltpu.SemaphoreType.DMA((2, 2))],
    )
    return pl.pallas_call(
        _expert_kernel,
        grid_spec=grid_spec,
        out_shape=jax.ShapeDtypeStruct((n_slots, d), F32),
        compiler_params=_cparams(("arbitrary",), vmem=EXPERT_VMEM_LIMIT),
        name="experts",
    )(block_e, parity, next_e, n_used, xs, w1, b1p, w2, b2.reshape(n_e, 1, d), sel)


def _combine_kernel(dest_ref, dnext_ref, x1_ref, route_ref, ys_ref, o_ref, buf_ref, sem, *, bm):
    i = pl.program_id(0)
    n = pl.num_programs(0)
    slot = i % 2

    def gather(d_ref, buf):
        def issue(g, carry):
            base = pl.multiple_of(g * SUBLANES, SUBLANES)
            for r in range(SUBLANES):
                for kq in range(TOP_K):
                    _row_copy(ys_ref, d_ref[(base + r) * TOP_K + kq], buf_ref.at[buf, kq], base + r,
                              sem.at[buf]).start(priority=kq % 2)
            return carry

        lax.fori_loop(0, bm // SUBLANES, issue, 0)

    @pl.when(i == 0)
    def _():
        gather(dest_ref, 0)

    for nxt in range(2):
        @pl.when(jnp.logical_and(i + 1 < n, (i + 1) % 2 == nxt))
        def _(nxt=nxt):
            gather(dnext_ref, nxt)

    for kq in range(TOP_K):
        pltpu.make_async_copy(ys_ref.at[pl.ds(0, bm)], buf_ref.at[slot, kq], sem.at[slot]).wait()
    acc = x1_ref[...]
    for kq in range(TOP_K):
        acc = acc + route_ref[:, TOP_K + kq:TOP_K + kq + 1] * buf_ref[slot, kq]
    o_ref[...] = acc


def _combine(dest_flat, x1, route, ys, bm=512):
    t, d = x1.shape
    n = t // bm
    return pl.pallas_call(
        functools.partial(_combine_kernel, bm=bm),
        grid=(n,),
        in_specs=[pl.BlockSpec((bm * TOP_K,), lambda i: (i,), memory_space=pltpu.SMEM),
                  pl.BlockSpec((bm * TOP_K,), lambda i: (jnp.minimum(i + 1, n - 1),), memory_space=pltpu.SMEM),
                  pl.BlockSpec((bm, d), lambda i: (i, 0)),
                  pl.BlockSpec((bm, LANES), lambda i: (i, 0)),
                  pl.BlockSpec(memory_space=pl.ANY)],
        out_specs=pl.BlockSpec((bm, d), lambda i: (i, 0)),
        out_shape=jax.ShapeDtypeStruct((t, d), F32),
        scratch_shapes=[pltpu.VMEM((2, TOP_K, bm, d), F32), pltpu.SemaphoreType.DMA((2,))],
        compiler_params=_cparams(("arbitrary",)),
        name="combine",
    )(dest_flat, dest_flat, x1, route, ys)


def _layer(x, mem, ln1, w_in, token_mu, rwkv_w0, rwkv_w2, rwkv_a0, rwkv_a2, rwkv_g2,
           rwkv_k_k, rwkv_k_a, rwkv_r_k, rwkv_ln_w, rwkv_ln_b, dil_q_norm, dil_k_norm,
           ln_mem, w_mem_kv, mem_q_norm, mem_k_norm, p_rwkv, p_dil, p_mem, w_out,
           ln2, w_router, b_router, w1, b1, w2, b2):
    b, s, d = x.shape
    t = b * s
    x2 = x.reshape(t, d)
    zr, dil_qkv, mq, gates = _in_proj(x2, ln1, w_in, dil_q_norm, dil_k_norm, mem_q_norm)
    y_rwkv = _rwkv(zr.reshape(b, s, RWKV_IN), token_mu, rwkv_w0, rwkv_w2, rwkv_a0, rwkv_a2, rwkv_g2,
                   rwkv_k_k, rwkv_k_a, rwkv_r_k.reshape(-1), rwkv_ln_w, rwkv_ln_b)
    outs, lses = [], []
    for g, (_, dilation) in enumerate(DIL_PATTERNS):
        o, l = _dil_group(*dil_qkv[3 * g:3 * g + 3], b, dilation)
        outs.append(o)
        lses.append(l)
    mk, mv = _mem_kv(mem, ln_mem, w_mem_kv, mem_k_norm)
    y_mem = _mem_attn(mq.reshape(b, s, MEM_DIM), mk, mv)
    x1, h2, route, counts = _mix(x2, y_rwkv.reshape(t, RWKV_DIM), outs, lses, y_mem.reshape(t, MEM_DIM), gates,
                                 p_rwkv, p_dil, p_mem, w_out, ln2, w_router, b_router)

    counts = counts[0, :N_EXPERTS].astype(jnp.int32)
    nblk = (counts + EXPERT_ROWS - 1) // EXPERT_ROWS
    bend = jnp.cumsum(nblk)
    pstart = ((bend - nblk) * EXPERT_ROWS).astype(F32)
    pstart = jnp.zeros((1, LANES), F32).at[0, :N_EXPERTS].set(pstart)
    n_blocks = (t * TOP_K) // EXPERT_ROWS + N_EXPERTS
    block_e = jnp.sum(bend[None, :] <= jnp.arange(n_blocks, dtype=jnp.int32)[:, None], axis=1)
    block_e = jnp.minimum(block_e, N_EXPERTS - 1).astype(jnp.int32)
    n_used = bend[-1:].astype(jnp.int32)
    used = nblk > 0
    eids = jnp.arange(N_EXPERTS, dtype=jnp.int32)
    ordinal = jnp.cumsum(used.astype(jnp.int32)) - 1
    later = jnp.where(used[None, :] & (eids[None, :] > eids[:, None]), eids[None, :], N_EXPERTS)
    next_used = jnp.min(later, axis=1)
    next_used = jnp.where(next_used < N_EXPERTS, next_used, -1).astype(jnp.int32)
    is_e = (block_e[:, None] == eids[None, :]).astype(jnp.int32)
    parity = (jnp.sum(is_e * ordinal[None, :], axis=1) % 2).astype(jnp.int32)
    next_e = jnp.sum(is_e * next_used[None, :], axis=1).astype(jnp.int32)
    tail = n_used[0] + eids
    last_blk = jnp.concatenate([jnp.where(used, bend - 1, -1), jnp.where(tail < n_blocks, tail, -1)]).astype(jnp.int32)

    dest = _route(route, pstart)[:, :TOP_K].reshape(-1)
    xs = _dispatch(last_blk, dest, h2, n_blocks * EXPERT_ROWS)
    ys = _experts(block_e, parity, next_e, n_used, xs, w1, b1, w2, b2)
    out = _combine(dest, x1, route, ys)
    return out.reshape(b, s, d)


def kernel(x, mem, ln1, w_in, token_mu, rwkv_w0, rwkv_w2, rwkv_a0, rwkv_a2, rwkv_g2, rwkv_k_k, rwkv_k_a, rwkv_r_k, rwkv_ln_w, rwkv_ln_b, dil_q_norm, dil_k_norm, ln_mem, w_mem_kv, mem_q_norm, mem_k_norm, p_rwkv, p_dil, p_mem, w_out, ln2, w_router, b_router, w1, b1, w2, b2):
    params = (ln1, w_in, token_mu, rwkv_w0, rwkv_w2, rwkv_a0, rwkv_a2, rwkv_g2, rwkv_k_k, rwkv_k_a,
              rwkv_r_k, rwkv_ln_w, rwkv_ln_b, dil_q_norm, dil_k_norm, ln_mem, w_mem_kv, mem_q_norm,
              mem_k_norm, p_rwkv, p_dil, p_mem, w_out, ln2, w_router, b_router, w1, b1, w2, b2)
    for l in range(ln1.shape[0]):
        x = _layer(x, mem, *[p[l] for p in params])
    return x
```

```python
import functools

import jax
import jax.numpy as jnp
from jax import lax
from jax.experimental import pallas as pl
from jax.experimental.pallas import tpu as pltpu

F32 = jnp.float32
BF16 = jnp.bfloat16

NORM_EPS = 1e-5
HEAD_DIM = 64
RWKV_DIM = 512
RWKV_IN = 1792
GN_EPS = HEAD_DIM * 1e-5
DIL_DIM = 768
DIL_GROUP_DIM = 256
DIL_PATTERNS = ((128, 1), (512, 4), (2048, 16))
ATTN_BLOCK = 128
MEM_DIM = 512
MEM_HEAD_DIM = 128
N_EXPERTS = 32
TOP_K = 4
SWIGLU_ALPHA = 1.702
SWIGLU_LIMIT = 7.0
LANES = 128
SUBLANES = 8
CHUNK = 64
NEG_BIG = -1e30
VMEM_LIMIT = 48 * 1024 * 1024
EXPERT_VMEM_LIMIT = 56 * 1024 * 1024


def _dot(a, b):
    return jnp.dot(a, b, preferred_element_type=F32)


def _dot_nt(a, b):
    return lax.dot_general(a, b, (((1,), (1,)), ((), ())), preferred_element_type=F32)


def _split2(x):
    hi = x.astype(BF16)
    lo = (x - hi.astype(F32)).astype(BF16)
    return hi, lo


def _split3(x):
    hi = x.astype(BF16)
    r = x - hi.astype(F32)
    lo = r.astype(BF16)
    lo2 = (r - lo.astype(F32)).astype(BF16)
    return hi, lo, lo2


def _dot_rhs3(w_bf16, x):
    hi, lo, lo2 = _split3(x)
    return _dot(w_bf16, hi) + _dot(w_bf16, lo) + _dot(w_bf16, lo2)


def _dot_x3(a, b):
    ah, al = _split2(a)
    bh, bl = _split2(b)
    return _dot(ah, bh) + _dot(al, bh) + _dot(ah, bl)


def _sigmoid(x):
    return 1.0 / (1.0 + jnp.exp(-x))


def _pack_bf16_pairs(x):
    half = x.shape[1] // 2
    bits = lambda v: pltpu.bitcast(v.astype(BF16).astype(F32), jnp.uint32)
    return (bits(x[:, :half]) >> 16) | (bits(x[:, half:]) & jnp.uint32(0xFFFF0000))


def _unpack_bf16_pairs(w):
    lo = pltpu.bitcast(w << 16, F32)
    hi = pltpu.bitcast(w & jnp.uint32(0xFFFF0000), F32)
    return jnp.concatenate([lo, hi], axis=1).astype(BF16)


def _cparams(sem, vmem=VMEM_LIMIT):
    return pltpu.CompilerParams(dimension_semantics=sem, vmem_limit_bytes=vmem)


IN_CHUNK = 256
PERM_ROWS = 256
N_ZR = RWKV_IN // IN_CHUNK
N_DIL = 3 * DIL_DIM // IN_CHUNK
N_MQ = MEM_DIM // IN_CHUNK
N_GATE = 3 * 1024 // IN_CHUNK


def _in_proj_kernel(x_ref, ln_ref, w_ref, qg_ref, kg_ref, mg_ref, p4_ref, p16_ref,
                    zr_ref, *rest, bm):
    dil_refs, (mq_ref, gt_ref) = rest[:N_DIL], rest[N_DIL:]
    perm_refs = (None, p4_ref, p16_ref)
    x = x_ref[...]
    h = x * lax.rsqrt(jnp.mean(x * x, axis=-1, keepdims=True) + NORM_EPS) * ln_ref[...]
    hb = h.astype(BF16)

    lane = lax.broadcasted_iota(jnp.int32, (x.shape[0], LANES), 1)

    def seg_rms(z, seg, gain):
        z2 = z * z
        cols = []
        for cb in range(IN_CHUNK // LANES):
            blk = z2[:, cb * LANES:(cb + 1) * LANES]
            ss = jnp.zeros_like(blk)
            for sg in range(LANES // seg):
                m = (lane >= sg * seg) & (lane < (sg + 1) * seg)
                ss = jnp.where(m, jnp.sum(jnp.where(m, blk, 0.0), axis=-1, keepdims=True), ss)
            cols.append(ss)
        ss = jnp.concatenate(cols, axis=1)
        return z * lax.rsqrt(ss * (1.0 / seg) + NORM_EPS) * gain

    for c in range(N_ZR + N_DIL + N_MQ + N_GATE):
        z = _dot(hb, w_ref[:, c * IN_CHUNK:(c + 1) * IN_CHUNK])
        if c < N_ZR:
            zr_ref[:, c * IN_CHUNK:(c + 1) * IN_CHUNK] = z
            continue
        d = c - N_ZR
        if d < N_DIL:
            which, g = divmod(d, 3)
            if which == 0:
                z = seg_rms(z, HEAD_DIM, qg_ref[...])
            elif which == 1:
                z = seg_rms(z, HEAD_DIM, kg_ref[...])
            zb = z.astype(BF16)
            o_ref = dil_refs[g * 3 + which]
            dilation = DIL_PATTERNS[g][1]
            if dilation == 1:
                o_ref[...] = zb
            else:
                rows = PERM_ROWS // dilation
                for sub in range(bm // PERM_ROWS):
                    zp = _dot(perm_refs[g][...], zb[sub * PERM_ROWS:(sub + 1) * PERM_ROWS, :]).astype(BF16)
                    for r in range(dilation):
                        o_ref[sub * rows:(sub + 1) * rows, r * IN_CHUNK:(r + 1) * IN_CHUNK] = zp[r * rows:(r + 1) * rows, :]
            continue
        d -= N_DIL
        if d < N_MQ:
            mq_ref[:, d * IN_CHUNK:(d + 1) * IN_CHUNK] = seg_rms(z, MEM_HEAD_DIM, mg_ref[...]).astype(BF16)
            continue
        d -= N_MQ
        gt_ref[:, d * IN_CHUNK:(d + 1) * IN_CHUNK] = _sigmoid(z).astype(BF16)


def _class_perm(bm, dilation):
    i = jnp.arange(bm)
    src = (i % (bm // dilation)) * dilation + i // (bm // dilation)
    return (src[:, None] == i[None, :]).astype(BF16)


def _in_proj(x2, ln1, w_in, dil_q_norm, dil_k_norm, mem_q_norm, bm=512):
    t, d = x2.shape
    n_cols = w_in.shape[1]
    wb = w_in.astype(BF16)
    qg = (jnp.tile(dil_q_norm, IN_CHUNK // HEAD_DIM) * (HEAD_DIM ** -0.5)).reshape(1, IN_CHUNK)
    kg = jnp.tile(dil_k_norm, IN_CHUNK // HEAD_DIM).reshape(1, IN_CHUNK)
    mg = jnp.tile(mem_q_norm, IN_CHUNK // MEM_HEAD_DIM).reshape(1, IN_CHUNK)
    p4 = _class_perm(PERM_ROWS, DIL_PATTERNS[1][1])
    p16 = _class_perm(PERM_ROWS, DIL_PATTERNS[2][1])
    row = lambda w: pl.BlockSpec((bm, w), lambda i: (i, 0))
    const = lambda a: pl.BlockSpec(a.shape, lambda i: (0,) * a.ndim)
    dil_specs, dil_shapes = [], []
    for _, dilation in DIL_PATTERNS:
        for _ in range(3):
            dil_specs.append(pl.BlockSpec((bm // dilation, dilation * DIL_GROUP_DIM), lambda i: (i, 0)))
            dil_shapes.append(jax.ShapeDtypeStruct((t // dilation, dilation * DIL_GROUP_DIM), BF16))
    outs = pl.pallas_call(
        functools.partial(_in_proj_kernel, bm=bm),
        grid=(t // bm,),
        in_specs=[row(d), pl.BlockSpec((1, d), lambda i: (0, 0)),
                  pl.BlockSpec((d, n_cols), lambda i: (0, 0), pipeline_mode=pl.Buffered(1)),
                  const(qg), const(kg), const(mg), const(p4), const(p16)],
        out_specs=[row(RWKV_IN)] + dil_specs + [row(MEM_DIM), row(3 * d)],
        out_shape=[jax.ShapeDtypeStruct((t, RWKV_IN), F32)] + dil_shapes
                  + [jax.ShapeDtypeStruct((t, MEM_DIM), BF16), jax.ShapeDtypeStruct((t, 3 * d), BF16)],
        compiler_params=_cparams(("parallel",)),
        name="in_proj",
    )(x2, ln1.reshape(1, d), wb, qg, kg, mg, p4, p16)
    return outs[0], outs[1:1 + N_DIL], outs[1 + N_DIL], outs[2 + N_DIL]


def _rwkv_kernel(z_ref, mu_ref, w0_ref, a0_ref, wl_ref, kk_ref, ka_ref, rk_ref,
                 lnw_ref, lnb_ref, tri_ref, y_ref, carry_ref, state_ref, ybuf_ref, *, tt):
    s_idx = pl.program_id(1)

    @pl.when(s_idx == 0)
    def _():
        carry_ref[...] = jnp.zeros_like(carry_ref)
        state_ref[...] = jnp.zeros_like(state_ref)

    z = z_ref[0]
    rows = lax.broadcasted_iota(jnp.int32, z.shape, 0)
    prev = jnp.where(rows == 0, carry_ref[...], pltpu.roll(z, 1, axis=0))
    carry_ref[...] = z[tt - 1:tt, :]
    zl = z + (prev - z) * mu_ref[...]
    r = zl[:, 0:RWKV_DIM]
    k = zl[:, RWKV_DIM:2 * RWKV_DIM]
    v = zl[:, 2 * RWKV_DIM:3 * RWKV_DIM]
    zlo = zl[:, 3 * RWKV_DIM:RWKV_IN]
    lcol = lax.broadcasted_iota(jnp.int32, zlo.shape, 1)
    feat = jnp.where(lcol < 64, jnp.tanh(zlo), jnp.where(lcol < 128, zlo, _sigmoid(zlo)))
    lora = _dot(feat.astype(BF16), wl_ref[...])
    nu = -(w0_ref[...] + lora[:, 0:RWKV_DIM])
    softplus = jnp.maximum(nu, 0.0) + jnp.log(1.0 + jnp.exp(-jnp.abs(nu)))
    lw = -jnp.exp(-softplus - 0.5)
    a = _sigmoid(a0_ref[...] + lora[:, RWKV_DIM:2 * RWKV_DIM])
    g = lora[:, 2 * RWKV_DIM:3 * RWKV_DIM]

    n_pairs = RWKV_DIM // LANES
    n_chunks = tt // CHUNK
    head_lo = lax.broadcasted_iota(jnp.int32, (tt, LANES), 1) < HEAD_DIM

    def seg_sum(x):
        parts = []
        for p in range(n_pairs):
            xp = x[:, p * LANES:(p + 1) * LANES]
            lo = jnp.sum(jnp.where(head_lo, xp, 0.0), axis=-1, keepdims=True)
            hi = jnp.sum(jnp.where(head_lo, 0.0, xp), axis=-1, keepdims=True)
            parts.append(jnp.where(head_lo, lo, hi))
        return jnp.concatenate(parts, axis=1)

    kk = k * kk_ref[...]
    kk = kk / jnp.maximum(jnp.sqrt(seg_sum(kk * kk)), 1e-12)
    k2 = k * (1.0 + (a - 1.0) * ka_ref[...])
    a_s = -kk
    b_s = kk * a

    cum = _dot_rhs3(tri_ref[...], lw)
    tot = jnp.concatenate(
        [jnp.broadcast_to(cum[(c + 1) * CHUNK - 1:(c + 1) * CHUNK, :], (CHUNK, RWKV_DIM)) for c in range(n_chunks)],
        axis=0)
    e_neg = jnp.exp(-cum)
    e_end = jnp.exp(tot - cum)
    at = (a_s * jnp.exp(cum - lw)).astype(BF16)
    rt = (r * jnp.exp(cum)).astype(BF16)
    bt = (b_s * e_neg).astype(BF16)
    kt = (k2 * e_neg).astype(BF16)
    be = (b_s * e_end).astype(BF16)
    ke = (k2 * e_end).astype(BF16)
    vb = v.astype(BF16)
    e_tot = jnp.exp(tot)

    r128 = lax.broadcasted_iota(jnp.int32, (LANES, LANES), 0)
    c128 = lax.broadcasted_iota(jnp.int32, (LANES, LANES), 1)
    lane_lo = lax.broadcasted_iota(jnp.int32, (CHUNK, LANES), 1) < HEAD_DIM
    stril = c128 < r128
    tril = c128 <= r128
    eye = (c128 == r128).astype(F32)
    same16 = (r128 // 16) == (c128 // 16)
    same32 = (r128 // 32) == (c128 // 32)
    off16 = same32 & jnp.logical_not(same16)
    off32 = jnp.logical_not(same32)

    def bdiag(xp):
        zero = jnp.zeros_like(xp)
        return jnp.concatenate([jnp.where(lane_lo, xp, zero), jnp.where(lane_lo, zero, xp)], axis=0)

    units = [(ci, p) for ci in range(n_chunks) for p in range(n_pairs)]
    pick = lambda arr, u: arr[u[0] * CHUNK:(u[0] + 1) * CHUNK, u[1] * LANES:(u[1] + 1) * LANES]
    cat0 = lambda xs: jnp.concatenate(xs, axis=0)
    cat1 = lambda xs: jnp.concatenate(xs, axis=1)
    zero_b = jnp.zeros((LANES, LANES), BF16)

    at_b = [bdiag(pick(at, u)) for u in units]
    v_b = [bdiag(pick(vb, u)) for u in units]
    be_b = [bdiag(pick(be, u)) for u in units]
    ke_b = [bdiag(pick(ke, u)) for u in units]
    rt_b = [bdiag(pick(rt, u)) for u in units]
    a_ab, a_ak, a_rbk = [], [], []
    for i, u in enumerate(units):
        mq = _dot_nt(cat0([at_b[i], rt_b[i]]), cat0([bdiag(pick(bt, u)), bdiag(pick(kt, u))]))
        a_ab.append(jnp.where(stril, mq[:LANES, :LANES], 0.0).astype(BF16))
        a_ak.append(jnp.where(stril, mq[:LANES, LANES:], 0.0).astype(BF16))
        a_rbk.append(cat1([jnp.where(tril, mq[LANES:, :LANES], 0.0), jnp.where(tril, mq[LANES:, LANES:], 0.0)]).astype(BF16))
    w_b = [_dot(a_ak[i], v_b[i]).astype(BF16) for i in range(len(units))]

    d1 = [jnp.where(same16, x, jnp.zeros_like(x)) for x in a_ab]
    xs = [eye + d.astype(F32) for d in d1]
    d2 = [_dot(d, d).astype(BF16) for d in d1]
    t_ = [_dot(d2[i], cat1([xs[i].astype(BF16), d2[i]])) for i in range(len(units))]
    xs = [xs[i] + t_[i][:, :LANES] for i in range(len(units))]
    d4 = [t[:, LANES:].astype(BF16) for t in t_]
    t_ = [_dot(d4[i], cat1([xs[i].astype(BF16), d4[i]])) for i in range(len(units))]
    xs = [xs[i] + t_[i][:, :LANES] for i in range(len(units))]
    d8 = [t[:, LANES:].astype(BF16) for t in t_]
    xs = [xs[i] + _dot(d8[i], xs[i].astype(BF16)) for i in range(len(units))]
    for off in (off16, off32):
        xb = [x.astype(BF16) for x in xs]
        g_ = [_dot(jnp.where(off, a_ab[i], jnp.zeros_like(a_ab[i])), xb[i]).astype(BF16) for i in range(len(units))]
        xs = [xs[i] + _dot(xb[i], g_[i]) for i in range(len(units))]

    pq = [_dot(xs[i].astype(BF16), cat1([at_b[i], w_b[i]])) for i in range(len(units))]
    ry = [_dot(a_rbk[i], cat0([pq[i].astype(BF16), cat1([zero_b, v_b[i]])])) for i in range(len(units))]
    r2 = [(rt_b[i].astype(F32) + ry[i][:, :LANES]).astype(BF16) for i in range(len(units))]
    m_c = [_dot(pq[i][:, :LANES].T.astype(BF16), be_b[i]) for i in range(len(units))]
    n_c = [_dot(cat1([pq[i][:, LANES:].T.astype(BF16), v_b[i].astype(F32).T.astype(BF16)]), cat0([be_b[i], ke_b[i]]))
           for i in range(len(units))]

    state = [state_ref[p] for p in range(n_pairs)]
    for i, (ci, p) in enumerate(units):
        s0 = state[p]
        s_b = s0.astype(BF16)
        y = _dot_nt(r2[i], s_b) + ry[i][:, LANES:]
        e_row = e_tot[ci * CHUNK:ci * CHUNK + 1, p * LANES:(p + 1) * LANES]
        state[p] = s0 * e_row + _dot(s_b, m_c[i].astype(BF16)) + n_c[i]
        ybuf_ref[ci * CHUNK:(ci + 1) * CHUNK, p * LANES:(p + 1) * LANES] = jnp.where(lane_lo, y[:CHUNK], y[CHUNK:])
    for p in range(n_pairs):
        state_ref[p] = state[p]

    y = ybuf_ref[...]
    mean = seg_sum(y) * (1.0 / HEAD_DIM)
    yc = y - mean
    var = seg_sum(yc * yc) * (1.0 / HEAD_DIM)
    yn = yc * lax.rsqrt(var + GN_EPS) * lnw_ref[...] + lnb_ref[...]
    bonus = seg_sum(r * k2 * rk_ref[...]) * v
    y_ref[0] = ((yn + bonus) * g).astype(y_ref.dtype)


def _rwkv(zr, token_mu, w0, w2, a0, a2, g2, k_k, k_a, r_k, ln_w, ln_b, tt=256):
    b, s, _ = zr.shape
    i = jnp.arange(tt)
    same = (i[:, None] // CHUNK) == (i[None, :] // CHUNK)
    tri = (same & (i[None, :] <= i[:, None])).astype(BF16)
    vec = lambda a: a.reshape(1, -1)
    wl = jnp.zeros((RWKV_IN - 3 * RWKV_DIM, 3 * RWKV_DIM), F32)
    wl = wl.at[0:64, 0:RWKV_DIM].set(w2).at[64:128, RWKV_DIM:2 * RWKV_DIM].set(a2)
    wl = wl.at[128:256, 2 * RWKV_DIM:3 * RWKV_DIM].set(g2).astype(BF16)
    params = [vec(token_mu), vec(w0), vec(a0), wl, vec(k_k), vec(k_a), vec(r_k),
              vec(ln_w), vec(ln_b), tri]
    const = lambda a: pl.BlockSpec(a.shape, lambda bi, si: (0,) * a.ndim)
    return pl.pallas_call(
        functools.partial(_rwkv_kernel, tt=tt),
        grid=(b, s // tt),
        in_specs=[pl.BlockSpec((1, tt, RWKV_IN), lambda bi, si: (bi, si, 0))] + [const(a) for a in params],
        out_specs=pl.BlockSpec((1, tt, RWKV_DIM), lambda bi, si: (bi, si, 0)),
        out_shape=jax.ShapeDtypeStruct((b, s, RWKV_DIM), BF16),
        scratch_shapes=[pltpu.VMEM((1, RWKV_IN), F32),
                        pltpu.VMEM((RWKV_DIM // LANES, LANES, LANES), F32),
                        pltpu.VMEM((tt, RWKV_DIM), F32)],
        compiler_params=_cparams(("parallel", "arbitrary")),
        name="rwkv",
    )(zr, *params)


DIL_UNITS = 8


def _dil_kernel(q_ref, k_ref, v_ref, o_ref, l_ref, *, dilation, cps, nbs):
    cg = pl.program_id(1)
    jb = pl.program_id(2)
    bq = ATTN_BLOCK
    n_heads = DIL_GROUP_DIM // HEAD_DIM
    qi = lax.broadcasted_iota(jnp.int32, (bq, 2 * bq), 0)
    kj = lax.broadcasted_iota(jnp.int32, (bq, 2 * bq), 1)
    lane = lax.broadcasted_iota(jnp.int32, (bq, DIL_GROUP_DIM), 1)
    head_masks = [(lane >= h * HEAD_DIM) & (lane < (h + 1) * HEAD_DIM) for h in range(n_heads)]
    units = [(c, bb) for c in range(cps) for bb in range(nbs)]

    qs, kcats, vcats, masks, starts = [], [], [], [], []
    for c, bb in units:
        i = jb * nbs + bb
        cols = slice(c * DIL_GROUP_DIM, (c + 1) * DIL_GROUP_DIM)
        p0 = pl.multiple_of(jnp.maximum(i - 1, 0) * bq, bq)
        c0 = pl.multiple_of(i * bq, bq)
        qs.append(q_ref[0, bb * bq:(bb + 1) * bq, cols])
        kcats.append(jnp.concatenate([k_ref[0, pl.ds(p0, bq), cols], k_ref[0, pl.ds(c0, bq), cols]], axis=0))
        vcats.append(jnp.concatenate([v_ref[0, pl.ds(p0, bq), cols], v_ref[0, pl.ds(c0, bq), cols]], axis=0))
        first = (1 - jnp.minimum(i, 1)) * (2 * bq)
        masks.append(((kj < bq) & (kj >= qi + first)) | ((kj >= bq) & ((kj - bq) <= qi)))
        starts.append((cg * cps + c) + dilation * bq * i)
    pairs = [(u, h) for u in range(len(units)) for h in range(n_heads)]
    sc = [jnp.where(masks[u], _dot_nt(jnp.where(head_masks[h], qs[u], jnp.zeros_like(qs[u])), kcats[u]), NEG_BIG)
          for u, h in pairs]
    mx = [jnp.max(x, axis=-1, keepdims=True) for x in sc]
    pr = [jnp.exp(sc[n] - mx[n]) for n in range(len(pairs))]
    den = [jnp.sum(x, axis=-1, keepdims=True) for x in pr]
    ov = [_dot(pr[n].astype(BF16), vcats[pairs[n][0]]) / den[n] for n in range(len(pairs))]
    for u in range(len(units)):
        acc = jnp.zeros((bq, DIL_GROUP_DIM), F32)
        lacc = jnp.zeros((bq, DIL_GROUP_DIM), F32)
        for h in range(n_heads):
            n = u * n_heads + h
            acc = jnp.where(head_masks[h], ov[n], acc)
            lacc = jnp.where(head_masks[h], mx[n] + jnp.log(den[n]), lacc)
        rows = pl.ds(starts[u], bq, stride=dilation) if dilation > 1 else pl.ds(pl.multiple_of(starts[u], bq), bq)
        for half in range(DIL_GROUP_DIM // LANES):
            o_ref[0, half, rows, :] = acc[:, half * LANES:(half + 1) * LANES]
            l_ref[0, half, rows, :] = lacc[:, half * LANES:(half + 1) * LANES]


def _dil_group(q, k, v, b, dilation):
    n = q.shape[0] // b
    s = n * dilation
    nb = n // ATTN_BLOCK
    nbs = min(nb, DIL_UNITS)
    cps = DIL_UNITS // nbs
    halves = DIL_GROUP_DIM // LANES
    view = lambda a: a.reshape(b, n, dilation * DIL_GROUP_DIM)
    qmap = lambda bi, ci, ji: (bi, ji, ci)
    kmap = lambda bi, ci, ji: (bi, 0, ci)
    omap = lambda bi, ci, ji: (bi, 0, 0, 0)
    return pl.pallas_call(
        functools.partial(_dil_kernel, dilation=dilation, cps=cps, nbs=nbs),
        grid=(b, dilation // cps, nb // nbs),
        in_specs=[pl.BlockSpec((1, nbs * ATTN_BLOCK, cps * DIL_GROUP_DIM), qmap),
                  pl.BlockSpec((1, n, cps * DIL_GROUP_DIM), kmap),
                  pl.BlockSpec((1, n, cps * DIL_GROUP_DIM), kmap)],
        out_specs=[pl.BlockSpec((1, halves, s, LANES), omap),
                   pl.BlockSpec((1, halves, s, LANES), omap)],
        out_shape=[jax.ShapeDtypeStruct((b, halves, s, LANES), F32),
                   jax.ShapeDtypeStruct((b, halves, s, LANES), F32)],
        compiler_params=_cparams(("parallel", "arbitrary", "arbitrary")),
        name=f"dil_attn_d{dilation}",
    )(view(q), view(k), view(v))


MEM_ROWS = 256

def _mem_kv_kernel(m_ref, ln_ref, w_ref, kn_ref, k_ref, v_ref):
    x = m_ref[0]
    h = x * lax.rsqrt(jnp.mean(x * x, axis=-1, keepdims=True) + NORM_EPS) * ln_ref[...]
    kv = _dot(h.astype(BF16), w_ref[...])
    for hd in range(MEM_DIM // MEM_HEAD_DIM):
        sl = slice(hd * MEM_HEAD_DIM, (hd + 1) * MEM_HEAD_DIM)
        kh = kv[:, sl]
        kh = kh * lax.rsqrt(jnp.mean(kh * kh, axis=-1, keepdims=True) + NORM_EPS) * kn_ref[...]
        k_ref[0, :, sl] = kh.astype(BF16)
    v_ref[0] = kv[:, MEM_DIM:].astype(BF16)


def _mem_kv(mem, ln_mem, w_mem_kv, mem_k_norm):
    b, m, d = mem.shape
    return pl.pallas_call(
        _mem_kv_kernel,
        grid=(b,),
        in_specs=[pl.BlockSpec((1, m, d), lambda i: (i, 0, 0)),
                  pl.BlockSpec((1, d), lambda i: (0, 0)),
                  pl.BlockSpec((d, 2 * MEM_DIM), lambda i: (0, 0)),
                  pl.BlockSpec((1, MEM_HEAD_DIM), lambda i: (0, 0))],
        out_specs=[pl.BlockSpec((1, m, MEM_DIM), lambda i: (i, 0, 0)),
                   pl.BlockSpec((1, m, MEM_DIM), lambda i: (i, 0, 0))],
        out_shape=[jax.ShapeDtypeStruct((b, m, MEM_DIM), BF16),
                   jax.ShapeDtypeStruct((b, m, MEM_DIM), BF16)],
        compiler_params=_cparams(("parallel",)),
        name="mem_kv",
    )(mem, ln_mem.reshape(1, d), w_mem_kv.astype(BF16), mem_k_norm.reshape(1, MEM_HEAD_DIM))


def _mem_attn_kernel(q_ref, k_ref, v_ref, o_ref):
    bm = q_ref.shape[1]
    units = [(r0, hd) for r0 in range(0, bm, MEM_ROWS) for hd in range(MEM_DIM // MEM_HEAD_DIM)]
    cols = lambda hd: slice(hd * MEM_HEAD_DIM, (hd + 1) * MEM_HEAD_DIM)
    sc = [_dot_nt(q_ref[0, r0:r0 + MEM_ROWS, cols(hd)], k_ref[0, :, cols(hd)]) * (MEM_HEAD_DIM ** -0.5)
          for r0, hd in units]
    mx = [jnp.max(x, axis=-1, keepdims=True) for x in sc]
    pr = [jnp.exp(sc[n] - mx[n]) for n in range(len(units))]
    den = [jnp.sum(x, axis=-1, keepdims=True) for x in pr]
    for n, (r0, hd) in enumerate(units):
        o_ref[0, r0:r0 + MEM_ROWS, cols(hd)] = _dot((pr[n] / den[n]).astype(BF16), v_ref[0, :, cols(hd)]).astype(o_ref.dtype)


def _mem_attn(mq, mk, mv, bm=512):
    b, s, _ = mq.shape
    m = mk.shape[1]
    return pl.pallas_call(
        _mem_attn_kernel,
        grid=(b, s // bm),
        in_specs=[pl.BlockSpec((1, bm, MEM_DIM), lambda bi, si: (bi, si, 0)),
                  pl.BlockSpec((1, m, MEM_DIM), lambda bi, si: (bi, 0, 0)),
                  pl.BlockSpec((1, m, MEM_DIM), lambda bi, si: (bi, 0, 0))],
        out_specs=pl.BlockSpec((1, bm, MEM_DIM), lambda bi, si: (bi, si, 0)),
        out_shape=jax.ShapeDtypeStruct((b, s, MEM_DIM), BF16),
        compiler_params=_cparams(("parallel", "parallel")),
        name="mem_attn",
    )(mq, mk, mv)


def _mix_kernel(x_ref, yr_ref, o0_ref, o1_ref, o2_ref, l0_ref, l1_ref, l2_ref, ym_ref, gt_ref,
                pr_ref, pd_ref, pm_ref, wo_ref, ln2_ref, wr_ref, br_ref,
                x1_ref, h2_ref, route_ref, cnt_ref):
    d = x_ref.shape[1]
    wide = lambda ref: jnp.concatenate([ref[0, hf] for hf in range(DIL_GROUP_DIM // LANES)], axis=1)
    l0, l1, l2 = wide(l0_ref), wide(l1_ref), wide(l2_ref)
    m = jnp.maximum(jnp.maximum(l0, l1), l2)
    e0, e1, e2 = jnp.exp(l0 - m), jnp.exp(l1 - m), jnp.exp(l2 - m)
    y_dil = (e0 * wide(o0_ref) + e1 * wide(o1_ref) + e2 * wide(o2_ref)) / (e0 + e1 + e2)
    mixed = (gt_ref[:, 0:d].astype(F32) * _dot(yr_ref[...], pr_ref[...])
             + gt_ref[:, d:2 * d].astype(F32) * _dot(y_dil.astype(BF16), pd_ref[...])
             + gt_ref[:, 2 * d:3 * d].astype(F32) * _dot(ym_ref[...], pm_ref[...]))
    x1 = x_ref[...] + _dot(mixed.astype(BF16), wo_ref[...])
    x1_ref[...] = x1
    h2 = x1 * lax.rsqrt(jnp.mean(x1 * x1, axis=-1, keepdims=True) + NORM_EPS) * ln2_ref[...]
    h2_ref[...] = _pack_bf16_pairs(h2)
    logits = _dot_x3(h2, wr_ref[...]) + br_ref[...]
    lane = lax.broadcasted_iota(jnp.int32, logits.shape, 1)
    lane_f = lane.astype(F32)
    route = jnp.zeros(logits.shape, F32)
    onehot = jnp.zeros(logits.shape, F32)
    vals = []
    for kq in range(TOP_K):
        mx = jnp.max(logits, axis=-1, keepdims=True)
        idx = jnp.min(jnp.where(logits == mx, lane_f, float(LANES)), axis=-1, keepdims=True)
        hit = lane_f == idx
        vals.append(mx)
        route = jnp.where(lane == kq, idx, route)
        onehot = jnp.where(hit, 1.0, onehot)
        logits = jnp.where(hit, -jnp.inf, logits)
    ex = [jnp.exp(vq - vals[0]) for vq in vals]
    den = ex[0] + ex[1] + ex[2] + ex[3]
    for kq in range(TOP_K):
        route = jnp.where(lane == TOP_K + kq, ex[kq] / den, route)
    route_ref[...] = route

    @pl.when(pl.program_id(0) == 0)
    def _():
        cnt_ref[...] = jnp.zeros_like(cnt_ref)

    cnt_ref[...] += jnp.sum(onehot, axis=0, keepdims=True)


def _mix(x2, y_rwkv, outs, lses, y_mem, gates, p_rwkv, p_dil, p_mem, w_out, ln2, w_router, b_router, bm=512):
    t, d = x2.shape
    wr = jnp.zeros((d, LANES), F32).at[:, :N_EXPERTS].set(w_router)
    br = jnp.full((1, LANES), -jnp.inf, F32).at[0, :N_EXPERTS].set(b_router)
    row = lambda w: pl.BlockSpec((bm, w), lambda i: (i, 0))
    const = lambda a: pl.BlockSpec(a.shape, lambda i: (0,) * a.ndim)
    tiles_per_seq = outs[0].shape[2] // bm
    dil = pl.BlockSpec((1, DIL_GROUP_DIM // LANES, bm, LANES),
                       lambda i: (i // tiles_per_seq, 0, i % tiles_per_seq, 0))
    weights = [p_rwkv.astype(BF16), p_dil.astype(BF16), p_mem.astype(BF16), w_out.astype(BF16),
               ln2.reshape(1, d), wr, br]
    return pl.pallas_call(
        _mix_kernel,
        grid=(t // bm,),
        in_specs=[row(d), row(RWKV_DIM)] + [dil] * 6 + [row(MEM_DIM), row(3 * d)]
                 + [const(a) for a in weights],
        out_specs=[row(d), row(d // 2), row(LANES), pl.BlockSpec((1, LANES), lambda i: (0, 0))],
        out_shape=[jax.ShapeDtypeStruct((t, d), F32), jax.ShapeDtypeStruct((t, d // 2), jnp.uint32),
                   jax.ShapeDtypeStruct((t, LANES), F32), jax.ShapeDtypeStruct((1, LANES), F32)],
        compiler_params=_cparams(("arbitrary",)),
        name="mix",
    )(x2, y_rwkv, *outs, *lses, y_mem, gates, *weights)


def _route_kernel(route_ref, pstart_ref, tri_ref, dest_ref, carry_ref):
    @pl.when(pl.program_id(0) == 0)
    def _():
        carry_ref[...] = jnp.zeros_like(carry_ref)

    route = route_ref[...]
    lane = lax.broadcasted_iota(jnp.int32, route.shape, 1)
    lane_f = lane.astype(F32)
    hits = [lane_f == route[:, kq:kq + 1] for kq in range(TOP_K)]
    onehot = jnp.zeros(route.shape, F32)
    for hq in hits:
        onehot = jnp.where(hq, 1.0, onehot)
    rank = _dot(tri_ref[...], onehot.astype(BF16)) + carry_ref[...]
    slot = pstart_ref[...] + rank
    dest = jnp.zeros(route.shape, jnp.int32)
    for kq in range(TOP_K):
        dk = jnp.sum(jnp.where(hits[kq], slot, 0.0), axis=-1, keepdims=True)
        dest = jnp.where(lane == kq, dk.astype(jnp.int32), dest)
    dest_ref[...] = dest
    carry_ref[...] += jnp.sum(onehot, axis=0, keepdims=True)


def _route(route, pstart, bm=1024):
    t = route.shape[0]
    i = jnp.arange(bm)
    tri = (i[None, :] < i[:, None]).astype(BF16)
    return pl.pallas_call(
        _route_kernel,
        grid=(t // bm,),
        in_specs=[pl.BlockSpec((bm, LANES), lambda i: (i, 0)),
                  pl.BlockSpec((1, LANES), lambda i: (0, 0)),
                  pl.BlockSpec((bm, bm), lambda i: (0, 0))],
        out_specs=pl.BlockSpec((bm, LANES), lambda i: (i, 0)),
        out_shape=jax.ShapeDtypeStruct((t, LANES), jnp.int32),
        scratch_shapes=[pltpu.VMEM((1, LANES), F32)],
        compiler_params=_cparams(("arbitrary",)),
        name="route",
    )(route, pstart, tri)


def _row_copy(src_ref, src_row, dst_ref, dst_row, sem):
    return pltpu.make_async_copy(src_ref.at[pl.ds(src_row, 1)], dst_ref.at[pl.ds(dst_row, 1)], sem)


def _dispatch_kernel(last_ref, dest_ref, h_ref, xs_ref, zero_ref, hbuf_ref, sem_s, sem_l, zsem, *, bm):
    i = pl.program_id(0)
    n = pl.num_programs(0)

    def load(tile, b):
        rows = pl.ds(pl.multiple_of(tile * bm, bm), bm)
        return pltpu.make_async_copy(h_ref.at[rows], hbuf_ref.at[b], sem_l.at[b])

    def wait_scatter(b):
        for kq in range(TOP_K):
            pltpu.make_async_copy(hbuf_ref.at[0], xs_ref.at[pl.ds(0, bm)], sem_s.at[b]).wait()

    @pl.when(i == 0)
    def _():
        load(0, 0).start()
        zero_ref[...] = jnp.zeros_like(zero_ref)

        def zero_copy(e):
            row = pl.multiple_of(last_ref[e] * EXPERT_ROWS, EXPERT_ROWS)
            return pltpu.make_async_copy(zero_ref, xs_ref.at[pl.ds(row, EXPERT_ROWS)], zsem)

        def z_issue(e, carry):
            @pl.when(last_ref[e] >= 0)
            def _():
                zero_copy(e).start()
            return carry

        def z_drain(e, carry):
            @pl.when(last_ref[e] >= 0)
            def _():
                zero_copy(e).wait()
            return carry

        lax.fori_loop(0, 2 * N_EXPERTS, z_issue, 0)
        lax.fori_loop(0, 2 * N_EXPERTS, z_drain, 0)

    @pl.when(i + 1 < n)
    def _():
        load(i + 1, (i + 1) % 3).start()

    load(i, i % 3).wait()

    for phase in range(6):
        @pl.when(i % 6 == phase)
        def _(phase=phase):
            def issue(j, carry):
                for kq in range(TOP_K):
                    _row_copy(hbuf_ref.at[phase % 3], j, xs_ref, dest_ref[j * TOP_K + kq],
                              sem_s.at[phase % 2]).start(priority=kq % 2)
                return carry

            lax.fori_loop(0, bm, issue, 0)

    @pl.when(i > 0)
    def _():
        wait_scatter((i - 1) % 2)

    @pl.when(i == n - 1)
    def _():
        wait_scatter(i % 2)


def _dispatch(last_blk, dest_flat, h2, n_slots, bm=512):
    t, d = h2.shape
    grid_spec = pltpu.PrefetchScalarGridSpec(
        num_scalar_prefetch=1,
        grid=(t // bm,),
        in_specs=[pl.BlockSpec((bm * TOP_K,), lambda i, lb: (i,), memory_space=pltpu.SMEM),
                  pl.BlockSpec(memory_space=pl.ANY)],
        out_specs=pl.BlockSpec(memory_space=pl.ANY),
        scratch_shapes=[pltpu.VMEM((EXPERT_ROWS, d), h2.dtype), pltpu.VMEM((3, bm, d), h2.dtype),
                        pltpu.SemaphoreType.DMA((2,)), pltpu.SemaphoreType.DMA((3,)), pltpu.SemaphoreType.DMA],
    )
    return pl.pallas_call(
        functools.partial(_dispatch_kernel, bm=bm),
        grid_spec=grid_spec,
        out_shape=jax.ShapeDtypeStruct((n_slots, d), h2.dtype),
        compiler_params=_cparams(("arbitrary",)),
        name="dispatch",
    )(last_blk, dest_flat, h2)


FF_CHUNK = 256
EXPERT_ROWS = 256
BLOCKS_PER_STEP = 4


def _expert_kernel(be_ref, par_ref, nxt_ref, nu_ref, xs_ref, w1_ref, b1_ref, w2_ref, b2_ref, sel_ref, ys_ref,
                   w1f_ref, w2f_ref, w1p_ref, w2b_ref, act_ref, sem):
    i = pl.program_id(0)
    d_ff2 = w1_ref.shape[2]
    half = FF_CHUNK // 2

    def weight_copies(expert, buf):
        return (pltpu.make_async_copy(w1_ref.at[expert], w1f_ref.at[buf], sem.at[0, buf]),
                pltpu.make_async_copy(w2_ref.at[expert], w2f_ref.at[buf], sem.at[1, buf]))

    @pl.when(i == 0)
    def _():
        for cp in weight_copies(be_ref[0], par_ref[0]):
            cp.start()

    def convert(blk, e, slot):
        for cp in weight_copies(e, slot):
            cp.wait()
        nxt = nxt_ref[blk]

        @pl.when(nxt >= 0)
        def _():
            for cp in weight_copies(nxt, 1 - slot):
                cp.start()

        for c in range(d_ff2 // FF_CHUNK):
            sl = slice(c * FF_CHUNK, (c + 1) * FF_CHUNK)
            w1p_ref[:, sl] = _dot(w1f_ref[slot, :, sl].astype(BF16), sel_ref[...]).astype(BF16)
        w2b_ref[...] = w2f_ref[slot].astype(BF16)

    def compute(rows, e):
        n_rows = rows.stop - rows.start
        x = _unpack_bf16_pairs(xs_ref[rows, :])
        for c in range(d_ff2 // (2 * FF_CHUNK)):
            sl = slice(2 * c * FF_CHUNK, 2 * (c + 1) * FF_CHUNK)
            hb = _dot(x, w1p_ref[:, sl]) + b1_ref[e, :, sl]
            for j in range(2):
                x_glu = jnp.minimum(hb[:, j * FF_CHUNK:j * FF_CHUNK + half], SWIGLU_LIMIT)
                x_lin = jnp.clip(hb[:, j * FF_CHUNK + half:(j + 1) * FF_CHUNK], -SWIGLU_LIMIT, SWIGLU_LIMIT)
                act = x_glu * _sigmoid(SWIGLU_ALPHA * x_glu) * (x_lin + 1.0)
                act_ref[0:n_rows, (2 * c + j) * half:(2 * c + j + 1) * half] = act.astype(BF16)
        ys_ref[rows, :] = _dot(act_ref[0:n_rows, :], w2b_ref[...]) + b2_ref[e]

    blk0 = i * BLOCKS_PER_STEP
    n_used = nu_ref[0]

    def run(lo, n):
        first = blk0 + lo
        last = first + n - 1
        e = be_ref[first]
        rows = slice(lo * EXPERT_ROWS, (lo + n) * EXPERT_ROWS)
        fresh = jnp.logical_or(first == 0, be_ref[jnp.maximum(first - 1, 0)] != e)
        whole = jnp.logical_and(be_ref[last] == e, last < n_used)

        @pl.when(whole)
        def _():
            @pl.when(fresh)
            def _():
                convert(first, e, par_ref[first])

            compute(rows, e)

        if n > 1:
            @pl.when(jnp.logical_not(whole))
            def _():
                run(lo, n // 2)
                run(lo + n // 2, n // 2)
        else:
            @pl.when(jnp.logical_not(whole))
            def _():
                ys_ref[rows, :] = jnp.zeros((EXPERT_ROWS, ys_ref.shape[1]), F32)

    run(0, BLOCKS_PER_STEP)


def _chunk_deinterleave(a):
    lead = a.shape[:-1]
    a = a.reshape(lead + (a.shape[-1] // FF_CHUNK, FF_CHUNK // 2, 2))
    return jnp.swapaxes(a, -1, -2).reshape(lead + (-1,))


def _experts(block_e, parity, next_e, n_used, xs, w1, b1, w2, b2):
    n_slots, dx = xs.shape
    n_e, d, d_ff2 = w1.shape
    n_blocks = n_slots // EXPERT_ROWS
    i = jnp.arange(FF_CHUNK)
    src = jnp.where(i < FF_CHUNK // 2, 2 * i, 2 * (i - FF_CHUNK // 2) + 1)
    sel = (jnp.arange(FF_CHUNK)[:, None] == src[None, :]).astype(BF16)
    b1p = _chunk_deinterleave(b1).reshape(n_e, 1, d_ff2)
    step_rows = BLOCKS_PER_STEP * EXPERT_ROWS
    imap = lambda i, be, par, nxt, nu: (i, 0)
    xmap = lambda i, be, par, nxt, nu: (jnp.minimum(i, (nu[0] - 1) // BLOCKS_PER_STEP), 0)
    cmap = lambda i, be, par, nxt, nu: (0, 0, 0)
    grid_spec = pltpu.PrefetchScalarGridSpec(
        num_scalar_prefetch=4,
        grid=(n_blocks // BLOCKS_PER_STEP,),
        in_specs=[pl.BlockSpec((step_rows, dx), xmap),
                  pl.BlockSpec(memory_space=pl.ANY),
                  pl.BlockSpec((n_e, 1, d_ff2), cmap),
                  pl.BlockSpec(memory_space=pl.ANY),
                  pl.BlockSpec((n_e, 1, d), cmap),
                  pl.BlockSpec((FF_CHUNK, FF_CHUNK), lambda i, be, par, nxt, nu: (0, 0))],
        out_specs=pl.BlockSpec((step_rows, d), imap),
        scratch_shapes=[pltpu.VMEM((2, d, d_ff2), F32), pltpu.VMEM((2, d_ff2 // 2, d), F32),
                        pltpu.VMEM((d, d_ff2), BF16), pltpu.VMEM((d_ff2 // 2, d), BF16),
                        pltpu.VMEM((BLOCKS_PER_STEP * EXPERT_ROWS, d_ff2 // 2), BF16),
                        pltpu.SemaphoreType.DMA((2, 2))],
    )
    return pl.pallas_call(
        _expert_kernel,
        grid_spec=grid_spec,
        out_shape=jax.ShapeDtypeStruct((n_slots, d), F32),
        compiler_params=_cparams(("arbitrary",), vmem=EXPERT_VMEM_LIMIT),
        name="experts",
    )(block_e, parity, next_e, n_used, xs, w1, b1p, w2, b2.reshape(n_e, 1, d), sel)


def _combine_kernel(dest_ref, dnext_ref, x1_ref, route_ref, ys_ref, o_ref, buf_ref, sem, *, bm):
    i = pl.program_id(0)
    n = pl.num_programs(0)
    slot = i % 2

    def gather(d_ref, buf):
        def issue(g, carry):
            base = pl.multiple_of(g * SUBLANES, SUBLANES)
            for r in range(SUBLANES):
                for kq in range(TOP_K):
                    _row_copy(ys_ref, d_ref[(base + r) * TOP_K + kq], buf_ref.at[buf, kq], base + r,
                              sem.at[buf]).start(priority=kq % 2)
            return carry

        lax.fori_loop(0, bm // SUBLANES, issue, 0)

    @pl.when(i == 0)
    def _():
        gather(dest_ref, 0)

    for nxt in range(2):
        @pl.when(jnp.logical_and(i + 1 < n, (i + 1) % 2 == nxt))
        def _(nxt=nxt):
            gather(dnext_ref, nxt)

    for kq in range(TOP_K):
        pltpu.make_async_copy(ys_ref.at[pl.ds(0, bm)], buf_ref.at[slot, kq], sem.at[slot]).wait()
    acc = x1_ref[...]
    for kq in range(TOP_K):
        acc = acc + route_ref[:, TOP_K + kq:TOP_K + kq + 1] * buf_ref[slot, kq]
    o_ref[...] = acc


def _combine(dest_flat, x1, route, ys, bm=512):
    t, d = x1.shape
    n = t // bm
    return pl.pallas_call(
        functools.partial(_combine_kernel, bm=bm),
        grid=(n,),
        in_specs=[pl.BlockSpec((bm * TOP_K,), lambda i: (i,), memory_space=pltpu.SMEM),
                  pl.BlockSpec((bm * TOP_K,), lambda i: (jnp.minimum(i + 1, n - 1),), memory_space=pltpu.SMEM),
                  pl.BlockSpec((bm, d), lambda i: (i, 0)),
                  pl.BlockSpec((bm, LANES), lambda i: (i, 0)),
                  pl.BlockSpec(memory_space=pl.ANY)],
        out_specs=pl.BlockSpec((bm, d), lambda i: (i, 0)),
        out_shape=jax.ShapeDtypeStruct((t, d), F32),
        scratch_shapes=[pltpu.VMEM((2, TOP_K, bm, d), F32), pltpu.SemaphoreType.DMA((2,))],
        compiler_params=_cparams(("arbitrary",)),
        name="combine",
    )(dest_flat, dest_flat, x1, route, ys)


def _layer(x, mem, ln1, w_in, token_mu, rwkv_w0, rwkv_w2, rwkv_a0, rwkv_a2, rwkv_g2,
           rwkv_k_k, rwkv_k_a, rwkv_r_k, rwkv_ln_w, rwkv_ln_b, dil_q_norm, dil_k_norm,
           ln_mem, w_mem_kv, mem_q_norm, mem_k_norm, p_rwkv, p_dil, p_mem, w_out,
           ln2, w_router, b_router, w1, b1, w2, b2):
    b, s, d = x.shape
    t = b * s
    x2 = x.reshape(t, d)
    zr, dil_qkv, mq, gates = _in_proj(x2, ln1, w_in, dil_q_norm, dil_k_norm, mem_q_norm)
    y_rwkv = _rwkv(zr.reshape(b, s, RWKV_IN), token_mu, rwkv_w0, rwkv_w2, rwkv_a0, rwkv_a2, rwkv_g2,
                   rwkv_k_k, rwkv_k_a, rwkv_r_k.reshape(-1), rwkv_ln_w, rwkv_ln_b)
    outs, lses = [], []
    for g, (_, dilation) in enumerate(DIL_PATTERNS):
        o, l = _dil_group(*dil_qkv[3 * g:3 * g + 3], b, dilation)
        outs.append(o)
        lses.append(l)
    mk, mv = _mem_kv(mem, ln_mem, w_mem_kv, mem_k_norm)
    y_mem = _mem_attn(mq.reshape(b, s, MEM_DIM), mk, mv)
    x1, h2, route, counts = _mix(x2, y_rwkv.reshape(t, RWKV_DIM), outs, lses, y_mem.reshape(t, MEM_DIM), gates,
                                 p_rwkv, p_dil, p_mem, w_out, ln2, w_router, b_router)

    counts = counts[0, :N_EXPERTS].astype(jnp.int32)
    nblk = (counts + EXPERT_ROWS - 1) // EXPERT_ROWS
    bend = jnp.cumsum(nblk)
    pstart = ((bend - nblk) * EXPERT_ROWS).astype(F32)
    pstart = jnp.zeros((1, LANES), F32).at[0, :N_EXPERTS].set(pstart)
    n_blocks = (t * TOP_K) // EXPERT_ROWS + N_EXPERTS
    block_e = jnp.sum(bend[None, :] <= jnp.arange(n_blocks, dtype=jnp.int32)[:, None], axis=1)
    block_e = jnp.minimum(block_e, N_EXPERTS - 1).astype(jnp.int32)
    n_used = bend[-1:].astype(jnp.int32)
    used = nblk > 0
    eids = jnp.arange(N_EXPERTS, dtype=jnp.int32)
    ordinal = jnp.cumsum(used.astype(jnp.int32)) - 1
    later = jnp.where(used[None, :] & (eids[None, :] > eids[:, None]), eids[None, :], N_EXPERTS)
    next_used = jnp.min(later, axis=1)
    next_used = jnp.where(next_used < N_EXPERTS, next_used, -1).astype(jnp.int32)
    is_e = (block_e[:, None] == eids[None, :]).astype(jnp.int32)
    parity = (jnp.sum(is_e * ordinal[None, :], axis=1) % 2).astype(jnp.int32)
    next_e = jnp.sum(is_e * next_used[None, :], axis=1).astype(jnp.int32)
    tail = n_used[0] + eids
    last_blk = jnp.concatenate([jnp.where(used, bend - 1, -1), jnp.where(tail < n_blocks, tail, -1)]).astype(jnp.int32)

    dest = _route(route, pstart)[:, :TOP_K].reshape(-1)
    xs = _dispatch(last_blk, dest, h2, n_blocks * EXPERT_ROWS)
    ys = _experts(block_e, parity, next_e, n_used, xs, w1, b1, w2, b2)
    out = _combine(dest, x1, route, ys)
    return out.reshape(b, s, d)


def kernel(x, mem, ln1, w_in, token_mu, rwkv_w0, rwkv_w2, rwkv_a0, rwkv_a2, rwkv_g2, rwkv_k_k, rwkv_k_a, rwkv_r_k, rwkv_ln_w, rwkv_ln_b, dil_q_norm, dil_k_norm, ln_mem, w_mem_kv, mem_q_norm, mem_k_norm, p_rwkv, p_dil, p_mem, w_out, ln2, w_router, b_router, w1, b1, w2, b2):
    params = (ln1, w_in, token_mu, rwkv_w0, rwkv_w2, rwkv_a0, rwkv_a2, rwkv_g2, rwkv_k_k, rwkv_k_a,
              rwkv_r_k, rwkv_ln_w, rwkv_ln_b, dil_q_norm, dil_k_norm, ln_mem, w_mem_kv, mem_q_norm,
              mem_k_norm, p_rwkv, p_dil, p_mem, w_out, ln2, w_router, b_router, w1, b1, w2, b2)
    for l in range(ln1.shape[0]):
        x = _layer(x, mem, *[p[l] for p in params])
    return x
```
